```python
import jax, jax.numpy as jnp
from jax import lax
import numpy as np

D_MODEL = 2048
BATCH = 8
SEQ = 4096
DEPTH = 4

CHUNK = 64
Q_BLOCK = 128
EPS = 1e-6
NEG_INF = -1e30
ROPE_BASE = 10000.0

RET_HEADS = 8
RET_DK = 128
RET_DV = 256
RET_QK_W = RET_HEADS * RET_DK
RET_V_W = RET_HEADS * RET_DV

MLA_HEADS = 16
MLA_Q_RANK = 512
MLA_KV_RANK = 512
MLA_NOPE = 128
MLA_ROPE = 64
MLA_DV = 128
MLA_V_W = MLA_HEADS * MLA_DV

N_BRANCH = 2
IN_SPLITS = (RET_QK_W, RET_QK_W, RET_V_W, RET_V_W, MLA_Q_RANK, MLA_KV_RANK, MLA_ROPE, MLA_V_W, N_BRANCH * D_MODEL)
D_IN = 2 * RET_QK_W + 2 * RET_V_W + MLA_Q_RANK + MLA_KV_RANK + MLA_ROPE + MLA_V_W + N_BRANCH * D_MODEL

kernel_name = "hybrid_retention_mla_adaln_trunk"


def rms_norm(x, g):
    xf = x.astype(jnp.float32)
    y = xf * lax.rsqrt(jnp.mean(xf * xf, axis=-1, keepdims=True) + EPS)
    return (y * g.astype(jnp.float32)).astype(x.dtype)


def rope_tables(positions, dim):
    inv = 1.0 / (ROPE_BASE ** (jnp.arange(0, dim, 2, dtype=jnp.float32) / dim))
    ang = positions.astype(jnp.float32)[..., None] * inv
    return jnp.cos(ang), jnp.sin(ang)


def apply_rope(x, cos, sin):
    half = x.shape[-1] // 2
    x1, x2 = x[..., :half], x[..., half:]
    cos = cos.astype(x.dtype)
    sin = sin.astype(x.dtype)
    return jnp.concatenate([x1 * cos - x2 * sin, x1 * sin + x2 * cos], axis=-1)


def retention(q, k, v):
    B, S, H, dk = q.shape
    dv = v.shape[-1]
    nc = S // CHUNK
    f32 = jnp.float32
    log_gamma = jnp.log(1.0 - 2.0 ** (-5.0 - jnp.arange(H, dtype=f32)))
    qf = q.astype(f32).reshape(B, nc, CHUNK, H, dk)
    kf = (k.astype(f32) * (dk ** -0.5)).reshape(B, nc, CHUNK, H, dk)
    vf = v.astype(f32).reshape(B, nc, CHUNK, H, dv)
    idx = jnp.arange(CHUNK, dtype=f32)
    dmat = jnp.exp(jnp.abs(idx[:, None] - idx[None, :])[None] * log_gamma[:, None, None])
    scores = jnp.einsum('bnihd,bnjhd->bnhij', qf, kf) * dmat[None, None]
    o_intra = jnp.einsum('bnhij,bnjhe->bnihe', scores, vf)
    xi = jnp.exp((idx + 1.0)[:, None] * log_gamma[None, :])
    zeta = jnp.exp((CHUNK - 1.0 - idx)[:, None] * log_gamma[None, :])
    decay_chunk = jnp.exp(CHUNK * log_gamma)
    q_x = (qf * xi[None, None, :, :, None]).transpose(1, 0, 2, 3, 4)
    k_z = (kf * zeta[None, None, :, :, None]).transpose(1, 0, 2, 3, 4)
    v_t = vf.transpose(1, 0, 2, 3, 4)

    def step(state, inp):
        qc, kc, vc = inp
        o = jnp.einsum('bihd,bhde->bihe', qc, state)
        state = state * decay_chunk[None, :, None, None] + jnp.einsum('bjhd,bjhe->bhde', kc, vc)
        return state, o

    init = jnp.zeros((B, H, dk, dv), f32)
    _, o_cross = lax.scan(step, init, (q_x, k_z, v_t))
    o = o_intra + o_cross.transpose(1, 0, 2, 3, 4)
    return o.reshape(B, S, H, dv)


def head_group_norm(o):
    mu = jnp.mean(o, axis=-1, keepdims=True)
    var = jnp.mean(jnp.square(o - mu), axis=-1, keepdims=True)
    return (o - mu) * lax.rsqrt(var + EPS)


def mla_attention(q_nope, q_rope, k_nope, k_rope, v):
    B, S, H, _ = q_nope.shape
    nb = S // Q_BLOCK
    scale = (MLA_NOPE + MLA_ROPE) ** -0.5
    key_chunk = jnp.arange(S) // CHUNK

    def block(i):
        qs = i * Q_BLOCK
        qn = lax.dynamic_slice_in_dim(q_nope, qs, Q_BLOCK, axis=1)
        qr = lax.dynamic_slice_in_dim(q_rope, qs, Q_BLOCK, axis=1)
        s = (jnp.einsum('bqhd,bkhd->bhqk', qn, k_nope)
             + jnp.einsum('bqhd,bkd->bhqk', qr, k_rope)).astype(jnp.float32) * scale
        q_chunk = (qs + jnp.arange(Q_BLOCK)) // CHUNK
        mask = key_chunk[None, :] <= q_chunk[:, None]
        s = jnp.where(mask[None, None], s, NEG_INF)
        p = jax.nn.softmax(s, axis=-1)
        return jnp.einsum('bhqk,bkhd->bqhd', p.astype(v.dtype), v)

    out = lax.map(block, jnp.arange(nb))
    return out.transpose(1, 0, 2, 3, 4).reshape(B, S, H * v.shape[-1])


def hybrid_layer(x, c_act, cos_r, sin_r, cos_m, sin_m,
                 w_mod, b_mod, g_norm, w_in, g_cq, g_ckv, w_uq, w_ukv,
                 w_ret_proj, w_mla_proj, w_out):
    B, S, D = x.shape
    mod = c_act @ w_mod + b_mod
    shift, scale, gate = jnp.split(mod, 3, axis=-1)
    h = rms_norm(x, g_norm) * (1.0 + scale[:, None, :]) + shift[:, None, :]

    proj = h @ w_in
    points = np.cumsum(IN_SPLITS)[:-1].tolist()
    rq, rk, rv, rg, cq, ckv, kr, mg, bg = jnp.split(proj, points, axis=-1)

    rq = apply_rope(rq.reshape(B, S, RET_HEADS, RET_DK), cos_r[:, :, None], sin_r[:, :, None])
    rk = apply_rope(rk.reshape(B, S, RET_HEADS, RET_DK), cos_r[:, :, None], sin_r[:, :, None])
    rv = rv.reshape(B, S, RET_HEADS, RET_DV)
    o_ret = head_group_norm(retention(rq, rk, rv)).reshape(B, S, RET_V_W).astype(x.dtype)
    y_ret = (o_ret * jax.nn.silu(rg)) @ w_ret_proj

    q = (rms_norm(cq, g_cq) @ w_uq).reshape(B, S, MLA_HEADS, MLA_NOPE + MLA_ROPE)
    q_nope, q_rope = q[..., :MLA_NOPE], q[..., MLA_NOPE:]
    q_rope = apply_rope(q_rope, cos_m[:, :, None], sin_m[:, :, None])
    kv = (rms_norm(ckv, g_ckv) @ w_ukv).reshape(B, S, MLA_HEADS, MLA_NOPE + MLA_DV)
    k_nope, v = kv[..., :MLA_NOPE], kv[..., MLA_NOPE:]
    k_rope = apply_rope(kr, cos_m, sin_m)
    o_mla = mla_attention(q_nope, q_rope, k_nope, k_rope, v)
    y_mla = (o_mla * jax.nn.silu(mg)) @ w_mla_proj

    g_a, g_b = jnp.split(jax.nn.sigmoid(bg), 2, axis=-1)
    merged = g_a * y_ret + g_b * y_mla
    out = merged @ w_out
    return x + gate[:, None, :] * out


def _fwd_setup_inputs(seed: int = 0) -> dict:
    key = jax.random.key(seed)
    ks = jax.random.split(key, 16)
    f32 = jnp.float32

    def nrm(k, shape, fan_in, mult=1.0):
        return jax.random.normal(k, shape, f32) * (mult * fan_in ** -0.5)

    x = jax.random.normal(ks[0], (BATCH, SEQ, D_MODEL), f32)
    c = jax.random.normal(ks[1], (BATCH, D_MODEL), f32)
    positions = (jnp.arange(SEQ, dtype=jnp.int32)[None, :]
                 + jax.random.randint(ks[2], (BATCH, 1), 0, 1024, dtype=jnp.int32))
    w_mod = nrm(ks[3], (DEPTH, D_MODEL, 3 * D_MODEL), D_MODEL, 0.5)
    b_mod = 0.01 * jax.random.normal(ks[4], (DEPTH, 3 * D_MODEL), f32)
    g_norm = 1.0 + 0.02 * jax.random.normal(ks[5], (DEPTH, D_MODEL), f32)
    w_in = nrm(ks[6], (DEPTH, D_MODEL, D_IN), D_MODEL)
    g_cq = 1.0 + 0.02 * jax.random.normal(ks[7], (DEPTH, MLA_Q_RANK), f32)
    g_ckv = 1.0 + 0.02 * jax.random.normal(ks[8], (DEPTH, MLA_KV_RANK), f32)
    w_uq = nrm(ks[9], (DEPTH, MLA_Q_RANK, MLA_HEADS * (MLA_NOPE + MLA_ROPE)), MLA_Q_RANK)
    w_ukv = nrm(ks[10], (DEPTH, MLA_KV_RANK, MLA_HEADS * (MLA_NOPE + MLA_DV)), MLA_KV_RANK)
    w_ret_proj = nrm(ks[11], (DEPTH, RET_V_W, D_MODEL), RET_V_W)
    w_mla_proj = nrm(ks[12], (DEPTH, MLA_V_W, D_MODEL), MLA_V_W)
    w_out = nrm(ks[13], (DEPTH, D_MODEL, D_MODEL), D_MODEL)
    g_final = 1.0 + 0.02 * jax.random.normal(ks[14], (D_MODEL,), f32)
    return {"x": x, "c": c, "positions": positions, "w_mod": w_mod, "b_mod": b_mod,
            "g_norm": g_norm, "w_in": w_in, "g_cq": g_cq, "g_ckv": g_ckv, "w_uq": w_uq,
            "w_ukv": w_ukv, "w_ret_proj": w_ret_proj, "w_mla_proj": w_mla_proj,
            "w_out": w_out, "g_final": g_final}


def _fwd_reference(x, c, positions, w_mod, b_mod, g_norm, w_in, g_cq, g_ckv, w_uq, w_ukv,
              w_ret_proj, w_mla_proj, w_out, g_final):
    c_act = jax.nn.silu(c)
    cos_r, sin_r = rope_tables(positions, RET_DK)
    cos_m, sin_m = rope_tables(positions, MLA_ROPE)
    for l in range(DEPTH):
        x = hybrid_layer(x, c_act, cos_r, sin_r, cos_m, sin_m,
                         w_mod[l], b_mod[l], g_norm[l], w_in[l], g_cq[l], g_ckv[l],
                         w_uq[l], w_ukv[l], w_ret_proj[l], w_mla_proj[l], w_out[l])
    return rms_norm(x, g_final)


import jax as _jax
import jax.numpy as _jnp

TWIN_FORMAT = 'train_step'
FWD_PARAMS = ['x', 'c', 'positions', 'w_mod', 'b_mod', 'g_norm', 'w_in', 'g_cq', 'g_ckv', 'w_uq', 'w_ukv', 'w_ret_proj', 'w_mla_proj', 'w_out', 'g_final']
TWIN_WEIGHTS = ['w_mod', 'b_mod', 'g_norm', 'w_in', 'g_cq', 'g_ckv', 'w_uq', 'w_ukv', 'w_ret_proj', 'w_mla_proj', 'w_out', 'g_final']
TWIN_DIFF_INPUT = 'x'
TWIN_INPUTS = ['x', 'c', 'positions', 'w_mod', 'b_mod', 'g_norm', 'w_in', 'g_cq', 'g_ckv', 'w_uq', 'w_ukv', 'w_ret_proj', 'w_mla_proj', 'w_out', 'g_final', 'loss_target', 'm_w_mod', 'm_b_mod', 'm_g_norm', 'm_w_in', 'm_g_cq', 'm_g_ckv', 'm_w_uq', 'm_w_ukv', 'm_w_ret_proj', 'm_w_mla_proj', 'm_w_out', 'm_g_final', 'v_w_mod', 'v_b_mod', 'v_g_norm', 'v_w_in', 'v_g_cq', 'v_g_ckv', 'v_w_uq', 'v_w_ukv', 'v_w_ret_proj', 'v_w_mla_proj', 'v_w_out', 'v_g_final']
TWIN_OUTPUTS = ['loss', 'grad_x', 'grad_w_mod', 'grad_b_mod', 'grad_g_norm', 'grad_w_in', 'grad_g_cq', 'grad_g_ckv', 'grad_w_uq', 'grad_w_ukv', 'grad_w_ret_proj', 'grad_w_mla_proj', 'grad_w_out', 'grad_g_final', 'delta_w_mod', 'delta_b_mod', 'delta_g_norm', 'delta_w_in', 'delta_g_cq', 'delta_g_ckv', 'delta_w_uq', 'delta_w_ukv', 'delta_w_ret_proj', 'delta_w_mla_proj', 'delta_w_out', 'delta_g_final', 'new_m_w_mod', 'new_m_b_mod', 'new_m_g_norm', 'new_m_w_in', 'new_m_g_cq', 'new_m_g_ckv', 'new_m_w_uq', 'new_m_w_ukv', 'new_m_w_ret_proj', 'new_m_w_mla_proj', 'new_m_w_out', 'new_m_g_final', 'new_v_w_mod', 'new_v_b_mod', 'new_v_g_norm', 'new_v_w_in', 'new_v_g_cq', 'new_v_g_ckv', 'new_v_w_uq', 'new_v_w_ukv', 'new_v_w_ret_proj', 'new_v_w_mla_proj', 'new_v_w_out', 'new_v_g_final']
TWIN_LEAF_KINDS = {'loss': 'loss', 'grad_x': 'grad_x', 'grad_w_mod': 'grad_w', 'grad_b_mod': 'grad_w', 'grad_g_norm': 'grad_w', 'grad_w_in': 'grad_w', 'grad_g_cq': 'grad_w', 'grad_g_ckv': 'grad_w', 'grad_w_uq': 'grad_w', 'grad_w_ukv': 'grad_w', 'grad_w_ret_proj': 'grad_w', 'grad_w_mla_proj': 'grad_w', 'grad_w_out': 'grad_w', 'grad_g_final': 'grad_w', 'delta_w_mod': 'delta_w', 'delta_b_mod': 'delta_w', 'delta_g_norm': 'delta_w', 'delta_w_in': 'delta_w', 'delta_g_cq': 'delta_w', 'delta_g_ckv': 'delta_w', 'delta_w_uq': 'delta_w', 'delta_w_ukv': 'delta_w', 'delta_w_ret_proj': 'delta_w', 'delta_w_mla_proj': 'delta_w', 'delta_w_out': 'delta_w', 'delta_g_final': 'delta_w', 'new_m_w_mod': 'new_m', 'new_m_b_mod': 'new_m', 'new_m_g_norm': 'new_m', 'new_m_w_in': 'new_m', 'new_m_g_cq': 'new_m', 'new_m_g_ckv': 'new_m', 'new_m_w_uq': 'new_m', 'new_m_w_ukv': 'new_m', 'new_m_w_ret_proj': 'new_m', 'new_m_w_mla_proj': 'new_m', 'new_m_w_out': 'new_m', 'new_m_g_final': 'new_m', 'new_v_w_mod': 'new_v', 'new_v_b_mod': 'new_v', 'new_v_g_norm': 'new_v', 'new_v_w_in': 'new_v', 'new_v_g_cq': 'new_v', 'new_v_g_ckv': 'new_v', 'new_v_w_uq': 'new_v', 'new_v_w_ukv': 'new_v', 'new_v_w_ret_proj': 'new_v', 'new_v_w_mla_proj': 'new_v', 'new_v_w_out': 'new_v', 'new_v_g_final': 'new_v'}


def _forward(args):
    return _fwd_reference(*[args[k] for k in FWD_PARAMS])


def _output_shape():
    def fwd():
        inp = _fwd_setup_inputs(0)
        return _fwd_reference(*[inp[k] for k in FWD_PARAMS])
    out = _jax.eval_shape(fwd)
    return out.shape, out.dtype

N_MICROBATCH = 1
ADAM_LR = 0.001
ADAM_B1 = 0.9
ADAM_B2 = 0.999
ADAM_EPS = 1e-08
ADAM_WD = 0.01
ADAM_STEP = 10
PER_EXAMPLE_BATCH_AXIS = {'x': 0, 'c': 0, 'positions': 0, 'loss_target': 0}
SHARED_INPUTS = []
_WEIGHT_DTYPES = {'w_mod': _jnp.float32, 'b_mod': _jnp.float32, 'g_norm': _jnp.float32, 'w_in': _jnp.float32, 'g_cq': _jnp.float32, 'g_ckv': _jnp.float32, 'w_uq': _jnp.float32, 'w_ukv': _jnp.float32, 'w_ret_proj': _jnp.float32, 'w_mla_proj': _jnp.float32, 'w_out': _jnp.float32, 'g_final': _jnp.float32}
MOMENT_SCALE = {'w_mod': 1.818279e-02, 'b_mod': 3.087730e-02, 'g_norm': 1.895277e-02, 'w_in': 8.296867e-03, 'g_cq': 2.663909e-03, 'g_ckv': 6.563710e-03, 'w_uq': 1.072948e-03, 'w_ukv': 2.257327e-03, 'w_ret_proj': 9.614054e-03, 'w_mla_proj': 3.009155e-03, 'w_out': 1.006470e-02, 'g_final': 1.598140e+01}


def _to_microbatches(a, axis):
    t = _jnp.moveaxis(a, axis, 0)
    t = t.reshape((N_MICROBATCH, t.shape[0] // N_MICROBATCH) + t.shape[1:])
    return _jnp.moveaxis(t, 1, axis + 1)


def setup_inputs(seed: int = 0) -> dict:
    inp = _fwd_setup_inputs(seed)
    key = _jax.random.fold_in(_jax.random.key(seed), 7919)
    shape, _ = _output_shape()
    out = dict(inp)
    out["loss_target"] = _jax.random.normal(_jax.random.fold_in(key, 0), shape, _jnp.float32)
    for i, name in enumerate(TWIN_WEIGHTS):
        w = inp[name].astype(_jnp.float32)
        if MOMENT_SCALE is None:
            s = _jnp.sqrt(_jnp.mean(_jnp.square(w)) + 1e-30)
        else:
            s = MOMENT_SCALE[name]
        km, kv = _jax.random.split(_jax.random.fold_in(key, i + 1))
        out[name] = w
        out["m_" + name] = s * _jax.random.normal(km, w.shape, _jnp.float32)
        out["v_" + name] = (s * s) * _jax.random.uniform(kv, w.shape, _jnp.float32, 0.5, 1.5)
    if N_MICROBATCH > 1:
        for name, axis in PER_EXAMPLE_BATCH_AXIS.items():
            out[name] = _to_microbatches(out[name], axis)
    return {'x': out['x'], 'c': out['c'], 'positions': out['positions'], 'w_mod': out['w_mod'], 'b_mod': out['b_mod'], 'g_norm': out['g_norm'], 'w_in': out['w_in'], 'g_cq': out['g_cq'], 'g_ckv': out['g_ckv'], 'w_uq': out['w_uq'], 'w_ukv': out['w_ukv'], 'w_ret_proj': out['w_ret_proj'], 'w_mla_proj': out['w_mla_proj'], 'w_out': out['w_out'], 'g_final': out['g_final'], 'loss_target': out['loss_target'], 'm_w_mod': out['m_w_mod'], 'm_b_mod': out['m_b_mod'], 'm_g_norm': out['m_g_norm'], 'm_w_in': out['m_w_in'], 'm_g_cq': out['m_g_cq'], 'm_g_ckv': out['m_g_ckv'], 'm_w_uq': out['m_w_uq'], 'm_w_ukv': out['m_w_ukv'], 'm_w_ret_proj': out['m_w_ret_proj'], 'm_w_mla_proj': out['m_w_mla_proj'], 'm_w_out': out['m_w_out'], 'm_g_final': out['m_g_final'], 'v_w_mod': out['v_w_mod'], 'v_b_mod': out['v_b_mod'], 'v_g_norm': out['v_g_norm'], 'v_w_in': out['v_w_in'], 'v_g_cq': out['v_g_cq'], 'v_g_ckv': out['v_g_ckv'], 'v_w_uq': out['v_w_uq'], 'v_w_ukv': out['v_w_ukv'], 'v_w_ret_proj': out['v_w_ret_proj'], 'v_w_mla_proj': out['v_w_mla_proj'], 'v_w_out': out['v_w_out'], 'v_g_final': out['v_g_final']}


def _loss(weights, diff, rest, loss_target):
    with _jax.named_scope("forward"):
        args = {**rest, TWIN_DIFF_INPUT: diff, **{k: w.astype(_WEIGHT_DTYPES[k]) for k, w in weights.items()}}
        y = _forward(args)
    with _jax.named_scope("loss_head"):
        err = _jnp.square(y.astype(_jnp.float32) - loss_target)
        return 0.5 * _jnp.sum(_jnp.mean(err, axis=-1)) if err.ndim else 0.5 * err


def _adamw(w, g, m, v):
    m = ADAM_B1 * m + (1.0 - ADAM_B1) * g
    v = ADAM_B2 * v + (1.0 - ADAM_B2) * _jnp.square(g)
    m_hat = m / (1.0 - ADAM_B1 ** ADAM_STEP)
    v_hat = v / (1.0 - ADAM_B2 ** ADAM_STEP)
    delta = -ADAM_LR * (m_hat / (_jnp.sqrt(v_hat) + ADAM_EPS) + ADAM_WD * w)
    return delta, m, v


def reference(x, c, positions, w_mod, b_mod, g_norm, w_in, g_cq, g_ckv, w_uq, w_ukv, w_ret_proj, w_mla_proj, w_out, g_final, loss_target, m_w_mod, m_b_mod, m_g_norm, m_w_in, m_g_cq, m_g_ckv, m_w_uq, m_w_ukv, m_w_ret_proj, m_w_mla_proj, m_w_out, m_g_final, v_w_mod, v_b_mod, v_g_norm, v_w_in, v_g_cq, v_g_ckv, v_w_uq, v_w_ukv, v_w_ret_proj, v_w_mla_proj, v_w_out, v_g_final):
    given = dict(x=x, c=c, positions=positions, w_mod=w_mod, b_mod=b_mod, g_norm=g_norm, w_in=w_in, g_cq=g_cq, g_ckv=g_ckv, w_uq=w_uq, w_ukv=w_ukv, w_ret_proj=w_ret_proj, w_mla_proj=w_mla_proj, w_out=w_out, g_final=g_final, loss_target=loss_target, m_w_mod=m_w_mod, m_b_mod=m_b_mod, m_g_norm=m_g_norm, m_w_in=m_w_in, m_g_cq=m_g_cq, m_g_ckv=m_g_ckv, m_w_uq=m_w_uq, m_w_ukv=m_w_ukv, m_w_ret_proj=m_w_ret_proj, m_w_mla_proj=m_w_mla_proj, m_w_out=m_w_out, m_g_final=m_g_final, v_w_mod=v_w_mod, v_b_mod=v_b_mod, v_g_norm=v_g_norm, v_w_in=v_w_in, v_g_cq=v_g_cq, v_g_ckv=v_g_ckv, v_w_uq=v_w_uq, v_w_ukv=v_w_ukv, v_w_ret_proj=v_w_ret_proj, v_w_mla_proj=v_w_mla_proj, v_w_out=v_w_out, v_g_final=v_g_final)
    weights = {n: given[n] for n in TWIN_WEIGHTS}
    shared = {n: given[n] for n in SHARED_INPUTS}
    per_example = {n: given[n] for n in ['x', 'c', 'positions']}
    grad_fn = _jax.value_and_grad(_loss, argnums=(0, 1))

    def one_microbatch(ex, loss_target):
        ex = dict(ex)
        diff = ex.pop(TWIN_DIFF_INPUT)
        return grad_fn(weights, diff, {**shared, **ex}, loss_target)

    if N_MICROBATCH == 1:
        loss, (grad_w, grad_x) = one_microbatch(per_example, given["loss_target"])
    else:
        def body(carry, xs):
            loss_sum, grad_sum = carry
            l_k, (gw_k, gx_k) = one_microbatch(xs[0], xs[1])
            with _jax.named_scope("update"):
                return (loss_sum + l_k, _jax.tree.map(_jnp.add, grad_sum, gw_k)), gx_k

        init = (_jnp.zeros((), _jnp.float32), _jax.tree.map(_jnp.zeros_like, weights))
        (loss, grad_w), grad_x = _jax.lax.scan(body, init, (per_example, given["loss_target"]))
    with _jax.named_scope("update"):
        delta_w, new_m, new_v = {}, {}, {}
        for n in TWIN_WEIGHTS:
            delta_w[n], new_m[n], new_v[n] = _adamw(weights[n], grad_w[n], given["m_" + n], given["v_" + n])
    return (loss, grad_x, *[grad_w[n] for n in TWIN_WEIGHTS], *[delta_w[n] for n in TWIN_WEIGHTS],
            *[new_m[n] for n in TWIN_WEIGHTS], *[new_v[n] for n in TWIN_WEIGHTS])
```

```python
import functools
import itertools

import jax
import jax.numpy as jnp
import numpy as np
from jax import lax
from jax.experimental import pallas as pl
from jax.experimental.pallas import tpu as pltpu

F32 = jnp.float32
BF16 = jnp.bfloat16

N_DEV = 8
CHUNK = 64
EPS = 1e-6
NEG_INF = -1e30
ROPE_BASE = 10000.0
LANES = 128

RET_HEADS = 8
RET_DK = 128
RET_DV = 256
MLA_HEADS = 16
MLA_NOPE = 128
MLA_ROPE = 64
MLA_DV = 128
MLA_QW = 256

ADAM_LR = 0.001
ADAM_B1 = 0.9
ADAM_B2 = 0.999
ADAM_EPS = 1e-08
ADAM_WD = 0.01
ADAM_STEP = 10

VMEM_LIMIT_BYTES = 56 * 1024 * 1024
MESH = pl.DeviceIdType.MESH
ANY = pl.BlockSpec(memory_space=pl.ANY)


def _cp(*sem):
    return pltpu.CompilerParams(dimension_semantics=sem if sem else None,
                                vmem_limit_bytes=VMEM_LIMIT_BYTES)


def _tile(n, cap):
    best = None
    t = LANES
    while t <= min(n, cap):
        if n % t == 0:
            best = t
        t += LANES
    return best if best is not None else n


def _rtile(n, cap):
    t = cap
    while t > 8 and n % t:
        t //= 2
    return t if n % t == 0 else n


def _sigmoid(x):
    return 1.0 / (1.0 + jnp.exp(-x))


def _dot(a, b):
    return lax.dot_general(a, b, (((1,), (0,)), ((), ())), preferred_element_type=F32)


def _dot_nt(a, b):
    return lax.dot_general(a, b, (((1,), (1,)), ((), ())), preferred_element_type=F32)


def _dot_tn(a, b):
    return lax.dot_general(a, b, (((0,), (0,)), ((), ())), preferred_element_type=F32)


def _roll(x, s):
    return pltpu.roll(x, s, 1)


class Layout:
    def __init__(self, d_model, q_rank, kv_rank):
        assert q_rank == kv_rank
        self.d = d_model
        self.rank = q_rank
        self.ret_w = 2 * RET_DK + 2 * RET_DV
        self.ret_qk = RET_HEADS * RET_DK
        self.ret_v = RET_HEADS * RET_DV
        self.mla_v = MLA_HEADS * MLA_DV
        widths = {"bg": 2 * d_model, "mg": self.mla_v, "ret": RET_HEADS * self.ret_w,
                  "cqkv": 2 * q_rank, "kr": LANES}
        blocks = {"bg": 2 * d_model, "mg": self.mla_v, "ret": self.ret_w,
                  "cqkv": 2 * q_rank, "kr": LANES}
        for order in itertools.permutations(widths):
            off, offs, ok = 0, {}, True
            for name in order:
                if off % blocks[name]:
                    ok = False
                    break
                offs[name] = off
                off += widths[name]
            if ok:
                break
        assert ok, "no aligned layout"
        self.order, self.off, self.width, self.total = order, offs, widths, off
        lo, o = {}, 0
        for name, w in (("rq", self.ret_qk), ("rk", self.ret_qk), ("rv", self.ret_v),
                        ("rg", self.ret_v), ("cq", q_rank), ("ckv", kv_rank), ("kr", MLA_ROPE),
                        ("mg", self.mla_v), ("bg", 2 * d_model)):
            lo[name] = (o, w)
            o += w
        self.logical, self.d_in = lo, o

    def pieces(self):
        lo = self.logical
        out = []
        for name in self.order:
            if name == "bg":
                out.append(lo["bg"])
            elif name == "mg":
                out.append(lo["mg"])
            elif name == "ret":
                for h in range(RET_HEADS):
                    out.append((lo["rq"][0] + h * RET_DK, RET_DK))
                    out.append((lo["rk"][0] + h * RET_DK, RET_DK))
                    out.append((lo["rv"][0] + h * RET_DV, RET_DV))
                    out.append((lo["rg"][0] + h * RET_DV, RET_DV))
            elif name == "cqkv":
                out.append((lo["cq"][0], 2 * self.rank))
            elif name == "kr":
                out.append(lo["kr"])
                out.append((None, LANES - MLA_ROPE))
        return out

    def to_physical(self, w):
        parts = []
        for start, width in self.pieces():
            if start is None:
                parts.append(jnp.zeros((w.shape[0], width), w.dtype))
            else:
                parts.append(w[:, start:start + width])
        return jnp.concatenate(parts, axis=1)

    def to_logical(self, w):
        got, off = [], 0
        for start, width in self.pieces():
            if start is not None:
                got.append((start, w[:, off:off + width]))
            off += width
        got.sort(key=lambda t: t[0])
        return jnp.concatenate([p for _, p in got], axis=1)


def _uq_to_physical(w):
    k = w.shape[0]
    w3 = w.reshape(k, MLA_HEADS, MLA_NOPE + MLA_ROPE)
    pad = jnp.zeros((k, MLA_HEADS, MLA_QW - MLA_NOPE - MLA_ROPE), w.dtype)
    return jnp.concatenate([w3, pad], axis=2).reshape(k, MLA_HEADS * MLA_QW)


def _uq_to_logical(w):
    k = w.shape[0]
    return w.reshape(k, MLA_HEADS, MLA_QW)[:, :, :MLA_NOPE + MLA_ROPE].reshape(k, -1)


def _ukv_to_physical(w):
    k = w.shape[0]
    w3 = w.reshape(k, MLA_HEADS, MLA_NOPE + MLA_DV)
    return jnp.concatenate([w3[:, :, :MLA_NOPE].reshape(k, -1), w3[:, :, MLA_NOPE:].reshape(k, -1)], axis=1)


def _ukv_to_logical(w):
    k = w.shape[0]
    kn = w[:, :MLA_HEADS * MLA_NOPE].reshape(k, MLA_HEADS, MLA_NOPE)
    v = w[:, MLA_HEADS * MLA_NOPE:].reshape(k, MLA_HEADS, MLA_DV)
    return jnp.concatenate([kn, v], axis=2).reshape(k, -1)


def _matmul(a, b, *, ta=False, tb=False, out_dtype=F32, name, tm_cap=1024, tn_cap=1024, tk_cap=512):
    m, k = (a.shape[1], a.shape[0]) if ta else a.shape
    n = b.shape[0] if tb else b.shape[1]
    assert k == (b.shape[1] if tb else b.shape[0])
    tm, tn, tk = _tile(m, tm_cap), _tile(n, tn_cap), _tile(k, tk_cap)
    nk = k // tk

    def body(a_ref, b_ref, o_ref, acc_ref):
        kk = pl.program_id(2)

        @pl.when(kk == 0)
        def _():
            acc_ref[...] = jnp.zeros_like(acc_ref)

        dims = (((0 if ta else 1,), (1 if tb else 0,)), ((), ()))
        acc_ref[...] += lax.dot_general(a_ref[...].astype(BF16), b_ref[...].astype(BF16), dims,
                                        preferred_element_type=F32)

        @pl.when(kk == nk - 1)
        def _():
            o_ref[...] = acc_ref[...].astype(o_ref.dtype)

    a_spec = pl.BlockSpec((tk, tm), lambda i, j, kk: (kk, i)) if ta else pl.BlockSpec((tm, tk), lambda i, j, kk: (i, kk))
    b_spec = pl.BlockSpec((tn, tk), lambda i, j, kk: (j, kk)) if tb else pl.BlockSpec((tk, tn), lambda i, j, kk: (kk, j))
    return pl.pallas_call(
        body, name=name, grid=(m // tm, n // tn, nk),
        in_specs=[a_spec, b_spec],
        out_specs=pl.BlockSpec((tm, tn), lambda i, j, kk: (i, j)),
        out_shape=jax.ShapeDtypeStruct((m, n), out_dtype),
        scratch_shapes=[pltpu.VMEM((tm, tn), F32)],
        compiler_params=_cp("parallel", "parallel", "arbitrary"),
    )(a, b)


def _row(tm, w, cb=0):
    return pl.BlockSpec((tm, w), lambda i, cb=cb: (i, cb))


def _vec(w, cb=0):
    return pl.BlockSpec((1, w), lambda i, cb=cb: (0, cb))


def _dproj_out(dproj, s, lay, tm, name):
    w = lay.width[name]
    cb = lay.off[name] // w
    spec = _row(tm, w, cb)
    shape = jax.ShapeDtypeStruct((s, lay.total), BF16)
    return spec, shape


def _norm_mod_fwd(x, g, scale, shift):
    s, d = x.shape
    tm = _rtile(s, 256)

    def body(x_ref, g_ref, sc_ref, sh_ref, h_ref):
        xv = x_ref[...]
        xh = xv * lax.rsqrt(jnp.mean(xv * xv, axis=-1, keepdims=True) + EPS)
        h_ref[...] = ((xh * g_ref[...]) * (1.0 + sc_ref[...]) + sh_ref[...]).astype(BF16)

    return pl.pallas_call(
        body, name="norm_mod_fwd", grid=(s // tm,),
        in_specs=[_row(tm, d), _vec(d), _vec(d), _vec(d)],
        out_specs=_row(tm, d), out_shape=jax.ShapeDtypeStruct((s, d), BF16),
        compiler_params=_cp("parallel"),
    )(x, g, scale, shift)


def _norm_mod_bwd(x, g, scale, dh, dres):
    s, d = x.shape
    tm = _rtile(s, 256)

    def body(x_ref, g_ref, sc_ref, dh_ref, dres_ref, dx_ref, dsh_ref, dsc_ref, dg_ref):
        @pl.when(pl.program_id(0) == 0)
        def _():
            dsh_ref[...] = jnp.zeros_like(dsh_ref)
            dsc_ref[...] = jnp.zeros_like(dsc_ref)
            dg_ref[...] = jnp.zeros_like(dg_ref)

        xv, gv, dhv = x_ref[...], g_ref[...], dh_ref[...]
        rstd = lax.rsqrt(jnp.mean(xv * xv, axis=-1, keepdims=True) + EPS)
        xh = xv * rstd
        dy = dhv * (1.0 + sc_ref[...])
        dxh = dy * gv
        dx_ref[...] = dres_ref[...] + rstd * (dxh - xh * jnp.mean(dxh * xh, axis=-1, keepdims=True))
        dsh_ref[...] += jnp.sum(dhv, axis=0, keepdims=True)
        dsc_ref[...] += jnp.sum(dhv * (xh * gv), axis=0, keepdims=True)
        dg_ref[...] += jnp.sum(dy * xh, axis=0, keepdims=True)

    vec = jax.ShapeDtypeStruct((1, d), F32)
    return pl.pallas_call(
        body, name="norm_mod_bwd", grid=(s // tm,),
        in_specs=[_row(tm, d), _vec(d), _vec(d), _row(tm, d), _row(tm, d)],
        out_specs=[_row(tm, d), _vec(d), _vec(d), _vec(d)],
        out_shape=[jax.ShapeDtypeStruct((s, d), F32), vec, vec, vec],
        compiler_params=_cp("arbitrary"),
    )(x, g, scale, dh, dres)


def _final_loss(x, g, target):
    s, d = x.shape
    tm = _rtile(s, 256)

    def body(x_ref, g_ref, t_ref, l_ref, dx_ref, dg_ref):
        @pl.when(pl.program_id(0) == 0)
        def _():
            l_ref[...] = jnp.zeros_like(l_ref)
            dg_ref[...] = jnp.zeros_like(dg_ref)

        xv, gv = x_ref[...], g_ref[...]
        rstd = lax.rsqrt(jnp.mean(xv * xv, axis=-1, keepdims=True) + EPS)
        xh = xv * rstd
        err = xh * gv - t_ref[...]
        row = jnp.mean(err * err, axis=-1, keepdims=True)
        l_ref[...] += 0.5 * jnp.sum(row, axis=0, keepdims=True)
        dy = err / d
        dxh = dy * gv
        dx_ref[...] = rstd * (dxh - xh * jnp.mean(dxh * xh, axis=-1, keepdims=True))
        dg_ref[...] += jnp.sum(dy * xh, axis=0, keepdims=True)

    return pl.pallas_call(
        body, name="final_loss", grid=(s // tm,),
        in_specs=[_row(tm, d), _vec(d), _row(tm, d)],
        out_specs=[_vec(LANES), _row(tm, d), _vec(d)],
        out_shape=[jax.ShapeDtypeStruct((1, LANES), F32), jax.ShapeDtypeStruct((s, d), F32),
                   jax.ShapeDtypeStruct((1, d), F32)],
        compiler_params=_cp("arbitrary"),
    )(x, g, target)


def _resid_fwd(x, out, gate):
    s, d = x.shape
    tm = _rtile(s, 256)

    def body(x_ref, o_ref, g_ref, y_ref):
        y_ref[...] = x_ref[...] + g_ref[...] * o_ref[...]

    return pl.pallas_call(
        body, name="resid_fwd", grid=(s // tm,),
        in_specs=[_row(tm, d), _row(tm, d), _vec(d)],
        out_specs=_row(tm, d), out_shape=jax.ShapeDtypeStruct((s, d), F32),
        compiler_params=_cp("parallel"),
    )(x, out, gate)


def _resid_bwd(dxn, out, gate):
    s, d = dxn.shape
    tm = _rtile(s, 256)

    def body(dx_ref, o_ref, g_ref, do_ref, dg_ref):
        @pl.when(pl.program_id(0) == 0)
        def _():
            dg_ref[...] = jnp.zeros_like(dg_ref)

        dxv = dx_ref[...]
        do_ref[...] = (dxv * g_ref[...]).astype(BF16)
        dg_ref[...] += jnp.sum(dxv * o_ref[...], axis=0, keepdims=True)

    return pl.pallas_call(
        body, name="resid_bwd", grid=(s // tm,),
        in_specs=[_row(tm, d), _row(tm, d), _vec(d)],
        out_specs=[_row(tm, d), _vec(d)],
        out_shape=[jax.ShapeDtypeStruct((s, d), BF16), jax.ShapeDtypeStruct((1, d), F32)],
        compiler_params=_cp("arbitrary"),
    )(dxn, out, gate)


def _merge_fwd(y_ret, y_mla, proj, lay):
    s, d = y_ret.shape
    tm = _rtile(s, 256)
    cb = lay.off["bg"] // (2 * d)

    def body(a_ref, b_ref, bg_ref, m_ref):
        sg = _sigmoid(bg_ref[...])
        m_ref[...] = (sg[:, :d] * a_ref[...] + sg[:, d:] * b_ref[...]).astype(BF16)

    return pl.pallas_call(
        body, name="merge_fwd", grid=(s // tm,),
        in_specs=[_row(tm, d), _row(tm, d), _row(tm, 2 * d, cb)],
        out_specs=_row(tm, d), out_shape=jax.ShapeDtypeStruct((s, d), BF16),
        compiler_params=_cp("parallel"),
    )(y_ret, y_mla, proj)


def _merge_bwd(dm, y_ret, y_mla, proj, lay):
    s, d = dm.shape
    tm = _rtile(s, 256)
    cb = lay.off["bg"] // (2 * d)
    dp_spec, dp_shape = _dproj_out(None, s, lay, tm, "bg")

    def body(dm_ref, a_ref, b_ref, bg_ref, da_ref, db_ref, dp_ref):
        sg = _sigmoid(bg_ref[...])
        dmv = dm_ref[...]
        ga, gb = sg[:, :d], sg[:, d:]
        da_ref[...] = (dmv * ga).astype(BF16)
        db_ref[...] = (dmv * gb).astype(BF16)
        dp_ref[:, :d] = (dmv * a_ref[...] * ga * (1.0 - ga)).astype(BF16)
        dp_ref[:, d:] = (dmv * b_ref[...] * gb * (1.0 - gb)).astype(BF16)

    act = jax.ShapeDtypeStruct((s, d), BF16)
    return pl.pallas_call(
        body, name="merge_bwd", grid=(s // tm,),
        in_specs=[_row(tm, d), _row(tm, d), _row(tm, d), _row(tm, 2 * d, cb)],
        out_specs=[_row(tm, d), _row(tm, d), dp_spec],
        out_shape=[act, act, dp_shape],
        compiler_params=_cp("parallel"),
    )(dm, y_ret, y_mla, proj)


def _mla_prep(proj, g_cq, g_ckv, lay):
    s = proj.shape[0]
    r = lay.rank
    tm = _rtile(s, 512)
    cb = lay.off["cqkv"] // (2 * r)

    def body(p_ref, gq_ref, gk_ref, q_ref, k_ref):
        pv = p_ref[...]
        for lo, g_ref, o_ref in ((0, gq_ref, q_ref), (r, gk_ref, k_ref)):
            xv = pv[:, lo:lo + r]
            xh = xv * lax.rsqrt(jnp.mean(xv * xv, axis=-1, keepdims=True) + EPS)
            o_ref[...] = (xh * g_ref[...]).astype(BF16)

    act = jax.ShapeDtypeStruct((s, r), BF16)
    return pl.pallas_call(
        body, name="mla_prep", grid=(s // tm,),
        in_specs=[_row(tm, 2 * r, cb), _vec(r), _vec(r)],
        out_specs=[_row(tm, r), _row(tm, r)], out_shape=[act, act],
        compiler_params=_cp("parallel"),
    )(proj, g_cq, g_ckv)


def _mla_prep_bwd(proj, dqn, dkn, g_cq, g_ckv, dproj, lay):
    s = proj.shape[0]
    r = lay.rank
    tm = _rtile(s, 512)
    cb = lay.off["cqkv"] // (2 * r)
    dp_spec, dp_shape = _dproj_out(dproj, s, lay, tm, "cqkv")

    def body(p_ref, dq_ref, dk_ref, gq_ref, gk_ref, _, dp_ref, dgq_ref, dgk_ref):
        @pl.when(pl.program_id(0) == 0)
        def _():
            dgq_ref[...] = jnp.zeros_like(dgq_ref)
            dgk_ref[...] = jnp.zeros_like(dgk_ref)

        pv = p_ref[...]
        for lo, g_ref, d_ref, dg_ref in ((0, gq_ref, dq_ref, dgq_ref), (r, gk_ref, dk_ref, dgk_ref)):
            xv = pv[:, lo:lo + r]
            rstd = lax.rsqrt(jnp.mean(xv * xv, axis=-1, keepdims=True) + EPS)
            xh = xv * rstd
            dy = d_ref[...]
            dxh = dy * g_ref[...]
            dp_ref[:, lo:lo + r] = (rstd * (dxh - xh * jnp.mean(dxh * xh, axis=-1, keepdims=True))).astype(BF16)
            dg_ref[...] += jnp.sum(dy * xh, axis=0, keepdims=True)

    vec = jax.ShapeDtypeStruct((1, r), F32)
    return pl.pallas_call(
        body, name="mla_prep_bwd", grid=(s // tm,),
        in_specs=[_row(tm, 2 * r, cb), _row(tm, r), _row(tm, r), _vec(r), _vec(r), ANY],
        out_specs=[dp_spec, _vec(r), _vec(r)], out_shape=[dp_shape, vec, vec],
        input_output_aliases={5: 0},
        compiler_params=_cp("arbitrary"),
    )(proj, dqn, dkn, g_cq, g_ckv, dproj)


def _rope_tile(t, a, b, c):
    return t * a + _roll(t, 96) * b + _roll(t, 32) * c


def _rope_tile_bwd(dy, a, b, c):
    return dy * a + _roll(dy * b, 32) + _roll(dy * c, 96)


def _qk_prep(qp, kvp, proj, ta, tb, tc, lay):
    s = qp.shape[0]
    hq = MLA_HEADS * MLA_QW
    hv = MLA_HEADS * MLA_DV
    tm = _rtile(s, 256)
    kr_cb = lay.off["kr"] // LANES

    def body(q_ref, kv_ref, kr_ref, a_ref, b_ref, c_ref, qc_ref, kc_ref, v_ref):
        a, b, c = a_ref[...], b_ref[...], c_ref[...]
        krot = _rope_tile(kr_ref[...], a, b, c).astype(BF16)
        for h in range(MLA_HEADS):
            q0 = h * MLA_QW
            qc_ref[:, q0:q0 + MLA_NOPE] = q_ref[:, q0:q0 + MLA_NOPE].astype(BF16)
            qc_ref[:, q0 + MLA_NOPE:q0 + MLA_QW] = _rope_tile(q_ref[:, q0 + MLA_NOPE:q0 + MLA_QW], a, b, c).astype(BF16)
            kc_ref[:, q0:q0 + MLA_NOPE] = kv_ref[:, h * MLA_NOPE:(h + 1) * MLA_NOPE].astype(BF16)
            kc_ref[:, q0 + MLA_NOPE:q0 + MLA_QW] = krot
        v_ref[...] = kv_ref[:, MLA_HEADS * MLA_NOPE:].astype(BF16)

    return pl.pallas_call(
        body, name="qk_prep", grid=(s // tm,),
        in_specs=[_row(tm, hq), _row(tm, hq), _row(tm, LANES, kr_cb), _row(tm, LANES), _row(tm, LANES), _row(tm, LANES)],
        out_specs=[_row(tm, hq), _row(tm, hq), _row(tm, hv)],
        out_shape=[jax.ShapeDtypeStruct((s, hq), BF16), jax.ShapeDtypeStruct((s, hq), BF16),
                   jax.ShapeDtypeStruct((s, hv), BF16)],
        compiler_params=_cp("parallel"),
    )(qp, kvp, proj, ta, tb, tc)


def _kv_bwd_prep(dk_cat, dv, ta, tb, tc, dproj, lay):
    s = dk_cat.shape[0]
    hq = MLA_HEADS * MLA_QW
    hv = MLA_HEADS * MLA_DV
    tm = _rtile(s, 256)
    dp_spec, dp_shape = _dproj_out(dproj, s, lay, tm, "kr")

    def body(dk_ref, dv_ref, a_ref, b_ref, c_ref, _, dkv_ref, dp_ref):
        acc = jnp.zeros((tm, LANES), F32)
        for h in range(MLA_HEADS):
            q0 = h * MLA_QW
            dkv_ref[:, h * MLA_NOPE:(h + 1) * MLA_NOPE] = dk_ref[:, q0:q0 + MLA_NOPE].astype(BF16)
            acc = acc + dk_ref[:, q0 + MLA_NOPE:q0 + MLA_QW]
        dkv_ref[:, MLA_HEADS * MLA_NOPE:] = dv_ref[...].astype(BF16)
        dp_ref[...] = _rope_tile_bwd(acc, a_ref[...], b_ref[...], c_ref[...]).astype(BF16)

    return pl.pallas_call(
        body, name="kv_bwd_prep", grid=(s // tm,),
        in_specs=[_row(tm, hq), _row(tm, hv), _row(tm, LANES), _row(tm, LANES), _row(tm, LANES), ANY],
        out_specs=[_row(tm, hq), dp_spec],
        out_shape=[jax.ShapeDtypeStruct((s, hq), BF16), dp_shape],
        input_output_aliases={5: 1},
        compiler_params=_cp("parallel"),
    )(dk_cat, dv, ta, tb, tc, dproj)


def _mla_gate_bwd(du, o, proj, dproj, lay):
    s, vw = du.shape
    tm = _rtile(s, 256)
    cb = lay.off["mg"] // vw
    dp_spec, dp_shape = _dproj_out(dproj, s, lay, tm, "mg")

    def body(du_ref, o_ref, g_ref, _, do_ref, dp_ref):
        gv, duv = g_ref[...], du_ref[...]
        sg = _sigmoid(gv)
        do_ref[...] = (duv * (gv * sg)).astype(BF16)
        dp_ref[...] = (duv * o_ref[...] * (sg + gv * sg * (1.0 - sg))).astype(BF16)

    return pl.pallas_call(
        body, name="mla_gate_bwd", grid=(s // tm,),
        in_specs=[_row(tm, vw), _row(tm, vw), _row(tm, vw, cb), ANY],
        out_specs=[_row(tm, vw), dp_spec],
        out_shape=[jax.ShapeDtypeStruct((s, vw), BF16), dp_shape],
        input_output_aliases={3: 1},
        compiler_params=_cp("parallel"),
    )(du, o, proj, dproj)


def _ret_tables(lg):
    ri = lax.broadcasted_iota(jnp.int32, (CHUNK, CHUNK), 0).astype(F32)
    ci = lax.broadcasted_iota(jnp.int32, (CHUNK, CHUNK), 1).astype(F32)
    col = lax.broadcasted_iota(jnp.int32, (CHUNK, 1), 0).astype(F32)
    dmat = jnp.exp(jnp.abs(ri - ci) * lg)
    xi = jnp.exp((col + 1.0) * lg)
    zeta = jnp.exp((CHUNK - 1.0 - col) * lg)
    decay = jnp.exp(jnp.full((1, 1), CHUNK, F32) * lg)
    return dmat, xi, zeta, decay


def _ret_qkvg(blk, cs, sn):
    dv = RET_DV
    q = blk[:, :RET_DK]
    k = blk[:, RET_DK:2 * RET_DK]
    q = q * cs + _roll(q, RET_DK // 2) * sn
    k = (k * cs + _roll(k, RET_DK // 2) * sn) * (RET_DK ** -0.5)
    return q, k, blk[:, 2 * RET_DK:2 * RET_DK + dv], blk[:, 2 * RET_DK + dv:]


def _group_norm(o):
    mu = jnp.mean(o, axis=-1, keepdims=True)
    oc = o - mu
    rstd = lax.rsqrt(jnp.mean(oc * oc, axis=-1, keepdims=True) + EPS)
    return oc * rstd, rstd


def _ret_fwd(proj, lg, cosr, sinr, lay):
    s = proj.shape[0]
    dv, w = RET_DV, lay.ret_w
    tb = _rtile(s, 512)
    nb, nch = s // tb, tb // CHUNK
    cb0 = lay.off["ret"] // w

    def body(lg_ref, p_ref, cos_ref, sin_ref, o_ref, u_ref, st_ref, state):
        @pl.when(pl.program_id(1) == 0)
        def _():
            state[...] = jnp.zeros_like(state)

        dmat, xi, zeta, decay = _ret_tables(lg_ref[pl.program_id(0)])
        for c in range(nch):
            rows = slice(c * CHUNK, (c + 1) * CHUNK)
            q, k, v, g = _ret_qkvg(p_ref[rows, :], cos_ref[rows, :], sin_ref[rows, :])
            qb, kb, vb = q.astype(BF16), k.astype(BF16), v.astype(BF16)
            sc = _dot_nt(qb, kb) * dmat
            st = state[...]
            o = _dot(sc.astype(BF16), vb) + _dot((q * xi).astype(BF16), st.astype(BF16))
            st_ref[c] = st.astype(BF16)
            state[...] = st * decay + _dot_tn((k * zeta).astype(BF16), vb)
            o_ref[rows, :] = o
            n, _ = _group_norm(o)
            u_ref[rows, :] = (n * (g * _sigmoid(g))).astype(BF16)

    return pl.pallas_call(
        body, name="ret_fwd", grid=(RET_HEADS, nb),
        in_specs=[pl.BlockSpec(memory_space=pltpu.SMEM),
                  pl.BlockSpec((tb, w), lambda h, b: (b, cb0 + h)),
                  pl.BlockSpec((tb, RET_DK), lambda h, b: (b, 0)),
                  pl.BlockSpec((tb, RET_DK), lambda h, b: (b, 0))],
        out_specs=[pl.BlockSpec((tb, dv), lambda h, b: (b, h)),
                   pl.BlockSpec((tb, dv), lambda h, b: (b, h)),
                   pl.BlockSpec((None, nch, RET_DK, dv), lambda h, b: (h, b, 0, 0))],
        out_shape=[jax.ShapeDtypeStruct((s, RET_HEADS * dv), F32),
                   jax.ShapeDtypeStruct((s, RET_HEADS * dv), BF16),
                   jax.ShapeDtypeStruct((RET_HEADS, s // CHUNK, RET_DK, dv), BF16)],
        scratch_shapes=[pltpu.VMEM((RET_DK, dv), F32)],
        compiler_params=_cp("parallel", "arbitrary"),
    )(lg, proj, cosr, sinr)


def _ret_bwd(proj, lg, cosr, sinr, o, du, states, dproj, lay):
    s = proj.shape[0]
    dv, w = RET_DV, lay.ret_w
    tb = _rtile(s, 512)
    nb, nch = s // tb, tb // CHUNK
    cb0 = lay.off["ret"] // w

    def body(lg_ref, p_ref, cos_ref, sin_ref, o_ref, du_ref, st_ref, _, dp_ref, dstate):
        @pl.when(pl.program_id(1) == 0)
        def _():
            dstate[...] = jnp.zeros_like(dstate)

        dmat, xi, zeta, decay = _ret_tables(lg_ref[pl.program_id(0)])
        for c in reversed(range(nch)):
            rows = slice(c * CHUNK, (c + 1) * CHUNK)
            cs, sn = cos_ref[rows, :], sin_ref[rows, :]
            q, k, v, g = _ret_qkvg(p_ref[rows, :], cs, sn)
            qb, kb, vb = q.astype(BF16), k.astype(BF16), v.astype(BF16)
            n, rstd = _group_norm(o_ref[rows, :])
            sg = _sigmoid(g)
            duv = du_ref[rows, :]
            dn = duv * (g * sg)
            dg = duv * n * (sg + g * sg * (1.0 - sg))
            do = rstd * (dn - jnp.mean(dn, axis=-1, keepdims=True) - n * jnp.mean(dn * n, axis=-1, keepdims=True))
            dob = do.astype(BF16)
            rb = st_ref[c]
            drb = dstate[...].astype(BF16)
            sc = (_dot_nt(qb, kb) * dmat).astype(BF16)
            dsc = (_dot_nt(dob, vb) * dmat).astype(BF16)
            qx = (q * xi).astype(BF16)
            kz = (k * zeta).astype(BF16)
            dq = _dot(dsc, kb) + _dot_nt(dob, rb) * xi
            dk = (_dot_tn(dsc, qb) + _dot_nt(vb, drb) * zeta) * (RET_DK ** -0.5)
            dvv = _dot_tn(sc, dob) + _dot(kz, drb)
            dstate[...] = dstate[...] * decay + _dot_tn(qx, dob)
            dp_ref[rows, :RET_DK] = (dq * cs + _roll(dq * sn, RET_DK // 2)).astype(BF16)
            dp_ref[rows, RET_DK:2 * RET_DK] = (dk * cs + _roll(dk * sn, RET_DK // 2)).astype(BF16)
            dp_ref[rows, 2 * RET_DK:2 * RET_DK + dv] = dvv.astype(BF16)
            dp_ref[rows, 2 * RET_DK + dv:] = dg.astype(BF16)

    rev = lambda h, b: (nb - 1 - b, h)
    return pl.pallas_call(
        body, name="ret_bwd", grid=(RET_HEADS, nb),
        in_specs=[pl.BlockSpec(memory_space=pltpu.SMEM),
                  pl.BlockSpec((tb, w), lambda h, b: (nb - 1 - b, cb0 + h)),
                  pl.BlockSpec((tb, RET_DK), lambda h, b: (nb - 1 - b, 0)),
                  pl.BlockSpec((tb, RET_DK), lambda h, b: (nb - 1 - b, 0)),
                  pl.BlockSpec((tb, dv), rev),
                  pl.BlockSpec((tb, dv), rev),
                  pl.BlockSpec((None, nch, RET_DK, dv), lambda h, b: (h, nb - 1 - b, 0, 0)),
                  ANY],
        out_specs=pl.BlockSpec((tb, w), lambda h, b: (nb - 1 - b, cb0 + h)),
        out_shape=jax.ShapeDtypeStruct((s, lay.total), BF16),
        input_output_aliases={7: 0},
        scratch_shapes=[pltpu.VMEM((RET_DK, dv), F32)],
        compiler_params=_cp("parallel", "arbitrary"),
    )(lg, proj, cosr, sinr, o, du, states, dproj)


def _attn_mask(bq):
    qi = lax.broadcasted_iota(jnp.int32, (bq, bq), 0) // CHUNK
    ki = lax.broadcasted_iota(jnp.int32, (bq, bq), 1) // CHUNK
    return ki <= qi


def _attn_fwd(q_cat, k_cat, v, proj, lay):
    s = q_cat.shape[0]
    bq = _rtile(s, 512)
    nq = s // bq
    scale = (MLA_NOPE + MLA_ROPE) ** -0.5
    mg_cb = lay.off["mg"] // MLA_DV

    def body(q_ref, k_ref, v_ref, g_ref, o_ref, u_ref, lse_ref):
        i = pl.program_id(1)
        q = q_ref[...]

        def step(j, carry, masked):
            m, l, acc = carry
            r0 = pl.multiple_of(j * bq, bq)
            sc = _dot_nt(q, k_ref[pl.ds(r0, bq), :]) * scale
            if masked:
                sc = jnp.where(_attn_mask(bq), sc, NEG_INF)
            mn = jnp.maximum(m, jnp.max(sc, axis=-1, keepdims=True))
            p = jnp.exp(sc - mn)
            alpha = jnp.exp(m - mn)
            l = alpha * l + jnp.sum(p, axis=-1, keepdims=True)
            acc = alpha * acc + _dot(p.astype(BF16), v_ref[pl.ds(r0, bq), :])
            return mn, l, acc

        init = (jnp.full((bq, 1), NEG_INF, F32), jnp.zeros((bq, 1), F32), jnp.zeros((bq, MLA_DV), F32))
        carry = lax.fori_loop(0, i, lambda j, c: step(j, c, False), init)
        m, l, acc = step(i, carry, True)
        o = acc / l
        gv = g_ref[...]
        o_ref[...] = o
        u_ref[...] = (o * (gv * _sigmoid(gv))).astype(BF16)
        lse_ref[...] = m + jnp.log(l)

    return pl.pallas_call(
        body, name="attn_fwd", grid=(MLA_HEADS, nq),
        in_specs=[pl.BlockSpec((bq, MLA_QW), lambda h, i: (i, h)),
                  pl.BlockSpec((s, MLA_QW), lambda h, i: (0, h)),
                  pl.BlockSpec((s, MLA_DV), lambda h, i: (0, h)),
                  pl.BlockSpec((bq, MLA_DV), lambda h, i: (i, mg_cb + h))],
        out_specs=[pl.BlockSpec((bq, MLA_DV), lambda h, i: (i, h)),
                   pl.BlockSpec((bq, MLA_DV), lambda h, i: (i, h)),
                   pl.BlockSpec((None, bq, 1), lambda h, i: (h, i, 0))],
        out_shape=[jax.ShapeDtypeStruct((s, MLA_HEADS * MLA_DV), F32),
                   jax.ShapeDtypeStruct((s, MLA_HEADS * MLA_DV), BF16),
                   jax.ShapeDtypeStruct((MLA_HEADS, s, 1), F32)],
        compiler_params=_cp("parallel", "parallel"),
    )(q_cat, k_cat, v, proj)


def _attn_dq(q_cat, k_cat, v, do, o, lse, ta, tb, tc):
    s = q_cat.shape[0]
    bq = _rtile(s, 512)
    nq = s // bq
    scale = (MLA_NOPE + MLA_ROPE) ** -0.5

    def body(q_ref, k_ref, v_ref, do_ref, o_ref, lse_ref, a_ref, b_ref, c_ref, dq_ref, dl_ref):
        i = pl.program_id(1)
        q, dob, lse = q_ref[...], do_ref[...], lse_ref[...]
        delta = jnp.sum(dob.astype(F32) * o_ref[...], axis=-1, keepdims=True)

        def step(j, dq, masked):
            r0 = pl.multiple_of(j * bq, bq)
            kb, vb = k_ref[pl.ds(r0, bq), :], v_ref[pl.ds(r0, bq), :]
            sc = _dot_nt(q, kb) * scale
            if masked:
                sc = jnp.where(_attn_mask(bq), sc, NEG_INF)
            p = jnp.exp(sc - lse)
            ds = p * (_dot_nt(dob, vb) - delta) * scale
            return dq + _dot(ds.astype(BF16), kb)

        dq = lax.fori_loop(0, i, lambda j, c: step(j, c, False), jnp.zeros((bq, MLA_QW), F32))
        dq = step(i, dq, True)
        dq_ref[:, :MLA_NOPE] = dq[:, :MLA_NOPE].astype(BF16)
        dq_ref[:, MLA_NOPE:] = _rope_tile_bwd(dq[:, MLA_NOPE:], a_ref[...], b_ref[...], c_ref[...]).astype(BF16)
        dl_ref[...] = delta

    tab = pl.BlockSpec((bq, LANES), lambda h, i: (i, 0))
    return pl.pallas_call(
        body, name="attn_dq", grid=(MLA_HEADS, nq),
        in_specs=[pl.BlockSpec((bq, MLA_QW), lambda h, i: (i, h)),
                  pl.BlockSpec((s, MLA_QW), lambda h, i: (0, h)),
                  pl.BlockSpec((s, MLA_DV), lambda h, i: (0, h)),
                  pl.BlockSpec((bq, MLA_DV), lambda h, i: (i, h)),
                  pl.BlockSpec((bq, MLA_DV), lambda h, i: (i, h)),
                  pl.BlockSpec((None, bq, 1), lambda h, i: (h, i, 0)),
                  tab, tab, tab],
        out_specs=[pl.BlockSpec((bq, MLA_QW), lambda h, i: (i, h)),
                   pl.BlockSpec((None, bq, 1), lambda h, i: (h, i, 0))],
        out_shape=[jax.ShapeDtypeStruct((s, MLA_HEADS * MLA_QW), BF16),
                   jax.ShapeDtypeStruct((MLA_HEADS, s, 1), F32)],
        compiler_params=_cp("parallel", "parallel"),
    )(q_cat, k_cat, v, do, o, lse, ta, tb, tc)


def _attn_dkv(q_cat, k_cat, v, do, lse, delta):
    s = q_cat.shape[0]
    bk = _rtile(s, 512)
    nk = s // bk
    scale = (MLA_NOPE + MLA_ROPE) ** -0.5

    def body(q_ref, k_ref, v_ref, do_ref, lse_ref, dl_ref, dk_ref, dv_ref):
        j = pl.program_id(1)
        kb, vb = k_ref[...], v_ref[...]

        def step(i, carry, masked):
            dk, dvv = carry
            r0 = pl.multiple_of(i * bk, bk)
            q, dob = q_ref[pl.ds(r0, bk), :], do_ref[pl.ds(r0, bk), :]
            sc = _dot_nt(q, kb) * scale
            if masked:
                sc = jnp.where(_attn_mask(bk), sc, NEG_INF)
            p = jnp.exp(sc - lse_ref[pl.ds(r0, bk), :])
            dvv = dvv + _dot_tn(p.astype(BF16), dob)
            ds = p * (_dot_nt(dob, vb) - dl_ref[pl.ds(r0, bk), :]) * scale
            return dk + _dot_tn(ds.astype(BF16), q), dvv

        carry = step(j, (jnp.zeros((bk, MLA_QW), F32), jnp.zeros((bk, MLA_DV), F32)), True)
        dk, dvv = lax.fori_loop(j + 1, nk, lambda i, c: step(i, c, False), carry)
        dk_ref[...] = dk
        dv_ref[...] = dvv

    return pl.pallas_call(
        body, name="attn_dkv", grid=(MLA_HEADS, nk),
        in_specs=[pl.BlockSpec((s, MLA_QW), lambda h, j: (0, h)),
                  pl.BlockSpec((bk, MLA_QW), lambda h, j: (j, h)),
                  pl.BlockSpec((bk, MLA_DV), lambda h, j: (j, h)),
                  pl.BlockSpec((s, MLA_DV), lambda h, j: (0, h)),
                  pl.BlockSpec((None, s, 1), lambda h, j: (h, 0, 0)),
                  pl.BlockSpec((None, s, 1), lambda h, j: (h, 0, 0))],
        out_specs=[pl.BlockSpec((bk, MLA_QW), lambda h, j: (j, h)),
                   pl.BlockSpec((bk, MLA_DV), lambda h, j: (j, h))],
        out_shape=[jax.ShapeDtypeStruct((s, MLA_HEADS * MLA_QW), F32),
                   jax.ShapeDtypeStruct((s, MLA_HEADS * MLA_DV), F32)],
        compiler_params=_cp("parallel", "parallel"),
    )(q_cat, k_cat, v, do, lse, delta)


def _place():
    return lax.axis_index("x"), lax.axis_index("y"), lax.axis_index("c")


def _slot(px, py, pc):
    return 4 * px + 2 * py + pc


def _all_gather(shards, layer, name, vmem=False):
    n = len(shards)

    def body(*refs):
        srcs, outs = refs[:n], refs[n:2 * n]
        send_sems, recv_sems, local_sems = refs[2 * n:]
        x, y, c = _place()
        me, sibling = (x, y, c), (x, y, 1 - c)
        chips = [(1 - x, y), (x, 1 - y), (1 - x, 1 - y)]
        firsts, passes, locals_ = [], [], []

        def copy(a, k, block, to, src=None):
            dst = outs[a].at[_slot(*block)]
            return pltpu.make_async_remote_copy(
                src_ref=dst if src is None else src, dst_ref=dst,
                send_sem=send_sems.at[7 * a + k], recv_sem=recv_sems.at[7 * a + k],
                device_id=to, device_id_type=MESH)

        for a in range(n):
            src = srcs[a] if layer is None else srcs[a].at[layer]
            mine = pltpu.make_async_copy(src, outs[a].at[_slot(*me)], local_sems.at[a])
            mine.start()
            locals_.append(mine)
            first = [copy(a, 0, me, sibling, src=src)]
            first += [copy(a, 1 + j, me, (*chip, c), src=src) for j, chip in enumerate(chips)]
            for cp in first:
                cp.start()
            firsts += first
        for a in range(n):
            for j, chip in enumerate(chips):
                copy(a, 1 + j, (*chip, c), me).wait_recv()
                fwd = copy(a, 4 + j, (*chip, c), sibling)
                fwd.start()
                passes.append(fwd)
        for a in range(n):
            copy(a, 0, sibling, me).wait_recv()
            for j, chip in enumerate(chips):
                copy(a, 4 + j, (*chip, 1 - c), me).wait_recv()
        for cp in firsts + passes:
            cp.wait_send()
        for mine in locals_:
            mine.wait()

    space = pl.BlockSpec(memory_space=pltpu.VMEM) if vmem else ANY
    out_shape = [jax.ShapeDtypeStruct((N_DEV,) + (a.shape if layer is None else a.shape[1:]), a.dtype) for a in shards]
    return pl.pallas_call(
        body, name=name,
        in_specs=[space] * n, out_specs=[space] * n, out_shape=out_shape,
        scratch_shapes=[pltpu.SemaphoreType.DMA((7 * n,)), pltpu.SemaphoreType.DMA((7 * n,)),
                        pltpu.SemaphoreType.DMA((n,))],
        compiler_params=pltpu.CompilerParams(has_side_effects=True),
    )(*shards)


def _exchange_grads(parts, name):
    n = len(parts)

    def body(*refs):
        srcs, outs = refs[:n], refs[n:2 * n]
        send_sems, recv_sems, local_sems = refs[2 * n:]
        x, y, c = _place()
        me = _slot(x, y, c)
        copies = []
        for a in range(n):
            mine = pltpu.make_async_copy(srcs[a].at[me], outs[a].at[me], local_sems.at[a])
            mine.start()
            copies.append(mine)
            for r in range(1, N_DEV):
                px = 1 - x if r & 4 else x
                py = 1 - y if r & 2 else y
                pc = 1 - c if r & 1 else c
                cp = pltpu.make_async_remote_copy(
                    src_ref=srcs[a].at[_slot(px, py, pc)], dst_ref=outs[a].at[me],
                    send_sem=send_sems.at[7 * a + r - 1], recv_sem=recv_sems.at[7 * a + r - 1],
                    device_id=(px, py, pc), device_id_type=MESH)
                cp.start()
                copies.append(cp)
        for cp in copies:
            cp.wait()

    return pl.pallas_call(
        body, name=name,
        in_specs=[ANY] * n, out_specs=[ANY] * n,
        out_shape=[jax.ShapeDtypeStruct(a.shape, a.dtype) for a in parts],
        scratch_shapes=[pltpu.SemaphoreType.DMA((7 * n,)), pltpu.SemaphoreType.DMA((7 * n,)),
                        pltpu.SemaphoreType.DMA((n,))],
        compiler_params=pltpu.CompilerParams(has_side_effects=True),
    )(*parts)


def _adam_math(g, w, m, v):
    m = ADAM_B1 * m + (1.0 - ADAM_B1) * g
    v = ADAM_B2 * v + (1.0 - ADAM_B2) * (g * g)
    m_hat = m / (1.0 - ADAM_B1 ** ADAM_STEP)
    v_hat = v / (1.0 - ADAM_B2 ** ADAM_STEP)
    delta = -ADAM_LR * (m_hat / (jnp.sqrt(v_hat) + ADAM_EPS) + ADAM_WD * w)
    return delta, m, v


def _adam_sharded(recvs, w, m, v, name):
    nl, r, c = w.shape
    tr = _rtile(r, 128)
    nr = r // tr

    def body(*refs):
        g_refs = refs[:nl]
        w_ref, m_ref, v_ref, go_ref, d_ref, mo_ref, vo_ref = refs[nl:]
        layer = pl.program_id(0)
        for l in range(nl):
            @pl.when(layer == l)
            def _(l=l):
                g = g_refs[l][0].astype(F32)
                for i in range(1, N_DEV):
                    g = g + g_refs[l][i].astype(F32)
                delta, mn, vn = _adam_math(g, w_ref[...], m_ref[...], v_ref[...])
                go_ref[...] = g
                d_ref[...] = delta
                mo_ref[...] = mn
                vo_ref[...] = vn

    def recv_spec(l):
        def index(layer, i):
            return 0, jnp.where(layer == l, i, jnp.where(layer < l, 0, nr - 1)), 0
        return pl.BlockSpec((N_DEV, tr, c), index)

    blk = pl.BlockSpec((None, tr, c), lambda layer, i: (layer, i, 0))
    out = jax.ShapeDtypeStruct(w.shape, F32)
    return pl.pallas_call(
        body, name=name, grid=(nl, nr),
        in_specs=[recv_spec(l) for l in range(nl)] + [blk, blk, blk],
        out_specs=[blk] * 4, out_shape=[out] * 4,
        compiler_params=_cp("arbitrary", "arbitrary"),
    )(*recvs, w, m, v)


def _adam_mod(c_all_t, dmod, w, m, v):
    nl, d, c = w.shape
    tr = _rtile(d, 128)

    def body(ct_ref, dm_ref, w_ref, m_ref, v_ref, go_ref, d_ref, mo_ref, vo_ref):
        ct = ct_ref[...].astype(BF16).astype(F32)
        dm = dm_ref[...].astype(BF16).astype(F32)
        g = ct[:, 0:1] * dm[0:1, :]
        for b in range(1, N_DEV):
            g = g + ct[:, b:b + 1] * dm[b:b + 1, :]
        delta, mn, vn = _adam_math(g, w_ref[...], m_ref[...], v_ref[...])
        go_ref[...] = g
        d_ref[...] = delta
        mo_ref[...] = mn
        vo_ref[...] = vn

    blk = pl.BlockSpec((None, tr, c), lambda layer, i: (layer, i, 0))
    out = jax.ShapeDtypeStruct(w.shape, F32)
    return pl.pallas_call(
        body, name="adam_mod", grid=(nl, d // tr),
        in_specs=[pl.BlockSpec((tr, N_DEV), lambda layer, i: (i, 0)),
                  pl.BlockSpec((None, N_DEV, c), lambda layer, i: (layer, 0, 0)), blk, blk, blk],
        out_specs=[blk] * 4, out_shape=[out] * 4,
        compiler_params=_cp("parallel", "parallel"),
    )(c_all_t, dmod, w, m, v)


def _adam_small(g, w, m, v, name):
    def body(g_ref, w_ref, m_ref, v_ref, d_ref, mo_ref, vo_ref):
        delta, mn, vn = _adam_math(g_ref[...], w_ref[...], m_ref[...], v_ref[...])
        d_ref[...] = delta
        mo_ref[...] = mn
        vo_ref[...] = vn

    out = jax.ShapeDtypeStruct(w.shape, F32)
    return pl.pallas_call(body, name=name, out_shape=[out] * 3)(g, w, m, v)


def _sum_devices(parts):
    def body(p_ref, o_ref):
        acc = p_ref[0]
        for i in range(1, N_DEV):
            acc = acc + p_ref[i]
        o_ref[...] = acc

    return pl.pallas_call(body, name="sum_devices",
                          out_shape=jax.ShapeDtypeStruct(parts.shape[1:], F32))(parts)


def _rope_tables(positions):
    pos = positions.astype(F32)[:, None]

    def cs(dim):
        inv = 1.0 / (ROPE_BASE ** (jnp.arange(0, dim, 2, dtype=F32) / dim))
        ang = pos * inv
        return jnp.cos(ang), jnp.sin(ang)

    cr, sr = cs(RET_DK)
    cm, sm = cs(MLA_ROPE)
    z = jnp.zeros_like(cm)
    pad = jnp.zeros((pos.shape[0], LANES - MLA_ROPE), F32)
    cosr = jnp.concatenate([cr, cr], axis=1)
    sinr = jnp.concatenate([-sr, sr], axis=1)
    ta = jnp.concatenate([cm, cm, pad], axis=1)
    tb = jnp.concatenate([-sm, z, pad], axis=1)
    tc = jnp.concatenate([z, sm, pad], axis=1)
    return cosr, sinr, ta, tb, tc


def _layer_fwd(x, mod, g_norm, g_cq, g_ckv, wts, tabs, lg, lay):
    d = x.shape[1]
    cosr, sinr, ta, tb, tc = tabs
    shift, scale, gate = mod[:, :d], mod[:, d:2 * d], mod[:, 2 * d:]
    h = _norm_mod_fwd(x, g_norm, scale, shift)
    proj = _matmul(h, wts["in"], name="mm_proj", tn_cap=1920)
    o_ret, u_ret, states = _ret_fwd(proj, lg, cosr, sinr, lay)
    y_ret = _matmul(u_ret, wts["ret"], name="mm_y")
    cqn, ckvn = _mla_prep(proj, g_cq, g_ckv, lay)
    qp = _matmul(cqn, wts["uq"], name="mm_up")
    kvp = _matmul(ckvn, wts["ukv"], name="mm_up")
    q_cat, k_cat, v = _qk_prep(qp, kvp, proj, ta, tb, tc, lay)
    o_mla, u_mla, lse = _attn_fwd(q_cat, k_cat, v, proj, lay)
    y_mla = _matmul(u_mla, wts["mla"], name="mm_y")
    merged = _merge_fwd(y_ret, y_mla, proj, lay)
    out = _matmul(merged, wts["out"], name="mm_y")
    x_next = _resid_fwd(x, out, gate)
    saved = dict(x=x, h=h, proj=proj, o_ret=o_ret, u_ret=u_ret, states=states, y_ret=y_ret, cqn=cqn,
                 ckvn=ckvn, q_cat=q_cat, k_cat=k_cat, v=v, o_mla=o_mla, u_mla=u_mla, lse=lse,
                 y_mla=y_mla, merged=merged, out=out)
    return x_next, saved


def _layer_bwd(dxn, sv, mod, g_norm, g_cq, g_ckv, wts, tabs, lg, lay):
    d = dxn.shape[1]
    cosr, sinr, ta, tb, tc = tabs
    scale, gate = mod[:, d:2 * d], mod[:, 2 * d:]
    gdt = BF16
    dout, dgate = _resid_bwd(dxn, sv["out"], gate)
    dmerged = _matmul(dout, wts["out"], tb=True, name="mm_dy")
    dw_out = _matmul(sv["merged"], dout, ta=True, out_dtype=gdt, name="mm_dw")
    dy_ret, dy_mla, dproj = _merge_bwd(dmerged, sv["y_ret"], sv["y_mla"], sv["proj"], lay)
    du_ret = _matmul(dy_ret, wts["ret"], tb=True, name="mm_dy")
    dw_ret = _matmul(sv["u_ret"], dy_ret, ta=True, out_dtype=gdt, name="mm_dw")
    dproj = _ret_bwd(sv["proj"], lg, cosr, sinr, sv["o_ret"], du_ret, sv["states"], dproj, lay)
    du_mla = _matmul(dy_mla, wts["mla"], tb=True, name="mm_dy")
    dw_mla = _matmul(sv["u_mla"], dy_mla, ta=True, out_dtype=gdt, name="mm_dw")
    do_mla, dproj = _mla_gate_bwd(du_mla, sv["o_mla"], sv["proj"], dproj, lay)
    dqp, delta = _attn_dq(sv["q_cat"], sv["k_cat"], sv["v"], do_mla, sv["o_mla"], sv["lse"], ta, tb, tc)
    dk_cat, dv = _attn_dkv(sv["q_cat"], sv["k_cat"], sv["v"], do_mla, sv["lse"], delta)
    dkvp, dproj = _kv_bwd_prep(dk_cat, dv, ta, tb, tc, dproj, lay)
    dcqn = _matmul(dqp, wts["uq"], tb=True, name="mm_dlat")
    dckvn = _matmul(dkvp, wts["ukv"], tb=True, name="mm_dlat")
    dw_uq = _matmul(sv["cqn"], dqp, ta=True, out_dtype=gdt, name="mm_dwup")
    dw_ukv = _matmul(sv["ckvn"], dkvp, ta=True, out_dtype=gdt, name="mm_dwup")
    dproj, dg_cq, dg_ckv = _mla_prep_bwd(sv["proj"], dcqn, dckvn, g_cq, g_ckv, dproj, lay)
    dh = _matmul(dproj, wts["in"], tb=True, name="mm_dh", tk_cap=896)
    dw_in = _matmul(sv["h"], dproj, ta=True, out_dtype=gdt, name="mm_dwin", tn_cap=1920)
    dx, dshift, dscale, dg_norm = _norm_mod_bwd(sv["x"], g_norm, scale, dh, dxn)
    dmod = jnp.concatenate([dshift, dscale, dgate], axis=1)
    big = dict(w_in=dw_in, uq=dw_uq, ukv=dw_ukv, ret=dw_ret, mla=dw_mla, out=dw_out)
    small = dict(dmod=dmod, g_norm=dg_norm, g_cq=dg_cq, g_ckv=dg_ckv)
    return dx, big, small


def _to_owner_blocks_cols(g, n_local):
    k = g.shape[0]
    return g.reshape(k, N_DEV, n_local).transpose(1, 0, 2)


def _from_owner_blocks_cols(g):
    return g.transpose(1, 0, 2).reshape(g.shape[1], -1)


def kernel(x, c, positions, w_mod, b_mod, g_norm, w_in, g_cq, g_ckv, w_uq, w_ukv, w_ret_proj, w_mla_proj, w_out, g_final, loss_target, m_w_mod, m_b_mod, m_g_norm, m_w_in, m_g_cq, m_g_ckv, m_w_uq, m_w_ukv, m_w_ret_proj, m_w_mla_proj, m_w_out, m_g_final, v_w_mod, v_b_mod, v_g_norm, v_w_in, v_g_cq, v_g_ckv, v_w_uq, v_w_ukv, v_w_ret_proj, v_w_mla_proj, v_w_out, v_g_final):
    nl, d, _ = w_mod.shape
    s = x.shape[1]
    rank = g_cq.shape[1]
    lay = Layout(d, rank, g_ckv.shape[1])
    me = _slot(*_place())
    x0 = x.reshape(s, d)
    target = loss_target.reshape(s, d)
    tabs = _rope_tables(positions.reshape(s))
    lg = jnp.log(1.0 - 2.0 ** (-5.0 - jnp.arange(RET_HEADS, dtype=F32)))

    c_act = c * _sigmoid(c)
    (c_all,) = _all_gather([c_act.reshape(d // LANES, LANES)], None, "gather_c", vmem=True)
    c_all = c_all.reshape(N_DEV, d)
    n_mod = w_mod.shape[2]
    mod_part = jnp.stack([_matmul(c_all, w_mod[l], name="mm_mod", tm_cap=8) for l in range(nl)])
    (mod_all,) = _all_gather([mod_part.reshape(-1, LANES)], None, "gather_mod", vmem=True)
    mod_all = mod_all.reshape(N_DEV, nl, N_DEV, n_mod)
    mod = lax.dynamic_index_in_dim(mod_all, me, axis=2, keepdims=False)
    mod = mod.transpose(1, 0, 2).reshape(nl, N_DEV * n_mod) + b_mod

    shards = [w.astype(BF16) for w in (w_in, w_uq, w_ukv, w_ret_proj, w_mla_proj, w_out)]
    xl, saved, wts_all = x0, [], []
    for l in range(nl):
        g_in, g_uq, g_ukv, g_ret, g_mla, g_out = _all_gather(shards, l, "gather_w")
        wts = {
            "in": lay.to_physical(_from_owner_blocks_cols(g_in)),
            "uq": _uq_to_physical(_from_owner_blocks_cols(g_uq)),
            "ukv": _ukv_to_physical(_from_owner_blocks_cols(g_ukv)),
            "ret": g_ret.reshape(-1, d), "mla": g_mla.reshape(-1, d), "out": g_out.reshape(-1, d),
        }
        wts_all.append(wts)
        xl, sv = _layer_fwd(xl, mod[l:l + 1], g_norm[l:l + 1], g_cq[l:l + 1], g_ckv[l:l + 1], wts, tabs, lg, lay)
        saved.append(sv)
    loss_lanes, dx, dg_final = _final_loss(xl, g_final.reshape(1, d), target)

    recv = [None] * nl
    small = [None] * nl
    for l in reversed(range(nl)):
        dx, big, small[l] = _layer_bwd(dx, saved[l], mod[l:l + 1], g_norm[l:l + 1], g_cq[l:l + 1],
                                       g_ckv[l:l + 1], wts_all[l], tabs, lg, lay)
        parts = [
            _to_owner_blocks_cols(lay.to_logical(big["w_in"]), w_in.shape[2]),
            _to_owner_blocks_cols(_uq_to_logical(big["uq"]), w_uq.shape[2]),
            _to_owner_blocks_cols(_ukv_to_logical(big["ukv"]), w_ukv.shape[2]),
            big["ret"].reshape(N_DEV, -1, d), big["mla"].reshape(N_DEV, -1, d), big["out"].reshape(N_DEV, -1, d),
        ]
        recv[l] = _exchange_grads(parts, "exchange_grads")
    grad_x = dx.reshape(x.shape)

    pack = jnp.concatenate(
        [jnp.concatenate([sm[k] for sm in small], axis=0).reshape(-1)
         for k in ("dmod", "g_norm", "g_cq", "g_ckv")] + [dg_final.reshape(-1), loss_lanes.reshape(-1)])
    (pack_all,) = _all_gather([pack.reshape(-1, LANES)], None, "gather_small", vmem=True)
    tot = _sum_devices(pack_all).reshape(-1)
    sizes = [nl * 3 * d, nl * d, nl * rank, nl * rank, d]
    offs = np.cumsum([0] + sizes)
    grad_b_mod = tot[offs[0]:offs[1]].reshape(nl, 3 * d)
    grad_g_norm = tot[offs[1]:offs[2]].reshape(nl, d)
    grad_g_cq = tot[offs[2]:offs[3]].reshape(nl, rank)
    grad_g_ckv = tot[offs[3]:offs[4]].reshape(nl, rank)
    grad_g_final = tot[offs[4]:offs[5]]
    loss = tot[offs[5]]
    dmod_all = pack_all.reshape(N_DEV, -1)[:, :sizes[0]].reshape(N_DEV, nl, 3 * d)
    dmod_mine = lax.dynamic_slice_in_dim(dmod_all, me * n_mod, n_mod, axis=2).transpose(1, 0, 2)

    out = {}
    out["w_mod"] = _adam_mod(c_all.T, dmod_mine, w_mod, m_w_mod, v_w_mod)
    for i, (key, w, m, v) in enumerate((("w_in", w_in, m_w_in, v_w_in), ("w_uq", w_uq, m_w_uq, v_w_uq),
                                        ("w_ukv", w_ukv, m_w_ukv, v_w_ukv),
                                        ("w_ret_proj", w_ret_proj, m_w_ret_proj, v_w_ret_proj),
                                        ("w_mla_proj", w_mla_proj, m_w_mla_proj, v_w_mla_proj),
                                        ("w_out", w_out, m_w_out, v_w_out))):
        out[key] = _adam_sharded([recv[l][i] for l in range(nl)], w, m, v, "adam_" + key)
    for key, g, w, m, v in (("b_mod", grad_b_mod, b_mod, m_b_mod, v_b_mod),
                            ("g_norm", grad_g_norm, g_norm, m_g_norm, v_g_norm),
                            ("g_cq", grad_g_cq, g_cq, m_g_cq, v_g_cq),
                            ("g_ckv", grad_g_ckv, g_ckv, m_g_ckv, v_g_ckv),
                            ("g_final", grad_g_final.reshape(1, d), g_final.reshape(1, d),
                             m_g_final.reshape(1, d), v_g_final.reshape(1, d))):
        out[key] = (g,) + tuple(_adam_small(g, w, m, v, "adam_" + key))
    out["g_final"] = tuple(a.reshape(d) for a in out["g_final"])

    names = ("w_mod", "b_mod", "g_norm", "w_in", "g_cq", "g_ckv", "w_uq", "w_ukv", "w_ret_proj",
             "w_mla_proj", "w_out", "g_final")
    return (loss, grad_x, *[out[k][0] for k in names], *[out[k][1] for k in names],
            *[out[k][2] for k in names], *[out[k][3] for k in names])
```

```python
import functools
import itertools

import jax
import jax.numpy as jnp
import numpy as np
from jax import lax
from jax.experimental import pallas as pl
from jax.experimental.pallas import tpu as pltpu

F32 = jnp.float32
BF16 = jnp.bfloat16

N_DEV = 8
CHUNK = 64
EPS = 1e-6
NEG_INF = -1e30
ROPE_BASE = 10000.0
LANES = 128

RET_HEADS = 8
RET_DK = 128
RET_DV = 256
MLA_HEADS = 16
MLA_NOPE = 128
MLA_ROPE = 64
MLA_DV = 128
MLA_QW = 256
QK_SCALE = (MLA_NOPE + MLA_ROPE) ** -0.5
QK_LOG2_SCALE = QK_SCALE * 1.4426950408889634
LN2 = 0.6931471805599453

ADAM_LR = 0.001
ADAM_B1 = 0.9
ADAM_B2 = 0.999
ADAM_EPS = 1e-08
ADAM_WD = 0.01
ADAM_STEP = 10

VMEM_LIMIT_BYTES = 56 * 1024 * 1024
MESH = pl.DeviceIdType.MESH
ANY = pl.BlockSpec(memory_space=pl.ANY)


def _cp(*sem):
    return pltpu.CompilerParams(dimension_semantics=sem if sem else None,
                                vmem_limit_bytes=VMEM_LIMIT_BYTES)


def _tile(n, cap):
    best = None
    t = LANES
    while t <= min(n, cap):
        if n % t == 0:
            best = t
        t += LANES
    return best if best is not None else n


def _rtile(n, cap):
    t = cap
    while t > 8 and n % t:
        t //= 2
    return t if n % t == 0 else n


def _sigmoid(x):
    return 1.0 / (1.0 + jnp.exp(-x))


def _dot(a, b):
    return lax.dot_general(a, b, (((1,), (0,)), ((), ())), preferred_element_type=F32)


def _dot_nt(a, b):
    return lax.dot_general(a, b, (((1,), (1,)), ((), ())), preferred_element_type=F32)


def _dot_tn(a, b):
    return lax.dot_general(a, b, (((0,), (0,)), ((), ())), preferred_element_type=F32)


def _roll(x, s):
    return pltpu.roll(x, s, 1)


class Layout:
    def __init__(self, d_model, q_rank, kv_rank):
        assert q_rank == kv_rank
        self.d = d_model
        self.rank = q_rank
        self.ret_w = 2 * RET_DK + 2 * RET_DV
        self.ret_qk = RET_HEADS * RET_DK
        self.ret_v = RET_HEADS * RET_DV
        self.mla_v = MLA_HEADS * MLA_DV
        widths = {"bg": 2 * d_model, "mg": self.mla_v, "ret": RET_HEADS * self.ret_w,
                  "cqkv": 2 * q_rank, "kr": LANES}
        blocks = {"bg": 2 * d_model, "mg": self.mla_v, "ret": self.ret_w,
                  "cqkv": 2 * q_rank, "kr": LANES}
        for order in itertools.permutations(widths):
            off, offs, ok = 0, {}, True
            for name in order:
                if off % blocks[name]:
                    ok = False
                    break
                offs[name] = off
                off += widths[name]
            if ok:
                break
        assert ok, "no aligned layout"
        self.order, self.off, self.width, self.total = order, offs, widths, off
        lo, o = {}, 0
        for name, w in (("rq", self.ret_qk), ("rk", self.ret_qk), ("rv", self.ret_v),
                        ("rg", self.ret_v), ("cq", q_rank), ("ckv", kv_rank), ("kr", MLA_ROPE),
                        ("mg", self.mla_v), ("bg", 2 * d_model)):
            lo[name] = (o, w)
            o += w
        self.logical, self.d_in = lo, o

    def pieces(self):
        lo = self.logical
        out = []
        for name in self.order:
            if name == "bg":
                out.append(lo["bg"])
            elif name == "mg":
                out.append(lo["mg"])
            elif name == "ret":
                for h in range(RET_HEADS):
                    out.append((lo["rq"][0] + h * RET_DK, RET_DK))
                    out.append((lo["rk"][0] + h * RET_DK, RET_DK))
                    out.append((lo["rv"][0] + h * RET_DV, RET_DV))
                    out.append((lo["rg"][0] + h * RET_DV, RET_DV))
            elif name == "cqkv":
                out.append((lo["cq"][0], 2 * self.rank))
            elif name == "kr":
                out.append(lo["kr"])
                out.append((None, LANES - MLA_ROPE))
        return out

    def to_physical(self, w):
        parts = []
        for start, width in self.pieces():
            if start is None:
                parts.append(jnp.zeros((w.shape[0], width), w.dtype))
            else:
                parts.append(w[:, start:start + width])
        return jnp.concatenate(parts, axis=1)

    def to_logical(self, w):
        got, off = [], 0
        for start, width in self.pieces():
            if start is not None:
                got.append((start, w[:, off:off + width]))
            off += width
        got.sort(key=lambda t: t[0])
        return jnp.concatenate([p for _, p in got], axis=1)


def _uq_to_physical(w):
    k = w.shape[0]
    w3 = w.reshape(k, MLA_HEADS, MLA_NOPE + MLA_ROPE)
    pad = jnp.zeros((k, MLA_HEADS, MLA_QW - MLA_NOPE - MLA_ROPE), w.dtype)
    return jnp.concatenate([w3, pad], axis=2).reshape(k, MLA_HEADS * MLA_QW)


def _uq_to_logical(w):
    k = w.shape[0]
    return w.reshape(k, MLA_HEADS, MLA_QW)[:, :, :MLA_NOPE + MLA_ROPE].reshape(k, -1)


def _ukv_to_physical(w):
    k = w.shape[0]
    w3 = w.reshape(k, MLA_HEADS, MLA_NOPE + MLA_DV)
    return jnp.concatenate([w3[:, :, :MLA_NOPE].reshape(k, -1), w3[:, :, MLA_NOPE:].reshape(k, -1)], axis=1)


def _ukv_to_logical(w):
    k = w.shape[0]
    kn = w[:, :MLA_HEADS * MLA_NOPE].reshape(k, MLA_HEADS, MLA_NOPE)
    v = w[:, MLA_HEADS * MLA_NOPE:].reshape(k, MLA_HEADS, MLA_DV)
    return jnp.concatenate([kn, v], axis=2).reshape(k, -1)


def _matmul(a, b, *, ta=False, tb=False, out_dtype=F32, name, tm_cap=1024, tn_cap=1024, tk_cap=512):
    m, k = (a.shape[1], a.shape[0]) if ta else a.shape
    n = b.shape[0] if tb else b.shape[1]
    assert k == (b.shape[1] if tb else b.shape[0])
    tm, tn, tk = _tile(m, tm_cap), _tile(n, tn_cap), _tile(k, tk_cap)
    nk = k // tk

    def body(a_ref, b_ref, o_ref, acc_ref):
        kk = pl.program_id(2)

        @pl.when(kk == 0)
        def _():
            acc_ref[...] = jnp.zeros_like(acc_ref)

        dims = (((0 if ta else 1,), (1 if tb else 0,)), ((), ()))
        acc_ref[...] += lax.dot_general(a_ref[...].astype(BF16), b_ref[...].astype(BF16), dims,
                                        preferred_element_type=F32)

        @pl.when(kk == nk - 1)
        def _():
            o_ref[...] = acc_ref[...].astype(o_ref.dtype)

    a_spec = pl.BlockSpec((tk, tm), lambda i, j, kk: (kk, i)) if ta else pl.BlockSpec((tm, tk), lambda i, j, kk: (i, kk))
    b_spec = pl.BlockSpec((tn, tk), lambda i, j, kk: (j, kk)) if tb else pl.BlockSpec((tk, tn), lambda i, j, kk: (kk, j))
    return pl.pallas_call(
        body, name=name, grid=(m // tm, n // tn, nk),
        in_specs=[a_spec, b_spec],
        out_specs=pl.BlockSpec((tm, tn), lambda i, j, kk: (i, j)),
        out_shape=jax.ShapeDtypeStruct((m, n), out_dtype),
        scratch_shapes=[pltpu.VMEM((tm, tn), F32)],
        compiler_params=_cp("parallel", "parallel", "arbitrary"),
    )(a, b)


def _row(tm, w, cb=0):
    return pl.BlockSpec((tm, w), lambda i, cb=cb: (i, cb))


def _vec(w, cb=0):
    return pl.BlockSpec((1, w), lambda i, cb=cb: (0, cb))


def _dproj_out(dproj, s, lay, tm, name):
    w = lay.width[name]
    cb = lay.off[name] // w
    spec = _row(tm, w, cb)
    shape = jax.ShapeDtypeStruct((s, lay.total), BF16)
    return spec, shape


def _norm_mod_fwd(x, g, scale, shift, deps):
    s, d = x.shape
    tm = _rtile(s, 256)

    def body(x_ref, g_ref, sc_ref, sh_ref, *rest):
        h_ref = rest[len(deps)]
        xv = x_ref[...]
        xh = xv * lax.rsqrt(jnp.mean(xv * xv, axis=-1, keepdims=True) + EPS)
        h_ref[...] = ((xh * g_ref[...]) * (1.0 + sc_ref[...]) + sh_ref[...]).astype(BF16)

    return pl.pallas_call(
        body, name="norm_mod_fwd", grid=(s // tm,),
        in_specs=[_row(tm, d), _vec(d), _vec(d), _vec(d)] + [ANY] * len(deps),
        out_specs=_row(tm, d), out_shape=jax.ShapeDtypeStruct((s, d), BF16),
        compiler_params=_cp("parallel"),
    )(x, g, scale, shift, *deps)


def _norm_mod_bwd(x, g, scale, dh, dres):
    s, d = x.shape
    tm = _rtile(s, 256)

    def body(x_ref, g_ref, sc_ref, dh_ref, dres_ref, dx_ref, dsh_ref, dsc_ref, dg_ref):
        @pl.when(pl.program_id(0) == 0)
        def _():
            dsh_ref[...] = jnp.zeros_like(dsh_ref)
            dsc_ref[...] = jnp.zeros_like(dsc_ref)
            dg_ref[...] = jnp.zeros_like(dg_ref)

        xv, gv, dhv = x_ref[...], g_ref[...], dh_ref[...]
        rstd = lax.rsqrt(jnp.mean(xv * xv, axis=-1, keepdims=True) + EPS)
        xh = xv * rstd
        dy = dhv * (1.0 + sc_ref[...])
        dxh = dy * gv
        dx_ref[...] = dres_ref[...] + rstd * (dxh - xh * jnp.mean(dxh * xh, axis=-1, keepdims=True))
        dsh_ref[...] += jnp.sum(dhv, axis=0, keepdims=True)
        dsc_ref[...] += jnp.sum(dhv * (xh * gv), axis=0, keepdims=True)
        dg_ref[...] += jnp.sum(dy * xh, axis=0, keepdims=True)

    vec = jax.ShapeDtypeStruct((1, d), F32)
    return pl.pallas_call(
        body, name="norm_mod_bwd", grid=(s // tm,),
        in_specs=[_row(tm, d), _vec(d), _vec(d), _row(tm, d), _row(tm, d)],
        out_specs=[_row(tm, d), _vec(d), _vec(d), _vec(d)],
        out_shape=[jax.ShapeDtypeStruct((s, d), F32), vec, vec, vec],
        compiler_params=_cp("arbitrary"),
    )(x, g, scale, dh, dres)


def _final_loss(x, g, target):
    s, d = x.shape
    tm = _rtile(s, 256)

    def body(x_ref, g_ref, t_ref, l_ref, dx_ref, dg_ref):
        @pl.when(pl.program_id(0) == 0)
        def _():
            l_ref[...] = jnp.zeros_like(l_ref)
            dg_ref[...] = jnp.zeros_like(dg_ref)

        xv, gv = x_ref[...], g_ref[...]
        rstd = lax.rsqrt(jnp.mean(xv * xv, axis=-1, keepdims=True) + EPS)
        xh = xv * rstd
        err = xh * gv - t_ref[...]
        row = jnp.mean(err * err, axis=-1, keepdims=True)
        l_ref[...] += 0.5 * jnp.sum(row, axis=0, keepdims=True)
        dy = err / d
        dxh = dy * gv
        dx_ref[...] = rstd * (dxh - xh * jnp.mean(dxh * xh, axis=-1, keepdims=True))
        dg_ref[...] += jnp.sum(dy * xh, axis=0, keepdims=True)

    return pl.pallas_call(
        body, name="final_loss", grid=(s // tm,),
        in_specs=[_row(tm, d), _vec(d), _row(tm, d)],
        out_specs=[_vec(LANES), _row(tm, d), _vec(d)],
        out_shape=[jax.ShapeDtypeStruct((1, LANES), F32), jax.ShapeDtypeStruct((s, d), F32),
                   jax.ShapeDtypeStruct((1, d), F32)],
        compiler_params=_cp("arbitrary"),
    )(x, g, target)


def _resid_fwd(x, out, gate):
    s, d = x.shape
    tm = _rtile(s, 256)

    def body(x_ref, o_ref, g_ref, y_ref):
        y_ref[...] = x_ref[...] + g_ref[...] * o_ref[...]

    return pl.pallas_call(
        body, name="resid_fwd", grid=(s // tm,),
        in_specs=[_row(tm, d), _row(tm, d), _vec(d)],
        out_specs=_row(tm, d), out_shape=jax.ShapeDtypeStruct((s, d), F32),
        compiler_params=_cp("parallel"),
    )(x, out, gate)


def _resid_bwd(dxn, out, gate, deps):
    s, d = dxn.shape
    tm = _rtile(s, 256)

    def body(dx_ref, o_ref, g_ref, *rest):
        do_ref, dg_ref = rest[len(deps):]

        @pl.when(pl.program_id(0) == 0)
        def _():
            dg_ref[...] = jnp.zeros_like(dg_ref)

        dxv = dx_ref[...]
        do_ref[...] = (dxv * g_ref[...]).astype(BF16)
        dg_ref[...] += jnp.sum(dxv * o_ref[...], axis=0, keepdims=True)

    return pl.pallas_call(
        body, name="resid_bwd", grid=(s // tm,),
        in_specs=[_row(tm, d), _row(tm, d), _vec(d)] + [ANY] * len(deps),
        out_specs=[_row(tm, d), _vec(d)],
        out_shape=[jax.ShapeDtypeStruct((s, d), BF16), jax.ShapeDtypeStruct((1, d), F32)],
        compiler_params=_cp("arbitrary"),
    )(dxn, out, gate, *deps)


def _merge_fwd(y_ret, y_mla, proj, lay):
    s, d = y_ret.shape
    tm = _rtile(s, 256)
    cb = lay.off["bg"] // (2 * d)

    def body(a_ref, b_ref, bg_ref, m_ref):
        sg = _sigmoid(bg_ref[...])
        m_ref[...] = (sg[:, :d] * a_ref[...] + sg[:, d:] * b_ref[...]).astype(BF16)

    return pl.pallas_call(
        body, name="merge_fwd", grid=(s // tm,),
        in_specs=[_row(tm, d), _row(tm, d), _row(tm, 2 * d, cb)],
        out_specs=_row(tm, d), out_shape=jax.ShapeDtypeStruct((s, d), BF16),
        compiler_params=_cp("parallel"),
    )(y_ret, y_mla, proj)


def _merge_bwd(dm, y_ret, y_mla, proj, lay):
    s, d = dm.shape
    tm = _rtile(s, 256)
    cb = lay.off["bg"] // (2 * d)
    dp_spec, dp_shape = _dproj_out(None, s, lay, tm, "bg")

    def body(dm_ref, a_ref, b_ref, bg_ref, da_ref, db_ref, dp_ref):
        sg = _sigmoid(bg_ref[...])
        dmv = dm_ref[...]
        ga, gb = sg[:, :d], sg[:, d:]
        da_ref[...] = (dmv * ga).astype(BF16)
        db_ref[...] = (dmv * gb).astype(BF16)
        dp_ref[:, :d] = (dmv * a_ref[...] * ga * (1.0 - ga)).astype(BF16)
        dp_ref[:, d:] = (dmv * b_ref[...] * gb * (1.0 - gb)).astype(BF16)

    act = jax.ShapeDtypeStruct((s, d), BF16)
    return pl.pallas_call(
        body, name="merge_bwd", grid=(s // tm,),
        in_specs=[_row(tm, d), _row(tm, d), _row(tm, d), _row(tm, 2 * d, cb)],
        out_specs=[_row(tm, d), _row(tm, d), dp_spec],
        out_shape=[act, act, dp_shape],
        compiler_params=_cp("parallel"),
    )(dm, y_ret, y_mla, proj)


def _mla_prep(proj, g_cq, g_ckv, lay):
    s = proj.shape[0]
    r = lay.rank
    tm = _rtile(s, 512)
    cb = lay.off["cqkv"] // (2 * r)

    def body(p_ref, gq_ref, gk_ref, q_ref, k_ref):
        pv = p_ref[...]
        for lo, g_ref, o_ref in ((0, gq_ref, q_ref), (r, gk_ref, k_ref)):
            xv = pv[:, lo:lo + r]
            xh = xv * lax.rsqrt(jnp.mean(xv * xv, axis=-1, keepdims=True) + EPS)
            o_ref[...] = (xh * g_ref[...]).astype(BF16)

    act = jax.ShapeDtypeStruct((s, r), BF16)
    return pl.pallas_call(
        body, name="mla_prep", grid=(s // tm,),
        in_specs=[_row(tm, 2 * r, cb), _vec(r), _vec(r)],
        out_specs=[_row(tm, r), _row(tm, r)], out_shape=[act, act],
        compiler_params=_cp("parallel"),
    )(proj, g_cq, g_ckv)


def _mla_prep_bwd(proj, dqn, dkn, g_cq, g_ckv, dproj, lay):
    s = proj.shape[0]
    r = lay.rank
    tm = _rtile(s, 512)
    cb = lay.off["cqkv"] // (2 * r)
    dp_spec, dp_shape = _dproj_out(dproj, s, lay, tm, "cqkv")

    def body(p_ref, dq_ref, dk_ref, gq_ref, gk_ref, _, dp_ref, dgq_ref, dgk_ref):
        @pl.when(pl.program_id(0) == 0)
        def _():
            dgq_ref[...] = jnp.zeros_like(dgq_ref)
            dgk_ref[...] = jnp.zeros_like(dgk_ref)

        pv = p_ref[...]
        for lo, g_ref, d_ref, dg_ref in ((0, gq_ref, dq_ref, dgq_ref), (r, gk_ref, dk_ref, dgk_ref)):
            xv = pv[:, lo:lo + r]
            rstd = lax.rsqrt(jnp.mean(xv * xv, axis=-1, keepdims=True) + EPS)
            xh = xv * rstd
            dy = d_ref[...]
            dxh = dy * g_ref[...]
            dp_ref[:, lo:lo + r] = (rstd * (dxh - xh * jnp.mean(dxh * xh, axis=-1, keepdims=True))).astype(BF16)
            dg_ref[...] += jnp.sum(dy * xh, axis=0, keepdims=True)

    vec = jax.ShapeDtypeStruct((1, r), F32)
    return pl.pallas_call(
        body, name="mla_prep_bwd", grid=(s // tm,),
        in_specs=[_row(tm, 2 * r, cb), _row(tm, r), _row(tm, r), _vec(r), _vec(r), ANY],
        out_specs=[dp_spec, _vec(r), _vec(r)], out_shape=[dp_shape, vec, vec],
        input_output_aliases={5: 0},
        compiler_params=_cp("arbitrary"),
    )(proj, dqn, dkn, g_cq, g_ckv, dproj)


def _rope_tile(t, a, b, c):
    return t * a + _roll(t, 96) * b + _roll(t, 32) * c


def _rope_tile_bwd(dy, a, b, c):
    return dy * a + _roll(dy * b, 32) + _roll(dy * c, 96)


def _qk_prep(qp, kvp, proj, ta, tb, tc, lay):
    s = qp.shape[0]
    hq = MLA_HEADS * MLA_QW
    hv = MLA_HEADS * MLA_DV
    tm = _rtile(s, 256)
    kr_cb = lay.off["kr"] // LANES

    def body(q_ref, kv_ref, kr_ref, a_ref, b_ref, c_ref, qc_ref, kc_ref, v_ref):
        a, b, c = a_ref[...], b_ref[...], c_ref[...]
        krot = _rope_tile(kr_ref[...], a, b, c).astype(BF16)
        for h in range(MLA_HEADS):
            q0 = h * MLA_QW
            qc_ref[:, q0:q0 + MLA_NOPE] = (q_ref[:, q0:q0 + MLA_NOPE] * QK_LOG2_SCALE).astype(BF16)
            qc_ref[:, q0 + MLA_NOPE:q0 + MLA_QW] = (
                _rope_tile(q_ref[:, q0 + MLA_NOPE:q0 + MLA_QW], a, b, c) * QK_LOG2_SCALE).astype(BF16)
            kc_ref[:, q0:q0 + MLA_NOPE] = kv_ref[:, h * MLA_NOPE:(h + 1) * MLA_NOPE].astype(BF16)
            kc_ref[:, q0 + MLA_NOPE:q0 + MLA_QW] = krot
        v_ref[...] = kv_ref[:, MLA_HEADS * MLA_NOPE:].astype(BF16)

    return pl.pallas_call(
        body, name="qk_prep", grid=(s // tm,),
        in_specs=[_row(tm, hq), _row(tm, hq), _row(tm, LANES, kr_cb), _row(tm, LANES), _row(tm, LANES), _row(tm, LANES)],
        out_specs=[_row(tm, hq), _row(tm, hq), _row(tm, hv)],
        out_shape=[jax.ShapeDtypeStruct((s, hq), BF16), jax.ShapeDtypeStruct((s, hq), BF16),
                   jax.ShapeDtypeStruct((s, hv), BF16)],
        compiler_params=_cp("parallel"),
    )(qp, kvp, proj, ta, tb, tc)


def _kv_bwd_prep(dk_cat, dv, ta, tb, tc, dproj, lay):
    s = dk_cat.shape[0]
    hq = MLA_HEADS * MLA_QW
    hv = MLA_HEADS * MLA_DV
    tm = _rtile(s, 256)
    dp_spec, dp_shape = _dproj_out(dproj, s, lay, tm, "kr")

    def body(dk_ref, dv_ref, a_ref, b_ref, c_ref, _, dkv_ref, dp_ref):
        acc = jnp.zeros((tm, LANES), F32)
        for h in range(MLA_HEADS):
            q0 = h * MLA_QW
            dkv_ref[:, h * MLA_NOPE:(h + 1) * MLA_NOPE] = dk_ref[:, q0:q0 + MLA_NOPE].astype(BF16)
            acc = acc + dk_ref[:, q0 + MLA_NOPE:q0 + MLA_QW]
        dkv_ref[:, MLA_HEADS * MLA_NOPE:] = dv_ref[...].astype(BF16)
        dp_ref[...] = _rope_tile_bwd(acc, a_ref[...], b_ref[...], c_ref[...]).astype(BF16)

    return pl.pallas_call(
        body, name="kv_bwd_prep", grid=(s // tm,),
        in_specs=[_row(tm, hq), _row(tm, hv), _row(tm, LANES), _row(tm, LANES), _row(tm, LANES), ANY],
        out_specs=[_row(tm, hq), dp_spec],
        out_shape=[jax.ShapeDtypeStruct((s, hq), BF16), dp_shape],
        input_output_aliases={5: 1},
        compiler_params=_cp("parallel"),
    )(dk_cat, dv, ta, tb, tc, dproj)


def _mla_gate_bwd(du, o, proj, dproj, lay):
    s, vw = du.shape
    tm = _rtile(s, 256)
    cb = lay.off["mg"] // vw
    dp_spec, dp_shape = _dproj_out(dproj, s, lay, tm, "mg")

    def body(du_ref, o_ref, g_ref, _, do_ref, dp_ref):
        gv, duv = g_ref[...], du_ref[...]
        sg = _sigmoid(gv)
        do_ref[...] = (duv * (gv * sg)).astype(BF16)
        dp_ref[...] = (duv * o_ref[...] * (sg + gv * sg * (1.0 - sg))).astype(BF16)

    return pl.pallas_call(
        body, name="mla_gate_bwd", grid=(s // tm,),
        in_specs=[_row(tm, vw), _row(tm, vw), _row(tm, vw, cb), ANY],
        out_specs=[_row(tm, vw), dp_spec],
        out_shape=[jax.ShapeDtypeStruct((s, vw), BF16), dp_shape],
        input_output_aliases={3: 1},
        compiler_params=_cp("parallel"),
    )(du, o, proj, dproj)


def _ret_tables(lg):
    ri = lax.broadcasted_iota(jnp.int32, (CHUNK, CHUNK), 0).astype(F32)
    ci = lax.broadcasted_iota(jnp.int32, (CHUNK, CHUNK), 1).astype(F32)
    col = lax.broadcasted_iota(jnp.int32, (CHUNK, 1), 0).astype(F32)
    dmat = jnp.exp(jnp.abs(ri - ci) * lg)
    xi = jnp.exp((col + 1.0) * lg)
    zeta = jnp.exp((CHUNK - 1.0 - col) * lg)
    decay = jnp.exp(jnp.full((1, 1), CHUNK, F32) * lg)
    return dmat, xi, zeta, decay


def _ret_qkvg(blk, cs, sn):
    dv = RET_DV
    q = blk[:, :RET_DK]
    k = blk[:, RET_DK:2 * RET_DK]
    q = q * cs + _roll(q, RET_DK // 2) * sn
    k = (k * cs + _roll(k, RET_DK // 2) * sn) * (RET_DK ** -0.5)
    return q, k, blk[:, 2 * RET_DK:2 * RET_DK + dv], blk[:, 2 * RET_DK + dv:]


def _group_norm(o):
    mu = jnp.mean(o, axis=-1, keepdims=True)
    oc = o - mu
    rstd = lax.rsqrt(jnp.mean(oc * oc, axis=-1, keepdims=True) + EPS)
    return oc * rstd, rstd


def _ret_fwd(proj, lg, cosr, sinr, lay):
    s = proj.shape[0]
    dv, w = RET_DV, lay.ret_w
    tb = _rtile(s, 512)
    nb, nch = s // tb, tb // CHUNK
    cb0 = lay.off["ret"] // w

    def body(lg_ref, p_ref, cos_ref, sin_ref, o_ref, u_ref, st_ref, state):
        @pl.when(pl.program_id(1) == 0)
        def _():
            state[...] = jnp.zeros_like(state)

        dmat, xi, zeta, decay = _ret_tables(lg_ref[pl.program_id(0)])
        for c in range(nch):
            rows = slice(c * CHUNK, (c + 1) * CHUNK)
            q, k, v, g = _ret_qkvg(p_ref[rows, :], cos_ref[rows, :], sin_ref[rows, :])
            qb, kb, vb = q.astype(BF16), k.astype(BF16), v.astype(BF16)
            sc = _dot_nt(qb, kb) * dmat
            st = state[...]
            o = _dot(sc.astype(BF16), vb) + _dot((q * xi).astype(BF16), st.astype(BF16))
            st_ref[c] = st.astype(BF16)
            state[...] = st * decay + _dot_tn((k * zeta).astype(BF16), vb)
            o_ref[rows, :] = o
            n, _ = _group_norm(o)
            u_ref[rows, :] = (n * (g * _sigmoid(g))).astype(BF16)

    return pl.pallas_call(
        body, name="ret_fwd", grid=(RET_HEADS, nb),
        in_specs=[pl.BlockSpec(memory_space=pltpu.SMEM),
                  pl.BlockSpec((tb, w), lambda h, b: (b, cb0 + h)),
                  pl.BlockSpec((tb, RET_DK), lambda h, b: (b, 0)),
                  pl.BlockSpec((tb, RET_DK), lambda h, b: (b, 0))],
        out_specs=[pl.BlockSpec((tb, dv), lambda h, b: (b, h)),
                   pl.BlockSpec((tb, dv), lambda h, b: (b, h)),
                   pl.BlockSpec((None, nch, RET_DK, dv), lambda h, b: (h, b, 0, 0))],
        out_shape=[jax.ShapeDtypeStruct((s, RET_HEADS * dv), F32),
                   jax.ShapeDtypeStruct((s, RET_HEADS * dv), BF16),
                   jax.ShapeDtypeStruct((RET_HEADS, s // CHUNK, RET_DK, dv), BF16)],
        scratch_shapes=[pltpu.VMEM((RET_DK, dv), F32)],
        compiler_params=_cp("parallel", "arbitrary"),
    )(lg, proj, cosr, sinr)


def _ret_bwd(proj, lg, cosr, sinr, o, du, states, dproj, lay):
    s = proj.shape[0]
    dv, w = RET_DV, lay.ret_w
    tb = _rtile(s, 512)
    nb, nch = s // tb, tb // CHUNK
    cb0 = lay.off["ret"] // w

    def body(lg_ref, p_ref, cos_ref, sin_ref, o_ref, du_ref, st_ref, _, dp_ref, dstate):
        @pl.when(pl.program_id(1) == 0)
        def _():
            dstate[...] = jnp.zeros_like(dstate)

        dmat, xi, zeta, decay = _ret_tables(lg_ref[pl.program_id(0)])
        for c in reversed(range(nch)):
            rows = slice(c * CHUNK, (c + 1) * CHUNK)
            cs, sn = cos_ref[rows, :], sin_ref[rows, :]
            q, k, v, g = _ret_qkvg(p_ref[rows, :], cs, sn)
            qb, kb, vb = q.astype(BF16), k.astype(BF16), v.astype(BF16)
            n, rstd = _group_norm(o_ref[rows, :])
            sg = _sigmoid(g)
            duv = du_ref[rows, :]
            dn = duv * (g * sg)
            dg = duv * n * (sg + g * sg * (1.0 - sg))
            do = rstd * (dn - jnp.mean(dn, axis=-1, keepdims=True) - n * jnp.mean(dn * n, axis=-1, keepdims=True))
            dob = do.astype(BF16)
            rb = st_ref[c]
            drb = dstate[...].astype(BF16)
            sc = (_dot_nt(qb, kb) * dmat).astype(BF16)
            dsc = (_dot_nt(dob, vb) * dmat).astype(BF16)
            qx = (q * xi).astype(BF16)
            kz = (k * zeta).astype(BF16)
            dq = _dot(dsc, kb) + _dot_nt(dob, rb) * xi
            dk = (_dot_tn(dsc, qb) + _dot_nt(vb, drb) * zeta) * (RET_DK ** -0.5)
            dvv = _dot_tn(sc, dob) + _dot(kz, drb)
            dstate[...] = dstate[...] * decay + _dot_tn(qx, dob)
            dp_ref[rows, :RET_DK] = (dq * cs + _roll(dq * sn, RET_DK // 2)).astype(BF16)
            dp_ref[rows, RET_DK:2 * RET_DK] = (dk * cs + _roll(dk * sn, RET_DK // 2)).astype(BF16)
            dp_ref[rows, 2 * RET_DK:2 * RET_DK + dv] = dvv.astype(BF16)
            dp_ref[rows, 2 * RET_DK + dv:] = dg.astype(BF16)

    rev = lambda h, b: (nb - 1 - b, h)
    return pl.pallas_call(
        body, name="ret_bwd", grid=(RET_HEADS, nb),
        in_specs=[pl.BlockSpec(memory_space=pltpu.SMEM),
                  pl.BlockSpec((tb, w), lambda h, b: (nb - 1 - b, cb0 + h)),
                  pl.BlockSpec((tb, RET_DK), lambda h, b: (nb - 1 - b, 0)),
                  pl.BlockSpec((tb, RET_DK), lambda h, b: (nb - 1 - b, 0)),
                  pl.BlockSpec((tb, dv), rev),
                  pl.BlockSpec((tb, dv), rev),
                  pl.BlockSpec((None, nch, RET_DK, dv), lambda h, b: (h, nb - 1 - b, 0, 0)),
                  ANY],
        out_specs=pl.BlockSpec((tb, w), lambda h, b: (nb - 1 - b, cb0 + h)),
        out_shape=jax.ShapeDtypeStruct((s, lay.total), BF16),
        input_output_aliases={7: 0},
        scratch_shapes=[pltpu.VMEM((RET_DK, dv), F32)],
        compiler_params=_cp("parallel", "arbitrary"),
    )(lg, proj, cosr, sinr, o, du, states, dproj)


ATTN_SPLIT = 1


def _attn_mask(rows, cols, row0, col0, keys_on_rows):
    ri = (lax.broadcasted_iota(jnp.int32, (rows, cols), 0) + row0) // CHUNK
    ci = (lax.broadcasted_iota(jnp.int32, (rows, cols), 1) + col0) // CHUNK
    return ri <= ci if keys_on_rows else ci <= ri


def _attn_fwd(q_cat, k_cat, v, proj, lay):
    s = q_cat.shape[0]
    bq = _rtile(s, 512)
    nq = s // bq
    hb = bq // ATTN_SPLIT
    mg_cb = lay.off["mg"] // MLA_DV

    def body(q_ref, k_ref, v_ref, g_ref, o_ref, u_ref, lse_ref):
        i = pl.program_id(1)
        qs = [q_ref[a * hb:(a + 1) * hb, :] for a in range(ATTN_SPLIT)]

        def step(j, carry, masked):
            r0 = pl.multiple_of(j * bq, bq)
            out = []
            for a in range(ATTN_SPLIT):
                m, l, acc = carry[a]
                nkeys = (a + 1) * hb if masked else bq
                sc = _dot_nt(qs[a], k_ref[pl.ds(r0, nkeys), :])
                if masked:
                    sc = jnp.where(_attn_mask(hb, nkeys, a * hb, 0, False), sc, NEG_INF)
                mn = jnp.maximum(m, jnp.max(sc, axis=-1, keepdims=True))
                p = jnp.exp2(sc - mn)
                alpha = jnp.exp2(m - mn)
                l = alpha * l + jnp.sum(p, axis=-1, keepdims=True)
                acc = alpha * acc + _dot(p.astype(BF16), v_ref[pl.ds(r0, nkeys), :])
                out.append((mn, l, acc))
            return tuple(out)

        init = tuple((jnp.full((hb, 1), NEG_INF, F32), jnp.zeros((hb, 1), F32), jnp.zeros((hb, MLA_DV), F32))
                     for _ in range(ATTN_SPLIT))
        carry = lax.fori_loop(0, i, lambda j, c: step(j, c, False), init)
        carry = step(i, carry, True)
        for a, (m, l, acc) in enumerate(carry):
            rows = slice(a * hb, (a + 1) * hb)
            o = acc / l
            gv = g_ref[rows, :]
            o_ref[rows, :] = o
            u_ref[rows, :] = (o * (gv * _sigmoid(gv))).astype(BF16)
            lse_ref[rows, :] = m + jnp.log2(l)

    return pl.pallas_call(
        body, name="attn_fwd", grid=(MLA_HEADS, nq),
        in_specs=[pl.BlockSpec((bq, MLA_QW), lambda h, i: (i, h)),
                  pl.BlockSpec((s, MLA_QW), lambda h, i: (0, h)),
                  pl.BlockSpec((s, MLA_DV), lambda h, i: (0, h)),
                  pl.BlockSpec((bq, MLA_DV), lambda h, i: (i, mg_cb + h))],
        out_specs=[pl.BlockSpec((bq, MLA_DV), lambda h, i: (i, h)),
                   pl.BlockSpec((bq, MLA_DV), lambda h, i: (i, h)),
                   pl.BlockSpec((None, bq, 1), lambda h, i: (h, i, 0))],
        out_shape=[jax.ShapeDtypeStruct((s, MLA_HEADS * MLA_DV), F32),
                   jax.ShapeDtypeStruct((s, MLA_HEADS * MLA_DV), BF16),
                   jax.ShapeDtypeStruct((MLA_HEADS, s, 1), F32)],
        compiler_params=_cp("parallel", "parallel"),
    )(q_cat, k_cat, v, proj)


def _attn_dq(q_cat, k_cat, v, do, o, lse, ta, tb, tc):
    s = q_cat.shape[0]
    bq = _rtile(s, 512)
    nq = s // bq
    hb = bq // ATTN_SPLIT

    def body(q_ref, k_ref, v_ref, do_ref, o_ref, lse_ref, a_ref, b_ref, c_ref, dq_ref, dl_ref):
        i = pl.program_id(1)
        halves = []
        for a in range(ATTN_SPLIT):
            rows = slice(a * hb, (a + 1) * hb)
            dob = do_ref[rows, :]
            delta = jnp.sum(dob.astype(F32) * o_ref[rows, :], axis=-1, keepdims=True)
            dl_ref[rows, :] = delta
            halves.append((q_ref[rows, :], dob, lse_ref[rows, :], delta))

        def step(j, dqs, masked):
            r0 = pl.multiple_of(j * bq, bq)
            out = []
            for a, (q, dob, lse, delta) in enumerate(halves):
                nkeys = (a + 1) * hb if masked else bq
                kb, vb = k_ref[pl.ds(r0, nkeys), :], v_ref[pl.ds(r0, nkeys), :]
                sc = _dot_nt(q, kb)
                if masked:
                    sc = jnp.where(_attn_mask(hb, nkeys, a * hb, 0, False), sc, NEG_INF)
                ds = jnp.exp2(sc - lse) * (_dot_nt(dob, vb) - delta)
                out.append(dqs[a] + _dot(ds.astype(BF16), kb))
            return tuple(out)

        dqs = lax.fori_loop(0, i, lambda j, c: step(j, c, False),
                            tuple(jnp.zeros((hb, MLA_QW), F32) for _ in range(ATTN_SPLIT)))
        dqs = step(i, dqs, True)
        for a, dq in enumerate(dqs):
            rows = slice(a * hb, (a + 1) * hb)
            dq = dq * QK_SCALE
            dq_ref[rows, :MLA_NOPE] = dq[:, :MLA_NOPE].astype(BF16)
            dq_ref[rows, MLA_NOPE:] = _rope_tile_bwd(dq[:, MLA_NOPE:], a_ref[rows, :], b_ref[rows, :],
                                                     c_ref[rows, :]).astype(BF16)

    tab = pl.BlockSpec((bq, LANES), lambda h, i: (i, 0))
    return pl.pallas_call(
        body, name="attn_dq", grid=(MLA_HEADS, nq),
        in_specs=[pl.BlockSpec((bq, MLA_QW), lambda h, i: (i, h)),
                  pl.BlockSpec((s, MLA_QW), lambda h, i: (0, h)),
                  pl.BlockSpec((s, MLA_DV), lambda h, i: (0, h)),
                  pl.BlockSpec((bq, MLA_DV), lambda h, i: (i, h)),
                  pl.BlockSpec((bq, MLA_DV), lambda h, i: (i, h)),
                  pl.BlockSpec((None, bq, 1), lambda h, i: (h, i, 0)),
                  tab, tab, tab],
        out_specs=[pl.BlockSpec((bq, MLA_QW), lambda h, i: (i, h)),
                   pl.BlockSpec((None, bq, 1), lambda h, i: (h, i, 0))],
        out_shape=[jax.ShapeDtypeStruct((s, MLA_HEADS * MLA_QW), BF16),
                   jax.ShapeDtypeStruct((MLA_HEADS, s, 1), F32)],
        compiler_params=_cp("parallel", "parallel"),
    )(q_cat, k_cat, v, do, o, lse, ta, tb, tc)


def _attn_dkv(q_cat, k_cat, v, do, lse, delta):
    s = q_cat.shape[0]
    bk = _rtile(s, 512)
    nk = s // bk
    hb = bk // ATTN_SPLIT
    lse = lse.reshape(MLA_HEADS, nk, 1, bk)
    delta = delta.reshape(MLA_HEADS, nk, 1, bk)

    def body(q_ref, k_ref, v_ref, do_ref, lse_ref, dl_ref, dk_ref, dv_ref):
        j = pl.program_id(1)
        halves = [(k_ref[a * hb:(a + 1) * hb, :], v_ref[a * hb:(a + 1) * hb, :]) for a in range(ATTN_SPLIT)]

        def step(i, carry, masked):
            r0 = pl.multiple_of(i * bk, bk)
            out = []
            for a, (kb, vb) in enumerate(halves):
                dk, dvv = carry[a]
                c0 = a * hb if masked else 0
                q, dob = q_ref[pl.ds(r0 + c0, bk - c0), :], do_ref[pl.ds(r0 + c0, bk - c0), :]
                sc = _dot_nt(kb, q)
                if masked:
                    sc = jnp.where(_attn_mask(hb, bk - c0, a * hb, c0, True), sc, NEG_INF)
                p = jnp.exp2(sc - lse_ref[i][:, c0:])
                dvv = dvv + _dot(p.astype(BF16), dob)
                ds = p * (_dot_nt(vb, dob) - dl_ref[i][:, c0:])
                out.append((dk + _dot(ds.astype(BF16), q), dvv))
            return tuple(out)

        init = tuple((jnp.zeros((hb, MLA_QW), F32), jnp.zeros((hb, MLA_DV), F32)) for _ in range(ATTN_SPLIT))
        carry = step(j, init, True)
        carry = lax.fori_loop(j + 1, nk, lambda i, c: step(i, c, False), carry)
        for a, (dk, dvv) in enumerate(carry):
            dk_ref[a * hb:(a + 1) * hb, :] = dk * LN2
            dv_ref[a * hb:(a + 1) * hb, :] = dvv

    return pl.pallas_call(
        body, name="attn_dkv", grid=(MLA_HEADS, nk),
        in_specs=[pl.BlockSpec((s, MLA_QW), lambda h, j: (0, h)),
                  pl.BlockSpec((bk, MLA_QW), lambda h, j: (j, h)),
                  pl.BlockSpec((bk, MLA_DV), lambda h, j: (j, h)),
                  pl.BlockSpec((s, MLA_DV), lambda h, j: (0, h)),
                  pl.BlockSpec((None, nk, 1, bk), lambda h, j: (h, 0, 0, 0)),
                  pl.BlockSpec((None, nk, 1, bk), lambda h, j: (h, 0, 0, 0))],
        out_specs=[pl.BlockSpec((bk, MLA_QW), lambda h, j: (j, h)),
                   pl.BlockSpec((bk, MLA_DV), lambda h, j: (j, h))],
        out_shape=[jax.ShapeDtypeStruct((s, MLA_HEADS * MLA_QW), F32),
                   jax.ShapeDtypeStruct((s, MLA_HEADS * MLA_DV), F32)],
        compiler_params=_cp("parallel", "parallel"),
    )(q_cat, k_cat, v, do, lse, delta)


def _place():
    return lax.axis_index("x"), lax.axis_index("y"), lax.axis_index("c")


def _slot(px, py, pc):
    return 4 * px + 2 * py + pc


def _all_gather(shards, layer, name, vmem=False):
    n = len(shards)

    def body(*refs):
        srcs, outs = refs[:n], refs[n:2 * n]
        send_sems, recv_sems, local_sems = refs[2 * n:]
        x, y, c = _place()
        me, sibling = (x, y, c), (x, y, 1 - c)
        chips = [(1 - x, y), (x, 1 - y), (1 - x, 1 - y)]
        firsts, passes, locals_ = [], [], []

        def copy(a, k, block, to, src=None):
            dst = outs[a].at[_slot(*block)]
            return pltpu.make_async_remote_copy(
                src_ref=dst if src is None else src, dst_ref=dst,
                send_sem=send_sems.at[7 * a + k], recv_sem=recv_sems.at[7 * a + k],
                device_id=to, device_id_type=MESH)

        for a in range(n):
            src = srcs[a] if layer is None else srcs[a].at[layer]
            mine = pltpu.make_async_copy(src, outs[a].at[_slot(*me)], local_sems.at[a])
            mine.start()
            locals_.append(mine)
            first = [copy(a, 0, me, sibling, src=src)]
            first += [copy(a, 1 + j, me, (*chip, c), src=src) for j, chip in enumerate(chips)]
            for cp in first:
                cp.start()
            firsts += first
        for a in range(n):
            for j, chip in enumerate(chips):
                copy(a, 1 + j, (*chip, c), me).wait_recv()
                fwd = copy(a, 4 + j, (*chip, c), sibling)
                fwd.start()
                passes.append(fwd)
        for a in range(n):
            copy(a, 0, sibling, me).wait_recv()
            for j, chip in enumerate(chips):
                copy(a, 4 + j, (*chip, 1 - c), me).wait_recv()
        for cp in firsts + passes:
            cp.wait_send()
        for mine in locals_:
            mine.wait()

    space = pl.BlockSpec(memory_space=pltpu.VMEM) if vmem else ANY
    out_shape = [jax.ShapeDtypeStruct((N_DEV,) + (a.shape if layer is None else a.shape[1:]), a.dtype) for a in shards]
    return pl.pallas_call(
        body, name=name,
        in_specs=[space] * n, out_specs=[space] * n, out_shape=out_shape,
        scratch_shapes=[pltpu.SemaphoreType.DMA((7 * n,)), pltpu.SemaphoreType.DMA((7 * n,)),
                        pltpu.SemaphoreType.DMA((n,))],
        compiler_params=pltpu.CompilerParams(has_side_effects=True),
    )(*shards)


def _exchange_grads(parts, name):
    n = len(parts)

    def body(*refs):
        srcs, outs = refs[:n], refs[n:2 * n]
        send_sems, recv_sems, local_sems = refs[2 * n:]
        x, y, c = _place()
        me = _slot(x, y, c)
        copies = []
        for a in range(n):
            mine = pltpu.make_async_copy(srcs[a].at[me], outs[a].at[me], local_sems.at[a])
            mine.start()
            copies.append(mine)
            for r in range(1, N_DEV):
                px = 1 - x if r & 4 else x
                py = 1 - y if r & 2 else y
                pc = 1 - c if r & 1 else c
                cp = pltpu.make_async_remote_copy(
                    src_ref=srcs[a].at[_slot(px, py, pc)], dst_ref=outs[a].at[me],
                    send_sem=send_sems.at[7 * a + r - 1], recv_sem=recv_sems.at[7 * a + r - 1],
                    device_id=(px, py, pc), device_id_type=MESH)
                cp.start()
                copies.append(cp)
        for cp in copies:
            cp.wait()

    return pl.pallas_call(
        body, name=name,
        in_specs=[ANY] * n, out_specs=[ANY] * n,
        out_shape=[jax.ShapeDtypeStruct(a.shape, a.dtype) for a in parts],
        scratch_shapes=[pltpu.SemaphoreType.DMA((7 * n,)), pltpu.SemaphoreType.DMA((7 * n,)),
                        pltpu.SemaphoreType.DMA((n,))],
        compiler_params=pltpu.CompilerParams(has_side_effects=True),
    )(*parts)


HBM = pl.BlockSpec(memory_space=pltpu.HBM)
SEM = pl.BlockSpec(memory_space=pltpu.SEMAPHORE)
EFFECT = pltpu.SideEffectType.DATAFLOW_SIDE_EFFECTING


def _push_copies(srcs, lands, send_sems, recv_sems, local_sems, by_peer):
    x, y, c = _place()
    me = _slot(x, y, c)
    local, remote = [], []
    for a, (src, land) in enumerate(zip(srcs, lands)):
        local.append(pltpu.make_async_copy(src.at[me] if by_peer else src, land.at[me], local_sems.at[a]))
        for r in range(1, N_DEV):
            peer = (1 - x if r & 4 else x, 1 - y if r & 2 else y, 1 - c if r & 1 else c)
            remote.append(pltpu.make_async_remote_copy(
                src_ref=src.at[_slot(*peer)] if by_peer else src, dst_ref=land.at[me],
                send_sem=send_sems.at[7 * a + r - 1], recv_sem=recv_sems.at[7 * a + r - 1],
                device_id=peer, device_id_type=MESH))
    return local, remote


def _push_start(srcs, by_peer, after, name):
    n = len(srcs)
    srcs = [pltpu.with_memory_space_constraint(a, pltpu.HBM) for a in srcs]
    lands = [pltpu.with_memory_space_constraint(
        lax.empty((N_DEV,) + (a.shape[1:] if by_peer else a.shape), a.dtype), pltpu.HBM) for a in srcs]

    def body(*refs):
        k = 2 * n + len(after)
        local, remote = _push_copies(refs[:n], refs[n:2 * n], refs[k], refs[k + 1], refs[k + 2], by_peer)
        for cp in local + remote:
            cp.start()
        token = refs[k + 3 + 2 * n]
        token[...] = jnp.zeros_like(token)

    outs = pl.pallas_call(
        body, name=name,
        out_shape=(pltpu.SemaphoreType.DMA((7 * n,)), pltpu.SemaphoreType.DMA((7 * n,)),
                   pltpu.SemaphoreType.DMA((n,)),
                   *[pltpu.HBM(a.shape, a.dtype) for a in srcs], *[pltpu.HBM(a.shape, a.dtype) for a in lands],
                   jax.ShapeDtypeStruct((8, LANES), F32)),
        in_specs=[HBM] * (2 * n) + [ANY] * len(after),
        out_specs=(SEM, SEM, SEM, *([HBM] * (2 * n)), pl.BlockSpec(memory_space=pltpu.VMEM)),
        input_output_aliases={i: 3 + i for i in range(2 * n)},
        compiler_params=pltpu.CompilerParams(has_side_effects=EFFECT),
    )(*srcs, *lands, *after)
    return outs[:3], outs[3:3 + n], outs[3 + n:3 + 2 * n], outs[3 + 2 * n]


def _push_wait(sems, srcs, lands, by_peer, after, name):
    n = len(srcs)

    def body(*refs):
        local, remote = _push_copies(refs[:n], refs[n:2 * n], refs[2 * n], refs[2 * n + 1], refs[2 * n + 2], by_peer)
        for cp in local:
            cp.wait()
        for cp in remote:
            cp.wait_send()
            cp.wait_recv()

    outs = pl.pallas_call(
        body, name=name,
        out_shape=[pltpu.HBM(a.shape, a.dtype) for a in list(srcs) + list(lands)],
        in_specs=[HBM] * (2 * n) + [SEM] * 3 + [ANY] * len(after),
        out_specs=[HBM] * (2 * n),
        input_output_aliases={i: i for i in range(2 * n)},
        compiler_params=pltpu.CompilerParams(has_side_effects=EFFECT),
    )(*srcs, *lands, *sems, *after)
    return outs[n:]


def _adam_math(g, w, m, v):
    m = ADAM_B1 * m + (1.0 - ADAM_B1) * g
    v = ADAM_B2 * v + (1.0 - ADAM_B2) * (g * g)
    m_hat = m / (1.0 - ADAM_B1 ** ADAM_STEP)
    v_hat = v / (1.0 - ADAM_B2 ** ADAM_STEP)
    delta = -ADAM_LR * (m_hat / (jnp.sqrt(v_hat) + ADAM_EPS) + ADAM_WD * w)
    return delta, m, v


def _adam_sharded(recvs, w, m, v, name):
    nl, r, c = w.shape
    tr = _rtile(r, 128)
    nr = r // tr

    def body(*refs):
        g_refs = refs[:nl]
        w_ref, m_ref, v_ref, go_ref, d_ref, mo_ref, vo_ref = refs[nl:]
        layer = pl.program_id(0)
        for l in range(nl):
            @pl.when(layer == l)
            def _(l=l):
                g = g_refs[l][0].astype(F32)
                for i in range(1, N_DEV):
                    g = g + g_refs[l][i].astype(F32)
                delta, mn, vn = _adam_math(g, w_ref[...], m_ref[...], v_ref[...])
                go_ref[...] = g
                d_ref[...] = delta
                mo_ref[...] = mn
                vo_ref[...] = vn

    def recv_spec(l):
        def index(layer, i):
            return 0, jnp.where(layer == l, i, jnp.where(layer < l, 0, nr - 1)), 0
        return pl.BlockSpec((N_DEV, tr, c), index)

    blk = pl.BlockSpec((None, tr, c), lambda layer, i: (layer, i, 0))
    out = jax.ShapeDtypeStruct(w.shape, F32)
    return pl.pallas_call(
        body, name=name, grid=(nl, nr),
        in_specs=[recv_spec(l) for l in range(nl)] + [blk, blk, blk],
        out_specs=[blk] * 4, out_shape=[out] * 4,
        compiler_params=_cp("arbitrary", "arbitrary"),
    )(*recvs, w, m, v)


def _adam_mod(c_all_t, dmod, w, m, v):
    nl, d, c = w.shape
    tr = _rtile(d, 128)

    def body(ct_ref, dm_ref, w_ref, m_ref, v_ref, go_ref, d_ref, mo_ref, vo_ref):
        ct = ct_ref[...].astype(BF16).astype(F32)
        dm = dm_ref[...].astype(BF16).astype(F32)
        g = ct[:, 0:1] * dm[0:1, :]
        for b in range(1, N_DEV):
            g = g + ct[:, b:b + 1] * dm[b:b + 1, :]
        delta, mn, vn = _adam_math(g, w_ref[...], m_ref[...], v_ref[...])
        go_ref[...] = g
        d_ref[...] = delta
        mo_ref[...] = mn
        vo_ref[...] = vn

    blk = pl.BlockSpec((None, tr, c), lambda layer, i: (layer, i, 0))
    out = jax.ShapeDtypeStruct(w.shape, F32)
    return pl.pallas_call(
        body, name="adam_mod", grid=(nl, d // tr),
        in_specs=[pl.BlockSpec((tr, N_DEV), lambda layer, i: (i, 0)),
                  pl.BlockSpec((None, N_DEV, c), lambda layer, i: (layer, 0, 0)), blk, blk, blk],
        out_specs=[blk] * 4, out_shape=[out] * 4,
        compiler_params=_cp("parallel", "parallel"),
    )(c_all_t, dmod, w, m, v)


def _adam_small(g, w, m, v, name):
    def body(g_ref, w_ref, m_ref, v_ref, d_ref, mo_ref, vo_ref):
        delta, mn, vn = _adam_math(g_ref[...], w_ref[...], m_ref[...], v_ref[...])
        d_ref[...] = delta
        mo_ref[...] = mn
        vo_ref[...] = vn

    out = jax.ShapeDtypeStruct(w.shape, F32)
    return pl.pallas_call(body, name=name, out_shape=[out] * 3)(g, w, m, v)


def _sum_devices(parts):
    def body(p_ref, o_ref):
        acc = p_ref[0]
        for i in range(1, N_DEV):
            acc = acc + p_ref[i]
        o_ref[...] = acc

    return pl.pallas_call(body, name="sum_devices",
                          out_shape=jax.ShapeDtypeStruct(parts.shape[1:], F32))(parts)


def _rope_tables(positions):
    pos = positions.astype(F32)[:, None]

    def cs(dim):
        inv = 1.0 / (ROPE_BASE ** (jnp.arange(0, dim, 2, dtype=F32) / dim))
        ang = pos * inv
        return jnp.cos(ang), jnp.sin(ang)

    cr, sr = cs(RET_DK)
    cm, sm = cs(MLA_ROPE)
    z = jnp.zeros_like(cm)
    pad = jnp.zeros((pos.shape[0], LANES - MLA_ROPE), F32)
    cosr = jnp.concatenate([cr, cr], axis=1)
    sinr = jnp.concatenate([-sr, sr], axis=1)
    ta = jnp.concatenate([cm, cm, pad], axis=1)
    tb = jnp.concatenate([-sm, z, pad], axis=1)
    tc = jnp.concatenate([z, sm, pad], axis=1)
    return cosr, sinr, ta, tb, tc


def _layer_fwd(x, mod, g_norm, g_cq, g_ckv, wts, tabs, lg, lay, deps):
    d = x.shape[1]
    cosr, sinr, ta, tb, tc = tabs
    shift, scale, gate = mod[:, :d], mod[:, d:2 * d], mod[:, 2 * d:]
    h = _norm_mod_fwd(x, g_norm, scale, shift, deps)
    proj = _matmul(h, wts["in"], name="mm_proj", tn_cap=1920)
    o_ret, u_ret, states = _ret_fwd(proj, lg, cosr, sinr, lay)
    y_ret = _matmul(u_ret, wts["ret"], name="mm_y")
    cqn, ckvn = _mla_prep(proj, g_cq, g_ckv, lay)
    qp = _matmul(cqn, wts["uq"], name="mm_up")
    kvp = _matmul(ckvn, wts["ukv"], name="mm_up")
    q_cat, k_cat, v = _qk_prep(qp, kvp, proj, ta, tb, tc, lay)
    o_mla, u_mla, lse = _attn_fwd(q_cat, k_cat, v, proj, lay)
    y_mla = _matmul(u_mla, wts["mla"], name="mm_y")
    merged = _merge_fwd(y_ret, y_mla, proj, lay)
    out = _matmul(merged, wts["out"], name="mm_y")
    x_next = _resid_fwd(x, out, gate)
    saved = dict(x=x, h=h, proj=proj, o_ret=o_ret, u_ret=u_ret, states=states, y_ret=y_ret, cqn=cqn,
                 ckvn=ckvn, q_cat=q_cat, k_cat=k_cat, v=v, o_mla=o_mla, u_mla=u_mla, lse=lse,
                 y_mla=y_mla, merged=merged, out=out)
    return x_next, saved


def _layer_bwd(dxn, sv, mod, g_norm, g_cq, g_ckv, wts, tabs, lg, lay, deps):
    d = dxn.shape[1]
    cosr, sinr, ta, tb, tc = tabs
    scale, gate = mod[:, d:2 * d], mod[:, 2 * d:]
    gdt = BF16
    dout, dgate = _resid_bwd(dxn, sv["out"], gate, deps)
    dmerged = _matmul(dout, wts["out"], tb=True, name="mm_dy")
    dw_out = _matmul(sv["merged"], dout, ta=True, out_dtype=gdt, name="mm_dw")
    dy_ret, dy_mla, dproj = _merge_bwd(dmerged, sv["y_ret"], sv["y_mla"], sv["proj"], lay)
    du_ret = _matmul(dy_ret, wts["ret"], tb=True, name="mm_dy")
    dw_ret = _matmul(sv["u_ret"], dy_ret, ta=True, out_dtype=gdt, name="mm_dw")
    dproj = _ret_bwd(sv["proj"], lg, cosr, sinr, sv["o_ret"], du_ret, sv["states"], dproj, lay)
    du_mla = _matmul(dy_mla, wts["mla"], tb=True, name="mm_dy")
    dw_mla = _matmul(sv["u_mla"], dy_mla, ta=True, out_dtype=gdt, name="mm_dw")
    do_mla, dproj = _mla_gate_bwd(du_mla, sv["o_mla"], sv["proj"], dproj, lay)
    dqp, delta = _attn_dq(sv["q_cat"], sv["k_cat"], sv["v"], do_mla, sv["o_mla"], sv["lse"], ta, tb, tc)
    dk_cat, dv = _attn_dkv(sv["q_cat"], sv["k_cat"], sv["v"], do_mla, sv["lse"], delta)
    dkvp, dproj = _kv_bwd_prep(dk_cat, dv, ta, tb, tc, dproj, lay)
    dcqn = _matmul(dqp, wts["uq"], tb=True, name="mm_dlat")
    dckvn = _matmul(dkvp, wts["ukv"], tb=True, name="mm_dlat")
    dw_uq = _matmul(sv["cqn"], dqp, ta=True, out_dtype=gdt, name="mm_dwup")
    dw_ukv = _matmul(sv["ckvn"], dkvp, ta=True, out_dtype=gdt, name="mm_dwup")
    dproj, dg_cq, dg_ckv = _mla_prep_bwd(sv["proj"], dcqn, dckvn, g_cq, g_ckv, dproj, lay)
    dh = _matmul(dproj, wts["in"], tb=True, name="mm_dh", tk_cap=896)
    dw_in = _matmul(sv["h"], dproj, ta=True, out_dtype=gdt, name="mm_dwin", tn_cap=1920)
    dx, dshift, dscale, dg_norm = _norm_mod_bwd(sv["x"], g_norm, scale, dh, dxn)
    dmod = jnp.concatenate([dshift, dscale, dgate], axis=1)
    big = dict(w_in=dw_in, uq=dw_uq, ukv=dw_ukv, ret=dw_ret, mla=dw_mla, out=dw_out)
    small = dict(dmod=dmod, g_norm=dg_norm, g_cq=dg_cq, g_ckv=dg_ckv)
    return dx, big, small


def _to_owner_blocks_cols(g, n_local):
    k = g.shape[0]
    return g.reshape(k, N_DEV, n_local).transpose(1, 0, 2)


def _from_owner_blocks_cols(g):
    return g.transpose(1, 0, 2).reshape(g.shape[1], -1)


def kernel(x, c, positions, w_mod, b_mod, g_norm, w_in, g_cq, g_ckv, w_uq, w_ukv, w_ret_proj, w_mla_proj, w_out, g_final, loss_target, m_w_mod, m_b_mod, m_g_norm, m_w_in, m_g_cq, m_g_ckv, m_w_uq, m_w_ukv, m_w_ret_proj, m_w_mla_proj, m_w_out, m_g_final, v_w_mod, v_b_mod, v_g_norm, v_w_in, v_g_cq, v_g_ckv, v_w_uq, v_w_ukv, v_w_ret_proj, v_w_mla_proj, v_w_out, v_g_final):
    nl, d, _ = w_mod.shape
    s = x.shape[1]
    rank = g_cq.shape[1]
    lay = Layout(d, rank, g_ckv.shape[1])
    me = _slot(*_place())
    x0 = x.reshape(s, d)
    target = loss_target.reshape(s, d)
    tabs = _rope_tables(positions.reshape(s))
    lg = jnp.log(1.0 - 2.0 ** (-5.0 - jnp.arange(RET_HEADS, dtype=F32)))

    c_act = c * _sigmoid(c)
    (c_all,) = _all_gather([c_act.reshape(d // LANES, LANES)], None, "gather_c", vmem=True)
    c_all = c_all.reshape(N_DEV, d)
    n_mod = w_mod.shape[2]
    mod_part = jnp.stack([_matmul(c_all, w_mod[l], name="mm_mod", tm_cap=8) for l in range(nl)])
    (mod_all,) = _all_gather([mod_part.reshape(-1, LANES)], None, "gather_mod", vmem=True)
    mod_all = mod_all.reshape(N_DEV, nl, N_DEV, n_mod)
    mod = lax.dynamic_index_in_dim(mod_all, me, axis=2, keepdims=False)
    mod = mod.transpose(1, 0, 2).reshape(nl, N_DEV * n_mod) + b_mod

    shards = [[w[l].astype(BF16) for w in (w_in, w_uq, w_ukv, w_ret_proj, w_mla_proj, w_out)] for l in range(nl)]
    xl, saved, wts_all = x0, [], []
    gathered = _all_gather(shards[0], None, "gather_w")
    for l in range(nl):
        g_in, g_uq, g_ukv, g_ret, g_mla, g_out = gathered
        deps = []
        if l + 1 < nl:
            sems, srcs, lands, token = _push_start(shards[l + 1], False, [g_out], "gather_start_%d" % (l + 1))
            deps = [token]
        wts = {
            "in": lay.to_physical(_from_owner_blocks_cols(g_in)),
            "uq": _uq_to_physical(_from_owner_blocks_cols(g_uq)),
            "ukv": _ukv_to_physical(_from_owner_blocks_cols(g_ukv)),
            "ret": g_ret.reshape(-1, d), "mla": g_mla.reshape(-1, d), "out": g_out.reshape(-1, d),
        }
        wts_all.append(wts)
        xl, sv = _layer_fwd(xl, mod[l:l + 1], g_norm[l:l + 1], g_cq[l:l + 1], g_ckv[l:l + 1], wts, tabs, lg, lay,
                            deps)
        saved.append(sv)
        if l + 1 < nl:
            gathered = _push_wait(sems, srcs, lands, False, [xl], "gather_wait_%d" % (l + 1))
    loss_lanes, dx, dg_final = _final_loss(xl, g_final.reshape(1, d), target)

    recv = [None] * nl
    small = [None] * nl
    flying, deps = None, []
    for l in reversed(range(nl)):
        dx, big, small[l] = _layer_bwd(dx, saved[l], mod[l:l + 1], g_norm[l:l + 1], g_cq[l:l + 1],
                                       g_ckv[l:l + 1], wts_all[l], tabs, lg, lay, deps)
        parts = [
            _to_owner_blocks_cols(lay.to_logical(big["w_in"]), w_in.shape[2]),
            _to_owner_blocks_cols(_uq_to_logical(big["uq"]), w_uq.shape[2]),
            _to_owner_blocks_cols(_ukv_to_logical(big["ukv"]), w_ukv.shape[2]),
            big["ret"].reshape(N_DEV, -1, d), big["mla"].reshape(N_DEV, -1, d), big["out"].reshape(N_DEV, -1, d),
        ]
        after = []
        if flying is not None:
            fl, sems, srcs, lands = flying
            recv[fl] = _push_wait(sems, srcs, lands, True, [dx], "exchange_wait_%d" % fl)
            after = [recv[fl][0]]
        if l > 0:
            sems, srcs, lands, token = _push_start(parts, True, after, "exchange_start_%d" % l)
            flying, deps = (l, sems, srcs, lands), [token]
        else:
            recv[l] = _exchange_grads(parts, "exchange_grads")
    grad_x = dx.reshape(x.shape)

    pack = jnp.concatenate(
        [jnp.concatenate([sm[k] for sm in small], axis=0).reshape(-1)
         for k in ("dmod", "g_norm", "g_cq", "g_ckv")] + [dg_final.reshape(-1), loss_lanes.reshape(-1)])
    (pack_all,) = _all_gather([pack.reshape(-1, LANES)], None, "gather_small", vmem=True)
    tot = _sum_devices(pack_all).reshape(-1)
    sizes = [nl * 3 * d, nl * d, nl * rank, nl * rank, d]
    offs = np.cumsum([0] + sizes)
    grad_b_mod = tot[offs[0]:offs[1]].reshape(nl, 3 * d)
    grad_g_norm = tot[offs[1]:offs[2]].reshape(nl, d)
    grad_g_cq = tot[offs[2]:offs[3]].reshape(nl, rank)
    grad_g_ckv = tot[offs[3]:offs[4]].reshape(nl, rank)
    grad_g_final = tot[offs[4]:offs[5]]
    loss = tot[offs[5]]
    dmod_all = pack_all.reshape(N_DEV, -1)[:, :sizes[0]].reshape(N_DEV, nl, 3 * d)
    dmod_mine = lax.dynamic_slice_in_dim(dmod_all, me * n_mod, n_mod, axis=2).transpose(1, 0, 2)

    out = {}
    out["w_mod"] = _adam_mod(c_all.T, dmod_mine, w_mod, m_w_mod, v_w_mod)
    for i, (key, w, m, v) in enumerate((("w_in", w_in, m_w_in, v_w_in), ("w_uq", w_uq, m_w_uq, v_w_uq),
                                        ("w_ukv", w_ukv, m_w_ukv, v_w_ukv),
                                        ("w_ret_proj", w_ret_proj, m_w_ret_proj, v_w_ret_proj),
                                        ("w_mla_proj", w_mla_proj, m_w_mla_proj, v_w_mla_proj),
                                        ("w_out", w_out, m_w_out, v_w_out))):
        out[key] = _adam_sharded([recv[l][i] for l in range(nl)], w, m, v, "adam_" + key)
    for key, g, w, m, v in (("b_mod", grad_b_mod, b_mod, m_b_mod, v_b_mod),
                            ("g_norm", grad_g_norm, g_norm, m_g_norm, v_g_norm),
                            ("g_cq", grad_g_cq, g_cq, m_g_cq, v_g_cq),
                            ("g_ckv", grad_g_ckv, g_ckv, m_g_ckv, v_g_ckv),
                            ("g_final", grad_g_final.reshape(1, d), g_final.reshape(1, d),
                             m_g_final.reshape(1, d), v_g_final.reshape(1, d))):
        out[key] = (g,) + tuple(_adam_small(g, w, m, v, "adam_" + key))
    out["g_final"] = tuple(a.reshape(d) for a in out["g_final"])

    names = ("w_mod", "b_mod", "g_norm", "w_in", "g_cq", "g_ckv", "w_uq", "w_ukv", "w_ret_proj",
             "w_mla_proj", "w_out", "g_final")
    return (loss, grad_x, *[out[k][0] for k in names], *[out[k][1] for k in names],
            *[out[k][2] for k in names], *[out[k][3] for k in names])
```

```python
import functools
import itertools

import jax
import jax.numpy as jnp
import numpy as np
from jax import lax
from jax.experimental import pallas as pl
from jax.experimental.pallas import tpu as pltpu

F32 = jnp.float32
BF16 = jnp.bfloat16

N_DEV = 8
CHUNK = 64
EPS = 1e-6
NEG_INF = -1e30
ROPE_BASE = 10000.0
LANES = 128

RET_HEADS = 8
RET_DK = 128
RET_DV = 256
MLA_HEADS = 16
MLA_NOPE = 128
MLA_ROPE = 64
MLA_DV = 128
MLA_QW = 256
QK_SCALE = (MLA_NOPE + MLA_ROPE) ** -0.5
QK_LOG2_SCALE = QK_SCALE * 1.4426950408889634
LN2 = 0.6931471805599453

ADAM_LR = 0.001
ADAM_B1 = 0.9
ADAM_B2 = 0.999
ADAM_EPS = 1e-08
ADAM_WD = 0.01
ADAM_STEP = 10

VMEM_LIMIT_BYTES = 56 * 1024 * 1024
MESH = pl.DeviceIdType.MESH
ANY = pl.BlockSpec(memory_space=pl.ANY)


def _cp(*sem):
    return pltpu.CompilerParams(dimension_semantics=sem if sem else None,
                                vmem_limit_bytes=VMEM_LIMIT_BYTES)


def _tile(n, cap):
    best = None
    t = LANES
    while t <= min(n, cap):
        if n % t == 0:
            best = t
        t += LANES
    return best if best is not None else n


def _rtile(n, cap):
    t = cap
    while t > 8 and n % t:
        t //= 2
    return t if n % t == 0 else n


def _sigmoid(x):
    return 1.0 / (1.0 + jnp.exp(-x))


def _dot(a, b):
    return lax.dot_general(a, b, (((1,), (0,)), ((), ())), preferred_element_type=F32)


def _dot_nt(a, b):
    return lax.dot_general(a, b, (((1,), (1,)), ((), ())), preferred_element_type=F32)


def _dot_tn(a, b):
    return lax.dot_general(a, b, (((0,), (0,)), ((), ())), preferred_element_type=F32)


def _roll(x, s):
    return pltpu.roll(x, s, 1)


class Layout:
    def __init__(self, d_model, q_rank, kv_rank):
        assert q_rank == kv_rank
        self.d = d_model
        self.rank = q_rank
        self.ret_w = 2 * RET_DK + 2 * RET_DV
        self.ret_qk = RET_HEADS * RET_DK
        self.ret_v = RET_HEADS * RET_DV
        self.mla_v = MLA_HEADS * MLA_DV
        widths = {"bg": 2 * d_model, "mg": self.mla_v, "ret": RET_HEADS * self.ret_w,
                  "cqkv": 2 * q_rank, "kr": LANES}
        blocks = {"bg": 2 * d_model, "mg": self.mla_v, "ret": self.ret_w,
                  "cqkv": 2 * q_rank, "kr": LANES}
        for order in itertools.permutations(widths):
            off, offs, ok = 0, {}, True
            for name in order:
                if off % blocks[name]:
                    ok = False
                    break
                offs[name] = off
                off += widths[name]
            if ok:
                break
        assert ok, "no aligned layout"
        self.order, self.off, self.width, self.total = order, offs, widths, off
        lo, o = {}, 0
        for name, w in (("rq", self.ret_qk), ("rk", self.ret_qk), ("rv", self.ret_v),
                        ("rg", self.ret_v), ("cq", q_rank), ("ckv", kv_rank), ("kr", MLA_ROPE),
                        ("mg", self.mla_v), ("bg", 2 * d_model)):
            lo[name] = (o, w)
            o += w
        self.logical, self.d_in = lo, o

    def pieces(self):
        lo = self.logical
        out = []
        for name in self.order:
            if name == "bg":
                out.append(lo["bg"])
            elif name == "mg":
                out.append(lo["mg"])
            elif name == "ret":
                for h in range(RET_HEADS):
                    out.append((lo["rq"][0] + h * RET_DK, RET_DK))
                    out.append((lo["rk"][0] + h * RET_DK, RET_DK))
                    out.append((lo["rv"][0] + h * RET_DV, RET_DV))
                    out.append((lo["rg"][0] + h * RET_DV, RET_DV))
            elif name == "cqkv":
                out.append((lo["cq"][0], 2 * self.rank))
            elif name == "kr":
                out.append(lo["kr"])
                out.append((None, LANES - MLA_ROPE))
        return out

    def to_physical(self, w):
        parts = []
        for start, width in self.pieces():
            if start is None:
                parts.append(jnp.zeros((w.shape[0], width), w.dtype))
            else:
                parts.append(w[:, start:start + width])
        return jnp.concatenate(parts, axis=1)

    def to_logical(self, w):
        got, off = [], 0
        for start, width in self.pieces():
            if start is not None:
                got.append((start, w[:, off:off + width]))
            off += width
        got.sort(key=lambda t: t[0])
        return jnp.concatenate([p for _, p in got], axis=1)


def _uq_to_physical(w):
    k = w.shape[0]
    w3 = w.reshape(k, MLA_HEADS, MLA_NOPE + MLA_ROPE)
    pad = jnp.zeros((k, MLA_HEADS, MLA_QW - MLA_NOPE - MLA_ROPE), w.dtype)
    return jnp.concatenate([w3, pad], axis=2).reshape(k, MLA_HEADS * MLA_QW)


def _uq_to_logical(w):
    k = w.shape[0]
    return w.reshape(k, MLA_HEADS, MLA_QW)[:, :, :MLA_NOPE + MLA_ROPE].reshape(k, -1)


def _ukv_to_physical(w):
    k = w.shape[0]
    w3 = w.reshape(k, MLA_HEADS, MLA_NOPE + MLA_DV)
    return jnp.concatenate([w3[:, :, :MLA_NOPE].reshape(k, -1), w3[:, :, MLA_NOPE:].reshape(k, -1)], axis=1)


def _ukv_to_logical(w):
    k = w.shape[0]
    kn = w[:, :MLA_HEADS * MLA_NOPE].reshape(k, MLA_HEADS, MLA_NOPE)
    v = w[:, MLA_HEADS * MLA_NOPE:].reshape(k, MLA_HEADS, MLA_DV)
    return jnp.concatenate([kn, v], axis=2).reshape(k, -1)


def _matmul(a, b, *, ta=False, tb=False, out_dtype=F32, name, tm_cap=1024, tn_cap=1024, tk_cap=512, deps=()):
    m, k = (a.shape[1], a.shape[0]) if ta else a.shape
    n = b.shape[0] if tb else b.shape[1]
    assert k == (b.shape[1] if tb else b.shape[0])
    tm, tn, tk = _tile(m, tm_cap), _tile(n, tn_cap), _tile(k, tk_cap)
    nk = k // tk

    def body(a_ref, b_ref, *rest):
        o_ref, acc_ref = rest[len(deps):]
        kk = pl.program_id(2)

        @pl.when(kk == 0)
        def _():
            acc_ref[...] = jnp.zeros_like(acc_ref)

        dims = (((0 if ta else 1,), (1 if tb else 0,)), ((), ()))
        acc_ref[...] += lax.dot_general(a_ref[...].astype(BF16), b_ref[...].astype(BF16), dims,
                                        preferred_element_type=F32)

        @pl.when(kk == nk - 1)
        def _():
            o_ref[...] = acc_ref[...].astype(o_ref.dtype)

    a_spec = pl.BlockSpec((tk, tm), lambda i, j, kk: (kk, i)) if ta else pl.BlockSpec((tm, tk), lambda i, j, kk: (i, kk))
    b_spec = pl.BlockSpec((tn, tk), lambda i, j, kk: (j, kk)) if tb else pl.BlockSpec((tk, tn), lambda i, j, kk: (kk, j))
    return pl.pallas_call(
        body, name=name, grid=(m // tm, n // tn, nk),
        in_specs=[a_spec, b_spec] + [ANY] * len(deps),
        out_specs=pl.BlockSpec((tm, tn), lambda i, j, kk: (i, j)),
        out_shape=jax.ShapeDtypeStruct((m, n), out_dtype),
        scratch_shapes=[pltpu.VMEM((tm, tn), F32)],
        compiler_params=_cp("parallel", "parallel", "arbitrary"),
    )(a, b, *deps)


def _row(tm, w, cb=0):
    return pl.BlockSpec((tm, w), lambda i, cb=cb: (i, cb))


def _vec(w, cb=0):
    return pl.BlockSpec((1, w), lambda i, cb=cb: (0, cb))


def _dproj_out(dproj, s, lay, tm, name):
    w = lay.width[name]
    cb = lay.off[name] // w
    spec = _row(tm, w, cb)
    shape = jax.ShapeDtypeStruct((s, lay.total), BF16)
    return spec, shape


def _norm_mod_fwd(x, g, scale, shift, deps):
    s, d = x.shape
    tm = _rtile(s, 256)

    def body(x_ref, g_ref, sc_ref, sh_ref, *rest):
        h_ref = rest[len(deps)]
        xv = x_ref[...]
        xh = xv * lax.rsqrt(jnp.mean(xv * xv, axis=-1, keepdims=True) + EPS)
        h_ref[...] = ((xh * g_ref[...]) * (1.0 + sc_ref[...]) + sh_ref[...]).astype(BF16)

    return pl.pallas_call(
        body, name="norm_mod_fwd", grid=(s // tm,),
        in_specs=[_row(tm, d), _vec(d), _vec(d), _vec(d)] + [ANY] * len(deps),
        out_specs=_row(tm, d), out_shape=jax.ShapeDtypeStruct((s, d), BF16),
        compiler_params=_cp("parallel"),
    )(x, g, scale, shift, *deps)


def _norm_mod_bwd(x, g, scale, dh, dres):
    s, d = x.shape
    tm = _rtile(s, 256)

    def body(x_ref, g_ref, sc_ref, dh_ref, dres_ref, dx_ref, dsh_ref, dsc_ref, dg_ref):
        @pl.when(pl.program_id(0) == 0)
        def _():
            dsh_ref[...] = jnp.zeros_like(dsh_ref)
            dsc_ref[...] = jnp.zeros_like(dsc_ref)
            dg_ref[...] = jnp.zeros_like(dg_ref)

        xv, gv, dhv = x_ref[...], g_ref[...], dh_ref[...]
        rstd = lax.rsqrt(jnp.mean(xv * xv, axis=-1, keepdims=True) + EPS)
        xh = xv * rstd
        dy = dhv * (1.0 + sc_ref[...])
        dxh = dy * gv
        dx_ref[...] = dres_ref[...] + rstd * (dxh - xh * jnp.mean(dxh * xh, axis=-1, keepdims=True))
        dsh_ref[...] += jnp.sum(dhv, axis=0, keepdims=True)
        dsc_ref[...] += jnp.sum(dhv * (xh * gv), axis=0, keepdims=True)
        dg_ref[...] += jnp.sum(dy * xh, axis=0, keepdims=True)

    vec = jax.ShapeDtypeStruct((1, d), F32)
    return pl.pallas_call(
        body, name="norm_mod_bwd", grid=(s // tm,),
        in_specs=[_row(tm, d), _vec(d), _vec(d), _row(tm, d), _row(tm, d)],
        out_specs=[_row(tm, d), _vec(d), _vec(d), _vec(d)],
        out_shape=[jax.ShapeDtypeStruct((s, d), F32), vec, vec, vec],
        compiler_params=_cp("arbitrary"),
    )(x, g, scale, dh, dres)


def _final_loss(x, g, target):
    s, d = x.shape
    tm = _rtile(s, 256)

    def body(x_ref, g_ref, t_ref, l_ref, dx_ref, dg_ref):
        @pl.when(pl.program_id(0) == 0)
        def _():
            l_ref[...] = jnp.zeros_like(l_ref)
            dg_ref[...] = jnp.zeros_like(dg_ref)

        xv, gv = x_ref[...], g_ref[...]
        rstd = lax.rsqrt(jnp.mean(xv * xv, axis=-1, keepdims=True) + EPS)
        xh = xv * rstd
        err = xh * gv - t_ref[...]
        row = jnp.mean(err * err, axis=-1, keepdims=True)
        l_ref[...] += 0.5 * jnp.sum(row, axis=0, keepdims=True)
        dy = err / d
        dxh = dy * gv
        dx_ref[...] = rstd * (dxh - xh * jnp.mean(dxh * xh, axis=-1, keepdims=True))
        dg_ref[...] += jnp.sum(dy * xh, axis=0, keepdims=True)

    return pl.pallas_call(
        body, name="final_loss", grid=(s // tm,),
        in_specs=[_row(tm, d), _vec(d), _row(tm, d)],
        out_specs=[_vec(LANES), _row(tm, d), _vec(d)],
        out_shape=[jax.ShapeDtypeStruct((1, LANES), F32), jax.ShapeDtypeStruct((s, d), F32),
                   jax.ShapeDtypeStruct((1, d), F32)],
        compiler_params=_cp("arbitrary"),
    )(x, g, target)


def _resid_fwd(x, out, gate):
    s, d = x.shape
    tm = _rtile(s, 256)

    def body(x_ref, o_ref, g_ref, y_ref):
        y_ref[...] = x_ref[...] + g_ref[...] * o_ref[...]

    return pl.pallas_call(
        body, name="resid_fwd", grid=(s // tm,),
        in_specs=[_row(tm, d), _row(tm, d), _vec(d)],
        out_specs=_row(tm, d), out_shape=jax.ShapeDtypeStruct((s, d), F32),
        compiler_params=_cp("parallel"),
    )(x, out, gate)


def _resid_bwd(dxn, out, gate, deps):
    s, d = dxn.shape
    tm = _rtile(s, 256)

    def body(dx_ref, o_ref, g_ref, *rest):
        do_ref, dg_ref = rest[len(deps):]

        @pl.when(pl.program_id(0) == 0)
        def _():
            dg_ref[...] = jnp.zeros_like(dg_ref)

        dxv = dx_ref[...]
        do_ref[...] = (dxv * g_ref[...]).astype(BF16)
        dg_ref[...] += jnp.sum(dxv * o_ref[...], axis=0, keepdims=True)

    return pl.pallas_call(
        body, name="resid_bwd", grid=(s // tm,),
        in_specs=[_row(tm, d), _row(tm, d), _vec(d)] + [ANY] * len(deps),
        out_specs=[_row(tm, d), _vec(d)],
        out_shape=[jax.ShapeDtypeStruct((s, d), BF16), jax.ShapeDtypeStruct((1, d), F32)],
        compiler_params=_cp("arbitrary"),
    )(dxn, out, gate, *deps)


def _merge_fwd(y_ret, y_mla, proj, lay):
    s, d = y_ret.shape
    tm = _rtile(s, 256)
    cb = lay.off["bg"] // (2 * d)

    def body(a_ref, b_ref, bg_ref, m_ref):
        sg = _sigmoid(bg_ref[...])
        m_ref[...] = (sg[:, :d] * a_ref[...] + sg[:, d:] * b_ref[...]).astype(BF16)

    return pl.pallas_call(
        body, name="merge_fwd", grid=(s // tm,),
        in_specs=[_row(tm, d), _row(tm, d), _row(tm, 2 * d, cb)],
        out_specs=_row(tm, d), out_shape=jax.ShapeDtypeStruct((s, d), BF16),
        compiler_params=_cp("parallel"),
    )(y_ret, y_mla, proj)


def _merge_bwd(dm, y_ret, y_mla, proj, lay):
    s, d = dm.shape
    tm = _rtile(s, 256)
    cb = lay.off["bg"] // (2 * d)
    dp_spec, dp_shape = _dproj_out(None, s, lay, tm, "bg")

    def body(dm_ref, a_ref, b_ref, bg_ref, da_ref, db_ref, dp_ref):
        sg = _sigmoid(bg_ref[...])
        dmv = dm_ref[...]
        ga, gb = sg[:, :d], sg[:, d:]
        da_ref[...] = (dmv * ga).astype(BF16)
        db_ref[...] = (dmv * gb).astype(BF16)
        dp_ref[:, :d] = (dmv * a_ref[...] * ga * (1.0 - ga)).astype(BF16)
        dp_ref[:, d:] = (dmv * b_ref[...] * gb * (1.0 - gb)).astype(BF16)

    act = jax.ShapeDtypeStruct((s, d), BF16)
    return pl.pallas_call(
        body, name="merge_bwd", grid=(s // tm,),
        in_specs=[_row(tm, d), _row(tm, d), _row(tm, d), _row(tm, 2 * d, cb)],
        out_specs=[_row(tm, d), _row(tm, d), dp_spec],
        out_shape=[act, act, dp_shape],
        compiler_params=_cp("parallel"),
    )(dm, y_ret, y_mla, proj)


def _mla_prep(proj, g_cq, g_ckv, lay):
    s = proj.shape[0]
    r = lay.rank
    tm = _rtile(s, 512)
    cb = lay.off["cqkv"] // (2 * r)

    def body(p_ref, gq_ref, gk_ref, q_ref, k_ref):
        pv = p_ref[...]
        for lo, g_ref, o_ref in ((0, gq_ref, q_ref), (r, gk_ref, k_ref)):
            xv = pv[:, lo:lo + r]
            xh = xv * lax.rsqrt(jnp.mean(xv * xv, axis=-1, keepdims=True) + EPS)
            o_ref[...] = (xh * g_ref[...]).astype(BF16)

    act = jax.ShapeDtypeStruct((s, r), BF16)
    return pl.pallas_call(
        body, name="mla_prep", grid=(s // tm,),
        in_specs=[_row(tm, 2 * r, cb), _vec(r), _vec(r)],
        out_specs=[_row(tm, r), _row(tm, r)], out_shape=[act, act],
        compiler_params=_cp("parallel"),
    )(proj, g_cq, g_ckv)


def _mla_prep_bwd(proj, dqn, dkn, g_cq, g_ckv, dproj, lay):
    s = proj.shape[0]
    r = lay.rank
    tm = _rtile(s, 512)
    cb = lay.off["cqkv"] // (2 * r)
    dp_spec, dp_shape = _dproj_out(dproj, s, lay, tm, "cqkv")

    def body(p_ref, dq_ref, dk_ref, gq_ref, gk_ref, _, dp_ref, dgq_ref, dgk_ref):
        @pl.when(pl.program_id(0) == 0)
        def _():
            dgq_ref[...] = jnp.zeros_like(dgq_ref)
            dgk_ref[...] = jnp.zeros_like(dgk_ref)

        pv = p_ref[...]
        for lo, g_ref, d_ref, dg_ref in ((0, gq_ref, dq_ref, dgq_ref), (r, gk_ref, dk_ref, dgk_ref)):
            xv = pv[:, lo:lo + r]
            rstd = lax.rsqrt(jnp.mean(xv * xv, axis=-1, keepdims=True) + EPS)
            xh = xv * rstd
            dy = d_ref[...]
            dxh = dy * g_ref[...]
            dp_ref[:, lo:lo + r] = (rstd * (dxh - xh * jnp.mean(dxh * xh, axis=-1, keepdims=True))).astype(BF16)
            dg_ref[...] += jnp.sum(dy * xh, axis=0, keepdims=True)

    vec = jax.ShapeDtypeStruct((1, r), F32)
    return pl.pallas_call(
        body, name="mla_prep_bwd", grid=(s // tm,),
        in_specs=[_row(tm, 2 * r, cb), _row(tm, r), _row(tm, r), _vec(r), _vec(r), ANY],
        out_specs=[dp_spec, _vec(r), _vec(r)], out_shape=[dp_shape, vec, vec],
        input_output_aliases={5: 0},
        compiler_params=_cp("arbitrary"),
    )(proj, dqn, dkn, g_cq, g_ckv, dproj)


def _rope_tile(t, a, b, c):
    return t * a + _roll(t, 96) * b + _roll(t, 32) * c


def _rope_tile_bwd(dy, a, b, c):
    return dy * a + _roll(dy * b, 32) + _roll(dy * c, 96)


def _attn_block(s):
    return _rtile(s, 512)


def _qk_prep(qp, kvp, proj, ta, tb, tc, lay):
    s = qp.shape[0]
    hq = MLA_HEADS * MLA_QW
    hv = MLA_HEADS * MLA_DV
    blk = _attn_block(s)
    tm = _rtile(blk, 256)
    per = blk // tm
    kr_cb = lay.off["kr"] // LANES

    def body(q_ref, kv_ref, kr_ref, a_ref, b_ref, c_ref, qc_ref, kc_ref, v_ref, kt_ref, vt_ref):
        a, b, c = a_ref[...], b_ref[...], c_ref[...]
        krot = _rope_tile(kr_ref[...], a, b, c)
        krot_b, krot_t = krot.astype(BF16), krot.T.astype(BF16)
        for h in range(MLA_HEADS):
            q0 = h * MLA_QW
            qc_ref[:, q0:q0 + MLA_NOPE] = (q_ref[:, q0:q0 + MLA_NOPE] * QK_LOG2_SCALE).astype(BF16)
            qc_ref[:, q0 + MLA_NOPE:q0 + MLA_QW] = (
                _rope_tile(q_ref[:, q0 + MLA_NOPE:q0 + MLA_QW], a, b, c) * QK_LOG2_SCALE).astype(BF16)
            kn = kv_ref[:, h * MLA_NOPE:(h + 1) * MLA_NOPE]
            kc_ref[:, q0:q0 + MLA_NOPE] = kn.astype(BF16)
            kc_ref[:, q0 + MLA_NOPE:q0 + MLA_QW] = krot_b
            kt_ref[h, :MLA_NOPE, :] = kn.T.astype(BF16)
            kt_ref[h, MLA_NOPE:, :] = krot_t
            vh = kv_ref[:, (MLA_HEADS + h) * MLA_NOPE:(MLA_HEADS + h + 1) * MLA_NOPE]
            v_ref[:, h * MLA_DV:(h + 1) * MLA_DV] = vh.astype(BF16)
            vt_ref[h] = vh.T.astype(BF16)

    return pl.pallas_call(
        body, name="qk_prep", grid=(s // tm,),
        in_specs=[_row(tm, hq), _row(tm, hq), _row(tm, LANES, kr_cb), _row(tm, LANES), _row(tm, LANES), _row(tm, LANES)],
        out_specs=[_row(tm, hq), _row(tm, hq), _row(tm, hv),
                   pl.BlockSpec((MLA_HEADS, None, MLA_QW, tm), lambda i: (0, i // per, 0, i % per)),
                   pl.BlockSpec((MLA_HEADS, None, MLA_DV, tm), lambda i: (0, i // per, 0, i % per))],
        out_shape=[jax.ShapeDtypeStruct((s, hq), BF16), jax.ShapeDtypeStruct((s, hq), BF16),
                   jax.ShapeDtypeStruct((s, hv), BF16),
                   jax.ShapeDtypeStruct((MLA_HEADS, s // blk, MLA_QW, blk), BF16),
                   jax.ShapeDtypeStruct((MLA_HEADS, s // blk, MLA_DV, blk), BF16)],
        compiler_params=_cp("parallel"),
    )(qp, kvp, proj, ta, tb, tc)


def _kv_bwd_prep(dk_cat, dv, ta, tb, tc, dproj, lay):
    s = dk_cat.shape[0]
    hq = MLA_HEADS * MLA_QW
    hv = MLA_HEADS * MLA_DV
    tm = _rtile(s, 256)
    dp_spec, dp_shape = _dproj_out(dproj, s, lay, tm, "kr")

    def body(dk_ref, dv_ref, a_ref, b_ref, c_ref, _, dkv_ref, dp_ref):
        acc = jnp.zeros((tm, LANES), F32)
        for h in range(MLA_HEADS):
            q0 = h * MLA_QW
            dkv_ref[:, h * MLA_NOPE:(h + 1) * MLA_NOPE] = dk_ref[:, q0:q0 + MLA_NOPE].astype(BF16)
            acc = acc + dk_ref[:, q0 + MLA_NOPE:q0 + MLA_QW]
        dkv_ref[:, MLA_HEADS * MLA_NOPE:] = dv_ref[...].astype(BF16)
        dp_ref[...] = _rope_tile_bwd(acc, a_ref[...], b_ref[...], c_ref[...]).astype(BF16)

    return pl.pallas_call(
        body, name="kv_bwd_prep", grid=(s // tm,),
        in_specs=[_row(tm, hq), _row(tm, hv), _row(tm, LANES), _row(tm, LANES), _row(tm, LANES), ANY],
        out_specs=[_row(tm, hq), dp_spec],
        out_shape=[jax.ShapeDtypeStruct((s, hq), BF16), dp_shape],
        input_output_aliases={5: 1},
        compiler_params=_cp("parallel"),
    )(dk_cat, dv, ta, tb, tc, dproj)


def _mla_gate_bwd(du, o, proj, dproj, lay):
    s, vw = du.shape
    tm = _rtile(s, 256)
    cb = lay.off["mg"] // vw
    dp_spec, dp_shape = _dproj_out(dproj, s, lay, tm, "mg")

    def body(du_ref, o_ref, g_ref, _, do_ref, dp_ref):
        gv, duv = g_ref[...], du_ref[...]
        sg = _sigmoid(gv)
        do_ref[...] = (duv * (gv * sg)).astype(BF16)
        dp_ref[...] = (duv * o_ref[...] * (sg + gv * sg * (1.0 - sg))).astype(BF16)

    return pl.pallas_call(
        body, name="mla_gate_bwd", grid=(s // tm,),
        in_specs=[_row(tm, vw), _row(tm, vw), _row(tm, vw, cb), ANY],
        out_specs=[_row(tm, vw), dp_spec],
        out_shape=[jax.ShapeDtypeStruct((s, vw), BF16), dp_shape],
        input_output_aliases={3: 1},
        compiler_params=_cp("parallel"),
    )(du, o, proj, dproj)


def _ret_tables(lg):
    ri = lax.broadcasted_iota(jnp.int32, (CHUNK, CHUNK), 0).astype(F32)
    ci = lax.broadcasted_iota(jnp.int32, (CHUNK, CHUNK), 1).astype(F32)
    col = lax.broadcasted_iota(jnp.int32, (CHUNK, 1), 0).astype(F32)
    dmat = jnp.exp(jnp.abs(ri - ci) * lg)
    xi = jnp.exp((col + 1.0) * lg)
    zeta = jnp.exp((CHUNK - 1.0 - col) * lg)
    decay = jnp.exp(jnp.full((1, 1), CHUNK, F32) * lg)
    return dmat, xi, zeta, decay


def _ret_qkvg(blk, cs, sn):
    dv = RET_DV
    q = blk[:, :RET_DK]
    k = blk[:, RET_DK:2 * RET_DK]
    q = q * cs + _roll(q, RET_DK // 2) * sn
    k = (k * cs + _roll(k, RET_DK // 2) * sn) * (RET_DK ** -0.5)
    return q, k, blk[:, 2 * RET_DK:2 * RET_DK + dv], blk[:, 2 * RET_DK + dv:]


def _group_norm(o):
    mu = jnp.mean(o, axis=-1, keepdims=True)
    oc = o - mu
    rstd = lax.rsqrt(jnp.mean(oc * oc, axis=-1, keepdims=True) + EPS)
    return oc * rstd, rstd


def _ret_fwd(proj, lg, cosr, sinr, lay):
    s = proj.shape[0]
    dv, w = RET_DV, lay.ret_w
    tb = _rtile(s, 512)
    nb, nch = s // tb, tb // CHUNK
    cb0 = lay.off["ret"] // w

    def body(lg_ref, p_ref, cos_ref, sin_ref, o_ref, u_ref, st_ref, state):
        @pl.when(pl.program_id(1) == 0)
        def _():
            state[...] = jnp.zeros_like(state)

        dmat, xi, zeta, decay = _ret_tables(lg_ref[pl.program_id(0)])
        for c in range(nch):
            rows = slice(c * CHUNK, (c + 1) * CHUNK)
            q, k, v, g = _ret_qkvg(p_ref[rows, :], cos_ref[rows, :], sin_ref[rows, :])
            qb, kb, vb = q.astype(BF16), k.astype(BF16), v.astype(BF16)
            sc = _dot_nt(qb, kb) * dmat
            st = state[...]
            o = _dot(sc.astype(BF16), vb) + _dot((q * xi).astype(BF16), st.astype(BF16))
            st_ref[c] = st.astype(BF16)
            state[...] = st * decay + _dot_tn((k * zeta).astype(BF16), vb)
            o_ref[rows, :] = o
            n, _ = _group_norm(o)
            u_ref[rows, :] = (n * (g * _sigmoid(g))).astype(BF16)

    return pl.pallas_call(
        body, name="ret_fwd", grid=(RET_HEADS, nb),
        in_specs=[pl.BlockSpec(memory_space=pltpu.SMEM),
                  pl.BlockSpec((tb, w), lambda h, b: (b, cb0 + h)),
                  pl.BlockSpec((tb, RET_DK), lambda h, b: (b, 0)),
                  pl.BlockSpec((tb, RET_DK), lambda h, b: (b, 0))],
        out_specs=[pl.BlockSpec((tb, dv), lambda h, b: (b, h)),
                   pl.BlockSpec((tb, dv), lambda h, b: (b, h)),
                   pl.BlockSpec((None, nch, RET_DK, dv), lambda h, b: (h, b, 0, 0))],
        out_shape=[jax.ShapeDtypeStruct((s, RET_HEADS * dv), F32),
                   jax.ShapeDtypeStruct((s, RET_HEADS * dv), BF16),
                   jax.ShapeDtypeStruct((RET_HEADS, s // CHUNK, RET_DK, dv), BF16)],
        scratch_shapes=[pltpu.VMEM((RET_DK, dv), F32)],
        compiler_params=_cp("parallel", "arbitrary"),
    )(lg, proj, cosr, sinr)


def _ret_bwd(proj, lg, cosr, sinr, o, du, states, dproj, lay):
    s = proj.shape[0]
    dv, w = RET_DV, lay.ret_w
    tb = _rtile(s, 512)
    nb, nch = s // tb, tb // CHUNK
    cb0 = lay.off["ret"] // w

    def body(lg_ref, p_ref, cos_ref, sin_ref, o_ref, du_ref, st_ref, _, dp_ref, dstate):
        @pl.when(pl.program_id(1) == 0)
        def _():
            dstate[...] = jnp.zeros_like(dstate)

        dmat, xi, zeta, decay = _ret_tables(lg_ref[pl.program_id(0)])
        for c in reversed(range(nch)):
            rows = slice(c * CHUNK, (c + 1) * CHUNK)
            cs, sn = cos_ref[rows, :], sin_ref[rows, :]
            q, k, v, g = _ret_qkvg(p_ref[rows, :], cs, sn)
            qb, kb, vb = q.astype(BF16), k.astype(BF16), v.astype(BF16)
            n, rstd = _group_norm(o_ref[rows, :])
            sg = _sigmoid(g)
            duv = du_ref[rows, :]
            dn = duv * (g * sg)
            dg = duv * n * (sg + g * sg * (1.0 - sg))
            do = rstd * (dn - jnp.mean(dn, axis=-1, keepdims=True) - n * jnp.mean(dn * n, axis=-1, keepdims=True))
            dob = do.astype(BF16)
            rb = st_ref[c]
            drb = dstate[...].astype(BF16)
            sc = (_dot_nt(qb, kb) * dmat).astype(BF16)
            dsc = (_dot_nt(dob, vb) * dmat).astype(BF16)
            qx = (q * xi).astype(BF16)
            kz = (k * zeta).astype(BF16)
            dq = _dot(dsc, kb) + _dot_nt(dob, rb) * xi
            dk = (_dot_tn(dsc, qb) + _dot_nt(vb, drb) * zeta) * (RET_DK ** -0.5)
            dvv = _dot_tn(sc, dob) + _dot(kz, drb)
            dstate[...] = dstate[...] * decay + _dot_tn(qx, dob)
            dp_ref[rows, :RET_DK] = (dq * cs + _roll(dq * sn, RET_DK // 2)).astype(BF16)
            dp_ref[rows, RET_DK:2 * RET_DK] = (dk * cs + _roll(dk * sn, RET_DK // 2)).astype(BF16)
            dp_ref[rows, 2 * RET_DK:2 * RET_DK + dv] = dvv.astype(BF16)
            dp_ref[rows, 2 * RET_DK + dv:] = dg.astype(BF16)

    rev = lambda h, b: (nb - 1 - b, h)
    return pl.pallas_call(
        body, name="ret_bwd", grid=(RET_HEADS, nb),
        in_specs=[pl.BlockSpec(memory_space=pltpu.SMEM),
                  pl.BlockSpec((tb, w), lambda h, b: (nb - 1 - b, cb0 + h)),
                  pl.BlockSpec((tb, RET_DK), lambda h, b: (nb - 1 - b, 0)),
                  pl.BlockSpec((tb, RET_DK), lambda h, b: (nb - 1 - b, 0)),
                  pl.BlockSpec((tb, dv), rev),
                  pl.BlockSpec((tb, dv), rev),
                  pl.BlockSpec((None, nch, RET_DK, dv), lambda h, b: (h, nb - 1 - b, 0, 0)),
                  ANY],
        out_specs=pl.BlockSpec((tb, w), lambda h, b: (nb - 1 - b, cb0 + h)),
        out_shape=jax.ShapeDtypeStruct((s, lay.total), BF16),
        input_output_aliases={7: 0},
        scratch_shapes=[pltpu.VMEM((RET_DK, dv), F32)],
        compiler_params=_cp("parallel", "arbitrary"),
    )(lg, proj, cosr, sinr, o, du, states, dproj)


def _attn_mask(rows, cols, row0, col0, keys_on_rows):
    ri = (lax.broadcasted_iota(jnp.int32, (rows, cols), 0) + row0) // CHUNK
    ci = (lax.broadcasted_iota(jnp.int32, (rows, cols), 1) + col0) // CHUNK
    return ri <= ci if keys_on_rows else ci <= ri


def _attn_fwd(q_cat, k_cat, v_t, proj, lay):
    s = q_cat.shape[0]
    bq = _attn_block(s)
    nq = s // bq
    mg_cb = lay.off["mg"] // MLA_DV

    def body(q_ref, k_ref, vt_ref, g_ref, o_ref, u_ref, lse_ref):
        i = pl.program_id(1)
        q = q_ref[...]

        def scores(j):
            r0 = pl.multiple_of(j * bq, bq)
            return _dot_nt(k_ref[pl.ds(r0, bq), :], q)

        def update(j, sc, m, l, acc):
            mn = jnp.maximum(m, jnp.max(sc, axis=0, keepdims=True))
            p = jnp.exp2(sc - mn)
            alpha = jnp.exp2(m - mn)
            l = alpha * l + jnp.sum(p, axis=0, keepdims=True)
            acc = alpha * acc + _dot(vt_ref[j], p.astype(BF16))
            return mn, l, acc

        def hide(sc):
            return jnp.where(_attn_mask(bq, bq, 0, 0, True), sc, NEG_INF)

        def pair(j, carry, last):
            sa, sb = scores(j), scores(j + 1)
            return update(j + 1, hide(sb) if last else sb, *update(j, sa, *carry))

        init = (jnp.full((1, bq), NEG_INF, F32), jnp.zeros((1, bq), F32), jnp.zeros((MLA_DV, bq), F32))
        carry = lax.fori_loop(0, i // 2, lambda t, c: pair(2 * t, c, False), init)
        m, l, acc = lax.cond(i % 2 == 1, lambda c: pair(i - 1, c, True),
                             lambda c: update(i, hide(scores(i)), *c), carry)
        o = (acc / l).T
        gv = g_ref[...]
        o_ref[...] = o
        u_ref[...] = (o * (gv * _sigmoid(gv))).astype(BF16)
        lse_ref[...] = m + jnp.log2(l)

    return pl.pallas_call(
        body, name="attn_fwd", grid=(MLA_HEADS, nq),
        in_specs=[pl.BlockSpec((bq, MLA_QW), lambda h, i: (i, h)),
                  pl.BlockSpec((s, MLA_QW), lambda h, i: (0, h)),
                  pl.BlockSpec((None, nq, MLA_DV, bq), lambda h, i: (h, 0, 0, 0)),
                  pl.BlockSpec((bq, MLA_DV), lambda h, i: (i, mg_cb + h))],
        out_specs=[pl.BlockSpec((bq, MLA_DV), lambda h, i: (i, h)),
                   pl.BlockSpec((bq, MLA_DV), lambda h, i: (i, h)),
                   pl.BlockSpec((None, None, 1, bq), lambda h, i: (h, i, 0, 0))],
        out_shape=[jax.ShapeDtypeStruct((s, MLA_HEADS * MLA_DV), F32),
                   jax.ShapeDtypeStruct((s, MLA_HEADS * MLA_DV), BF16),
                   jax.ShapeDtypeStruct((MLA_HEADS, nq, 1, bq), F32)],
        compiler_params=_cp("parallel", "parallel"),
    )(q_cat, k_cat, v_t, proj)


def _attn_bwd(q_cat, k_cat, v, k_t, do, o, lse, ta, tb, tc):
    s = q_cat.shape[0]
    blk = _attn_block(s)
    nb = s // blk

    def body(q_ref, k_ref, v_ref, kt_ref, do_ref, o_ref, lse_ref, a_ref, b_ref, c_ref,
             dq_ref, dk_ref, dv_ref, dq_acc, delta):
        j = pl.program_id(1)

        @pl.when(j == 0)
        def _():
            dq_acc[...] = jnp.zeros_like(dq_acc)
            for i in range(nb):
                rows = slice(i * blk, (i + 1) * blk)
                prod = do_ref[rows, :].astype(F32) * o_ref[rows, :]
                delta[i] = jnp.sum(prod.T, axis=0, keepdims=True)

        kb, vb, kt = k_ref[...], v_ref[...], kt_ref[...]

        def step(i, carry, masked):
            dk, dvv = carry
            r0 = pl.multiple_of(i * blk, blk)
            q, dob = q_ref[pl.ds(r0, blk), :], do_ref[pl.ds(r0, blk), :]
            sc = _dot_nt(kb, q)
            if masked:
                sc = jnp.where(_attn_mask(blk, blk, 0, 0, True), sc, NEG_INF)
            p = jnp.exp2(sc - lse_ref[i])
            dvv = dvv + _dot(p.astype(BF16), dob)
            ds = (p * (_dot_nt(vb, dob) - delta[i])).astype(BF16)
            dk = dk + _dot(ds, q)
            dq_acc[i] = dq_acc[i] + _dot(kt, ds)
            return dk, dvv

        carry = step(j, (jnp.zeros((blk, MLA_QW), F32), jnp.zeros((blk, MLA_DV), F32)), True)
        dk, dvv = lax.fori_loop(j + 1, nb, lambda i, c: step(i, c, False), carry)
        dk_ref[...] = dk * LN2
        dv_ref[...] = dvv

        @pl.when(j == nb - 1)
        def _():
            for i in range(nb):
                rows = slice(i * blk, (i + 1) * blk)
                dq = dq_acc[i].T * QK_SCALE
                dq_ref[rows, :MLA_NOPE] = dq[:, :MLA_NOPE].astype(BF16)
                dq_ref[rows, MLA_NOPE:] = _rope_tile_bwd(dq[:, MLA_NOPE:], a_ref[rows, :], b_ref[rows, :],
                                                         c_ref[rows, :]).astype(BF16)

    tab = pl.BlockSpec((s, LANES), lambda h, j: (0, 0))
    return pl.pallas_call(
        body, name="attn_bwd", grid=(MLA_HEADS, nb),
        in_specs=[pl.BlockSpec((s, MLA_QW), lambda h, j: (0, h)),
                  pl.BlockSpec((blk, MLA_QW), lambda h, j: (j, h)),
                  pl.BlockSpec((blk, MLA_DV), lambda h, j: (j, h)),
                  pl.BlockSpec((None, None, MLA_QW, blk), lambda h, j: (h, j, 0, 0)),
                  pl.BlockSpec((s, MLA_DV), lambda h, j: (0, h)),
                  pl.BlockSpec((s, MLA_DV), lambda h, j: (0, h)),
                  pl.BlockSpec((None, nb, 1, blk), lambda h, j: (h, 0, 0, 0)),
                  tab, tab, tab],
        out_specs=[pl.BlockSpec((s, MLA_QW), lambda h, j: (0, h)),
                   pl.BlockSpec((blk, MLA_QW), lambda h, j: (j, h)),
                   pl.BlockSpec((blk, MLA_DV), lambda h, j: (j, h))],
        out_shape=[jax.ShapeDtypeStruct((s, MLA_HEADS * MLA_QW), BF16),
                   jax.ShapeDtypeStruct((s, MLA_HEADS * MLA_QW), F32),
                   jax.ShapeDtypeStruct((s, MLA_HEADS * MLA_DV), F32)],
        scratch_shapes=[pltpu.VMEM((nb, MLA_QW, blk), F32), pltpu.VMEM((nb, 1, blk), F32)],
        compiler_params=_cp("parallel", "arbitrary"),
    )(q_cat, k_cat, v, k_t, do, o, lse, ta, tb, tc)


def _place():
    return lax.axis_index("x"), lax.axis_index("y"), lax.axis_index("c")


def _slot(px, py, pc):
    return 4 * px + 2 * py + pc


def _all_gather(shards, layer, name, vmem=False):
    n = len(shards)

    def body(*refs):
        srcs, outs = refs[:n], refs[n:2 * n]
        send_sems, recv_sems, local_sems = refs[2 * n:]
        x, y, c = _place()
        me, sibling = (x, y, c), (x, y, 1 - c)
        chips = [(1 - x, y), (x, 1 - y), (1 - x, 1 - y)]
        firsts, passes, locals_ = [], [], []

        def copy(a, k, block, to, src=None):
            dst = outs[a].at[_slot(*block)]
            return pltpu.make_async_remote_copy(
                src_ref=dst if src is None else src, dst_ref=dst,
                send_sem=send_sems.at[7 * a + k], recv_sem=recv_sems.at[7 * a + k],
                device_id=to, device_id_type=MESH)

        for a in range(n):
            src = srcs[a] if layer is None else srcs[a].at[layer]
            mine = pltpu.make_async_copy(src, outs[a].at[_slot(*me)], local_sems.at[a])
            mine.start()
            locals_.append(mine)
            first = [copy(a, 0, me, sibling, src=src)]
            first += [copy(a, 1 + j, me, (*chip, c), src=src) for j, chip in enumerate(chips)]
            for cp in first:
                cp.start()
            firsts += first
        for a in range(n):
            for j, chip in enumerate(chips):
                copy(a, 1 + j, (*chip, c), me).wait_recv()
                fwd = copy(a, 4 + j, (*chip, c), sibling)
                fwd.start()
                passes.append(fwd)
        for a in range(n):
            copy(a, 0, sibling, me).wait_recv()
            for j, chip in enumerate(chips):
                copy(a, 4 + j, (*chip, 1 - c), me).wait_recv()
        for cp in firsts + passes:
            cp.wait_send()
        for mine in locals_:
            mine.wait()

    space = pl.BlockSpec(memory_space=pltpu.VMEM) if vmem else ANY
    out_shape = [jax.ShapeDtypeStruct((N_DEV,) + (a.shape if layer is None else a.shape[1:]), a.dtype) for a in shards]
    return pl.pallas_call(
        body, name=name,
        in_specs=[space] * n, out_specs=[space] * n, out_shape=out_shape,
        scratch_shapes=[pltpu.SemaphoreType.DMA((7 * n,)), pltpu.SemaphoreType.DMA((7 * n,)),
                        pltpu.SemaphoreType.DMA((n,))],
        compiler_params=pltpu.CompilerParams(has_side_effects=True),
    )(*shards)


HBM = pl.BlockSpec(memory_space=pltpu.HBM)
SEM = pl.BlockSpec(memory_space=pltpu.SEMAPHORE)
EFFECT = pltpu.SideEffectType.DATAFLOW_SIDE_EFFECTING


def _push_copies(srcs, lands, send_sems, recv_sems, local_sems, by_peer):
    x, y, c = _place()
    me = _slot(x, y, c)
    local, remote = [], []
    for a, (src, land) in enumerate(zip(srcs, lands)):
        local.append(pltpu.make_async_copy(src.at[me] if by_peer else src, land.at[me], local_sems.at[a]))
        for r in range(1, N_DEV):
            peer = (1 - x if r & 4 else x, 1 - y if r & 2 else y, 1 - c if r & 1 else c)
            remote.append(pltpu.make_async_remote_copy(
                src_ref=src.at[_slot(*peer)] if by_peer else src, dst_ref=land.at[me],
                send_sem=send_sems.at[7 * a + r - 1], recv_sem=recv_sems.at[7 * a + r - 1],
                device_id=peer, device_id_type=MESH))
    return local, remote


def _push_start(srcs, by_peer, after, name):
    n = len(srcs)
    srcs = [pltpu.with_memory_space_constraint(a, pltpu.HBM) for a in srcs]
    lands = [pltpu.with_memory_space_constraint(
        lax.empty((N_DEV,) + (a.shape[1:] if by_peer else a.shape), a.dtype), pltpu.HBM) for a in srcs]

    def body(*refs):
        k = 2 * n + len(after)
        local, remote = _push_copies(refs[:n], refs[n:2 * n], refs[k], refs[k + 1], refs[k + 2], by_peer)
        for cp in local + remote:
            cp.start()
        token = refs[k + 3 + 2 * n]
        token[...] = jnp.zeros_like(token)

    outs = pl.pallas_call(
        body, name=name,
        out_shape=(pltpu.SemaphoreType.DMA((7 * n,)), pltpu.SemaphoreType.DMA((7 * n,)),
                   pltpu.SemaphoreType.DMA((n,)),
                   *[pltpu.HBM(a.shape, a.dtype) for a in srcs], *[pltpu.HBM(a.shape, a.dtype) for a in lands],
                   jax.ShapeDtypeStruct((8, LANES), F32)),
        in_specs=[HBM] * (2 * n) + [ANY] * len(after),
        out_specs=(SEM, SEM, SEM, *([HBM] * (2 * n)), pl.BlockSpec(memory_space=pltpu.VMEM)),
        input_output_aliases={i: 3 + i for i in range(2 * n)},
        compiler_params=pltpu.CompilerParams(has_side_effects=EFFECT),
    )(*srcs, *lands, *after)
    return outs[:3], outs[3:3 + n], outs[3 + n:3 + 2 * n], outs[3 + 2 * n]


def _push_wait(sems, srcs, lands, by_peer, after, name):
    n = len(srcs)

    def body(*refs):
        local, remote = _push_copies(refs[:n], refs[n:2 * n], refs[2 * n], refs[2 * n + 1], refs[2 * n + 2], by_peer)
        for cp in local:
            cp.wait()
        for cp in remote:
            cp.wait_send()
            cp.wait_recv()

    outs = pl.pallas_call(
        body, name=name,
        out_shape=[pltpu.HBM(a.shape, a.dtype) for a in list(srcs) + list(lands)],
        in_specs=[HBM] * (2 * n) + [SEM] * 3 + [ANY] * len(after),
        out_specs=[HBM] * (2 * n),
        input_output_aliases={i: i for i in range(2 * n)},
        compiler_params=pltpu.CompilerParams(has_side_effects=EFFECT),
    )(*srcs, *lands, *sems, *after)
    return outs[n:]


def _adam_math(g, w, m, v):
    m = ADAM_B1 * m + (1.0 - ADAM_B1) * g
    v = ADAM_B2 * v + (1.0 - ADAM_B2) * (g * g)
    m_hat = m / (1.0 - ADAM_B1 ** ADAM_STEP)
    v_hat = v / (1.0 - ADAM_B2 ** ADAM_STEP)
    delta = -ADAM_LR * (m_hat / (jnp.sqrt(v_hat) + ADAM_EPS) + ADAM_WD * w)
    return delta, m, v


def _adam_sharded(recvs, first, w, m, v, prev, name):
    nl, r, c = w.shape
    n = len(recvs)
    tr = _rtile(r, 128)
    nr = r // tr
    prev = list(prev) if prev is not None else []

    def body(*refs):
        g_refs = refs[:n]
        w_ref, m_ref, v_ref = refs[n:n + 3]
        go_ref, d_ref, mo_ref, vo_ref = refs[n + 3 + len(prev):]
        layer = pl.program_id(0)
        for l in range(n):
            @pl.when(layer == l)
            def _(l=l):
                g = g_refs[l][0].astype(F32)
                for i in range(1, N_DEV):
                    g = g + g_refs[l][i].astype(F32)
                delta, mn, vn = _adam_math(g, w_ref[...], m_ref[...], v_ref[...])
                go_ref[...] = g
                d_ref[...] = delta
                mo_ref[...] = mn
                vo_ref[...] = vn

    def recv_spec(l):
        def index(layer, i):
            return 0, jnp.where(layer == l, i, jnp.where(layer < l, 0, nr - 1)), 0
        return pl.BlockSpec((N_DEV, tr, c), index)

    blk = pl.BlockSpec((None, tr, c), lambda layer, i: (first + layer, i, 0))
    out = jax.ShapeDtypeStruct(w.shape, F32)
    return pl.pallas_call(
        body, name=name, grid=(n, nr),
        in_specs=[recv_spec(l) for l in range(n)] + [blk, blk, blk] + [ANY] * len(prev),
        out_specs=[blk] * 4, out_shape=[out] * 4,
        input_output_aliases={n + 3 + k: k for k in range(len(prev))},
        compiler_params=_cp("arbitrary", "arbitrary"),
    )(*recvs, w, m, v, *prev)


def _adam_mod(c_all_t, dmod, w, m, v):
    nl, d, c = w.shape
    tr = _rtile(d, 128)

    def body(ct_ref, dm_ref, w_ref, m_ref, v_ref, go_ref, d_ref, mo_ref, vo_ref):
        ct = ct_ref[...].astype(BF16).astype(F32)
        dm = dm_ref[...].astype(BF16).astype(F32)
        g = ct[:, 0:1] * dm[0:1, :]
        for b in range(1, N_DEV):
            g = g + ct[:, b:b + 1] * dm[b:b + 1, :]
        delta, mn, vn = _adam_math(g, w_ref[...], m_ref[...], v_ref[...])
        go_ref[...] = g
        d_ref[...] = delta
        mo_ref[...] = mn
        vo_ref[...] = vn

    blk = pl.BlockSpec((None, tr, c), lambda layer, i: (layer, i, 0))
    out = jax.ShapeDtypeStruct(w.shape, F32)
    return pl.pallas_call(
        body, name="adam_mod", grid=(nl, d // tr),
        in_specs=[pl.BlockSpec((tr, N_DEV), lambda layer, i: (i, 0)),
                  pl.BlockSpec((None, N_DEV, c), lambda layer, i: (layer, 0, 0)), blk, blk, blk],
        out_specs=[blk] * 4, out_shape=[out] * 4,
        compiler_params=_cp("parallel", "parallel"),
    )(c_all_t, dmod, w, m, v)


def _adam_small(g, w, m, v, name):
    def body(g_ref, w_ref, m_ref, v_ref, d_ref, mo_ref, vo_ref):
        delta, mn, vn = _adam_math(g_ref[...], w_ref[...], m_ref[...], v_ref[...])
        d_ref[...] = delta
        mo_ref[...] = mn
        vo_ref[...] = vn

    out = jax.ShapeDtypeStruct(w.shape, F32)
    return pl.pallas_call(body, name=name, out_shape=[out] * 3)(g, w, m, v)


def _sum_devices(parts):
    def body(p_ref, o_ref):
        acc = p_ref[0]
        for i in range(1, N_DEV):
            acc = acc + p_ref[i]
        o_ref[...] = acc

    return pl.pallas_call(body, name="sum_devices",
                          out_shape=jax.ShapeDtypeStruct(parts.shape[1:], F32))(parts)


def _rope_tables(positions):
    pos = positions.astype(F32)[:, None]

    def cs(dim):
        inv = 1.0 / (ROPE_BASE ** (jnp.arange(0, dim, 2, dtype=F32) / dim))
        ang = pos * inv
        return jnp.cos(ang), jnp.sin(ang)

    cr, sr = cs(RET_DK)
    cm, sm = cs(MLA_ROPE)
    z = jnp.zeros_like(cm)
    pad = jnp.zeros((pos.shape[0], LANES - MLA_ROPE), F32)
    cosr = jnp.concatenate([cr, cr], axis=1)
    sinr = jnp.concatenate([-sr, sr], axis=1)
    ta = jnp.concatenate([cm, cm, pad], axis=1)
    tb = jnp.concatenate([-sm, z, pad], axis=1)
    tc = jnp.concatenate([z, sm, pad], axis=1)
    return cosr, sinr, ta, tb, tc


def _layer_fwd(x, mod, g_norm, g_cq, g_ckv, wts, tabs, lg, lay, deps):
    d = x.shape[1]
    cosr, sinr, ta, tb, tc = tabs
    shift, scale, gate = mod[:, :d], mod[:, d:2 * d], mod[:, 2 * d:]
    h = _norm_mod_fwd(x, g_norm, scale, shift, deps)
    proj = _matmul(h, wts["in"], name="mm_proj", tn_cap=1920)
    o_ret, u_ret, states = _ret_fwd(proj, lg, cosr, sinr, lay)
    y_ret = _matmul(u_ret, wts["ret"], name="mm_y")
    cqn, ckvn = _mla_prep(proj, g_cq, g_ckv, lay)
    qp = _matmul(cqn, wts["uq"], name="mm_up")
    kvp = _matmul(ckvn, wts["ukv"], name="mm_up")
    q_cat, k_cat, v, k_t, v_t = _qk_prep(qp, kvp, proj, ta, tb, tc, lay)
    o_mla, u_mla, lse = _attn_fwd(q_cat, k_cat, v_t, proj, lay)
    y_mla = _matmul(u_mla, wts["mla"], name="mm_y")
    merged = _merge_fwd(y_ret, y_mla, proj, lay)
    out = _matmul(merged, wts["out"], name="mm_y")
    x_next = _resid_fwd(x, out, gate)
    saved = dict(x=x, h=h, proj=proj, o_ret=o_ret, u_ret=u_ret, states=states, y_ret=y_ret, cqn=cqn,
                 ckvn=ckvn, q_cat=q_cat, k_cat=k_cat, v=v, k_t=k_t, o_mla=o_mla, u_mla=u_mla, lse=lse,
                 y_mla=y_mla, merged=merged, out=out)
    return x_next, saved


def _to_owner_blocks_cols(g, n_local):
    k = g.shape[0]
    return g.reshape(k, N_DEV, n_local).transpose(1, 0, 2)


def _from_owner_blocks_cols(g):
    return g.transpose(1, 0, 2).reshape(g.shape[1], -1)


def _layer_bwd(dxn, sv, mod, g_norm, g_cq, g_ckv, wts, tabs, lg, lay, deps, shard_cols, push):
    d = dxn.shape[1]
    n_in, n_uq, n_ukv = shard_cols
    cosr, sinr, ta, tb, tc = tabs
    scale, gate = mod[:, d:2 * d], mod[:, 2 * d:]
    gdt = BF16
    dout, dgate = _resid_bwd(dxn, sv["out"], gate, deps)
    dmerged = _matmul(dout, wts["out"], tb=True, name="mm_dy")
    dw_out = _matmul(sv["merged"], dout, ta=True, out_dtype=gdt, name="mm_dw")
    dy_ret, dy_mla, dproj = _merge_bwd(dmerged, sv["y_ret"], sv["y_mla"], sv["proj"], lay)
    du_ret = _matmul(dy_ret, wts["ret"], tb=True, name="mm_dy")
    dw_ret = _matmul(sv["u_ret"], dy_ret, ta=True, out_dtype=gdt, name="mm_dw")
    dproj = _ret_bwd(sv["proj"], lg, cosr, sinr, sv["o_ret"], du_ret, sv["states"], dproj, lay)
    du_mla = _matmul(dy_mla, wts["mla"], tb=True, name="mm_dy")
    dw_mla = _matmul(sv["u_mla"], dy_mla, ta=True, out_dtype=gdt, name="mm_dw")
    do_mla, dproj = _mla_gate_bwd(du_mla, sv["o_mla"], sv["proj"], dproj, lay)
    dqp, dk_cat, dv = _attn_bwd(sv["q_cat"], sv["k_cat"], sv["v"], sv["k_t"], do_mla, sv["o_mla"], sv["lse"],
                                ta, tb, tc)
    dkvp, dproj = _kv_bwd_prep(dk_cat, dv, ta, tb, tc, dproj, lay)
    dcqn = _matmul(dqp, wts["uq"], tb=True, name="mm_dlat")
    dckvn = _matmul(dkvp, wts["ukv"], tb=True, name="mm_dlat")
    dw_uq = _matmul(sv["cqn"], dqp, ta=True, out_dtype=gdt, name="mm_dwup")
    dw_ukv = _matmul(sv["ckvn"], dkvp, ta=True, out_dtype=gdt, name="mm_dwup")
    sent = push("a", [_to_owner_blocks_cols(_uq_to_logical(dw_uq), n_uq),
                      _to_owner_blocks_cols(_ukv_to_logical(dw_ukv), n_ukv),
                      dw_ret.reshape(N_DEV, -1, d), dw_mla.reshape(N_DEV, -1, d), dw_out.reshape(N_DEV, -1, d)])
    dproj, dg_cq, dg_ckv = _mla_prep_bwd(sv["proj"], dcqn, dckvn, g_cq, g_ckv, dproj, lay)
    dw_in = _matmul(sv["h"], dproj, ta=True, out_dtype=gdt, name="mm_dwin", tn_cap=1920, deps=sent)
    sent = push("b", [_to_owner_blocks_cols(lay.to_logical(dw_in), n_in)])
    dh = _matmul(dproj, wts["in"], tb=True, name="mm_dh", tk_cap=896, deps=sent)
    dx, dshift, dscale, dg_norm = _norm_mod_bwd(sv["x"], g_norm, scale, dh, dxn)
    dmod = jnp.concatenate([dshift, dscale, dgate], axis=1)
    small = dict(dmod=dmod, g_norm=dg_norm, g_cq=dg_cq, g_ckv=dg_ckv)
    return dx, small


def kernel(x, c, positions, w_mod, b_mod, g_norm, w_in, g_cq, g_ckv, w_uq, w_ukv, w_ret_proj, w_mla_proj, w_out, g_final, loss_target, m_w_mod, m_b_mod, m_g_norm, m_w_in, m_g_cq, m_g_ckv, m_w_uq, m_w_ukv, m_w_ret_proj, m_w_mla_proj, m_w_out, m_g_final, v_w_mod, v_b_mod, v_g_norm, v_w_in, v_g_cq, v_g_ckv, v_w_uq, v_w_ukv, v_w_ret_proj, v_w_mla_proj, v_w_out, v_g_final):
    nl, d, _ = w_mod.shape
    s = x.shape[1]
    rank = g_cq.shape[1]
    lay = Layout(d, rank, g_ckv.shape[1])
    me = _slot(*_place())
    x0 = x.reshape(s, d)
    target = loss_target.reshape(s, d)
    tabs = _rope_tables(positions.reshape(s))
    lg = jnp.log(1.0 - 2.0 ** (-5.0 - jnp.arange(RET_HEADS, dtype=F32)))

    c_act = c * _sigmoid(c)
    (c_all,) = _all_gather([c_act.reshape(d // LANES, LANES)], None, "gather_c", vmem=True)
    c_all = c_all.reshape(N_DEV, d)
    n_mod = w_mod.shape[2]
    mod_part = jnp.stack([_matmul(c_all, w_mod[l], name="mm_mod", tm_cap=8) for l in range(nl)])
    (mod_all,) = _all_gather([mod_part.reshape(-1, LANES)], None, "gather_mod", vmem=True)
    mod_all = mod_all.reshape(N_DEV, nl, N_DEV, n_mod)
    mod = lax.dynamic_index_in_dim(mod_all, me, axis=2, keepdims=False)
    mod = mod.transpose(1, 0, 2).reshape(nl, N_DEV * n_mod) + b_mod

    shards = [[w[l].astype(BF16) for w in (w_in, w_uq, w_ukv, w_ret_proj, w_mla_proj, w_out)] for l in range(nl)]
    xl, saved, wts_all = x0, [], []
    gathered = _all_gather(shards[0], None, "gather_w")
    for l in range(nl):
        g_in, g_uq, g_ukv, g_ret, g_mla, g_out = gathered
        deps = []
        if l + 1 < nl:
            sems, srcs, lands, token = _push_start(shards[l + 1], False, [g_out], "gather_start_%d" % (l + 1))
            deps = [token]
        wts = {
            "in": lay.to_physical(_from_owner_blocks_cols(g_in)),
            "uq": _uq_to_physical(_from_owner_blocks_cols(g_uq)),
            "ukv": _ukv_to_physical(_from_owner_blocks_cols(g_ukv)),
            "ret": g_ret.reshape(-1, d), "mla": g_mla.reshape(-1, d), "out": g_out.reshape(-1, d),
        }
        wts_all.append(wts)
        xl, sv = _layer_fwd(xl, mod[l:l + 1], g_norm[l:l + 1], g_cq[l:l + 1], g_ckv[l:l + 1], wts, tabs, lg, lay,
                            deps)
        saved.append(sv)
        if l + 1 < nl:
            gathered = _push_wait(sems, srcs, lands, False, [xl], "gather_wait_%d" % (l + 1))
    loss_lanes, dx, dg_final = _final_loss(xl, g_final.reshape(1, d), target)

    small = [None] * nl
    flying = {l: [] for l in range(nl)}
    recv = {}
    shard_cols = (w_in.shape[2], w_uq.shape[2], w_ukv.shape[2])

    def pusher(l):
        def push(group, arrays):
            sems, srcs, lands, token = _push_start(arrays, True, [], "exchange_start_%d%s" % (l, group))
            flying[l].append((group, sems, srcs, lands))
            return [token]
        return push

    def land(l, after):
        got = {}
        for group, sems, srcs, lands in flying[l]:
            got[group] = _push_wait(sems, srcs, lands, True, after, "exchange_wait_%d%s" % (l, group))
        recv[l] = list(got["b"]) + list(got["a"])

    for l in reversed(range(nl)):
        dx, small[l] = _layer_bwd(dx, saved[l], mod[l:l + 1], g_norm[l:l + 1], g_cq[l:l + 1], g_ckv[l:l + 1],
                                  wts_all[l], tabs, lg, lay, [], shard_cols, pusher(l))
        if l + 1 < nl:
            land(l + 1, [dx])
    grad_x = dx.reshape(x.shape)

    pack = jnp.concatenate(
        [jnp.concatenate([sm[k] for sm in small], axis=0).reshape(-1)
         for k in ("dmod", "g_norm", "g_cq", "g_ckv")] + [dg_final.reshape(-1), loss_lanes.reshape(-1)])
    (pack_all,) = _all_gather([pack.reshape(-1, LANES)], None, "gather_small", vmem=True)
    tot = _sum_devices(pack_all).reshape(-1)
    sizes = [nl * 3 * d, nl * d, nl * rank, nl * rank, d]
    offs = np.cumsum([0] + sizes)
    grad_b_mod = tot[offs[0]:offs[1]].reshape(nl, 3 * d)
    grad_g_norm = tot[offs[1]:offs[2]].reshape(nl, d)
    grad_g_cq = tot[offs[2]:offs[3]].reshape(nl, rank)
    grad_g_ckv = tot[offs[3]:offs[4]].reshape(nl, rank)
    grad_g_final = tot[offs[4]:offs[5]]
    loss = tot[offs[5]]
    dmod_all = pack_all.reshape(N_DEV, -1)[:, :sizes[0]].reshape(N_DEV, nl, 3 * d)
    dmod_mine = lax.dynamic_slice_in_dim(dmod_all, me * n_mod, n_mod, axis=2).transpose(1, 0, 2)

    out = {}
    out["w_mod"] = _adam_mod(c_all.T, dmod_mine, w_mod, m_w_mod, v_w_mod)
    sharded = (("w_in", w_in, m_w_in, v_w_in), ("w_uq", w_uq, m_w_uq, v_w_uq), ("w_ukv", w_ukv, m_w_ukv, v_w_ukv),
               ("w_ret_proj", w_ret_proj, m_w_ret_proj, v_w_ret_proj),
               ("w_mla_proj", w_mla_proj, m_w_mla_proj, v_w_mla_proj), ("w_out", w_out, m_w_out, v_w_out))
    if nl > 1:
        for i, (key, w, m, v) in enumerate(sharded):
            out[key] = _adam_sharded([recv[l][i] for l in range(1, nl)], 1, w, m, v, None, "adam_" + key)
    land(0, [out["w_mod"][0]] + [out[key][0] for key, _, _, _ in sharded if key in out])
    for i, (key, w, m, v) in enumerate(sharded):
        out[key] = _adam_sharded([recv[0][i]], 0, w, m, v, out.get(key), "adam0_" + key)
    for key, g, w, m, v in (("b_mod", grad_b_mod, b_mod, m_b_mod, v_b_mod),
                            ("g_norm", grad_g_norm, g_norm, m_g_norm, v_g_norm),
                            ("g_cq", grad_g_cq, g_cq, m_g_cq, v_g_cq),
                            ("g_ckv", grad_g_ckv, g_ckv, m_g_ckv, v_g_ckv),
                            ("g_final", grad_g_final.reshape(1, d), g_final.reshape(1, d),
                             m_g_final.reshape(1, d), v_g_final.reshape(1, d))):
        out[key] = (g,) + tuple(_adam_small(g, w, m, v, "adam_" + key))
    out["g_final"] = tuple(a.reshape(d) for a in out["g_final"])

    names = ("w_mod", "b_mod", "g_norm", "w_in", "g_cq", "g_ckv", "w_uq", "w_ukv", "w_ret_proj",
             "w_mla_proj", "w_out", "g_final")
    return (loss, grad_x, *[out[k][0] for k in names], *[out[k][1] for k in names],
            *[out[k][2] for k in names], *[out[k][3] for k in names])
```

```python
import functools
import itertools

import jax
import jax.numpy as jnp
import numpy as np
from jax import lax
from jax.experimental import pallas as pl
from jax.experimental.pallas import tpu as pltpu

F32 = jnp.float32
BF16 = jnp.bfloat16

N_DEV = 8
CHUNK = 64
EPS = 1e-6
NEG_INF = -1e30
ROPE_BASE = 10000.0
LANES = 128

RET_HEADS = 8
RET_DK = 128
RET_DV = 256
MLA_HEADS = 16
MLA_NOPE = 128
MLA_ROPE = 64
MLA_DV = 128
MLA_QW = 256
QK_SCALE = (MLA_NOPE + MLA_ROPE) ** -0.5
QK_LOG2_SCALE = QK_SCALE * 1.4426950408889634
LN2 = 0.6931471805599453

ADAM_LR = 0.001
ADAM_B1 = 0.9
ADAM_B2 = 0.999
ADAM_EPS = 1e-08
ADAM_WD = 0.01
ADAM_STEP = 10

VMEM_LIMIT_BYTES = 56 * 1024 * 1024
MESH = pl.DeviceIdType.MESH
ANY = pl.BlockSpec(memory_space=pl.ANY)


def _cp(*sem):
    return pltpu.CompilerParams(dimension_semantics=sem if sem else None,
                                vmem_limit_bytes=VMEM_LIMIT_BYTES)


def _tile(n, cap):
    best = None
    t = LANES
    while t <= min(n, cap):
        if n % t == 0:
            best = t
        t += LANES
    return best if best is not None else n


def _rtile(n, cap):
    t = cap
    while t > 8 and n % t:
        t //= 2
    return t if n % t == 0 else n


def _sigmoid(x):
    return 1.0 / (1.0 + jnp.exp(-x))


def _dot(a, b):
    return lax.dot_general(a, b, (((1,), (0,)), ((), ())), preferred_element_type=F32)


def _dot_nt(a, b):
    return lax.dot_general(a, b, (((1,), (1,)), ((), ())), preferred_element_type=F32)


def _dot_tn(a, b):
    return lax.dot_general(a, b, (((0,), (0,)), ((), ())), preferred_element_type=F32)


def _roll(x, s):
    return pltpu.roll(x, s, 1)


class Layout:
    def __init__(self, d_model, q_rank, kv_rank):
        assert q_rank == kv_rank
        self.d = d_model
        self.rank = q_rank
        self.ret_w = 2 * RET_DK + 2 * RET_DV
        self.ret_qk = RET_HEADS * RET_DK
        self.ret_v = RET_HEADS * RET_DV
        self.mla_v = MLA_HEADS * MLA_DV
        widths = {"bg": 2 * d_model, "mg": self.mla_v, "ret": RET_HEADS * self.ret_w,
                  "cqkv": 2 * q_rank, "kr": LANES}
        blocks = {"bg": 2 * d_model, "mg": self.mla_v, "ret": self.ret_w,
                  "cqkv": 2 * q_rank, "kr": LANES}
        for order in itertools.permutations(widths):
            off, offs, ok = 0, {}, True
            for name in order:
                if off % blocks[name]:
                    ok = False
                    break
                offs[name] = off
                off += widths[name]
            if ok:
                break
        assert ok, "no aligned layout"
        self.order, self.off, self.width, self.total = order, offs, widths, off
        lo, o = {}, 0
        for name, w in (("rq", self.ret_qk), ("rk", self.ret_qk), ("rv", self.ret_v),
                        ("rg", self.ret_v), ("cq", q_rank), ("ckv", kv_rank), ("kr", MLA_ROPE),
                        ("mg", self.mla_v), ("bg", 2 * d_model)):
            lo[name] = (o, w)
            o += w
        self.logical, self.d_in = lo, o

    def pieces(self):
        lo = self.logical
        out = []
        for name in self.order:
            if name == "bg":
                out.append(lo["bg"])
            elif name == "mg":
                out.append(lo["mg"])
            elif name == "ret":
                for h in range(RET_HEADS):
                    out.append((lo["rq"][0] + h * RET_DK, RET_DK))
                    out.append((lo["rk"][0] + h * RET_DK, RET_DK))
                    out.append((lo["rv"][0] + h * RET_DV, RET_DV))
                    out.append((lo["rg"][0] + h * RET_DV, RET_DV))
            elif name == "cqkv":
                out.append((lo["cq"][0], 2 * self.rank))
            elif name == "kr":
                out.append(lo["kr"])
                out.append((None, LANES - MLA_ROPE))
        return out


RELAYOUT_CHUNK = 512


def _relayout_plan(lay, n_local):
    plan, off = [], 0
    for start, width in lay.pieces():
        done = 0
        while done < width:
            w = min(RELAYOUT_CHUNK, width - done)
            srcs = []
            if start is not None:
                lo, hi = start + done, start + done + w
                while lo < hi:
                    j = lo // n_local
                    end = min(hi, (j + 1) * n_local)
                    srcs.append((j, lo - j * n_local, end - j * n_local))
                    lo = end
            plan.append((off + done, w, srcs))
            done += w
        off += width
    merged = []
    for p, w, srcs in plan:
        if merged and merged[-1][0] % LANES == 0 and (merged[-1][1] % LANES) and p == merged[-1][0] + merged[-1][1]:
            q, qw, qs = merged.pop()
            merged.append((q, qw + w, qs + ([("pad", w)] if not srcs else srcs)))
        else:
            merged.append((p, w, srcs))
    return merged


def _assemble_w_in(g, lay):
    _, d, n_local = g.shape
    tr = _rtile(d, 256)
    plan = _relayout_plan(lay, n_local)

    def body(g_ref, o_ref):
        for p, w, srcs in plan:
            parts = []
            for src in srcs:
                if src[0] == "pad":
                    parts.append(jnp.zeros((tr, src[1]), F32))
                else:
                    j, a, b = src
                    parts.append(g_ref[j, :, a:b].astype(F32))
            if not parts:
                parts = [jnp.zeros((tr, w), F32)]
            val = parts[0] if len(parts) == 1 else jnp.concatenate(parts, axis=1)
            o_ref[:, p:p + w] = val.astype(o_ref.dtype)

    return pl.pallas_call(
        body, name="assemble_w_in", grid=(d // tr,),
        in_specs=[pl.BlockSpec((N_DEV, tr, n_local), lambda i: (0, i, 0))],
        out_specs=pl.BlockSpec((tr, lay.total), lambda i: (i, 0)),
        out_shape=jax.ShapeDtypeStruct((d, lay.total), g.dtype),
        compiler_params=_cp("parallel"),
    )(g)


def _scatter_dw_in(dw, lay, n_local):
    d = dw.shape[0]
    tr = _rtile(d, 256)
    plan = _relayout_plan(lay, n_local)
    runs = [[] for _ in range(N_DEV)]
    for p, w, srcs in plan:
        at = p
        for src in srcs:
            if src[0] == "pad":
                at += src[1]
                continue
            j, a, b = src
            runs[j].append((a, b, at))
            at += b - a
    for r in runs:
        r.sort()

    def body(dw_ref, o_ref):
        for j in range(N_DEV):
            parts = [dw_ref[:, at:at + (b - a)].astype(F32) for a, b, at in runs[j]]
            val = parts[0] if len(parts) == 1 else jnp.concatenate(parts, axis=1)
            o_ref[j] = val.astype(BF16)

    return pl.pallas_call(
        body, name="scatter_dw_in", grid=(d // tr,),
        in_specs=[pl.BlockSpec((tr, lay.total), lambda i: (i, 0))],
        out_specs=pl.BlockSpec((N_DEV, tr, n_local), lambda i: (0, i, 0)),
        out_shape=jax.ShapeDtypeStruct((N_DEV, d, n_local), BF16),
        compiler_params=_cp("parallel"),
    )(dw)


def _uq_to_physical(w):
    k = w.shape[0]
    w3 = w.reshape(k, MLA_HEADS, MLA_NOPE + MLA_ROPE)
    pad = jnp.zeros((k, MLA_HEADS, MLA_QW - MLA_NOPE - MLA_ROPE), w.dtype)
    return jnp.concatenate([w3, pad], axis=2).reshape(k, MLA_HEADS * MLA_QW)


def _uq_to_logical(w):
    k = w.shape[0]
    return w.reshape(k, MLA_HEADS, MLA_QW)[:, :, :MLA_NOPE + MLA_ROPE].reshape(k, -1)


def _ukv_to_physical(w):
    k = w.shape[0]
    w3 = w.reshape(k, MLA_HEADS, MLA_NOPE + MLA_DV)
    return jnp.concatenate([w3[:, :, :MLA_NOPE].reshape(k, -1), w3[:, :, MLA_NOPE:].reshape(k, -1)], axis=1)


def _ukv_to_logical(w):
    k = w.shape[0]
    kn = w[:, :MLA_HEADS * MLA_NOPE].reshape(k, MLA_HEADS, MLA_NOPE)
    v = w[:, MLA_HEADS * MLA_NOPE:].reshape(k, MLA_HEADS, MLA_DV)
    return jnp.concatenate([kn, v], axis=2).reshape(k, -1)


def _matmul(a, b, *, ta=False, tb=False, out_dtype=F32, name, tm_cap=1024, tn_cap=1024, tk_cap=2048, deps=()):
    m, k = (a.shape[1], a.shape[0]) if ta else a.shape
    n = b.shape[0] if tb else b.shape[1]
    assert k == (b.shape[1] if tb else b.shape[0])
    tm, tn, tk = _tile(m, tm_cap), _tile(n, tn_cap), _tile(k, tk_cap)
    nk = k // tk

    def body(a_ref, b_ref, *rest):
        dims = (((0 if ta else 1,), (1 if tb else 0,)), ((), ()))
        part = lax.dot_general(a_ref[...].astype(BF16), b_ref[...].astype(BF16), dims, preferred_element_type=F32)
        if nk == 1:
            rest[len(deps)][...] = part.astype(out_dtype)
            return
        o_ref, acc_ref = rest[len(deps):]
        kk = pl.program_id(2)

        @pl.when(kk == 0)
        def _():
            acc_ref[...] = part

        @pl.when(jnp.logical_and(kk > 0, kk < nk - 1))
        def _():
            acc_ref[...] += part

        @pl.when(kk == nk - 1)
        def _():
            o_ref[...] = (acc_ref[...] + part).astype(o_ref.dtype)

    a_spec = pl.BlockSpec((tk, tm), lambda i, j, kk: (kk, i)) if ta else pl.BlockSpec((tm, tk), lambda i, j, kk: (i, kk))
    b_spec = pl.BlockSpec((tn, tk), lambda i, j, kk: (j, kk)) if tb else pl.BlockSpec((tk, tn), lambda i, j, kk: (kk, j))
    return pl.pallas_call(
        body, name=name, grid=(m // tm, n // tn, nk),
        in_specs=[a_spec, b_spec] + [ANY] * len(deps),
        out_specs=pl.BlockSpec((tm, tn), lambda i, j, kk: (i, j)),
        out_shape=jax.ShapeDtypeStruct((m, n), out_dtype),
        scratch_shapes=[pltpu.VMEM((tm, tn), F32)] if nk > 1 else [],
        compiler_params=_cp("parallel", "parallel", "arbitrary"),
    )(a, b, *deps)


def _row(tm, w, cb=0):
    return pl.BlockSpec((tm, w), lambda i, cb=cb: (i, cb))


def _vec(w, cb=0):
    return pl.BlockSpec((1, w), lambda i, cb=cb: (0, cb))


def _dproj_out(dproj, s, lay, tm, name):
    w = lay.width[name]
    cb = lay.off[name] // w
    spec = _row(tm, w, cb)
    shape = jax.ShapeDtypeStruct((s, lay.total), BF16)
    return spec, shape


def _norm_mod_fwd(x, g, scale, shift, deps):
    s, d = x.shape
    tm = _rtile(s, 256)

    def body(x_ref, g_ref, sc_ref, sh_ref, *rest):
        h_ref = rest[len(deps)]
        xv = x_ref[...]
        xh = xv * lax.rsqrt(jnp.mean(xv * xv, axis=-1, keepdims=True) + EPS)
        h_ref[...] = ((xh * g_ref[...]) * (1.0 + sc_ref[...]) + sh_ref[...]).astype(BF16)

    return pl.pallas_call(
        body, name="norm_mod_fwd", grid=(s // tm,),
        in_specs=[_row(tm, d), _vec(d), _vec(d), _vec(d)] + [ANY] * len(deps),
        out_specs=_row(tm, d), out_shape=jax.ShapeDtypeStruct((s, d), BF16),
        compiler_params=_cp("parallel"),
    )(x, g, scale, shift, *deps)


def _norm_mod_bwd(x, g, scale, dh, dres):
    s, d = x.shape
    tm = _rtile(s, 256)

    def body(x_ref, g_ref, sc_ref, dh_ref, dres_ref, dx_ref, dsh_ref, dsc_ref, dg_ref):
        @pl.when(pl.program_id(0) == 0)
        def _():
            dsh_ref[...] = jnp.zeros_like(dsh_ref)
            dsc_ref[...] = jnp.zeros_like(dsc_ref)
            dg_ref[...] = jnp.zeros_like(dg_ref)

        xv, gv, dhv = x_ref[...], g_ref[...], dh_ref[...]
        rstd = lax.rsqrt(jnp.mean(xv * xv, axis=-1, keepdims=True) + EPS)
        xh = xv * rstd
        dy = dhv * (1.0 + sc_ref[...])
        dxh = dy * gv
        dx_ref[...] = dres_ref[...] + rstd * (dxh - xh * jnp.mean(dxh * xh, axis=-1, keepdims=True))
        dsh_ref[...] += jnp.sum(dhv, axis=0, keepdims=True)
        dsc_ref[...] += jnp.sum(dhv * (xh * gv), axis=0, keepdims=True)
        dg_ref[...] += jnp.sum(dy * xh, axis=0, keepdims=True)

    vec = jax.ShapeDtypeStruct((1, d), F32)
    return pl.pallas_call(
        body, name="norm_mod_bwd", grid=(s // tm,),
        in_specs=[_row(tm, d), _vec(d), _vec(d), _row(tm, d), _row(tm, d)],
        out_specs=[_row(tm, d), _vec(d), _vec(d), _vec(d)],
        out_shape=[jax.ShapeDtypeStruct((s, d), F32), vec, vec, vec],
        compiler_params=_cp("arbitrary"),
    )(x, g, scale, dh, dres)


def _final_loss(x, g, target):
    s, d = x.shape
    tm = _rtile(s, 256)

    def body(x_ref, g_ref, t_ref, l_ref, dx_ref, dg_ref):
        @pl.when(pl.program_id(0) == 0)
        def _():
            l_ref[...] = jnp.zeros_like(l_ref)
            dg_ref[...] = jnp.zeros_like(dg_ref)

        xv, gv = x_ref[...], g_ref[...]
        rstd = lax.rsqrt(jnp.mean(xv * xv, axis=-1, keepdims=True) + EPS)
        xh = xv * rstd
        err = xh * gv - t_ref[...]
        row = jnp.mean(err * err, axis=-1, keepdims=True)
        l_ref[...] += 0.5 * jnp.sum(row, axis=0, keepdims=True)
        dy = err / d
        dxh = dy * gv
        dx_ref[...] = rstd * (dxh - xh * jnp.mean(dxh * xh, axis=-1, keepdims=True))
        dg_ref[...] += jnp.sum(dy * xh, axis=0, keepdims=True)

    return pl.pallas_call(
        body, name="final_loss", grid=(s // tm,),
        in_specs=[_row(tm, d), _vec(d), _row(tm, d)],
        out_specs=[_vec(LANES), _row(tm, d), _vec(d)],
        out_shape=[jax.ShapeDtypeStruct((1, LANES), F32), jax.ShapeDtypeStruct((s, d), F32),
                   jax.ShapeDtypeStruct((1, d), F32)],
        compiler_params=_cp("arbitrary"),
    )(x, g, target)


def _resid_fwd(x, out, gate):
    s, d = x.shape
    tm = _rtile(s, 256)

    def body(x_ref, o_ref, g_ref, y_ref):
        y_ref[...] = x_ref[...] + g_ref[...] * o_ref[...]

    return pl.pallas_call(
        body, name="resid_fwd", grid=(s // tm,),
        in_specs=[_row(tm, d), _row(tm, d), _vec(d)],
        out_specs=_row(tm, d), out_shape=jax.ShapeDtypeStruct((s, d), F32),
        compiler_params=_cp("parallel"),
    )(x, out, gate)


def _resid_bwd(dxn, out, gate, deps):
    s, d = dxn.shape
    tm = _rtile(s, 256)

    def body(dx_ref, o_ref, g_ref, *rest):
        do_ref, dg_ref = rest[len(deps):]

        @pl.when(pl.program_id(0) == 0)
        def _():
            dg_ref[...] = jnp.zeros_like(dg_ref)

        dxv = dx_ref[...]
        do_ref[...] = (dxv * g_ref[...]).astype(BF16)
        dg_ref[...] += jnp.sum(dxv * o_ref[...], axis=0, keepdims=True)

    return pl.pallas_call(
        body, name="resid_bwd", grid=(s // tm,),
        in_specs=[_row(tm, d), _row(tm, d), _vec(d)] + [ANY] * len(deps),
        out_specs=[_row(tm, d), _vec(d)],
        out_shape=[jax.ShapeDtypeStruct((s, d), BF16), jax.ShapeDtypeStruct((1, d), F32)],
        compiler_params=_cp("arbitrary"),
    )(dxn, out, gate, *deps)


def _merge_fwd(y_ret, y_mla, proj, lay):
    s, d = y_ret.shape
    tm = _rtile(s, 256)
    cb = lay.off["bg"] // (2 * d)

    def body(a_ref, b_ref, bg_ref, m_ref):
        sg = _sigmoid(bg_ref[...])
        m_ref[...] = (sg[:, :d] * a_ref[...] + sg[:, d:] * b_ref[...]).astype(BF16)

    return pl.pallas_call(
        body, name="merge_fwd", grid=(s // tm,),
        in_specs=[_row(tm, d), _row(tm, d), _row(tm, 2 * d, cb)],
        out_specs=_row(tm, d), out_shape=jax.ShapeDtypeStruct((s, d), BF16),
        compiler_params=_cp("parallel"),
    )(y_ret, y_mla, proj)


def _merge_bwd(dm, y_ret, y_mla, proj, lay):
    s, d = dm.shape
    tm = _rtile(s, 256)
    cb = lay.off["bg"] // (2 * d)
    dp_spec, dp_shape = _dproj_out(None, s, lay, tm, "bg")

    def body(dm_ref, a_ref, b_ref, bg_ref, da_ref, db_ref, dp_ref):
        sg = _sigmoid(bg_ref[...])
        dmv = dm_ref[...]
        ga, gb = sg[:, :d], sg[:, d:]
        da_ref[...] = (dmv * ga).astype(BF16)
        db_ref[...] = (dmv * gb).astype(BF16)
        dp_ref[:, :d] = (dmv * a_ref[...] * ga * (1.0 - ga)).astype(BF16)
        dp_ref[:, d:] = (dmv * b_ref[...] * gb * (1.0 - gb)).astype(BF16)

    act = jax.ShapeDtypeStruct((s, d), BF16)
    return pl.pallas_call(
        body, name="merge_bwd", grid=(s // tm,),
        in_specs=[_row(tm, d), _row(tm, d), _row(tm, d), _row(tm, 2 * d, cb)],
        out_specs=[_row(tm, d), _row(tm, d), dp_spec],
        out_shape=[act, act, dp_shape],
        compiler_params=_cp("parallel"),
    )(dm, y_ret, y_mla, proj)


def _mla_prep(proj, g_cq, g_ckv, lay):
    s = proj.shape[0]
    r = lay.rank
    tm = _rtile(s, 512)
    cb = lay.off["cqkv"] // (2 * r)

    def body(p_ref, gq_ref, gk_ref, q_ref, k_ref):
        pv = p_ref[...]
        for lo, g_ref, o_ref in ((0, gq_ref, q_ref), (r, gk_ref, k_ref)):
            xv = pv[:, lo:lo + r]
            xh = xv * lax.rsqrt(jnp.mean(xv * xv, axis=-1, keepdims=True) + EPS)
            o_ref[...] = (xh * g_ref[...]).astype(BF16)

    act = jax.ShapeDtypeStruct((s, r), BF16)
    return pl.pallas_call(
        body, name="mla_prep", grid=(s // tm,),
        in_specs=[_row(tm, 2 * r, cb), _vec(r), _vec(r)],
        out_specs=[_row(tm, r), _row(tm, r)], out_shape=[act, act],
        compiler_params=_cp("parallel"),
    )(proj, g_cq, g_ckv)


def _mla_prep_bwd(proj, dqn, dkn, g_cq, g_ckv, dproj, lay):
    s = proj.shape[0]
    r = lay.rank
    tm = _rtile(s, 512)
    cb = lay.off["cqkv"] // (2 * r)
    dp_spec, dp_shape = _dproj_out(dproj, s, lay, tm, "cqkv")

    def body(p_ref, dq_ref, dk_ref, gq_ref, gk_ref, _, dp_ref, dgq_ref, dgk_ref):
        @pl.when(pl.program_id(0) == 0)
        def _():
            dgq_ref[...] = jnp.zeros_like(dgq_ref)
            dgk_ref[...] = jnp.zeros_like(dgk_ref)

        pv = p_ref[...]
        for lo, g_ref, d_ref, dg_ref in ((0, gq_ref, dq_ref, dgq_ref), (r, gk_ref, dk_ref, dgk_ref)):
            xv = pv[:, lo:lo + r]
            rstd = lax.rsqrt(jnp.mean(xv * xv, axis=-1, keepdims=True) + EPS)
            xh = xv * rstd
            dy = d_ref[...]
            dxh = dy * g_ref[...]
            dp_ref[:, lo:lo + r] = (rstd * (dxh - xh * jnp.mean(dxh * xh, axis=-1, keepdims=True))).astype(BF16)
            dg_ref[...] += jnp.sum(dy * xh, axis=0, keepdims=True)

    vec = jax.ShapeDtypeStruct((1, r), F32)
    return pl.pallas_call(
        body, name="mla_prep_bwd", grid=(s // tm,),
        in_specs=[_row(tm, 2 * r, cb), _row(tm, r), _row(tm, r), _vec(r), _vec(r), ANY],
        out_specs=[dp_spec, _vec(r), _vec(r)], out_shape=[dp_shape, vec, vec],
        input_output_aliases={5: 0},
        compiler_params=_cp("arbitrary"),
    )(proj, dqn, dkn, g_cq, g_ckv, dproj)


def _rope_tile(t, a, b, c):
    return t * a + _roll(t, 96) * b + _roll(t, 32) * c


def _rope_tile_bwd(dy, a, b, c):
    return dy * a + _roll(dy * b, 32) + _roll(dy * c, 96)


def _attn_block(s):
    return _rtile(s, 512)


def _qk_prep(qp, kvp, proj, ta, tb, tc, lay):
    s = qp.shape[0]
    hq = MLA_HEADS * MLA_QW
    hv = MLA_HEADS * MLA_DV
    blk = _attn_block(s)
    tm = _rtile(blk, 256)
    per = blk // tm
    kr_cb = lay.off["kr"] // LANES

    def body(q_ref, kv_ref, kr_ref, a_ref, b_ref, c_ref, qc_ref, kc_ref, v_ref, kt_ref, vt_ref):
        a, b, c = a_ref[...], b_ref[...], c_ref[...]
        krot = _rope_tile(kr_ref[...], a, b, c)
        krot_b, krot_t = krot.astype(BF16), krot.T.astype(BF16)
        for h in range(MLA_HEADS):
            q0 = h * MLA_QW
            qc_ref[:, q0:q0 + MLA_NOPE] = (q_ref[:, q0:q0 + MLA_NOPE] * QK_LOG2_SCALE).astype(BF16)
            qc_ref[:, q0 + MLA_NOPE:q0 + MLA_QW] = (
                _rope_tile(q_ref[:, q0 + MLA_NOPE:q0 + MLA_QW], a, b, c) * QK_LOG2_SCALE).astype(BF16)
            kn = kv_ref[:, h * MLA_NOPE:(h + 1) * MLA_NOPE]
            kc_ref[:, q0:q0 + MLA_NOPE] = kn.astype(BF16)
            kc_ref[:, q0 + MLA_NOPE:q0 + MLA_QW] = krot_b
            kt_ref[h, :MLA_NOPE, :] = kn.T.astype(BF16)
            kt_ref[h, MLA_NOPE:, :] = krot_t
            vh = kv_ref[:, (MLA_HEADS + h) * MLA_NOPE:(MLA_HEADS + h + 1) * MLA_NOPE]
            v_ref[:, h * MLA_DV:(h + 1) * MLA_DV] = vh.astype(BF16)
            vt_ref[h] = vh.T.astype(BF16)

    return pl.pallas_call(
        body, name="qk_prep", grid=(s // tm,),
        in_specs=[_row(tm, hq), _row(tm, hq), _row(tm, LANES, kr_cb), _row(tm, LANES), _row(tm, LANES), _row(tm, LANES)],
        out_specs=[_row(tm, hq), _row(tm, hq), _row(tm, hv),
                   pl.BlockSpec((MLA_HEADS, None, MLA_QW, tm), lambda i: (0, i // per, 0, i % per)),
                   pl.BlockSpec((MLA_HEADS, None, MLA_DV, tm), lambda i: (0, i // per, 0, i % per))],
        out_shape=[jax.ShapeDtypeStruct((s, hq), BF16), jax.ShapeDtypeStruct((s, hq), BF16),
                   jax.ShapeDtypeStruct((s, hv), BF16),
                   jax.ShapeDtypeStruct((MLA_HEADS, s // blk, MLA_QW, blk), BF16),
                   jax.ShapeDtypeStruct((MLA_HEADS, s // blk, MLA_DV, blk), BF16)],
        compiler_params=_cp("parallel"),
    )(qp, kvp, proj, ta, tb, tc)


def _kv_bwd_prep(dk_cat, dv, ta, tb, tc, dproj, lay):
    s = dk_cat.shape[0]
    hq = MLA_HEADS * MLA_QW
    hv = MLA_HEADS * MLA_DV
    tm = _rtile(s, 256)
    dp_spec, dp_shape = _dproj_out(dproj, s, lay, tm, "kr")

    def body(dk_ref, dv_ref, a_ref, b_ref, c_ref, _, dkv_ref, dp_ref):
        acc = jnp.zeros((tm, LANES), F32)
        for h in range(MLA_HEADS):
            q0 = h * MLA_QW
            dkv_ref[:, h * MLA_NOPE:(h + 1) * MLA_NOPE] = dk_ref[:, q0:q0 + MLA_NOPE].astype(BF16)
            acc = acc + dk_ref[:, q0 + MLA_NOPE:q0 + MLA_QW]
        dkv_ref[:, MLA_HEADS * MLA_NOPE:] = dv_ref[...].astype(BF16)
        dp_ref[...] = _rope_tile_bwd(acc, a_ref[...], b_ref[...], c_ref[...]).astype(BF16)

    return pl.pallas_call(
        body, name="kv_bwd_prep", grid=(s // tm,),
        in_specs=[_row(tm, hq), _row(tm, hv), _row(tm, LANES), _row(tm, LANES), _row(tm, LANES), ANY],
        out_specs=[_row(tm, hq), dp_spec],
        out_shape=[jax.ShapeDtypeStruct((s, hq), BF16), dp_shape],
        input_output_aliases={5: 1},
        compiler_params=_cp("parallel"),
    )(dk_cat, dv, ta, tb, tc, dproj)


def _mla_gate_bwd(du, o, proj, dproj, lay):
    s, vw = du.shape
    tm = _rtile(s, 256)
    cb = lay.off["mg"] // vw
    dp_spec, dp_shape = _dproj_out(dproj, s, lay, tm, "mg")

    def body(du_ref, o_ref, g_ref, _, do_ref, dp_ref):
        gv, duv = g_ref[...], du_ref[...]
        sg = _sigmoid(gv)
        do_ref[...] = (duv * (gv * sg)).astype(BF16)
        dp_ref[...] = (duv * o_ref[...] * (sg + gv * sg * (1.0 - sg))).astype(BF16)

    return pl.pallas_call(
        body, name="mla_gate_bwd", grid=(s // tm,),
        in_specs=[_row(tm, vw), _row(tm, vw), _row(tm, vw, cb), ANY],
        out_specs=[_row(tm, vw), dp_spec],
        out_shape=[jax.ShapeDtypeStruct((s, vw), BF16), dp_shape],
        input_output_aliases={3: 1},
        compiler_params=_cp("parallel"),
    )(du, o, proj, dproj)


def _ret_tables(lg):
    ri = lax.broadcasted_iota(jnp.int32, (CHUNK, CHUNK), 0).astype(F32)
    ci = lax.broadcasted_iota(jnp.int32, (CHUNK, CHUNK), 1).astype(F32)
    col = lax.broadcasted_iota(jnp.int32, (CHUNK, 1), 0).astype(F32)
    dmat = jnp.exp(jnp.abs(ri - ci) * lg)
    xi = jnp.exp((col + 1.0) * lg)
    zeta = jnp.exp((CHUNK - 1.0 - col) * lg)
    decay = jnp.exp(jnp.full((1, 1), CHUNK, F32) * lg)
    return dmat, xi, zeta, decay


def _ret_qkvg(blk, cs, sn):
    dv = RET_DV
    q = blk[:, :RET_DK]
    k = blk[:, RET_DK:2 * RET_DK]
    q = q * cs + _roll(q, RET_DK // 2) * sn
    k = (k * cs + _roll(k, RET_DK // 2) * sn) * (RET_DK ** -0.5)
    return q, k, blk[:, 2 * RET_DK:2 * RET_DK + dv], blk[:, 2 * RET_DK + dv:]


def _group_norm(o):
    mu = jnp.mean(o, axis=-1, keepdims=True)
    oc = o - mu
    rstd = lax.rsqrt(jnp.mean(oc * oc, axis=-1, keepdims=True) + EPS)
    return oc * rstd, rstd


def _ret_fwd(proj, lg, cosr, sinr, lay):
    s = proj.shape[0]
    dv, w = RET_DV, lay.ret_w
    tb = _rtile(s, 512)
    nb, nch = s // tb, tb // CHUNK
    cb0 = lay.off["ret"] // w

    def body(lg_ref, p_ref, cos_ref, sin_ref, o_ref, u_ref, st_ref, state):
        @pl.when(pl.program_id(1) == 0)
        def _():
            state[...] = jnp.zeros_like(state)

        dmat, xi, zeta, decay = _ret_tables(lg_ref[pl.program_id(0)])
        for c in range(nch):
            rows = slice(c * CHUNK, (c + 1) * CHUNK)
            q, k, v, g = _ret_qkvg(p_ref[rows, :], cos_ref[rows, :], sin_ref[rows, :])
            qb, kb, vb = q.astype(BF16), k.astype(BF16), v.astype(BF16)
            sc = _dot_nt(qb, kb) * dmat
            st = state[...]
            o = _dot(sc.astype(BF16), vb) + _dot((q * xi).astype(BF16), st.astype(BF16))
            st_ref[c] = st.astype(BF16)
            state[...] = st * decay + _dot_tn((k * zeta).astype(BF16), vb)
            o_ref[rows, :] = o
            n, _ = _group_norm(o)
            u_ref[rows, :] = (n * (g * _sigmoid(g))).astype(BF16)

    return pl.pallas_call(
        body, name="ret_fwd", grid=(RET_HEADS, nb),
        in_specs=[pl.BlockSpec(memory_space=pltpu.SMEM),
                  pl.BlockSpec((tb, w), lambda h, b: (b, cb0 + h)),
                  pl.BlockSpec((tb, RET_DK), lambda h, b: (b, 0)),
                  pl.BlockSpec((tb, RET_DK), lambda h, b: (b, 0))],
        out_specs=[pl.BlockSpec((tb, dv), lambda h, b: (b, h)),
                   pl.BlockSpec((tb, dv), lambda h, b: (b, h)),
                   pl.BlockSpec((None, nch, RET_DK, dv), lambda h, b: (h, b, 0, 0))],
        out_shape=[jax.ShapeDtypeStruct((s, RET_HEADS * dv), F32),
                   jax.ShapeDtypeStruct((s, RET_HEADS * dv), BF16),
                   jax.ShapeDtypeStruct((RET_HEADS, s // CHUNK, RET_DK, dv), BF16)],
        scratch_shapes=[pltpu.VMEM((RET_DK, dv), F32)],
        compiler_params=_cp("parallel", "arbitrary"),
    )(lg, proj, cosr, sinr)


def _ret_bwd(proj, lg, cosr, sinr, o, du, states, dproj, lay):
    s = proj.shape[0]
    dv, w = RET_DV, lay.ret_w
    tb = _rtile(s, 512)
    nb, nch = s // tb, tb // CHUNK
    cb0 = lay.off["ret"] // w

    def body(lg_ref, p_ref, cos_ref, sin_ref, o_ref, du_ref, st_ref, _, dp_ref, dstate):
        @pl.when(pl.program_id(1) == 0)
        def _():
            dstate[...] = jnp.zeros_like(dstate)

        dmat, xi, zeta, decay = _ret_tables(lg_ref[pl.program_id(0)])
        for c in reversed(range(nch)):
            rows = slice(c * CHUNK, (c + 1) * CHUNK)
            cs, sn = cos_ref[rows, :], sin_ref[rows, :]
            q, k, v, g = _ret_qkvg(p_ref[rows, :], cs, sn)
            qb, kb, vb = q.astype(BF16), k.astype(BF16), v.astype(BF16)
            n, rstd = _group_norm(o_ref[rows, :])
            sg = _sigmoid(g)
            duv = du_ref[rows, :]
            dn = duv * (g * sg)
            dg = duv * n * (sg + g * sg * (1.0 - sg))
            do = rstd * (dn - jnp.mean(dn, axis=-1, keepdims=True) - n * jnp.mean(dn * n, axis=-1, keepdims=True))
            dob = do.astype(BF16)
            rb = st_ref[c]
            drb = dstate[...].astype(BF16)
            sc = (_dot_nt(qb, kb) * dmat).astype(BF16)
            dsc = (_dot_nt(dob, vb) * dmat).astype(BF16)
            qx = (q * xi).astype(BF16)
            kz = (k * zeta).astype(BF16)
            dq = _dot(dsc, kb) + _dot_nt(dob, rb) * xi
            dk = (_dot_tn(dsc, qb) + _dot_nt(vb, drb) * zeta) * (RET_DK ** -0.5)
            dvv = _dot_tn(sc, dob) + _dot(kz, drb)
            dstate[...] = dstate[...] * decay + _dot_tn(qx, dob)
            dp_ref[rows, :RET_DK] = (dq * cs + _roll(dq * sn, RET_DK // 2)).astype(BF16)
            dp_ref[rows, RET_DK:2 * RET_DK] = (dk * cs + _roll(dk * sn, RET_DK // 2)).astype(BF16)
            dp_ref[rows, 2 * RET_DK:2 * RET_DK + dv] = dvv.astype(BF16)
            dp_ref[rows, 2 * RET_DK + dv:] = dg.astype(BF16)

    rev = lambda h, b: (nb - 1 - b, h)
    return pl.pallas_call(
        body, name="ret_bwd", grid=(RET_HEADS, nb),
        in_specs=[pl.BlockSpec(memory_space=pltpu.SMEM),
                  pl.BlockSpec((tb, w), lambda h, b: (nb - 1 - b, cb0 + h)),
                  pl.BlockSpec((tb, RET_DK), lambda h, b: (nb - 1 - b, 0)),
                  pl.BlockSpec((tb, RET_DK), lambda h, b: (nb - 1 - b, 0)),
                  pl.BlockSpec((tb, dv), rev),
                  pl.BlockSpec((tb, dv), rev),
                  pl.BlockSpec((None, nch, RET_DK, dv), lambda h, b: (h, nb - 1 - b, 0, 0)),
                  ANY],
        out_specs=pl.BlockSpec((tb, w), lambda h, b: (nb - 1 - b, cb0 + h)),
        out_shape=jax.ShapeDtypeStruct((s, lay.total), BF16),
        input_output_aliases={7: 0},
        scratch_shapes=[pltpu.VMEM((RET_DK, dv), F32)],
        compiler_params=_cp("parallel", "arbitrary"),
    )(lg, proj, cosr, sinr, o, du, states, dproj)


def _attn_mask(rows, cols, row0, col0, keys_on_rows):
    ri = (lax.broadcasted_iota(jnp.int32, (rows, cols), 0) + row0) // CHUNK
    ci = (lax.broadcasted_iota(jnp.int32, (rows, cols), 1) + col0) // CHUNK
    return ri <= ci if keys_on_rows else ci <= ri


def _attn_fwd(q_cat, k_cat, v_t, proj, lay):
    s = q_cat.shape[0]
    bq = _attn_block(s)
    nq = s // bq
    mg_cb = lay.off["mg"] // MLA_DV

    def body(q_ref, k_ref, vt_ref, g_ref, o_ref, u_ref, lse_ref):
        i = pl.program_id(1)
        q = q_ref[...]

        def scores(j):
            r0 = pl.multiple_of(j * bq, bq)
            return _dot_nt(k_ref[pl.ds(r0, bq), :], q)

        def update(j, sc, m, l, acc):
            mn = jnp.maximum(m, jnp.max(sc, axis=0, keepdims=True))
            p = jnp.exp2(sc - mn)
            alpha = jnp.exp2(m - mn)
            l = alpha * l + jnp.sum(p, axis=0, keepdims=True)
            acc = alpha * acc + _dot(vt_ref[j], p.astype(BF16))
            return mn, l, acc

        def hide(sc):
            return jnp.where(_attn_mask(bq, bq, 0, 0, True), sc, NEG_INF)

        def pair(j, carry, last):
            sa, sb = scores(j), scores(j + 1)
            return update(j + 1, hide(sb) if last else sb, *update(j, sa, *carry))

        init = (jnp.full((1, bq), NEG_INF, F32), jnp.zeros((1, bq), F32), jnp.zeros((MLA_DV, bq), F32))
        carry = lax.fori_loop(0, i // 2, lambda t, c: pair(2 * t, c, False), init)
        m, l, acc = lax.cond(i % 2 == 1, lambda c: pair(i - 1, c, True),
                             lambda c: update(i, hide(scores(i)), *c), carry)
        o = (acc / l).T
        gv = g_ref[...]
        o_ref[...] = o
        u_ref[...] = (o * (gv * _sigmoid(gv))).astype(BF16)
        lse_ref[...] = m + jnp.log2(l)

    return pl.pallas_call(
        body, name="attn_fwd", grid=(MLA_HEADS, nq),
        in_specs=[pl.BlockSpec((bq, MLA_QW), lambda h, i: (i, h)),
                  pl.BlockSpec((s, MLA_QW), lambda h, i: (0, h)),
                  pl.BlockSpec((None, nq, MLA_DV, bq), lambda h, i: (h, 0, 0, 0)),
                  pl.BlockSpec((bq, MLA_DV), lambda h, i: (i, mg_cb + h))],
        out_specs=[pl.BlockSpec((bq, MLA_DV), lambda h, i: (i, h)),
                   pl.BlockSpec((bq, MLA_DV), lambda h, i: (i, h)),
                   pl.BlockSpec((None, None, 1, bq), lambda h, i: (h, i, 0, 0))],
        out_shape=[jax.ShapeDtypeStruct((s, MLA_HEADS * MLA_DV), F32),
                   jax.ShapeDtypeStruct((s, MLA_HEADS * MLA_DV), BF16),
                   jax.ShapeDtypeStruct((MLA_HEADS, nq, 1, bq), F32)],
        compiler_params=_cp("parallel", "parallel"),
    )(q_cat, k_cat, v_t, proj)


def _attn_bwd(q_cat, k_cat, v, k_t, do, o, lse, ta, tb, tc):
    s = q_cat.shape[0]
    blk = _attn_block(s)
    nb = s // blk

    def body(q_ref, k_ref, v_ref, kt_ref, do_ref, o_ref, lse_ref, a_ref, b_ref, c_ref,
             dq_ref, dk_ref, dv_ref, dq_acc, delta):
        j = pl.program_id(1)

        @pl.when(j == 0)
        def _():
            dq_acc[...] = jnp.zeros_like(dq_acc)
            for i in range(nb):
                rows = slice(i * blk, (i + 1) * blk)
                prod = do_ref[rows, :].astype(F32) * o_ref[rows, :]
                delta[i] = jnp.sum(prod.T, axis=0, keepdims=True)

        kb, vb, kt = k_ref[...], v_ref[...], kt_ref[...]

        def step(i, carry, masked):
            dk, dvv = carry
            r0 = pl.multiple_of(i * blk, blk)
            q, dob = q_ref[pl.ds(r0, blk), :], do_ref[pl.ds(r0, blk), :]
            sc = _dot_nt(kb, q)
            if masked:
                sc = jnp.where(_attn_mask(blk, blk, 0, 0, True), sc, NEG_INF)
            p = jnp.exp2(sc - lse_ref[i])
            dvv = dvv + _dot(p.astype(BF16), dob)
            ds = (p * (_dot_nt(vb, dob) - delta[i])).astype(BF16)
            dk = dk + _dot(ds, q)
            dq_acc[i] = dq_acc[i] + _dot(kt, ds)
            return dk, dvv

        carry = step(j, (jnp.zeros((blk, MLA_QW), F32), jnp.zeros((blk, MLA_DV), F32)), True)
        dk, dvv = lax.fori_loop(j + 1, nb, lambda i, c: step(i, c, False), carry)
        dk_ref[...] = dk * LN2
        dv_ref[...] = dvv

        @pl.when(j == nb - 1)
        def _():
            for i in range(nb):
                rows = slice(i * blk, (i + 1) * blk)
                dq = dq_acc[i].T * QK_SCALE
                dq_ref[rows, :MLA_NOPE] = dq[:, :MLA_NOPE].astype(BF16)
                dq_ref[rows, MLA_NOPE:] = _rope_tile_bwd(dq[:, MLA_NOPE:], a_ref[rows, :], b_ref[rows, :],
                                                         c_ref[rows, :]).astype(BF16)

    tab = pl.BlockSpec((s, LANES), lambda h, j: (0, 0))
    return pl.pallas_call(
        body, name="attn_bwd", grid=(MLA_HEADS, nb),
        in_specs=[pl.BlockSpec((s, MLA_QW), lambda h, j: (0, h)),
                  pl.BlockSpec((blk, MLA_QW), lambda h, j: (j, h)),
                  pl.BlockSpec((blk, MLA_DV), lambda h, j: (j, h)),
                  pl.BlockSpec((None, None, MLA_QW, blk), lambda h, j: (h, j, 0, 0)),
                  pl.BlockSpec((s, MLA_DV), lambda h, j: (0, h)),
                  pl.BlockSpec((s, MLA_DV), lambda h, j: (0, h)),
                  pl.BlockSpec((None, nb, 1, blk), lambda h, j: (h, 0, 0, 0)),
                  tab, tab, tab],
        out_specs=[pl.BlockSpec((s, MLA_QW), lambda h, j: (0, h)),
                   pl.BlockSpec((blk, MLA_QW), lambda h, j: (j, h)),
                   pl.BlockSpec((blk, MLA_DV), lambda h, j: (j, h))],
        out_shape=[jax.ShapeDtypeStruct((s, MLA_HEADS * MLA_QW), BF16),
                   jax.ShapeDtypeStruct((s, MLA_HEADS * MLA_QW), F32),
                   jax.ShapeDtypeStruct((s, MLA_HEADS * MLA_DV), F32)],
        scratch_shapes=[pltpu.VMEM((nb, MLA_QW, blk), F32), pltpu.VMEM((nb, 1, blk), F32)],
        compiler_params=_cp("parallel", "arbitrary"),
    )(q_cat, k_cat, v, k_t, do, o, lse, ta, tb, tc)


def _place():
    return lax.axis_index("x"), lax.axis_index("y"), lax.axis_index("c")


def _slot(px, py, pc):
    return 4 * px + 2 * py + pc


def _all_gather(shards, layer, name, vmem=False, deps=()):
    n = len(shards)

    def body(*refs):
        srcs, outs = refs[:n], refs[n + len(deps):2 * n + len(deps)]
        send_sems, recv_sems, local_sems = refs[2 * n + len(deps):]
        x, y, c = _place()
        me, sibling = (x, y, c), (x, y, 1 - c)
        chips = [(1 - x, y), (x, 1 - y), (1 - x, 1 - y)]
        firsts, passes, locals_ = [], [], []

        def copy(a, k, block, to, src=None):
            dst = outs[a].at[_slot(*block)]
            return pltpu.make_async_remote_copy(
                src_ref=dst if src is None else src, dst_ref=dst,
                send_sem=send_sems.at[7 * a + k], recv_sem=recv_sems.at[7 * a + k],
                device_id=to, device_id_type=MESH)

        for a in range(n):
            src = srcs[a] if layer is None else srcs[a].at[layer]
            mine = pltpu.make_async_copy(src, outs[a].at[_slot(*me)], local_sems.at[a])
            mine.start()
            locals_.append(mine)
            first = [copy(a, 0, me, sibling, src=src)]
            first += [copy(a, 1 + j, me, (*chip, c), src=src) for j, chip in enumerate(chips)]
            for cp in first:
                cp.start()
            firsts += first
        for a in range(n):
            for j, chip in enumerate(chips):
                copy(a, 1 + j, (*chip, c), me).wait_recv()
                fwd = copy(a, 4 + j, (*chip, c), sibling)
                fwd.start()
                passes.append(fwd)
        for a in range(n):
            copy(a, 0, sibling, me).wait_recv()
            for j, chip in enumerate(chips):
                copy(a, 4 + j, (*chip, 1 - c), me).wait_recv()
        for cp in firsts + passes:
            cp.wait_send()
        for mine in locals_:
            mine.wait()

    space = pl.BlockSpec(memory_space=pltpu.VMEM) if vmem else ANY
    out_shape = [jax.ShapeDtypeStruct((N_DEV,) + (a.shape if layer is None else a.shape[1:]), a.dtype) for a in shards]
    return pl.pallas_call(
        body, name=name,
        in_specs=[space] * n + [ANY] * len(deps), out_specs=[space] * n, out_shape=out_shape,
        scratch_shapes=[pltpu.SemaphoreType.DMA((7 * n,)), pltpu.SemaphoreType.DMA((7 * n,)),
                        pltpu.SemaphoreType.DMA((n,))],
        compiler_params=pltpu.CompilerParams(has_side_effects=True),
    )(*shards, *deps)


HBM = pl.BlockSpec(memory_space=pltpu.HBM)
SEM = pl.BlockSpec(memory_space=pltpu.SEMAPHORE)
EFFECT = pltpu.SideEffectType.DATAFLOW_SIDE_EFFECTING


def _push_copies(srcs, lands, send_sems, recv_sems, local_sems, by_peer):
    x, y, c = _place()
    me = _slot(x, y, c)
    local, remote = [], []
    for a, (src, land) in enumerate(zip(srcs, lands)):
        local.append(pltpu.make_async_copy(src.at[me] if by_peer else src, land.at[me], local_sems.at[a]))
        for r in range(1, N_DEV):
            peer = (1 - x if r & 4 else x, 1 - y if r & 2 else y, 1 - c if r & 1 else c)
            remote.append(pltpu.make_async_remote_copy(
                src_ref=src.at[_slot(*peer)] if by_peer else src, dst_ref=land.at[me],
                send_sem=send_sems.at[7 * a + r - 1], recv_sem=recv_sems.at[7 * a + r - 1],
                device_id=peer, device_id_type=MESH))
    return local, remote


def _push_start(srcs, by_peer, after, name):
    n = len(srcs)
    srcs = [pltpu.with_memory_space_constraint(a, pltpu.HBM) for a in srcs]
    lands = [pltpu.with_memory_space_constraint(
        lax.empty((N_DEV,) + (a.shape[1:] if by_peer else a.shape), a.dtype), pltpu.HBM) for a in srcs]

    def body(*refs):
        k = 2 * n + len(after)
        local, remote = _push_copies(refs[:n], refs[n:2 * n], refs[k], refs[k + 1], refs[k + 2], by_peer)
        for cp in local + remote:
            cp.start()
        token = refs[k + 3 + 2 * n]
        token[...] = jnp.zeros_like(token)

    outs = pl.pallas_call(
        body, name=name,
        out_shape=(pltpu.SemaphoreType.DMA((7 * n,)), pltpu.SemaphoreType.DMA((7 * n,)),
                   pltpu.SemaphoreType.DMA((n,)),
                   *[pltpu.HBM(a.shape, a.dtype) for a in srcs], *[pltpu.HBM(a.shape, a.dtype) for a in lands],
                   jax.ShapeDtypeStruct((8, LANES), F32)),
        in_specs=[HBM] * (2 * n) + [ANY] * len(after),
        out_specs=(SEM, SEM, SEM, *([HBM] * (2 * n)), pl.BlockSpec(memory_space=pltpu.VMEM)),
        input_output_aliases={i: 3 + i for i in range(2 * n)},
        compiler_params=pltpu.CompilerParams(has_side_effects=EFFECT),
    )(*srcs, *lands, *after)
    return outs[:3], outs[3:3 + n], outs[3 + n:3 + 2 * n], outs[3 + 2 * n]


def _push_wait(sems, srcs, lands, by_peer, after, name):
    n = len(srcs)

    def body(*refs):
        local, remote = _push_copies(refs[:n], refs[n:2 * n], refs[2 * n], refs[2 * n + 1], refs[2 * n + 2], by_peer)
        for cp in local:
            cp.wait()
        for cp in remote:
            cp.wait_send()
            cp.wait_recv()

    outs = pl.pallas_call(
        body, name=name,
        out_shape=[pltpu.HBM(a.shape, a.dtype) for a in list(srcs) + list(lands)],
        in_specs=[HBM] * (2 * n) + [SEM] * 3 + [ANY] * len(after),
        out_specs=[HBM] * (2 * n),
        input_output_aliases={i: i for i in range(2 * n)},
        compiler_params=pltpu.CompilerParams(has_side_effects=EFFECT),
    )(*srcs, *lands, *sems, *after)
    return outs[n:]


def _adam_math(g, w, m, v):
    m = ADAM_B1 * m + (1.0 - ADAM_B1) * g
    v = ADAM_B2 * v + (1.0 - ADAM_B2) * (g * g)
    m_hat = m / (1.0 - ADAM_B1 ** ADAM_STEP)
    v_hat = v / (1.0 - ADAM_B2 ** ADAM_STEP)
    delta = -ADAM_LR * (m_hat / (jnp.sqrt(v_hat) + ADAM_EPS) + ADAM_WD * w)
    return delta, m, v


def _adam_sharded(recvs, first, w, m, v, prev, name):
    nl, r, c = w.shape
    n = len(recvs)
    tr = _rtile(r, 128)
    nr = r // tr
    prev = list(prev) if prev is not None else []

    def body(*refs):
        g_refs = refs[:n]
        w_ref, m_ref, v_ref = refs[n:n + 3]
        go_ref, d_ref, mo_ref, vo_ref = refs[n + 3 + len(prev):]
        layer = pl.program_id(0)
        for l in range(n):
            @pl.when(layer == l)
            def _(l=l):
                g = g_refs[l][0].astype(F32)
                for i in range(1, N_DEV):
                    g = g + g_refs[l][i].astype(F32)
                delta, mn, vn = _adam_math(g, w_ref[...], m_ref[...], v_ref[...])
                go_ref[...] = g
                d_ref[...] = delta
                mo_ref[...] = mn
                vo_ref[...] = vn

    def recv_spec(l):
        def index(layer, i):
            return 0, jnp.where(layer == l, i, jnp.where(layer < l, 0, nr - 1)), 0
        return pl.BlockSpec((N_DEV, tr, c), index)

    blk = pl.BlockSpec((None, tr, c), lambda layer, i: (first + layer, i, 0))
    out = jax.ShapeDtypeStruct(w.shape, F32)
    return pl.pallas_call(
        body, name=name, grid=(n, nr),
        in_specs=[recv_spec(l) for l in range(n)] + [blk, blk, blk] + [ANY] * len(prev),
        out_specs=[blk] * 4, out_shape=[out] * 4,
        input_output_aliases={n + 3 + k: k for k in range(len(prev))},
        compiler_params=_cp("arbitrary", "arbitrary"),
    )(*recvs, w, m, v, *prev)


def _adam_mod(c_all_t, dmod, w, m, v):
    nl, d, c = w.shape
    tr = _rtile(d, 128)

    def body(ct_ref, dm_ref, w_ref, m_ref, v_ref, go_ref, d_ref, mo_ref, vo_ref):
        ct = ct_ref[...].astype(BF16).astype(F32)
        dm = dm_ref[...].astype(BF16).astype(F32)
        g = ct[:, 0:1] * dm[0:1, :]
        for b in range(1, N_DEV):
            g = g + ct[:, b:b + 1] * dm[b:b + 1, :]
        delta, mn, vn = _adam_math(g, w_ref[...], m_ref[...], v_ref[...])
        go_ref[...] = g
        d_ref[...] = delta
        mo_ref[...] = mn
        vo_ref[...] = vn

    blk = pl.BlockSpec((None, tr, c), lambda layer, i: (layer, i, 0))
    out = jax.ShapeDtypeStruct(w.shape, F32)
    return pl.pallas_call(
        body, name="adam_mod", grid=(nl, d // tr),
        in_specs=[pl.BlockSpec((tr, N_DEV), lambda layer, i: (i, 0)),
                  pl.BlockSpec((None, N_DEV, c), lambda layer, i: (layer, 0, 0)), blk, blk, blk],
        out_specs=[blk] * 4, out_shape=[out] * 4,
        compiler_params=_cp("parallel", "parallel"),
    )(c_all_t, dmod, w, m, v)


def _adam_small(g, w, m, v, name):
    def body(g_ref, w_ref, m_ref, v_ref, d_ref, mo_ref, vo_ref):
        delta, mn, vn = _adam_math(g_ref[...], w_ref[...], m_ref[...], v_ref[...])
        d_ref[...] = delta
        mo_ref[...] = mn
        vo_ref[...] = vn

    out = jax.ShapeDtypeStruct(w.shape, F32)
    return pl.pallas_call(body, name=name, out_shape=[out] * 3)(g, w, m, v)


def _sum_devices(parts):
    def body(p_ref, o_ref):
        acc = p_ref[0]
        for i in range(1, N_DEV):
            acc = acc + p_ref[i]
        o_ref[...] = acc

    return pl.pallas_call(body, name="sum_devices",
                          out_shape=jax.ShapeDtypeStruct(parts.shape[1:], F32))(parts)


def _rope_tables(positions):
    pos = positions.astype(F32)[:, None]

    def cs(dim):
        inv = 1.0 / (ROPE_BASE ** (jnp.arange(0, dim, 2, dtype=F32) / dim))
        ang = pos * inv
        return jnp.cos(ang), jnp.sin(ang)

    cr, sr = cs(RET_DK)
    cm, sm = cs(MLA_ROPE)
    z = jnp.zeros_like(cm)
    pad = jnp.zeros((pos.shape[0], LANES - MLA_ROPE), F32)
    cosr = jnp.concatenate([cr, cr], axis=1)
    sinr = jnp.concatenate([-sr, sr], axis=1)
    ta = jnp.concatenate([cm, cm, pad], axis=1)
    tb = jnp.concatenate([-sm, z, pad], axis=1)
    tc = jnp.concatenate([z, sm, pad], axis=1)
    return cosr, sinr, ta, tb, tc


def _layer_fwd(x, mod, g_norm, g_cq, g_ckv, wts, tabs, lg, lay, deps):
    d = x.shape[1]
    cosr, sinr, ta, tb, tc = tabs
    shift, scale, gate = mod[:, :d], mod[:, d:2 * d], mod[:, 2 * d:]
    h = _norm_mod_fwd(x, g_norm, scale, shift, deps)
    proj = _matmul(h, wts["in"], name="mm_proj", tn_cap=1920)
    o_ret, u_ret, states = _ret_fwd(proj, lg, cosr, sinr, lay)
    y_ret = _matmul(u_ret, wts["ret"], name="mm_y")
    cqn, ckvn = _mla_prep(proj, g_cq, g_ckv, lay)
    qp = _matmul(cqn, wts["uq"], name="mm_up")
    kvp = _matmul(ckvn, wts["ukv"], name="mm_up")
    q_cat, k_cat, v, k_t, v_t = _qk_prep(qp, kvp, proj, ta, tb, tc, lay)
    o_mla, u_mla, lse = _attn_fwd(q_cat, k_cat, v_t, proj, lay)
    y_mla = _matmul(u_mla, wts["mla"], name="mm_y")
    merged = _merge_fwd(y_ret, y_mla, proj, lay)
    out = _matmul(merged, wts["out"], name="mm_y")
    x_next = _resid_fwd(x, out, gate)
    saved = dict(x=x, h=h, proj=proj, o_ret=o_ret, u_ret=u_ret, states=states, y_ret=y_ret, cqn=cqn,
                 ckvn=ckvn, q_cat=q_cat, k_cat=k_cat, v=v, k_t=k_t, o_mla=o_mla, u_mla=u_mla, lse=lse,
                 y_mla=y_mla, merged=merged, out=out)
    return x_next, saved


def _to_owner_blocks_cols(g, n_local):
    k = g.shape[0]
    return g.reshape(k, N_DEV, n_local).transpose(1, 0, 2)


def _from_owner_blocks_cols(g):
    return g.transpose(1, 0, 2).reshape(g.shape[1], -1)


def _layer_bwd(dxn, sv, mod, g_norm, g_cq, g_ckv, wts, tabs, lg, lay, deps, shard_cols, push):
    d = dxn.shape[1]
    n_in, n_uq, n_ukv = shard_cols
    cosr, sinr, ta, tb, tc = tabs
    scale, gate = mod[:, d:2 * d], mod[:, 2 * d:]
    gdt = BF16
    dout, dgate = _resid_bwd(dxn, sv["out"], gate, deps)
    dmerged = _matmul(dout, wts["out"], tb=True, name="mm_dy")
    dw_out = _matmul(sv["merged"], dout, ta=True, out_dtype=gdt, name="mm_dw")
    dy_ret, dy_mla, dproj = _merge_bwd(dmerged, sv["y_ret"], sv["y_mla"], sv["proj"], lay)
    du_ret = _matmul(dy_ret, wts["ret"], tb=True, name="mm_dy")
    dw_ret = _matmul(sv["u_ret"], dy_ret, ta=True, out_dtype=gdt, name="mm_dw")
    dproj = _ret_bwd(sv["proj"], lg, cosr, sinr, sv["o_ret"], du_ret, sv["states"], dproj, lay)
    du_mla = _matmul(dy_mla, wts["mla"], tb=True, name="mm_dy")
    dw_mla = _matmul(sv["u_mla"], dy_mla, ta=True, out_dtype=gdt, name="mm_dw")
    do_mla, dproj = _mla_gate_bwd(du_mla, sv["o_mla"], sv["proj"], dproj, lay)
    dqp, dk_cat, dv = _attn_bwd(sv["q_cat"], sv["k_cat"], sv["v"], sv["k_t"], do_mla, sv["o_mla"], sv["lse"],
                                ta, tb, tc)
    dkvp, dproj = _kv_bwd_prep(dk_cat, dv, ta, tb, tc, dproj, lay)
    dcqn = _matmul(dqp, wts["uq"], tb=True, name="mm_dlat")
    dckvn = _matmul(dkvp, wts["ukv"], tb=True, name="mm_dlat")
    dw_uq = _matmul(sv["cqn"], dqp, ta=True, out_dtype=gdt, name="mm_dwup")
    dw_ukv = _matmul(sv["ckvn"], dkvp, ta=True, out_dtype=gdt, name="mm_dwup")
    sent = push("a", [_to_owner_blocks_cols(_uq_to_logical(dw_uq), n_uq),
                      _to_owner_blocks_cols(_ukv_to_logical(dw_ukv), n_ukv),
                      dw_ret.reshape(N_DEV, -1, d), dw_mla.reshape(N_DEV, -1, d), dw_out.reshape(N_DEV, -1, d)])
    dproj, dg_cq, dg_ckv = _mla_prep_bwd(sv["proj"], dcqn, dckvn, g_cq, g_ckv, dproj, lay)
    dw_in = _matmul(sv["h"], dproj, ta=True, out_dtype=gdt, name="mm_dwin", tn_cap=1920, deps=sent)
    sent = push("b", [_scatter_dw_in(dw_in, lay, n_in)])
    dh = _matmul(dproj, wts["in"], tb=True, name="mm_dh", deps=sent)
    dx, dshift, dscale, dg_norm = _norm_mod_bwd(sv["x"], g_norm, scale, dh, dxn)
    dmod = jnp.concatenate([dshift, dscale, dgate], axis=1)
    small = dict(dmod=dmod, g_norm=dg_norm, g_cq=dg_cq, g_ckv=dg_ckv)
    return dx, small


def kernel(x, c, positions, w_mod, b_mod, g_norm, w_in, g_cq, g_ckv, w_uq, w_ukv, w_ret_proj, w_mla_proj, w_out, g_final, loss_target, m_w_mod, m_b_mod, m_g_norm, m_w_in, m_g_cq, m_g_ckv, m_w_uq, m_w_ukv, m_w_ret_proj, m_w_mla_proj, m_w_out, m_g_final, v_w_mod, v_b_mod, v_g_norm, v_w_in, v_g_cq, v_g_ckv, v_w_uq, v_w_ukv, v_w_ret_proj, v_w_mla_proj, v_w_out, v_g_final):
    nl, d, _ = w_mod.shape
    s = x.shape[1]
    rank = g_cq.shape[1]
    lay = Layout(d, rank, g_ckv.shape[1])
    me = _slot(*_place())
    x0 = x.reshape(s, d)
    target = loss_target.reshape(s, d)
    tabs = _rope_tables(positions.reshape(s))
    lg = jnp.log(1.0 - 2.0 ** (-5.0 - jnp.arange(RET_HEADS, dtype=F32)))

    c_act = c * _sigmoid(c)
    (c_all,) = _all_gather([c_act.reshape(d // LANES, LANES)], None, "gather_c", vmem=True)
    c_all = c_all.reshape(N_DEV, d)
    n_mod = w_mod.shape[2]
    mod_part = jnp.stack([_matmul(c_all, w_mod[l], name="mm_mod", tm_cap=8) for l in range(nl)])
    (mod_all,) = _all_gather([mod_part.reshape(-1, LANES)], None, "gather_mod", vmem=True)
    mod_all = mod_all.reshape(N_DEV, nl, N_DEV, n_mod)
    mod = lax.dynamic_index_in_dim(mod_all, me, axis=2, keepdims=False)
    mod = mod.transpose(1, 0, 2).reshape(nl, N_DEV * n_mod) + b_mod

    shards = [[w[l].astype(BF16) for w in (w_in, w_uq, w_ukv, w_ret_proj, w_mla_proj, w_out)] for l in range(nl)]
    xl, saved, wts_all = x0, [], []
    gathered = _all_gather(shards[0], None, "gather_w")
    for l in range(nl):
        g_in, g_uq, g_ukv, g_ret, g_mla, g_out = gathered
        deps = []
        if l + 1 < nl:
            sems, srcs, lands, token = _push_start(shards[l + 1], False, [g_out], "gather_start_%d" % (l + 1))
            deps = [token]
        wts = {
            "in": _assemble_w_in(g_in, lay),
            "uq": _uq_to_physical(_from_owner_blocks_cols(g_uq)),
            "ukv": _ukv_to_physical(_from_owner_blocks_cols(g_ukv)),
            "ret": g_ret.reshape(-1, d), "mla": g_mla.reshape(-1, d), "out": g_out.reshape(-1, d),
        }
        wts_all.append(wts)
        xl, sv = _layer_fwd(xl, mod[l:l + 1], g_norm[l:l + 1], g_cq[l:l + 1], g_ckv[l:l + 1], wts, tabs, lg, lay,
                            deps)
        saved.append(sv)
        if l + 1 < nl:
            gathered = _push_wait(sems, srcs, lands, False, [xl], "gather_wait_%d" % (l + 1))
    loss_lanes, dx, dg_final = _final_loss(xl, g_final.reshape(1, d), target)

    small = [None] * nl
    flying = {l: [] for l in range(nl)}
    recv = {}
    shard_cols = (w_in.shape[2], w_uq.shape[2], w_ukv.shape[2])

    def pusher(l):
        def push(group, arrays):
            sems, srcs, lands, token = _push_start(arrays, True, [], "exchange_start_%d%s" % (l, group))
            flying[l].append((group, sems, srcs, lands))
            return [token]
        return push

    def land(l, after):
        got = {}
        for group, sems, srcs, lands in flying[l]:
            got[group] = _push_wait(sems, srcs, lands, True, after, "exchange_wait_%d%s" % (l, group))
        recv[l] = list(got["b"]) + list(got["a"])

    for l in reversed(range(nl)):
        dx, small[l] = _layer_bwd(dx, saved[l], mod[l:l + 1], g_norm[l:l + 1], g_cq[l:l + 1], g_ckv[l:l + 1],
                                  wts_all[l], tabs, lg, lay, [], shard_cols, pusher(l))
        if l + 1 < nl:
            land(l + 1, [dx])
    grad_x = dx.reshape(x.shape)

    out = {}
    sharded = (("w_in", w_in, m_w_in, v_w_in), ("w_uq", w_uq, m_w_uq, v_w_uq), ("w_ukv", w_ukv, m_w_ukv, v_w_ukv),
               ("w_ret_proj", w_ret_proj, m_w_ret_proj, v_w_ret_proj),
               ("w_mla_proj", w_mla_proj, m_w_mla_proj, v_w_mla_proj), ("w_out", w_out, m_w_out, v_w_out))
    if nl > 1:
        for i, (key, w, m, v) in enumerate(sharded):
            out[key] = _adam_sharded([recv[l][i] for l in range(1, nl)], 1, w, m, v, None, "adam_" + key)
    done = [out[key][0] for key, _, _, _ in sharded if key in out]

    pack = jnp.concatenate(
        [jnp.concatenate([sm[k] for sm in small], axis=0).reshape(-1)
         for k in ("dmod", "g_norm", "g_cq", "g_ckv")] + [dg_final.reshape(-1), loss_lanes.reshape(-1)])
    (pack_all,) = _all_gather([pack.reshape(-1, LANES)], None, "gather_small", vmem=True, deps=done)
    tot = _sum_devices(pack_all).reshape(-1)
    sizes = [nl * 3 * d, nl * d, nl * rank, nl * rank, d]
    offs = np.cumsum([0] + sizes)
    grad_b_mod = tot[offs[0]:offs[1]].reshape(nl, 3 * d)
    grad_g_norm = tot[offs[1]:offs[2]].reshape(nl, d)
    grad_g_cq = tot[offs[2]:offs[3]].reshape(nl, rank)
    grad_g_ckv = tot[offs[3]:offs[4]].reshape(nl, rank)
    grad_g_final = tot[offs[4]:offs[5]]
    loss = tot[offs[5]]
    dmod_all = pack_all.reshape(N_DEV, -1)[:, :sizes[0]].reshape(N_DEV, nl, 3 * d)
    dmod_mine = lax.dynamic_slice_in_dim(dmod_all, me * n_mod, n_mod, axis=2).transpose(1, 0, 2)

    out["w_mod"] = _adam_mod(c_all.T, dmod_mine, w_mod, m_w_mod, v_w_mod)
    land(0, [out["w_mod"][0]])
    for i, (key, w, m, v) in enumerate(sharded):
        out[key] = _adam_sharded([recv[0][i]], 0, w, m, v, out.get(key), "adam0_" + key)
    for key, g, w, m, v in (("b_mod", grad_b_mod, b_mod, m_b_mod, v_b_mod),
                            ("g_norm", grad_g_norm, g_norm, m_g_norm, v_g_norm),
                            ("g_cq", grad_g_cq, g_cq, m_g_cq, v_g_cq),
                            ("g_ckv", grad_g_ckv, g_ckv, m_g_ckv, v_g_ckv),
                            ("g_final", grad_g_final.reshape(1, d), g_final.reshape(1, d),
                             m_g_final.reshape(1, d), v_g_final.reshape(1, d))):
        out[key] = (g,) + tuple(_adam_small(g, w, m, v, "adam_" + key))
    out["g_final"] = tuple(a.reshape(d) for a in out["g_final"])

    names = ("w_mod", "b_mod", "g_norm", "w_in", "g_cq", "g_ckv", "w_uq", "w_ukv", "w_ret_proj",
             "w_mla_proj", "w_out", "g_final")
    return (loss, grad_x, *[out[k][0] for k in names], *[out[k][1] for k in names],
            *[out[k][2] for k in names], *[out[k][3] for k in names])
```

```python
import functools
import itertools

import jax
import jax.numpy as jnp
import numpy as np
from jax import lax
from jax.experimental import pallas as pl
from jax.experimental.pallas import tpu as pltpu

F32 = jnp.float32
BF16 = jnp.bfloat16

N_DEV = 8
CHUNK = 64
EPS = 1e-6
NEG_INF = -1e30
ROPE_BASE = 10000.0
LANES = 128

RET_HEADS = 8
RET_DK = 128
RET_DV = 256
MLA_HEADS = 16
MLA_NOPE = 128
MLA_ROPE = 64
MLA_DV = 128
MLA_QW = 256
QK_SCALE = (MLA_NOPE + MLA_ROPE) ** -0.5
QK_LOG2_SCALE = QK_SCALE * 1.4426950408889634
LN2 = 0.6931471805599453

ADAM_LR = 0.001
ADAM_B1 = 0.9
ADAM_B2 = 0.999
ADAM_EPS = 1e-08
ADAM_WD = 0.01
ADAM_STEP = 10

VMEM_LIMIT_BYTES = 56 * 1024 * 1024
MESH = pl.DeviceIdType.MESH
ANY = pl.BlockSpec(memory_space=pl.ANY)


def _cp(*sem):
    return pltpu.CompilerParams(dimension_semantics=sem if sem else None,
                                vmem_limit_bytes=VMEM_LIMIT_BYTES)


def _tile(n, cap):
    best = None
    t = LANES
    while t <= min(n, cap):
        if n % t == 0:
            best = t
        t += LANES
    return best if best is not None else n


def _rtile(n, cap):
    t = cap
    while t > 8 and n % t:
        t //= 2
    return t if n % t == 0 else n


def _sigmoid(x):
    return 1.0 / (1.0 + jnp.exp(-x))


def _dot(a, b):
    return lax.dot_general(a, b, (((1,), (0,)), ((), ())), preferred_element_type=F32)


def _dot_nt(a, b):
    return lax.dot_general(a, b, (((1,), (1,)), ((), ())), preferred_element_type=F32)


def _dot_tn(a, b):
    return lax.dot_general(a, b, (((0,), (0,)), ((), ())), preferred_element_type=F32)


def _roll(x, s):
    return pltpu.roll(x, s, 1)


class Layout:
    def __init__(self, d_model, q_rank, kv_rank):
        assert q_rank == kv_rank
        self.d = d_model
        self.rank = q_rank
        self.ret_w = 2 * RET_DK + 2 * RET_DV
        self.ret_qk = RET_HEADS * RET_DK
        self.ret_v = RET_HEADS * RET_DV
        self.mla_v = MLA_HEADS * MLA_DV
        widths = {"bg": 2 * d_model, "mg": self.mla_v, "ret": RET_HEADS * self.ret_w,
                  "cqkv": 2 * q_rank, "kr": LANES}
        blocks = {"bg": 2 * d_model, "mg": self.mla_v, "ret": self.ret_w,
                  "cqkv": 2 * q_rank, "kr": LANES}
        for order in itertools.permutations(widths):
            off, offs, ok = 0, {}, True
            for name in order:
                if off % blocks[name]:
                    ok = False
                    break
                offs[name] = off
                off += widths[name]
            if ok:
                break
        assert ok, "no aligned layout"
        self.order, self.off, self.width, self.total = order, offs, widths, off
        lo, o = {}, 0
        for name, w in (("rq", self.ret_qk), ("rk", self.ret_qk), ("rv", self.ret_v),
                        ("rg", self.ret_v), ("cq", q_rank), ("ckv", kv_rank), ("kr", MLA_ROPE),
                        ("mg", self.mla_v), ("bg", 2 * d_model)):
            lo[name] = (o, w)
            o += w
        self.logical, self.d_in = lo, o

    def pieces(self):
        lo = self.logical
        out = []
        for name in self.order:
            if name == "bg":
                out.append(lo["bg"])
            elif name == "mg":
                out.append(lo["mg"])
            elif name == "ret":
                for h in range(RET_HEADS):
                    out.append((lo["rq"][0] + h * RET_DK, RET_DK))
                    out.append((lo["rk"][0] + h * RET_DK, RET_DK))
                    out.append((lo["rv"][0] + h * RET_DV, RET_DV))
                    out.append((lo["rg"][0] + h * RET_DV, RET_DV))
            elif name == "cqkv":
                out.append((lo["cq"][0], 2 * self.rank))
            elif name == "kr":
                out.append(lo["kr"])
                out.append((None, LANES - MLA_ROPE))
        return out


RELAYOUT_CHUNK = 512


def _relayout_plan(lay, n_local):
    plan, off = [], 0
    for start, width in lay.pieces():
        done = 0
        while done < width:
            w = min(RELAYOUT_CHUNK, width - done)
            srcs = []
            if start is not None:
                lo, hi = start + done, start + done + w
                while lo < hi:
                    j = lo // n_local
                    end = min(hi, (j + 1) * n_local)
                    srcs.append((j, lo - j * n_local, end - j * n_local))
                    lo = end
            plan.append((off + done, w, srcs))
            done += w
        off += width
    merged = []
    for p, w, srcs in plan:
        if merged and merged[-1][0] % LANES == 0 and (merged[-1][1] % LANES) and p == merged[-1][0] + merged[-1][1]:
            q, qw, qs = merged.pop()
            merged.append((q, qw + w, qs + ([("pad", w)] if not srcs else srcs)))
        else:
            merged.append((p, w, srcs))
    return merged


def _assemble_w_in(g, lay):
    _, d, n_local = g.shape
    tr = _rtile(d, 256)
    plan = _relayout_plan(lay, n_local)

    def body(g_ref, o_ref):
        for p, w, srcs in plan:
            parts = []
            for src in srcs:
                if src[0] == "pad":
                    parts.append(jnp.zeros((tr, src[1]), F32))
                else:
                    j, a, b = src
                    parts.append(g_ref[j, :, a:b].astype(F32))
            if not parts:
                parts = [jnp.zeros((tr, w), F32)]
            val = parts[0] if len(parts) == 1 else jnp.concatenate(parts, axis=1)
            o_ref[:, p:p + w] = val.astype(o_ref.dtype)

    return pl.pallas_call(
        body, name="assemble_w_in", grid=(d // tr,),
        in_specs=[pl.BlockSpec((N_DEV, tr, n_local), lambda i: (0, i, 0))],
        out_specs=pl.BlockSpec((tr, lay.total), lambda i: (i, 0)),
        out_shape=jax.ShapeDtypeStruct((d, lay.total), g.dtype),
        compiler_params=_cp("parallel"),
    )(g)


def _scatter_dw_in(dw, lay, n_local):
    d = dw.shape[0]
    tr = _rtile(d, 256)
    n_pad = -(-n_local // LANES) * LANES
    plan = _relayout_plan(lay, n_local)
    runs = [[] for _ in range(N_DEV)]
    for p, w, srcs in plan:
        at = p
        for src in srcs:
            if src[0] == "pad":
                at += src[1]
                continue
            j, a, b = src
            runs[j].append((a, b, at))
            at += b - a
    for r in runs:
        r.sort()

    def body(dw_ref, o_ref):
        for j in range(N_DEV):
            parts = [dw_ref[:, at:at + (b - a)].astype(F32) for a, b, at in runs[j]]
            if n_pad > n_local:
                parts.append(jnp.zeros((tr, n_pad - n_local), F32))
            val = jnp.concatenate(parts, axis=1).T
            o_ref[j] = val[:n_local, :].astype(BF16)

    return pl.pallas_call(
        body, name="scatter_dw_in", grid=(d // tr,),
        in_specs=[pl.BlockSpec((tr, lay.total), lambda i: (i, 0))],
        out_specs=pl.BlockSpec((N_DEV, n_local, tr), lambda i: (0, 0, i)),
        out_shape=jax.ShapeDtypeStruct((N_DEV, n_local, d), BF16),
        compiler_params=_cp("parallel"),
    )(dw)


def _uq_to_physical(w):
    k = w.shape[0]
    w3 = w.reshape(k, MLA_HEADS, MLA_NOPE + MLA_ROPE)
    pad = jnp.zeros((k, MLA_HEADS, MLA_QW - MLA_NOPE - MLA_ROPE), w.dtype)
    return jnp.concatenate([w3, pad], axis=2).reshape(k, MLA_HEADS * MLA_QW)


def _uq_to_logical(w):
    k = w.shape[0]
    return w.reshape(k, MLA_HEADS, MLA_QW)[:, :, :MLA_NOPE + MLA_ROPE].reshape(k, -1)


def _ukv_to_physical(w):
    k = w.shape[0]
    w3 = w.reshape(k, MLA_HEADS, MLA_NOPE + MLA_DV)
    return jnp.concatenate([w3[:, :, :MLA_NOPE].reshape(k, -1), w3[:, :, MLA_NOPE:].reshape(k, -1)], axis=1)


def _ukv_to_logical(w):
    k = w.shape[0]
    kn = w[:, :MLA_HEADS * MLA_NOPE].reshape(k, MLA_HEADS, MLA_NOPE)
    v = w[:, MLA_HEADS * MLA_NOPE:].reshape(k, MLA_HEADS, MLA_DV)
    return jnp.concatenate([kn, v], axis=2).reshape(k, -1)


def _matmul(a, b, *, ta=False, tb=False, out_dtype=F32, name, tm_cap=1024, tn_cap=1024, tk_cap=2048, deps=()):
    m, k = (a.shape[1], a.shape[0]) if ta else a.shape
    n = b.shape[0] if tb else b.shape[1]
    assert k == (b.shape[1] if tb else b.shape[0])
    tm, tn, tk = _tile(m, tm_cap), _tile(n, tn_cap), _tile(k, tk_cap)
    nk = k // tk

    def body(a_ref, b_ref, *rest):
        dims = (((0 if ta else 1,), (1 if tb else 0,)), ((), ()))
        part = lax.dot_general(a_ref[...].astype(BF16), b_ref[...].astype(BF16), dims, preferred_element_type=F32)
        if nk == 1:
            rest[len(deps)][...] = part.astype(out_dtype)
            return
        o_ref, acc_ref = rest[len(deps):]
        kk = pl.program_id(2)

        @pl.when(kk == 0)
        def _():
            acc_ref[...] = part

        @pl.when(jnp.logical_and(kk > 0, kk < nk - 1))
        def _():
            acc_ref[...] += part

        @pl.when(kk == nk - 1)
        def _():
            o_ref[...] = (acc_ref[...] + part).astype(o_ref.dtype)

    a_spec = pl.BlockSpec((tk, tm), lambda i, j, kk: (kk, i)) if ta else pl.BlockSpec((tm, tk), lambda i, j, kk: (i, kk))
    b_spec = pl.BlockSpec((tn, tk), lambda i, j, kk: (j, kk)) if tb else pl.BlockSpec((tk, tn), lambda i, j, kk: (kk, j))
    return pl.pallas_call(
        body, name=name, grid=(m // tm, n // tn, nk),
        in_specs=[a_spec, b_spec] + [ANY] * len(deps),
        out_specs=pl.BlockSpec((tm, tn), lambda i, j, kk: (i, j)),
        out_shape=jax.ShapeDtypeStruct((m, n), out_dtype),
        scratch_shapes=[pltpu.VMEM((tm, tn), F32)] if nk > 1 else [],
        compiler_params=_cp("parallel", "parallel", "arbitrary"),
    )(a, b, *deps)


def _row(tm, w, cb=0):
    return pl.BlockSpec((tm, w), lambda i, cb=cb: (i, cb))


def _vec(w, cb=0):
    return pl.BlockSpec((1, w), lambda i, cb=cb: (0, cb))


def _dproj_out(dproj, s, lay, tm, name):
    w = lay.width[name]
    cb = lay.off[name] // w
    spec = _row(tm, w, cb)
    shape = jax.ShapeDtypeStruct((s, lay.total), BF16)
    return spec, shape


def _norm_mod_fwd(x, g, scale, shift, deps):
    s, d = x.shape
    tm = _rtile(s, 256)

    def body(x_ref, g_ref, sc_ref, sh_ref, *rest):
        h_ref = rest[len(deps)]
        xv = x_ref[...]
        xh = xv * lax.rsqrt(jnp.mean(xv * xv, axis=-1, keepdims=True) + EPS)
        h_ref[...] = ((xh * g_ref[...]) * (1.0 + sc_ref[...]) + sh_ref[...]).astype(BF16)

    return pl.pallas_call(
        body, name="norm_mod_fwd", grid=(s // tm,),
        in_specs=[_row(tm, d), _vec(d), _vec(d), _vec(d)] + [ANY] * len(deps),
        out_specs=_row(tm, d), out_shape=jax.ShapeDtypeStruct((s, d), BF16),
        compiler_params=_cp("parallel"),
    )(x, g, scale, shift, *deps)


def _norm_mod_bwd(x, g, scale, dh, dres):
    s, d = x.shape
    tm = _rtile(s, 256)

    def body(x_ref, g_ref, sc_ref, dh_ref, dres_ref, dx_ref, dsh_ref, dsc_ref, dg_ref):
        @pl.when(pl.program_id(0) == 0)
        def _():
            dsh_ref[...] = jnp.zeros_like(dsh_ref)
            dsc_ref[...] = jnp.zeros_like(dsc_ref)
            dg_ref[...] = jnp.zeros_like(dg_ref)

        xv, gv, dhv = x_ref[...], g_ref[...], dh_ref[...]
        rstd = lax.rsqrt(jnp.mean(xv * xv, axis=-1, keepdims=True) + EPS)
        xh = xv * rstd
        dy = dhv * (1.0 + sc_ref[...])
        dxh = dy * gv
        dx_ref[...] = dres_ref[...] + rstd * (dxh - xh * jnp.mean(dxh * xh, axis=-1, keepdims=True))
        dsh_ref[...] += jnp.sum(dhv, axis=0, keepdims=True)
        dsc_ref[...] += jnp.sum(dhv * (xh * gv), axis=0, keepdims=True)
        dg_ref[...] += jnp.sum(dy * xh, axis=0, keepdims=True)

    vec = jax.ShapeDtypeStruct((1, d), F32)
    return pl.pallas_call(
        body, name="norm_mod_bwd", grid=(s // tm,),
        in_specs=[_row(tm, d), _vec(d), _vec(d), _row(tm, d), _row(tm, d)],
        out_specs=[_row(tm, d), _vec(d), _vec(d), _vec(d)],
        out_shape=[jax.ShapeDtypeStruct((s, d), F32), vec, vec, vec],
        compiler_params=_cp("arbitrary"),
    )(x, g, scale, dh, dres)


def _final_loss(x, g, target):
    s, d = x.shape
    tm = _rtile(s, 256)

    def body(x_ref, g_ref, t_ref, l_ref, dx_ref, dg_ref):
        @pl.when(pl.program_id(0) == 0)
        def _():
            l_ref[...] = jnp.zeros_like(l_ref)
            dg_ref[...] = jnp.zeros_like(dg_ref)

        xv, gv = x_ref[...], g_ref[...]
        rstd = lax.rsqrt(jnp.mean(xv * xv, axis=-1, keepdims=True) + EPS)
        xh = xv * rstd
        err = xh * gv - t_ref[...]
        row = jnp.mean(err * err, axis=-1, keepdims=True)
        l_ref[...] += 0.5 * jnp.sum(row, axis=0, keepdims=True)
        dy = err / d
        dxh = dy * gv
        dx_ref[...] = rstd * (dxh - xh * jnp.mean(dxh * xh, axis=-1, keepdims=True))
        dg_ref[...] += jnp.sum(dy * xh, axis=0, keepdims=True)

    return pl.pallas_call(
        body, name="final_loss", grid=(s // tm,),
        in_specs=[_row(tm, d), _vec(d), _row(tm, d)],
        out_specs=[_vec(LANES), _row(tm, d), _vec(d)],
        out_shape=[jax.ShapeDtypeStruct((1, LANES), F32), jax.ShapeDtypeStruct((s, d), F32),
                   jax.ShapeDtypeStruct((1, d), F32)],
        compiler_params=_cp("arbitrary"),
    )(x, g, target)


def _resid_fwd(x, out, gate):
    s, d = x.shape
    tm = _rtile(s, 256)

    def body(x_ref, o_ref, g_ref, y_ref):
        y_ref[...] = x_ref[...] + g_ref[...] * o_ref[...]

    return pl.pallas_call(
        body, name="resid_fwd", grid=(s // tm,),
        in_specs=[_row(tm, d), _row(tm, d), _vec(d)],
        out_specs=_row(tm, d), out_shape=jax.ShapeDtypeStruct((s, d), F32),
        compiler_params=_cp("parallel"),
    )(x, out, gate)


def _resid_bwd(dxn, out, gate, deps):
    s, d = dxn.shape
    tm = _rtile(s, 256)

    def body(dx_ref, o_ref, g_ref, *rest):
        do_ref, dg_ref = rest[len(deps):]

        @pl.when(pl.program_id(0) == 0)
        def _():
            dg_ref[...] = jnp.zeros_like(dg_ref)

        dxv = dx_ref[...]
        do_ref[...] = (dxv * g_ref[...]).astype(BF16)
        dg_ref[...] += jnp.sum(dxv * o_ref[...], axis=0, keepdims=True)

    return pl.pallas_call(
        body, name="resid_bwd", grid=(s // tm,),
        in_specs=[_row(tm, d), _row(tm, d), _vec(d)] + [ANY] * len(deps),
        out_specs=[_row(tm, d), _vec(d)],
        out_shape=[jax.ShapeDtypeStruct((s, d), BF16), jax.ShapeDtypeStruct((1, d), F32)],
        compiler_params=_cp("arbitrary"),
    )(dxn, out, gate, *deps)


def _merge_fwd(y_ret, y_mla, proj, lay):
    s, d = y_ret.shape
    tm = _rtile(s, 256)
    cb = lay.off["bg"] // (2 * d)

    def body(a_ref, b_ref, bg_ref, m_ref):
        sg = _sigmoid(bg_ref[...])
        m_ref[...] = (sg[:, :d] * a_ref[...] + sg[:, d:] * b_ref[...]).astype(BF16)

    return pl.pallas_call(
        body, name="merge_fwd", grid=(s // tm,),
        in_specs=[_row(tm, d), _row(tm, d), _row(tm, 2 * d, cb)],
        out_specs=_row(tm, d), out_shape=jax.ShapeDtypeStruct((s, d), BF16),
        compiler_params=_cp("parallel"),
    )(y_ret, y_mla, proj)


def _merge_bwd(dm, y_ret, y_mla, proj, lay):
    s, d = dm.shape
    tm = _rtile(s, 256)
    cb = lay.off["bg"] // (2 * d)
    dp_spec, dp_shape = _dproj_out(None, s, lay, tm, "bg")

    def body(dm_ref, a_ref, b_ref, bg_ref, da_ref, db_ref, dp_ref):
        sg = _sigmoid(bg_ref[...])
        dmv = dm_ref[...]
        ga, gb = sg[:, :d], sg[:, d:]
        da_ref[...] = (dmv * ga).astype(BF16)
        db_ref[...] = (dmv * gb).astype(BF16)
        dp_ref[:, :d] = (dmv * a_ref[...] * ga * (1.0 - ga)).astype(BF16)
        dp_ref[:, d:] = (dmv * b_ref[...] * gb * (1.0 - gb)).astype(BF16)

    act = jax.ShapeDtypeStruct((s, d), BF16)
    return pl.pallas_call(
        body, name="merge_bwd", grid=(s // tm,),
        in_specs=[_row(tm, d), _row(tm, d), _row(tm, d), _row(tm, 2 * d, cb)],
        out_specs=[_row(tm, d), _row(tm, d), dp_spec],
        out_shape=[act, act, dp_shape],
        compiler_params=_cp("parallel"),
    )(dm, y_ret, y_mla, proj)


def _mla_prep(proj, g_cq, g_ckv, lay):
    s = proj.shape[0]
    r = lay.rank
    tm = _rtile(s, 512)
    cb = lay.off["cqkv"] // (2 * r)

    def body(p_ref, gq_ref, gk_ref, q_ref, k_ref):
        pv = p_ref[...]
        for lo, g_ref, o_ref in ((0, gq_ref, q_ref), (r, gk_ref, k_ref)):
            xv = pv[:, lo:lo + r]
            xh = xv * lax.rsqrt(jnp.mean(xv * xv, axis=-1, keepdims=True) + EPS)
            o_ref[...] = (xh * g_ref[...]).astype(BF16)

    act = jax.ShapeDtypeStruct((s, r), BF16)
    return pl.pallas_call(
        body, name="mla_prep", grid=(s // tm,),
        in_specs=[_row(tm, 2 * r, cb), _vec(r), _vec(r)],
        out_specs=[_row(tm, r), _row(tm, r)], out_shape=[act, act],
        compiler_params=_cp("parallel"),
    )(proj, g_cq, g_ckv)


def _mla_prep_bwd(proj, dqn, dkn, g_cq, g_ckv, dproj, lay):
    s = proj.shape[0]
    r = lay.rank
    tm = _rtile(s, 512)
    cb = lay.off["cqkv"] // (2 * r)
    dp_spec, dp_shape = _dproj_out(dproj, s, lay, tm, "cqkv")

    def body(p_ref, dq_ref, dk_ref, gq_ref, gk_ref, _, dp_ref, dgq_ref, dgk_ref):
        @pl.when(pl.program_id(0) == 0)
        def _():
            dgq_ref[...] = jnp.zeros_like(dgq_ref)
            dgk_ref[...] = jnp.zeros_like(dgk_ref)

        pv = p_ref[...]
        for lo, g_ref, d_ref, dg_ref in ((0, gq_ref, dq_ref, dgq_ref), (r, gk_ref, dk_ref, dgk_ref)):
            xv = pv[:, lo:lo + r]
            rstd = lax.rsqrt(jnp.mean(xv * xv, axis=-1, keepdims=True) + EPS)
            xh = xv * rstd
            dy = d_ref[...]
            dxh = dy * g_ref[...]
            dp_ref[:, lo:lo + r] = (rstd * (dxh - xh * jnp.mean(dxh * xh, axis=-1, keepdims=True))).astype(BF16)
            dg_ref[...] += jnp.sum(dy * xh, axis=0, keepdims=True)

    vec = jax.ShapeDtypeStruct((1, r), F32)
    return pl.pallas_call(
        body, name="mla_prep_bwd", grid=(s // tm,),
        in_specs=[_row(tm, 2 * r, cb), _row(tm, r), _row(tm, r), _vec(r), _vec(r), ANY],
        out_specs=[dp_spec, _vec(r), _vec(r)], out_shape=[dp_shape, vec, vec],
        input_output_aliases={5: 0},
        compiler_params=_cp("arbitrary"),
    )(proj, dqn, dkn, g_cq, g_ckv, dproj)


def _rope_tile(t, a, b, c):
    return t * a + _roll(t, 96) * b + _roll(t, 32) * c


def _rope_tile_bwd(dy, a, b, c):
    return dy * a + _roll(dy * b, 32) + _roll(dy * c, 96)


def _attn_block(s):
    return _rtile(s, 512)


def _qk_prep(qp, kvp, proj, ta, tb, tc, lay):
    s = qp.shape[0]
    hq = MLA_HEADS * MLA_QW
    hv = MLA_HEADS * MLA_DV
    blk = _attn_block(s)
    tm = _rtile(blk, 256)
    per = blk // tm
    kr_cb = lay.off["kr"] // LANES

    def body(q_ref, kv_ref, kr_ref, a_ref, b_ref, c_ref, qc_ref, kc_ref, v_ref, kt_ref, vt_ref):
        a, b, c = a_ref[...], b_ref[...], c_ref[...]
        krot = _rope_tile(kr_ref[...], a, b, c)
        krot_b, krot_t = krot.astype(BF16), krot.T.astype(BF16)
        for h in range(MLA_HEADS):
            q0 = h * MLA_QW
            qc_ref[:, q0:q0 + MLA_NOPE] = (q_ref[:, q0:q0 + MLA_NOPE] * QK_LOG2_SCALE).astype(BF16)
            qc_ref[:, q0 + MLA_NOPE:q0 + MLA_QW] = (
                _rope_tile(q_ref[:, q0 + MLA_NOPE:q0 + MLA_QW], a, b, c) * QK_LOG2_SCALE).astype(BF16)
            kn = kv_ref[:, h * MLA_NOPE:(h + 1) * MLA_NOPE]
            kc_ref[:, q0:q0 + MLA_NOPE] = kn.astype(BF16)
            kc_ref[:, q0 + MLA_NOPE:q0 + MLA_QW] = krot_b
            kt_ref[h, :MLA_NOPE, :] = kn.T.astype(BF16)
            kt_ref[h, MLA_NOPE:, :] = krot_t
            vh = kv_ref[:, (MLA_HEADS + h) * MLA_NOPE:(MLA_HEADS + h + 1) * MLA_NOPE]
            v_ref[:, h * MLA_DV:(h + 1) * MLA_DV] = vh.astype(BF16)
            vt_ref[h] = vh.T.astype(BF16)

    return pl.pallas_call(
        body, name="qk_prep", grid=(s // tm,),
        in_specs=[_row(tm, hq), _row(tm, hq), _row(tm, LANES, kr_cb), _row(tm, LANES), _row(tm, LANES), _row(tm, LANES)],
        out_specs=[_row(tm, hq), _row(tm, hq), _row(tm, hv),
                   pl.BlockSpec((MLA_HEADS, None, MLA_QW, tm), lambda i: (0, i // per, 0, i % per)),
                   pl.BlockSpec((MLA_HEADS, None, MLA_DV, tm), lambda i: (0, i // per, 0, i % per))],
        out_shape=[jax.ShapeDtypeStruct((s, hq), BF16), jax.ShapeDtypeStruct((s, hq), BF16),
                   jax.ShapeDtypeStruct((s, hv), BF16),
                   jax.ShapeDtypeStruct((MLA_HEADS, s // blk, MLA_QW, blk), BF16),
                   jax.ShapeDtypeStruct((MLA_HEADS, s // blk, MLA_DV, blk), BF16)],
        compiler_params=_cp("parallel"),
    )(qp, kvp, proj, ta, tb, tc)


def _kv_bwd_prep(dk_cat, dv, ta, tb, tc, dproj, lay):
    s = dk_cat.shape[0]
    hq = MLA_HEADS * MLA_QW
    hv = MLA_HEADS * MLA_DV
    tm = _rtile(s, 256)
    dp_spec, dp_shape = _dproj_out(dproj, s, lay, tm, "kr")

    def body(dk_ref, dv_ref, a_ref, b_ref, c_ref, _, dkv_ref, dp_ref):
        acc = jnp.zeros((tm, LANES), F32)
        for h in range(MLA_HEADS):
            q0 = h * MLA_QW
            dkv_ref[:, h * MLA_NOPE:(h + 1) * MLA_NOPE] = dk_ref[:, q0:q0 + MLA_NOPE].astype(BF16)
            acc = acc + dk_ref[:, q0 + MLA_NOPE:q0 + MLA_QW]
        dkv_ref[:, MLA_HEADS * MLA_NOPE:] = dv_ref[...].astype(BF16)
        dp_ref[...] = _rope_tile_bwd(acc, a_ref[...], b_ref[...], c_ref[...]).astype(BF16)

    return pl.pallas_call(
        body, name="kv_bwd_prep", grid=(s // tm,),
        in_specs=[_row(tm, hq), _row(tm, hv), _row(tm, LANES), _row(tm, LANES), _row(tm, LANES), ANY],
        out_specs=[_row(tm, hq), dp_spec],
        out_shape=[jax.ShapeDtypeStruct((s, hq), BF16), dp_shape],
        input_output_aliases={5: 1},
        compiler_params=_cp("parallel"),
    )(dk_cat, dv, ta, tb, tc, dproj)


def _mla_gate_bwd(du, o, proj, dproj, lay):
    s, vw = du.shape
    tm = _rtile(s, 256)
    cb = lay.off["mg"] // vw
    dp_spec, dp_shape = _dproj_out(dproj, s, lay, tm, "mg")

    def body(du_ref, o_ref, g_ref, _, do_ref, dp_ref):
        gv, duv = g_ref[...], du_ref[...]
        sg = _sigmoid(gv)
        do_ref[...] = (duv * (gv * sg)).astype(BF16)
        dp_ref[...] = (duv * o_ref[...] * (sg + gv * sg * (1.0 - sg))).astype(BF16)

    return pl.pallas_call(
        body, name="mla_gate_bwd", grid=(s // tm,),
        in_specs=[_row(tm, vw), _row(tm, vw), _row(tm, vw, cb), ANY],
        out_specs=[_row(tm, vw), dp_spec],
        out_shape=[jax.ShapeDtypeStruct((s, vw), BF16), dp_shape],
        input_output_aliases={3: 1},
        compiler_params=_cp("parallel"),
    )(du, o, proj, dproj)


RET_BLOCK = 256


def _ret_tables(lg, blk):
    ri = lax.broadcasted_iota(jnp.int32, (blk, blk), 0)
    ci = lax.broadcasted_iota(jnp.int32, (blk, blk), 1)
    col = lax.broadcasted_iota(jnp.int32, (blk, 1), 0).astype(F32)
    dist = jnp.abs(ri - ci).astype(F32)
    dmat = jnp.where(ci // CHUNK <= ri // CHUNK, jnp.exp(dist * lg), 0.0)
    xi = jnp.exp((col + 1.0) * lg)
    zeta = jnp.exp((blk - 1.0 - col) * lg)
    decay = jnp.exp(jnp.full((1, 1), blk, F32) * lg)
    return dmat, xi, zeta, decay


def _ret_qkvg(blk, cs, sn):
    dv = RET_DV
    q = blk[:, :RET_DK]
    k = blk[:, RET_DK:2 * RET_DK]
    q = q * cs + _roll(q, RET_DK // 2) * sn
    k = (k * cs + _roll(k, RET_DK // 2) * sn) * (RET_DK ** -0.5)
    return q, k, blk[:, 2 * RET_DK:2 * RET_DK + dv], blk[:, 2 * RET_DK + dv:]


def _group_norm(o):
    mu = jnp.mean(o, axis=-1, keepdims=True)
    oc = o - mu
    rstd = lax.rsqrt(jnp.mean(oc * oc, axis=-1, keepdims=True) + EPS)
    return oc * rstd, rstd


def _ret_fwd(proj, lg, cosr, sinr, lay):
    s = proj.shape[0]
    dv, w = RET_DV, lay.ret_w
    tb = _rtile(s, 512)
    blk = min(RET_BLOCK, tb)
    nb, nch = s // tb, tb // blk
    cb0 = lay.off["ret"] // w

    def body(lg_ref, p_ref, cos_ref, sin_ref, o_ref, u_ref, st_ref, state):
        @pl.when(pl.program_id(1) == 0)
        def _():
            state[...] = jnp.zeros_like(state)

        dmat, xi, zeta, decay = _ret_tables(lg_ref[pl.program_id(0)], blk)
        for c in range(nch):
            rows = slice(c * blk, (c + 1) * blk)
            q, k, v, g = _ret_qkvg(p_ref[rows, :], cos_ref[rows, :], sin_ref[rows, :])
            qb, kb, vb = q.astype(BF16), k.astype(BF16), v.astype(BF16)
            sc = _dot_nt(qb, kb) * dmat
            st = state[...]
            o = _dot(sc.astype(BF16), vb) + _dot((q * xi).astype(BF16), st.astype(BF16))
            st_ref[c] = st.astype(BF16)
            state[...] = st * decay + _dot_tn((k * zeta).astype(BF16), vb)
            o_ref[rows, :] = o
            n, _ = _group_norm(o)
            u_ref[rows, :] = (n * (g * _sigmoid(g))).astype(BF16)

    return pl.pallas_call(
        body, name="ret_fwd", grid=(RET_HEADS, nb),
        in_specs=[pl.BlockSpec(memory_space=pltpu.SMEM),
                  pl.BlockSpec((tb, w), lambda h, b: (b, cb0 + h)),
                  pl.BlockSpec((tb, RET_DK), lambda h, b: (b, 0)),
                  pl.BlockSpec((tb, RET_DK), lambda h, b: (b, 0))],
        out_specs=[pl.BlockSpec((tb, dv), lambda h, b: (b, h)),
                   pl.BlockSpec((tb, dv), lambda h, b: (b, h)),
                   pl.BlockSpec((None, nch, RET_DK, dv), lambda h, b: (h, b, 0, 0))],
        out_shape=[jax.ShapeDtypeStruct((s, RET_HEADS * dv), F32),
                   jax.ShapeDtypeStruct((s, RET_HEADS * dv), BF16),
                   jax.ShapeDtypeStruct((RET_HEADS, s // blk, RET_DK, dv), BF16)],
        scratch_shapes=[pltpu.VMEM((RET_DK, dv), F32)],
        compiler_params=_cp("parallel", "arbitrary"),
    )(lg, proj, cosr, sinr)


def _ret_bwd(proj, lg, cosr, sinr, o, du, states, dproj, lay):
    s = proj.shape[0]
    dv, w = RET_DV, lay.ret_w
    tb = _rtile(s, 512)
    blk = min(RET_BLOCK, tb)
    nb, nch = s // tb, tb // blk
    cb0 = lay.off["ret"] // w

    def body(lg_ref, p_ref, cos_ref, sin_ref, o_ref, du_ref, st_ref, _, dp_ref, dstate):
        @pl.when(pl.program_id(1) == 0)
        def _():
            dstate[...] = jnp.zeros_like(dstate)

        dmat, xi, zeta, decay = _ret_tables(lg_ref[pl.program_id(0)], blk)
        for c in reversed(range(nch)):
            rows = slice(c * blk, (c + 1) * blk)
            cs, sn = cos_ref[rows, :], sin_ref[rows, :]
            q, k, v, g = _ret_qkvg(p_ref[rows, :], cs, sn)
            qb, kb, vb = q.astype(BF16), k.astype(BF16), v.astype(BF16)
            n, rstd = _group_norm(o_ref[rows, :])
            sg = _sigmoid(g)
            duv = du_ref[rows, :]
            dn = duv * (g * sg)
            dg = duv * n * (sg + g * sg * (1.0 - sg))
            do = rstd * (dn - jnp.mean(dn, axis=-1, keepdims=True) - n * jnp.mean(dn * n, axis=-1, keepdims=True))
            dob = do.astype(BF16)
            rb = st_ref[c]
            drb = dstate[...].astype(BF16)
            sc = (_dot_nt(qb, kb) * dmat).astype(BF16)
            dsc = (_dot_nt(dob, vb) * dmat).astype(BF16)
            qx = (q * xi).astype(BF16)
            kz = (k * zeta).astype(BF16)
            dq = _dot(dsc, kb) + _dot_nt(dob, rb) * xi
            dk = (_dot_tn(dsc, qb) + _dot_nt(vb, drb) * zeta) * (RET_DK ** -0.5)
            dvv = _dot_tn(sc, dob) + _dot(kz, drb)
            dstate[...] = dstate[...] * decay + _dot_tn(qx, dob)
            dp_ref[rows, :RET_DK] = (dq * cs + _roll(dq * sn, RET_DK // 2)).astype(BF16)
            dp_ref[rows, RET_DK:2 * RET_DK] = (dk * cs + _roll(dk * sn, RET_DK // 2)).astype(BF16)
            dp_ref[rows, 2 * RET_DK:2 * RET_DK + dv] = dvv.astype(BF16)
            dp_ref[rows, 2 * RET_DK + dv:] = dg.astype(BF16)

    rev = lambda h, b: (nb - 1 - b, h)
    return pl.pallas_call(
        body, name="ret_bwd", grid=(RET_HEADS, nb),
        in_specs=[pl.BlockSpec(memory_space=pltpu.SMEM),
                  pl.BlockSpec((tb, w), lambda h, b: (nb - 1 - b, cb0 + h)),
                  pl.BlockSpec((tb, RET_DK), lambda h, b: (nb - 1 - b, 0)),
                  pl.BlockSpec((tb, RET_DK), lambda h, b: (nb - 1 - b, 0)),
                  pl.BlockSpec((tb, dv), rev),
                  pl.BlockSpec((tb, dv), rev),
                  pl.BlockSpec((None, nch, RET_DK, dv), lambda h, b: (h, nb - 1 - b, 0, 0)),
                  ANY],
        out_specs=pl.BlockSpec((tb, w), lambda h, b: (nb - 1 - b, cb0 + h)),
        out_shape=jax.ShapeDtypeStruct((s, lay.total), BF16),
        input_output_aliases={7: 0},
        scratch_shapes=[pltpu.VMEM((RET_DK, dv), F32)],
        compiler_params=_cp("parallel", "arbitrary"),
    )(lg, proj, cosr, sinr, o, du, states, dproj)


def _attn_mask(rows, cols, row0, col0, keys_on_rows):
    ri = (lax.broadcasted_iota(jnp.int32, (rows, cols), 0) + row0) // CHUNK
    ci = (lax.broadcasted_iota(jnp.int32, (rows, cols), 1) + col0) // CHUNK
    return ri <= ci if keys_on_rows else ci <= ri


def _attn_fwd(q_cat, k_cat, v_t, proj, lay):
    s = q_cat.shape[0]
    bq = _attn_block(s)
    nq = s // bq
    mg_cb = lay.off["mg"] // MLA_DV

    def body(q_ref, k_ref, vt_ref, g_ref, o_ref, u_ref, lse_ref):
        i = pl.program_id(1)
        q = q_ref[...]

        def scores(j):
            r0 = pl.multiple_of(j * bq, bq)
            return _dot_nt(k_ref[pl.ds(r0, bq), :], q)

        def update(j, sc, m, l, acc):
            mn = jnp.maximum(m, jnp.max(sc, axis=0, keepdims=True))
            p = jnp.exp2(sc - mn)
            alpha = jnp.exp2(m - mn)
            l = alpha * l + jnp.sum(p, axis=0, keepdims=True)
            acc = alpha * acc + _dot(vt_ref[j], p.astype(BF16))
            return mn, l, acc

        def hide(sc):
            return jnp.where(_attn_mask(bq, bq, 0, 0, True), sc, NEG_INF)

        def pair(j, carry, last):
            sa, sb = scores(j), scores(j + 1)
            return update(j + 1, hide(sb) if last else sb, *update(j, sa, *carry))

        init = (jnp.full((1, bq), NEG_INF, F32), jnp.zeros((1, bq), F32), jnp.zeros((MLA_DV, bq), F32))
        carry = lax.fori_loop(0, i // 2, lambda t, c: pair(2 * t, c, False), init)
        m, l, acc = lax.cond(i % 2 == 1, lambda c: pair(i - 1, c, True),
                             lambda c: update(i, hide(scores(i)), *c), carry)
        o = (acc / l).T
        gv = g_ref[...]
        o_ref[...] = o
        u_ref[...] = (o * (gv * _sigmoid(gv))).astype(BF16)
        lse_ref[...] = m + jnp.log2(l)

    return pl.pallas_call(
        body, name="attn_fwd", grid=(MLA_HEADS, nq),
        in_specs=[pl.BlockSpec((bq, MLA_QW), lambda h, i: (i, h)),
                  pl.BlockSpec((s, MLA_QW), lambda h, i: (0, h)),
                  pl.BlockSpec((None, nq, MLA_DV, bq), lambda h, i: (h, 0, 0, 0)),
                  pl.BlockSpec((bq, MLA_DV), lambda h, i: (i, mg_cb + h))],
        out_specs=[pl.BlockSpec((bq, MLA_DV), lambda h, i: (i, h)),
                   pl.BlockSpec((bq, MLA_DV), lambda h, i: (i, h)),
                   pl.BlockSpec((None, None, 1, bq), lambda h, i: (h, i, 0, 0))],
        out_shape=[jax.ShapeDtypeStruct((s, MLA_HEADS * MLA_DV), F32),
                   jax.ShapeDtypeStruct((s, MLA_HEADS * MLA_DV), BF16),
                   jax.ShapeDtypeStruct((MLA_HEADS, nq, 1, bq), F32)],
        compiler_params=_cp("parallel", "parallel"),
    )(q_cat, k_cat, v_t, proj)


def _attn_bwd(q_cat, k_cat, v, k_t, do, o, lse, ta, tb, tc):
    s = q_cat.shape[0]
    blk = _attn_block(s)
    nb = s // blk

    def body(q_ref, k_ref, v_ref, kt_ref, do_ref, o_ref, lse_ref, a_ref, b_ref, c_ref,
             dq_ref, dk_ref, dv_ref, dq_acc, delta):
        j = pl.program_id(1)

        @pl.when(j == 0)
        def _():
            dq_acc[...] = jnp.zeros_like(dq_acc)
            for i in range(nb):
                rows = slice(i * blk, (i + 1) * blk)
                prod = do_ref[rows, :].astype(F32) * o_ref[rows, :]
                delta[i] = jnp.sum(prod.T, axis=0, keepdims=True)

        kb, vb, kt = k_ref[...], v_ref[...], kt_ref[...]

        def step(i, carry, masked):
            dk, dvv = carry
            r0 = pl.multiple_of(i * blk, blk)
            q, dob = q_ref[pl.ds(r0, blk), :], do_ref[pl.ds(r0, blk), :]
            sc = _dot_nt(kb, q)
            if masked:
                sc = jnp.where(_attn_mask(blk, blk, 0, 0, True), sc, NEG_INF)
            p = jnp.exp2(sc - lse_ref[i])
            dvv = dvv + _dot(p.astype(BF16), dob)
            ds = (p * (_dot_nt(vb, dob) - delta[i])).astype(BF16)
            dk = dk + _dot(ds, q)
            dq_acc[i] = dq_acc[i] + _dot(kt, ds)
            return dk, dvv

        carry = step(j, (jnp.zeros((blk, MLA_QW), F32), jnp.zeros((blk, MLA_DV), F32)), True)
        dk, dvv = lax.fori_loop(j + 1, nb, lambda i, c: step(i, c, False), carry)
        dk_ref[...] = dk * LN2
        dv_ref[...] = dvv

        @pl.when(j == nb - 1)
        def _():
            for i in range(nb):
                rows = slice(i * blk, (i + 1) * blk)
                dq = dq_acc[i].T * QK_SCALE
                dq_ref[rows, :MLA_NOPE] = dq[:, :MLA_NOPE].astype(BF16)
                dq_ref[rows, MLA_NOPE:] = _rope_tile_bwd(dq[:, MLA_NOPE:], a_ref[rows, :], b_ref[rows, :],
                                                         c_ref[rows, :]).astype(BF16)

    tab = pl.BlockSpec((s, LANES), lambda h, j: (0, 0))
    return pl.pallas_call(
        body, name="attn_bwd", grid=(MLA_HEADS, nb),
        in_specs=[pl.BlockSpec((s, MLA_QW), lambda h, j: (0, h)),
                  pl.BlockSpec((blk, MLA_QW), lambda h, j: (j, h)),
                  pl.BlockSpec((blk, MLA_DV), lambda h, j: (j, h)),
                  pl.BlockSpec((None, None, MLA_QW, blk), lambda h, j: (h, j, 0, 0)),
                  pl.BlockSpec((s, MLA_DV), lambda h, j: (0, h)),
                  pl.BlockSpec((s, MLA_DV), lambda h, j: (0, h)),
                  pl.BlockSpec((None, nb, 1, blk), lambda h, j: (h, 0, 0, 0)),
                  tab, tab, tab],
        out_specs=[pl.BlockSpec((s, MLA_QW), lambda h, j: (0, h)),
                   pl.BlockSpec((blk, MLA_QW), lambda h, j: (j, h)),
                   pl.BlockSpec((blk, MLA_DV), lambda h, j: (j, h))],
        out_shape=[jax.ShapeDtypeStruct((s, MLA_HEADS * MLA_QW), BF16),
                   jax.ShapeDtypeStruct((s, MLA_HEADS * MLA_QW), F32),
                   jax.ShapeDtypeStruct((s, MLA_HEADS * MLA_DV), F32)],
        scratch_shapes=[pltpu.VMEM((nb, MLA_QW, blk), F32), pltpu.VMEM((nb, 1, blk), F32)],
        compiler_params=_cp("parallel", "arbitrary"),
    )(q_cat, k_cat, v, k_t, do, o, lse, ta, tb, tc)


def _place():
    return lax.axis_index("x"), lax.axis_index("y"), lax.axis_index("c")


def _slot(px, py, pc):
    return 4 * px + 2 * py + pc


def _all_gather(shards, layer, name, vmem=False, deps=()):
    n = len(shards)

    def body(*refs):
        srcs, outs = refs[:n], refs[n + len(deps):2 * n + len(deps)]
        send_sems, recv_sems, local_sems = refs[2 * n + len(deps):]
        x, y, c = _place()
        me, sibling = (x, y, c), (x, y, 1 - c)
        chips = [(1 - x, y), (x, 1 - y), (1 - x, 1 - y)]
        firsts, passes, locals_ = [], [], []

        def copy(a, k, block, to, src=None):
            dst = outs[a].at[_slot(*block)]
            return pltpu.make_async_remote_copy(
                src_ref=dst if src is None else src, dst_ref=dst,
                send_sem=send_sems.at[7 * a + k], recv_sem=recv_sems.at[7 * a + k],
                device_id=to, device_id_type=MESH)

        for a in range(n):
            src = srcs[a] if layer is None else srcs[a].at[layer]
            mine = pltpu.make_async_copy(src, outs[a].at[_slot(*me)], local_sems.at[a])
            mine.start()
            locals_.append(mine)
            first = [copy(a, 0, me, sibling, src=src)]
            first += [copy(a, 1 + j, me, (*chip, c), src=src) for j, chip in enumerate(chips)]
            for cp in first:
                cp.start()
            firsts += first
        for a in range(n):
            for j, chip in enumerate(chips):
                copy(a, 1 + j, (*chip, c), me).wait_recv()
                fwd = copy(a, 4 + j, (*chip, c), sibling)
                fwd.start()
                passes.append(fwd)
        for a in range(n):
            copy(a, 0, sibling, me).wait_recv()
            for j, chip in enumerate(chips):
                copy(a, 4 + j, (*chip, 1 - c), me).wait_recv()
        for cp in firsts + passes:
            cp.wait_send()
        for mine in locals_:
            mine.wait()

    space = pl.BlockSpec(memory_space=pltpu.VMEM) if vmem else ANY
    out_shape = [jax.ShapeDtypeStruct((N_DEV,) + (a.shape if layer is None else a.shape[1:]), a.dtype) for a in shards]
    return pl.pallas_call(
        body, name=name,
        in_specs=[space] * n + [ANY] * len(deps), out_specs=[space] * n, out_shape=out_shape,
        scratch_shapes=[pltpu.SemaphoreType.DMA((7 * n,)), pltpu.SemaphoreType.DMA((7 * n,)),
                        pltpu.SemaphoreType.DMA((n,))],
        compiler_params=pltpu.CompilerParams(has_side_effects=True),
    )(*shards, *deps)


HBM = pl.BlockSpec(memory_space=pltpu.HBM)
SEM = pl.BlockSpec(memory_space=pltpu.SEMAPHORE)
EFFECT = pltpu.SideEffectType.DATAFLOW_SIDE_EFFECTING


def _push_copies(srcs, lands, send_sems, recv_sems, local_sems, by_peer):
    x, y, c = _place()
    me = _slot(x, y, c)
    local, remote = [], []
    for a, (src, land) in enumerate(zip(srcs, lands)):
        local.append(pltpu.make_async_copy(src.at[me] if by_peer else src, land.at[me], local_sems.at[a]))
        for r in range(1, N_DEV):
            peer = (1 - x if r & 4 else x, 1 - y if r & 2 else y, 1 - c if r & 1 else c)
            remote.append(pltpu.make_async_remote_copy(
                src_ref=src.at[_slot(*peer)] if by_peer else src, dst_ref=land.at[me],
                send_sem=send_sems.at[7 * a + r - 1], recv_sem=recv_sems.at[7 * a + r - 1],
                device_id=peer, device_id_type=MESH))
    return local, remote


def _push_start(srcs, by_peer, after, name):
    n = len(srcs)
    srcs = [pltpu.with_memory_space_constraint(a, pltpu.HBM) for a in srcs]
    lands = [pltpu.with_memory_space_constraint(
        lax.empty((N_DEV,) + (a.shape[1:] if by_peer else a.shape), a.dtype), pltpu.HBM) for a in srcs]

    def body(*refs):
        k = 2 * n + len(after)
        local, remote = _push_copies(refs[:n], refs[n:2 * n], refs[k], refs[k + 1], refs[k + 2], by_peer)
        for cp in local + remote:
            cp.start()
        token = refs[k + 3 + 2 * n]
        token[...] = jnp.zeros_like(token)

    outs = pl.pallas_call(
        body, name=name,
        out_shape=(pltpu.SemaphoreType.DMA((7 * n,)), pltpu.SemaphoreType.DMA((7 * n,)),
                   pltpu.SemaphoreType.DMA((n,)),
                   *[pltpu.HBM(a.shape, a.dtype) for a in srcs], *[pltpu.HBM(a.shape, a.dtype) for a in lands],
                   jax.ShapeDtypeStruct((8, LANES), F32)),
        in_specs=[HBM] * (2 * n) + [ANY] * len(after),
        out_specs=(SEM, SEM, SEM, *([HBM] * (2 * n)), pl.BlockSpec(memory_space=pltpu.VMEM)),
        input_output_aliases={i: 3 + i for i in range(2 * n)},
        compiler_params=pltpu.CompilerParams(has_side_effects=EFFECT),
    )(*srcs, *lands, *after)
    return outs[:3], outs[3:3 + n], outs[3 + n:3 + 2 * n], outs[3 + 2 * n]


def _push_wait(sems, srcs, lands, by_peer, after, name):
    n = len(srcs)

    def body(*refs):
        local, remote = _push_copies(refs[:n], refs[n:2 * n], refs[2 * n], refs[2 * n + 1], refs[2 * n + 2], by_peer)
        for cp in local:
            cp.wait()
        for cp in remote:
            cp.wait_send()
            cp.wait_recv()

    outs = pl.pallas_call(
        body, name=name,
        out_shape=[pltpu.HBM(a.shape, a.dtype) for a in list(srcs) + list(lands)],
        in_specs=[HBM] * (2 * n) + [SEM] * 3 + [ANY] * len(after),
        out_specs=[HBM] * (2 * n),
        input_output_aliases={i: i for i in range(2 * n)},
        compiler_params=pltpu.CompilerParams(has_side_effects=EFFECT),
    )(*srcs, *lands, *sems, *after)
    return outs[n:]


def _adam_math(g, w, m, v):
    m = ADAM_B1 * m + (1.0 - ADAM_B1) * g
    v = ADAM_B2 * v + (1.0 - ADAM_B2) * (g * g)
    m_hat = m / (1.0 - ADAM_B1 ** ADAM_STEP)
    v_hat = v / (1.0 - ADAM_B2 ** ADAM_STEP)
    delta = -ADAM_LR * (m_hat / (jnp.sqrt(v_hat) + ADAM_EPS) + ADAM_WD * w)
    return delta, m, v


def _adam_sharded(recvs, first, w, m, v, prev, name):
    nl, r, c = w.shape
    n = len(recvs)
    by_cols = r % 128 != 0
    tr, tc = (r, _tile(c, LANES)) if by_cols else (_rtile(r, 128), c)
    nr = c // tc if by_cols else r // tr
    prev = list(prev) if prev is not None else []

    def tile(t):
        return (0, t) if by_cols else (t, 0)

    def body(*refs):
        g_refs = refs[:n]
        w_ref, m_ref, v_ref = refs[n:n + 3]
        go_ref, d_ref, mo_ref, vo_ref = refs[n + 3 + len(prev):]
        layer = pl.program_id(0)
        for l in range(n):
            @pl.when(layer == l)
            def _(l=l):
                g = g_refs[l][0].astype(F32)
                for i in range(1, N_DEV):
                    g = g + g_refs[l][i].astype(F32)
                delta, mn, vn = _adam_math(g, w_ref[...], m_ref[...], v_ref[...])
                go_ref[...] = g
                d_ref[...] = delta
                mo_ref[...] = mn
                vo_ref[...] = vn

    def recv_spec(l):
        def index(layer, i):
            return (0,) + tile(jnp.where(layer == l, i, jnp.where(layer < l, 0, nr - 1)))
        return pl.BlockSpec((N_DEV, tr, tc), index)

    blk = pl.BlockSpec((None, tr, tc), lambda layer, i: (first + layer,) + tile(i))
    out = jax.ShapeDtypeStruct(w.shape, F32)
    return pl.pallas_call(
        body, name=name, grid=(n, nr),
        in_specs=[recv_spec(l) for l in range(n)] + [blk, blk, blk] + [ANY] * len(prev),
        out_specs=[blk] * 4, out_shape=[out] * 4,
        input_output_aliases={n + 3 + k: k for k in range(len(prev))},
        compiler_params=_cp("arbitrary", "arbitrary"),
    )(*recvs, w, m, v, *prev)


def _adam_mod(c_all_t, dmod, w, m, v):
    nl, d, c = w.shape
    tr = _rtile(d, 128)

    def body(ct_ref, dm_ref, w_ref, m_ref, v_ref, go_ref, d_ref, mo_ref, vo_ref):
        ct = ct_ref[...].astype(BF16).astype(F32)
        dm = dm_ref[...].astype(BF16).astype(F32)
        g = ct[:, 0:1] * dm[0:1, :]
        for b in range(1, N_DEV):
            g = g + ct[:, b:b + 1] * dm[b:b + 1, :]
        delta, mn, vn = _adam_math(g, w_ref[...], m_ref[...], v_ref[...])
        go_ref[...] = g
        d_ref[...] = delta
        mo_ref[...] = mn
        vo_ref[...] = vn

    blk = pl.BlockSpec((None, tr, c), lambda layer, i: (layer, i, 0))
    out = jax.ShapeDtypeStruct(w.shape, F32)
    return pl.pallas_call(
        body, name="adam_mod", grid=(nl, d // tr),
        in_specs=[pl.BlockSpec((tr, N_DEV), lambda layer, i: (i, 0)),
                  pl.BlockSpec((None, N_DEV, c), lambda layer, i: (layer, 0, 0)), blk, blk, blk],
        out_specs=[blk] * 4, out_shape=[out] * 4,
        compiler_params=_cp("parallel", "parallel"),
    )(c_all_t, dmod, w, m, v)


def _adam_small(g, w, m, v, name):
    def body(g_ref, w_ref, m_ref, v_ref, d_ref, mo_ref, vo_ref):
        delta, mn, vn = _adam_math(g_ref[...], w_ref[...], m_ref[...], v_ref[...])
        d_ref[...] = delta
        mo_ref[...] = mn
        vo_ref[...] = vn

    out = jax.ShapeDtypeStruct(w.shape, F32)
    return pl.pallas_call(body, name=name, out_shape=[out] * 3)(g, w, m, v)


def _sum_devices(parts):
    def body(p_ref, o_ref):
        acc = p_ref[0]
        for i in range(1, N_DEV):
            acc = acc + p_ref[i]
        o_ref[...] = acc

    return pl.pallas_call(body, name="sum_devices",
                          out_shape=jax.ShapeDtypeStruct(parts.shape[1:], F32))(parts)


def _rope_tables(positions):
    pos = positions.astype(F32)[:, None]

    def cs(dim):
        inv = 1.0 / (ROPE_BASE ** (jnp.arange(0, dim, 2, dtype=F32) / dim))
        ang = pos * inv
        return jnp.cos(ang), jnp.sin(ang)

    cr, sr = cs(RET_DK)
    cm, sm = cs(MLA_ROPE)
    z = jnp.zeros_like(cm)
    pad = jnp.zeros((pos.shape[0], LANES - MLA_ROPE), F32)
    cosr = jnp.concatenate([cr, cr], axis=1)
    sinr = jnp.concatenate([-sr, sr], axis=1)
    ta = jnp.concatenate([cm, cm, pad], axis=1)
    tb = jnp.concatenate([-sm, z, pad], axis=1)
    tc = jnp.concatenate([z, sm, pad], axis=1)
    return cosr, sinr, ta, tb, tc


def _layer_fwd(x, mod, g_norm, g_cq, g_ckv, wts, tabs, lg, lay, deps):
    d = x.shape[1]
    cosr, sinr, ta, tb, tc = tabs
    shift, scale, gate = mod[:, :d], mod[:, d:2 * d], mod[:, 2 * d:]
    h = _norm_mod_fwd(x, g_norm, scale, shift, deps)
    proj = _matmul(h, wts["in"], name="mm_proj", tn_cap=1920)
    o_ret, u_ret, states = _ret_fwd(proj, lg, cosr, sinr, lay)
    y_ret = _matmul(u_ret, wts["ret"], name="mm_y")
    cqn, ckvn = _mla_prep(proj, g_cq, g_ckv, lay)
    qp = _matmul(cqn, wts["uq"], name="mm_up")
    kvp = _matmul(ckvn, wts["ukv"], name="mm_up")
    q_cat, k_cat, v, k_t, v_t = _qk_prep(qp, kvp, proj, ta, tb, tc, lay)
    o_mla, u_mla, lse = _attn_fwd(q_cat, k_cat, v_t, proj, lay)
    y_mla = _matmul(u_mla, wts["mla"], name="mm_y")
    merged = _merge_fwd(y_ret, y_mla, proj, lay)
    out = _matmul(merged, wts["out"], name="mm_y")
    x_next = _resid_fwd(x, out, gate)
    saved = dict(x=x, h=h, proj=proj, o_ret=o_ret, u_ret=u_ret, states=states, y_ret=y_ret, cqn=cqn,
                 ckvn=ckvn, q_cat=q_cat, k_cat=k_cat, v=v, k_t=k_t, o_mla=o_mla, u_mla=u_mla, lse=lse,
                 y_mla=y_mla, merged=merged, out=out)
    return x_next, saved


def _to_owner_blocks_cols(g, n_local):
    k = g.shape[0]
    return g.reshape(k, N_DEV, n_local).transpose(1, 0, 2)


def _from_owner_blocks_cols(g):
    return g.transpose(1, 0, 2).reshape(g.shape[1], -1)


def _layer_bwd(dxn, sv, mod, g_norm, g_cq, g_ckv, wts, tabs, lg, lay, deps, shard_cols, push):
    d = dxn.shape[1]
    n_in, n_uq, n_ukv = shard_cols
    cosr, sinr, ta, tb, tc = tabs
    scale, gate = mod[:, d:2 * d], mod[:, 2 * d:]
    gdt = BF16
    dout, dgate = _resid_bwd(dxn, sv["out"], gate, deps)
    dmerged = _matmul(dout, wts["out"], tb=True, name="mm_dy")
    dw_out = _matmul(sv["merged"], dout, ta=True, out_dtype=gdt, name="mm_dw")
    dy_ret, dy_mla, dproj = _merge_bwd(dmerged, sv["y_ret"], sv["y_mla"], sv["proj"], lay)
    du_ret = _matmul(dy_ret, wts["ret"], tb=True, name="mm_dy")
    dw_ret = _matmul(sv["u_ret"], dy_ret, ta=True, out_dtype=gdt, name="mm_dw")
    dproj = _ret_bwd(sv["proj"], lg, cosr, sinr, sv["o_ret"], du_ret, sv["states"], dproj, lay)
    du_mla = _matmul(dy_mla, wts["mla"], tb=True, name="mm_dy")
    dw_mla = _matmul(sv["u_mla"], dy_mla, ta=True, out_dtype=gdt, name="mm_dw")
    do_mla, dproj = _mla_gate_bwd(du_mla, sv["o_mla"], sv["proj"], dproj, lay)
    dqp, dk_cat, dv = _attn_bwd(sv["q_cat"], sv["k_cat"], sv["v"], sv["k_t"], do_mla, sv["o_mla"], sv["lse"],
                                ta, tb, tc)
    dkvp, dproj = _kv_bwd_prep(dk_cat, dv, ta, tb, tc, dproj, lay)
    dcqn = _matmul(dqp, wts["uq"], tb=True, name="mm_dlat")
    dckvn = _matmul(dkvp, wts["ukv"], tb=True, name="mm_dlat")
    dw_uq = _matmul(sv["cqn"], dqp, ta=True, out_dtype=gdt, name="mm_dwup")
    dw_ukv = _matmul(sv["ckvn"], dkvp, ta=True, out_dtype=gdt, name="mm_dwup")
    sent = push("a", [_to_owner_blocks_cols(_uq_to_logical(dw_uq), n_uq),
                      _to_owner_blocks_cols(_ukv_to_logical(dw_ukv), n_ukv),
                      dw_ret.reshape(N_DEV, -1, d), dw_mla.reshape(N_DEV, -1, d), dw_out.reshape(N_DEV, -1, d)])
    dproj, dg_cq, dg_ckv = _mla_prep_bwd(sv["proj"], dcqn, dckvn, g_cq, g_ckv, dproj, lay)
    dw_in = _matmul(sv["h"], dproj, ta=True, out_dtype=gdt, name="mm_dwin", tn_cap=1920, deps=sent)
    sent = push("b", [_scatter_dw_in(dw_in, lay, n_in)])
    dh = _matmul(dproj, wts["in"], tb=True, name="mm_dh", deps=sent)
    dx, dshift, dscale, dg_norm = _norm_mod_bwd(sv["x"], g_norm, scale, dh, dxn)
    dmod = jnp.concatenate([dshift, dscale, dgate], axis=1)
    small = dict(dmod=dmod, g_norm=dg_norm, g_cq=dg_cq, g_ckv=dg_ckv)
    return dx, small


def kernel(x, c, positions, w_mod, b_mod, g_norm, w_in, g_cq, g_ckv, w_uq, w_ukv, w_ret_proj, w_mla_proj, w_out, g_final, loss_target, m_w_mod, m_b_mod, m_g_norm, m_w_in, m_g_cq, m_g_ckv, m_w_uq, m_w_ukv, m_w_ret_proj, m_w_mla_proj, m_w_out, m_g_final, v_w_mod, v_b_mod, v_g_norm, v_w_in, v_g_cq, v_g_ckv, v_w_uq, v_w_ukv, v_w_ret_proj, v_w_mla_proj, v_w_out, v_g_final):
    nl, d, _ = w_mod.shape
    s = x.shape[1]
    rank = g_cq.shape[1]
    lay = Layout(d, rank, g_ckv.shape[1])
    me = _slot(*_place())
    x0 = x.reshape(s, d)
    target = loss_target.reshape(s, d)
    tabs = _rope_tables(positions.reshape(s))
    lg = jnp.log(1.0 - 2.0 ** (-5.0 - jnp.arange(RET_HEADS, dtype=F32)))

    c_act = c * _sigmoid(c)
    (c_all,) = _all_gather([c_act.reshape(d // LANES, LANES)], None, "gather_c", vmem=True)
    c_all = c_all.reshape(N_DEV, d)
    n_mod = w_mod.shape[2]
    mod_part = jnp.stack([_matmul(c_all, w_mod[l], name="mm_mod", tm_cap=8) for l in range(nl)])
    (mod_all,) = _all_gather([mod_part.reshape(-1, LANES)], None, "gather_mod", vmem=True)
    mod_all = mod_all.reshape(N_DEV, nl, N_DEV, n_mod)
    mod = lax.dynamic_index_in_dim(mod_all, me, axis=2, keepdims=False)
    mod = mod.transpose(1, 0, 2).reshape(nl, N_DEV * n_mod) + b_mod

    shards = [[w[l].astype(BF16) for w in (w_in, w_uq, w_ukv, w_ret_proj, w_mla_proj, w_out)] for l in range(nl)]
    xl, saved, wts_all = x0, [], []
    gathered = _all_gather(shards[0], None, "gather_w")
    for l in range(nl):
        g_in, g_uq, g_ukv, g_ret, g_mla, g_out = gathered
        deps = []
        if l + 1 < nl:
            sems, srcs, lands, token = _push_start(shards[l + 1], False, [g_out, mod], "gather_start_%d" % (l + 1))
            deps = [token]
        wts = {
            "in": _assemble_w_in(g_in, lay),
            "uq": _uq_to_physical(_from_owner_blocks_cols(g_uq)),
            "ukv": _ukv_to_physical(_from_owner_blocks_cols(g_ukv)),
            "ret": g_ret.reshape(-1, d), "mla": g_mla.reshape(-1, d), "out": g_out.reshape(-1, d),
        }
        wts_all.append(wts)
        xl, sv = _layer_fwd(xl, mod[l:l + 1], g_norm[l:l + 1], g_cq[l:l + 1], g_ckv[l:l + 1], wts, tabs, lg, lay,
                            deps)
        saved.append(sv)
        if l + 1 < nl:
            gathered = _push_wait(sems, srcs, lands, False, [xl], "gather_wait_%d" % (l + 1))
    loss_lanes, dx, dg_final = _final_loss(xl, g_final.reshape(1, d), target)

    small = [None] * nl
    flying = {l: [] for l in range(nl)}
    recv = {}
    shard_cols = (w_in.shape[2], w_uq.shape[2], w_ukv.shape[2])

    def pusher(l):
        def push(group, arrays):
            sems, srcs, lands, token = _push_start(arrays, True, [], "exchange_start_%d%s" % (l, group))
            flying[l].append((group, sems, srcs, lands))
            return [token]
        return push

    def land(l, after):
        got = {}
        for group, sems, srcs, lands in flying[l]:
            got[group] = _push_wait(sems, srcs, lands, True, after, "exchange_wait_%d%s" % (l, group))
        recv[l] = list(got["b"]) + list(got["a"])

    for l in reversed(range(nl)):
        dx, small[l] = _layer_bwd(dx, saved[l], mod[l:l + 1], g_norm[l:l + 1], g_cq[l:l + 1], g_ckv[l:l + 1],
                                  wts_all[l], tabs, lg, lay, [], shard_cols, pusher(l))
        if l + 1 < nl:
            land(l + 1, [dx])
    grad_x = dx.reshape(x.shape)

    out = {}
    w_in_t, m_w_in_t, v_w_in_t = (jnp.swapaxes(a, 1, 2) for a in (w_in, m_w_in, v_w_in))
    sharded = (("w_in", w_in_t, m_w_in_t, v_w_in_t), ("w_uq", w_uq, m_w_uq, v_w_uq), ("w_ukv", w_ukv, m_w_ukv, v_w_ukv),
               ("w_ret_proj", w_ret_proj, m_w_ret_proj, v_w_ret_proj),
               ("w_mla_proj", w_mla_proj, m_w_mla_proj, v_w_mla_proj), ("w_out", w_out, m_w_out, v_w_out))
    if nl > 1:
        for i, (key, w, m, v) in enumerate(sharded):
            out[key] = _adam_sharded([recv[l][i] for l in range(1, nl)], 1, w, m, v, None, "adam_" + key)
    done = [out[key][0] for key, _, _, _ in sharded if key in out]

    pack = jnp.concatenate(
        [jnp.concatenate([sm[k] for sm in small], axis=0).reshape(-1)
         for k in ("dmod", "g_norm", "g_cq", "g_ckv")] + [dg_final.reshape(-1), loss_lanes.reshape(-1)])
    (pack_all,) = _all_gather([pack.reshape(-1, LANES)], None, "gather_small", vmem=True, deps=done)
    tot = _sum_devices(pack_all).reshape(-1)
    sizes = [nl * 3 * d, nl * d, nl * rank, nl * rank, d]
    offs = np.cumsum([0] + sizes)
    grad_b_mod = tot[offs[0]:offs[1]].reshape(nl, 3 * d)
    grad_g_norm = tot[offs[1]:offs[2]].reshape(nl, d)
    grad_g_cq = tot[offs[2]:offs[3]].reshape(nl, rank)
    grad_g_ckv = tot[offs[3]:offs[4]].reshape(nl, rank)
    grad_g_final = tot[offs[4]:offs[5]]
    loss = tot[offs[5]]
    dmod_all = pack_all.reshape(N_DEV, -1)[:, :sizes[0]].reshape(N_DEV, nl, 3 * d)
    dmod_mine = lax.dynamic_slice_in_dim(dmod_all, me * n_mod, n_mod, axis=2).transpose(1, 0, 2)

    out["w_mod"] = _adam_mod(c_all.T, dmod_mine, w_mod, m_w_mod, v_w_mod)
    land(0, [out["w_mod"][0]])
    for i, (key, w, m, v) in enumerate(sharded):
        out[key] = _adam_sharded([recv[0][i]], 0, w, m, v, out.get(key), "adam0_" + key)
    out["w_in"] = tuple(jnp.swapaxes(a, 1, 2) for a in out["w_in"])
    for key, g, w, m, v in (("b_mod", grad_b_mod, b_mod, m_b_mod, v_b_mod),
                            ("g_norm", grad_g_norm, g_norm, m_g_norm, v_g_norm),
                            ("g_cq", grad_g_cq, g_cq, m_g_cq, v_g_cq),
                            ("g_ckv", grad_g_ckv, g_ckv, m_g_ckv, v_g_ckv),
                            ("g_final", grad_g_final.reshape(1, d), g_final.reshape(1, d),
                             m_g_final.reshape(1, d), v_g_final.reshape(1, d))):
        out[key] = (g,) + tuple(_adam_small(g, w, m, v, "adam_" + key))
    out["g_final"] = tuple(a.reshape(d) for a in out["g_final"])

    names = ("w_mod", "b_mod", "g_norm", "w_in", "g_cq", "g_ckv", "w_uq", "w_ukv", "w_ret_proj",
             "w_mla_proj", "w_out", "g_final")
    return (loss, grad_x, *[out[k][0] for k in names], *[out[k][1] for k in names],
            *[out[k][2] for k in names], *[out[k][3] for k in names])
```

```python
import functools
import itertools

import jax
import jax.numpy as jnp
import numpy as np
from jax import lax
from jax.experimental import pallas as pl
from jax.experimental.pallas import tpu as pltpu

F32 = jnp.float32
BF16 = jnp.bfloat16

N_DEV = 8
CHUNK = 64
EPS = 1e-6
NEG_INF = -1e30
ROPE_BASE = 10000.0
LANES = 128

RET_HEADS = 8
RET_DK = 128
RET_DV = 256
MLA_HEADS = 16
MLA_NOPE = 128
MLA_ROPE = 64
MLA_DV = 128
MLA_QW = 256
QK_SCALE = (MLA_NOPE + MLA_ROPE) ** -0.5
QK_LOG2_SCALE = QK_SCALE * 1.4426950408889634
LN2 = 0.6931471805599453

ADAM_LR = 0.001
ADAM_B1 = 0.9
ADAM_B2 = 0.999
ADAM_EPS = 1e-08
ADAM_WD = 0.01
ADAM_STEP = 10

VMEM_LIMIT_BYTES = 56 * 1024 * 1024
MESH = pl.DeviceIdType.MESH
ANY = pl.BlockSpec(memory_space=pl.ANY)


def _cp(*sem):
    return pltpu.CompilerParams(dimension_semantics=sem if sem else None,
                                vmem_limit_bytes=VMEM_LIMIT_BYTES)


def _tile(n, cap):
    best = None
    t = LANES
    while t <= min(n, cap):
        if n % t == 0:
            best = t
        t += LANES
    return best if best is not None else n


def _rtile(n, cap):
    t = cap
    while t > 8 and n % t:
        t //= 2
    return t if n % t == 0 else n


def _sigmoid(x):
    return 1.0 / (1.0 + jnp.exp(-x))


def _dot(a, b):
    return lax.dot_general(a, b, (((1,), (0,)), ((), ())), preferred_element_type=F32)


def _dot_nt(a, b):
    return lax.dot_general(a, b, (((1,), (1,)), ((), ())), preferred_element_type=F32)


def _dot_tn(a, b):
    return lax.dot_general(a, b, (((0,), (0,)), ((), ())), preferred_element_type=F32)


def _roll(x, s):
    return pltpu.roll(x, s, 1)


class Layout:
    def __init__(self, d_model, q_rank, kv_rank):
        assert q_rank == kv_rank
        self.d = d_model
        self.rank = q_rank
        self.ret_w = 2 * RET_DK + 2 * RET_DV
        self.ret_qk = RET_HEADS * RET_DK
        self.ret_v = RET_HEADS * RET_DV
        self.mla_v = MLA_HEADS * MLA_DV
        widths = {"bg": 2 * d_model, "mg": self.mla_v, "ret": RET_HEADS * self.ret_w,
                  "cqkv": 2 * q_rank, "kr": LANES}
        blocks = {"bg": 2 * d_model, "mg": self.mla_v, "ret": self.ret_w,
                  "cqkv": 2 * q_rank, "kr": LANES}
        for order in itertools.permutations(widths):
            off, offs, ok = 0, {}, True
            for name in order:
                if off % blocks[name]:
                    ok = False
                    break
                offs[name] = off
                off += widths[name]
            if ok:
                break
        assert ok, "no aligned layout"
        self.order, self.off, self.width, self.total = order, offs, widths, off
        lo, o = {}, 0
        for name, w in (("rq", self.ret_qk), ("rk", self.ret_qk), ("rv", self.ret_v),
                        ("rg", self.ret_v), ("cq", q_rank), ("ckv", kv_rank), ("kr", MLA_ROPE),
                        ("mg", self.mla_v), ("bg", 2 * d_model)):
            lo[name] = (o, w)
            o += w
        self.logical, self.d_in = lo, o

    def pieces(self):
        lo = self.logical
        out = []
        for name in self.order:
            if name == "bg":
                out.append(lo["bg"])
            elif name == "mg":
                out.append(lo["mg"])
            elif name == "ret":
                for h in range(RET_HEADS):
                    out.append((lo["rq"][0] + h * RET_DK, RET_DK))
                    out.append((lo["rk"][0] + h * RET_DK, RET_DK))
                    out.append((lo["rv"][0] + h * RET_DV, RET_DV))
                    out.append((lo["rg"][0] + h * RET_DV, RET_DV))
            elif name == "cqkv":
                out.append((lo["cq"][0], 2 * self.rank))
            elif name == "kr":
                out.append(lo["kr"])
                out.append((None, LANES - MLA_ROPE))
        return out


RELAYOUT_CHUNK = 512


def _relayout_plan(lay, n_local):
    plan, off = [], 0
    for start, width in lay.pieces():
        done = 0
        while done < width:
            w = min(RELAYOUT_CHUNK, width - done)
            srcs = []
            if start is not None:
                lo, hi = start + done, start + done + w
                while lo < hi:
                    j = lo // n_local
                    end = min(hi, (j + 1) * n_local)
                    srcs.append((j, lo - j * n_local, end - j * n_local))
                    lo = end
            plan.append((off + done, w, srcs))
            done += w
        off += width
    merged = []
    for p, w, srcs in plan:
        if merged and merged[-1][0] % LANES == 0 and (merged[-1][1] % LANES) and p == merged[-1][0] + merged[-1][1]:
            q, qw, qs = merged.pop()
            merged.append((q, qw + w, qs + ([("pad", w)] if not srcs else srcs)))
        else:
            merged.append((p, w, srcs))
    return merged


def _assemble_w_in(g, lay):
    _, d, n_local = g.shape
    tr = _rtile(d, 256)
    plan = _relayout_plan(lay, n_local)

    def body(g_ref, o_ref):
        for p, w, srcs in plan:
            parts = []
            for src in srcs:
                if src[0] == "pad":
                    parts.append(jnp.zeros((tr, src[1]), F32))
                else:
                    j, a, b = src
                    parts.append(g_ref[j, :, a:b].astype(F32))
            if not parts:
                parts = [jnp.zeros((tr, w), F32)]
            val = parts[0] if len(parts) == 1 else jnp.concatenate(parts, axis=1)
            o_ref[:, p:p + w] = val.astype(o_ref.dtype)

    return pl.pallas_call(
        body, name="assemble_w_in", grid=(d // tr,),
        in_specs=[pl.BlockSpec((N_DEV, tr, n_local), lambda i: (0, i, 0))],
        out_specs=pl.BlockSpec((tr, lay.total), lambda i: (i, 0)),
        out_shape=jax.ShapeDtypeStruct((d, lay.total), g.dtype),
        compiler_params=_cp("parallel"),
    )(g)


def _scatter_dw_in(dw, lay, n_local):
    d = dw.shape[0]
    tr = _rtile(d, 256)
    n_pad = -(-n_local // LANES) * LANES
    plan = _relayout_plan(lay, n_local)
    runs = [[] for _ in range(N_DEV)]
    for p, w, srcs in plan:
        at = p
        for src in srcs:
            if src[0] == "pad":
                at += src[1]
                continue
            j, a, b = src
            runs[j].append((a, b, at))
            at += b - a
    for r in runs:
        r.sort()

    def body(dw_ref, o_ref):
        for j in range(N_DEV):
            parts = [dw_ref[:, at:at + (b - a)].astype(F32) for a, b, at in runs[j]]
            if n_pad > n_local:
                parts.append(jnp.zeros((tr, n_pad - n_local), F32))
            val = jnp.concatenate(parts, axis=1).T
            o_ref[j] = val[:n_local, :].astype(BF16)

    return pl.pallas_call(
        body, name="scatter_dw_in", grid=(d // tr,),
        in_specs=[pl.BlockSpec((tr, lay.total), lambda i: (i, 0))],
        out_specs=pl.BlockSpec((N_DEV, n_local, tr), lambda i: (0, 0, i)),
        out_shape=jax.ShapeDtypeStruct((N_DEV, n_local, d), BF16),
        compiler_params=_cp("parallel"),
    )(dw)


def _uq_to_physical(w):
    k = w.shape[0]
    w3 = w.reshape(k, MLA_HEADS, MLA_NOPE + MLA_ROPE)
    pad = jnp.zeros((k, MLA_HEADS, MLA_QW - MLA_NOPE - MLA_ROPE), w.dtype)
    return jnp.concatenate([w3, pad], axis=2).reshape(k, MLA_HEADS * MLA_QW)


def _uq_to_logical(w):
    k = w.shape[0]
    return w.reshape(k, MLA_HEADS, MLA_QW)[:, :, :MLA_NOPE + MLA_ROPE].reshape(k, -1)


def _ukv_to_physical(w):
    k = w.shape[0]
    w3 = w.reshape(k, MLA_HEADS, MLA_NOPE + MLA_DV)
    return jnp.concatenate([w3[:, :, :MLA_NOPE].reshape(k, -1), w3[:, :, MLA_NOPE:].reshape(k, -1)], axis=1)


def _ukv_to_logical(w):
    k = w.shape[0]
    kn = w[:, :MLA_HEADS * MLA_NOPE].reshape(k, MLA_HEADS, MLA_NOPE)
    v = w[:, MLA_HEADS * MLA_NOPE:].reshape(k, MLA_HEADS, MLA_DV)
    return jnp.concatenate([kn, v], axis=2).reshape(k, -1)


def _matmul(a, b, *, ta=False, tb=False, out_dtype=F32, name, tm_cap=1024, tn_cap=1024, tk_cap=2048, deps=()):
    m, k = (a.shape[1], a.shape[0]) if ta else a.shape
    n = b.shape[0] if tb else b.shape[1]
    assert k == (b.shape[1] if tb else b.shape[0])
    tm, tn, tk = _tile(m, tm_cap), _tile(n, tn_cap), _tile(k, tk_cap)
    nk = k // tk

    def body(a_ref, b_ref, *rest):
        dims = (((0 if ta else 1,), (1 if tb else 0,)), ((), ()))
        part = lax.dot_general(a_ref[...].astype(BF16), b_ref[...].astype(BF16), dims, preferred_element_type=F32)
        if nk == 1:
            rest[len(deps)][...] = part.astype(out_dtype)
            return
        o_ref, acc_ref = rest[len(deps):]
        kk = pl.program_id(2)

        @pl.when(kk == 0)
        def _():
            acc_ref[...] = part

        @pl.when(jnp.logical_and(kk > 0, kk < nk - 1))
        def _():
            acc_ref[...] += part

        @pl.when(kk == nk - 1)
        def _():
            o_ref[...] = (acc_ref[...] + part).astype(o_ref.dtype)

    a_spec = pl.BlockSpec((tk, tm), lambda i, j, kk: (kk, i)) if ta else pl.BlockSpec((tm, tk), lambda i, j, kk: (i, kk))
    b_spec = pl.BlockSpec((tn, tk), lambda i, j, kk: (j, kk)) if tb else pl.BlockSpec((tk, tn), lambda i, j, kk: (kk, j))
    return pl.pallas_call(
        body, name=name, grid=(m // tm, n // tn, nk),
        in_specs=[a_spec, b_spec] + [ANY] * len(deps),
        out_specs=pl.BlockSpec((tm, tn), lambda i, j, kk: (i, j)),
        out_shape=jax.ShapeDtypeStruct((m, n), out_dtype),
        scratch_shapes=[pltpu.VMEM((tm, tn), F32)] if nk > 1 else [],
        compiler_params=_cp("parallel", "parallel", "arbitrary"),
    )(a, b, *deps)


def _row(tm, w, cb=0):
    return pl.BlockSpec((tm, w), lambda i, cb=cb: (i, cb))


def _vec(w, cb=0):
    return pl.BlockSpec((1, w), lambda i, cb=cb: (0, cb))


def _dproj_out(dproj, s, lay, tm, name):
    w = lay.width[name]
    cb = lay.off[name] // w
    spec = _row(tm, w, cb)
    shape = jax.ShapeDtypeStruct((s, lay.total), BF16)
    return spec, shape


def _norm_mod_fwd(x, g, scale, shift, deps):
    s, d = x.shape
    tm = _rtile(s, 256)

    def body(x_ref, g_ref, sc_ref, sh_ref, *rest):
        h_ref = rest[len(deps)]
        xv = x_ref[...]
        xh = xv * lax.rsqrt(jnp.mean(xv * xv, axis=-1, keepdims=True) + EPS)
        h_ref[...] = ((xh * g_ref[...]) * (1.0 + sc_ref[...]) + sh_ref[...]).astype(BF16)

    return pl.pallas_call(
        body, name="norm_mod_fwd", grid=(s // tm,),
        in_specs=[_row(tm, d), _vec(d), _vec(d), _vec(d)] + [ANY] * len(deps),
        out_specs=_row(tm, d), out_shape=jax.ShapeDtypeStruct((s, d), BF16),
        compiler_params=_cp("parallel"),
    )(x, g, scale, shift, *deps)


def _norm_mod_bwd(x, g, scale, dh, dres):
    s, d = x.shape
    tm = _rtile(s, 256)

    def body(x_ref, g_ref, sc_ref, dh_ref, dres_ref, dx_ref, dsh_ref, dsc_ref, dg_ref):
        @pl.when(pl.program_id(0) == 0)
        def _():
            dsh_ref[...] = jnp.zeros_like(dsh_ref)
            dsc_ref[...] = jnp.zeros_like(dsc_ref)
            dg_ref[...] = jnp.zeros_like(dg_ref)

        xv, gv, dhv = x_ref[...], g_ref[...], dh_ref[...]
        rstd = lax.rsqrt(jnp.mean(xv * xv, axis=-1, keepdims=True) + EPS)
        xh = xv * rstd
        dy = dhv * (1.0 + sc_ref[...])
        dxh = dy * gv
        dx_ref[...] = dres_ref[...] + rstd * (dxh - xh * jnp.mean(dxh * xh, axis=-1, keepdims=True))
        dsh_ref[...] += jnp.sum(dhv, axis=0, keepdims=True)
        dsc_ref[...] += jnp.sum(dhv * (xh * gv), axis=0, keepdims=True)
        dg_ref[...] += jnp.sum(dy * xh, axis=0, keepdims=True)

    vec = jax.ShapeDtypeStruct((1, d), F32)
    return pl.pallas_call(
        body, name="norm_mod_bwd", grid=(s // tm,),
        in_specs=[_row(tm, d), _vec(d), _vec(d), _row(tm, d), _row(tm, d)],
        out_specs=[_row(tm, d), _vec(d), _vec(d), _vec(d)],
        out_shape=[jax.ShapeDtypeStruct((s, d), F32), vec, vec, vec],
        compiler_params=_cp("arbitrary"),
    )(x, g, scale, dh, dres)


def _final_loss(x, g, target):
    s, d = x.shape
    tm = _rtile(s, 256)

    def body(x_ref, g_ref, t_ref, l_ref, dx_ref, dg_ref):
        @pl.when(pl.program_id(0) == 0)
        def _():
            l_ref[...] = jnp.zeros_like(l_ref)
            dg_ref[...] = jnp.zeros_like(dg_ref)

        xv, gv = x_ref[...], g_ref[...]
        rstd = lax.rsqrt(jnp.mean(xv * xv, axis=-1, keepdims=True) + EPS)
        xh = xv * rstd
        err = xh * gv - t_ref[...]
        row = jnp.mean(err * err, axis=-1, keepdims=True)
        l_ref[...] += 0.5 * jnp.sum(row, axis=0, keepdims=True)
        dy = err / d
        dxh = dy * gv
        dx_ref[...] = rstd * (dxh - xh * jnp.mean(dxh * xh, axis=-1, keepdims=True))
        dg_ref[...] += jnp.sum(dy * xh, axis=0, keepdims=True)

    return pl.pallas_call(
        body, name="final_loss", grid=(s // tm,),
        in_specs=[_row(tm, d), _vec(d), _row(tm, d)],
        out_specs=[_vec(LANES), _row(tm, d), _vec(d)],
        out_shape=[jax.ShapeDtypeStruct((1, LANES), F32), jax.ShapeDtypeStruct((s, d), F32),
                   jax.ShapeDtypeStruct((1, d), F32)],
        compiler_params=_cp("arbitrary"),
    )(x, g, target)


def _resid_fwd(x, out, gate):
    s, d = x.shape
    tm = _rtile(s, 256)

    def body(x_ref, o_ref, g_ref, y_ref):
        y_ref[...] = x_ref[...] + g_ref[...] * o_ref[...]

    return pl.pallas_call(
        body, name="resid_fwd", grid=(s // tm,),
        in_specs=[_row(tm, d), _row(tm, d), _vec(d)],
        out_specs=_row(tm, d), out_shape=jax.ShapeDtypeStruct((s, d), F32),
        compiler_params=_cp("parallel"),
    )(x, out, gate)


def _resid_bwd(dxn, out, gate, deps):
    s, d = dxn.shape
    tm = _rtile(s, 256)

    def body(dx_ref, o_ref, g_ref, *rest):
        do_ref, dg_ref = rest[len(deps):]

        @pl.when(pl.program_id(0) == 0)
        def _():
            dg_ref[...] = jnp.zeros_like(dg_ref)

        dxv = dx_ref[...]
        do_ref[...] = (dxv * g_ref[...]).astype(BF16)
        dg_ref[...] += jnp.sum(dxv * o_ref[...], axis=0, keepdims=True)

    return pl.pallas_call(
        body, name="resid_bwd", grid=(s // tm,),
        in_specs=[_row(tm, d), _row(tm, d), _vec(d)] + [ANY] * len(deps),
        out_specs=[_row(tm, d), _vec(d)],
        out_shape=[jax.ShapeDtypeStruct((s, d), BF16), jax.ShapeDtypeStruct((1, d), F32)],
        compiler_params=_cp("arbitrary"),
    )(dxn, out, gate, *deps)


def _merge_fwd(y_ret, y_mla, proj, lay):
    s, d = y_ret.shape
    tm = _rtile(s, 256)
    cb = lay.off["bg"] // (2 * d)

    def body(a_ref, b_ref, bg_ref, m_ref):
        sg = _sigmoid(bg_ref[...])
        m_ref[...] = (sg[:, :d] * a_ref[...] + sg[:, d:] * b_ref[...]).astype(BF16)

    return pl.pallas_call(
        body, name="merge_fwd", grid=(s // tm,),
        in_specs=[_row(tm, d), _row(tm, d), _row(tm, 2 * d, cb)],
        out_specs=_row(tm, d), out_shape=jax.ShapeDtypeStruct((s, d), BF16),
        compiler_params=_cp("parallel"),
    )(y_ret, y_mla, proj)


def _merge_bwd(dm, y_ret, y_mla, proj, lay):
    s, d = dm.shape
    tm = _rtile(s, 256)
    cb = lay.off["bg"] // (2 * d)
    dp_spec, dp_shape = _dproj_out(None, s, lay, tm, "bg")

    def body(dm_ref, a_ref, b_ref, bg_ref, da_ref, db_ref, dp_ref):
        sg = _sigmoid(bg_ref[...])
        dmv = dm_ref[...]
        ga, gb = sg[:, :d], sg[:, d:]
        da_ref[...] = (dmv * ga).astype(BF16)
        db_ref[...] = (dmv * gb).astype(BF16)
        dp_ref[:, :d] = (dmv * a_ref[...] * ga * (1.0 - ga)).astype(BF16)
        dp_ref[:, d:] = (dmv * b_ref[...] * gb * (1.0 - gb)).astype(BF16)

    act = jax.ShapeDtypeStruct((s, d), BF16)
    return pl.pallas_call(
        body, name="merge_bwd", grid=(s // tm,),
        in_specs=[_row(tm, d), _row(tm, d), _row(tm, d), _row(tm, 2 * d, cb)],
        out_specs=[_row(tm, d), _row(tm, d), dp_spec],
        out_shape=[act, act, dp_shape],
        compiler_params=_cp("parallel"),
    )(dm, y_ret, y_mla, proj)


def _mla_prep(proj, g_cq, g_ckv, lay):
    s = proj.shape[0]
    r = lay.rank
    tm = _rtile(s, 512)
    cb = lay.off["cqkv"] // (2 * r)

    def body(p_ref, gq_ref, gk_ref, q_ref, k_ref):
        pv = p_ref[...]
        for lo, g_ref, o_ref in ((0, gq_ref, q_ref), (r, gk_ref, k_ref)):
            xv = pv[:, lo:lo + r]
            xh = xv * lax.rsqrt(jnp.mean(xv * xv, axis=-1, keepdims=True) + EPS)
            o_ref[...] = (xh * g_ref[...]).astype(BF16)

    act = jax.ShapeDtypeStruct((s, r), BF16)
    return pl.pallas_call(
        body, name="mla_prep", grid=(s // tm,),
        in_specs=[_row(tm, 2 * r, cb), _vec(r), _vec(r)],
        out_specs=[_row(tm, r), _row(tm, r)], out_shape=[act, act],
        compiler_params=_cp("parallel"),
    )(proj, g_cq, g_ckv)


def _mla_prep_bwd(proj, dqn, dkn, g_cq, g_ckv, dproj, lay):
    s = proj.shape[0]
    r = lay.rank
    tm = _rtile(s, 512)
    cb = lay.off["cqkv"] // (2 * r)
    dp_spec, dp_shape = _dproj_out(dproj, s, lay, tm, "cqkv")

    def body(p_ref, dq_ref, dk_ref, gq_ref, gk_ref, _, dp_ref, dgq_ref, dgk_ref):
        @pl.when(pl.program_id(0) == 0)
        def _():
            dgq_ref[...] = jnp.zeros_like(dgq_ref)
            dgk_ref[...] = jnp.zeros_like(dgk_ref)

        pv = p_ref[...]
        for lo, g_ref, d_ref, dg_ref in ((0, gq_ref, dq_ref, dgq_ref), (r, gk_ref, dk_ref, dgk_ref)):
            xv = pv[:, lo:lo + r]
            rstd = lax.rsqrt(jnp.mean(xv * xv, axis=-1, keepdims=True) + EPS)
            xh = xv * rstd
            dy = d_ref[...]
            dxh = dy * g_ref[...]
            dp_ref[:, lo:lo + r] = (rstd * (dxh - xh * jnp.mean(dxh * xh, axis=-1, keepdims=True))).astype(BF16)
            dg_ref[...] += jnp.sum(dy * xh, axis=0, keepdims=True)

    vec = jax.ShapeDtypeStruct((1, r), F32)
    return pl.pallas_call(
        body, name="mla_prep_bwd", grid=(s // tm,),
        in_specs=[_row(tm, 2 * r, cb), _row(tm, r), _row(tm, r), _vec(r), _vec(r), ANY],
        out_specs=[dp_spec, _vec(r), _vec(r)], out_shape=[dp_shape, vec, vec],
        input_output_aliases={5: 0},
        compiler_params=_cp("arbitrary"),
    )(proj, dqn, dkn, g_cq, g_ckv, dproj)


def _rope_tile(t, a, b, c):
    return t * a + _roll(t, 96) * b + _roll(t, 32) * c


def _rope_tile_bwd(dy, a, b, c):
    return dy * a + _roll(dy * b, 32) + _roll(dy * c, 96)


def _attn_block(s):
    return _rtile(s, 512)


def _qk_prep(qp, kvp, proj, ta, tb, tc, lay):
    s = qp.shape[0]
    hq = MLA_HEADS * MLA_QW
    hv = MLA_HEADS * MLA_DV
    blk = _attn_block(s)
    tm = _rtile(blk, 256)
    per = blk // tm
    kr_cb = lay.off["kr"] // LANES

    def body(q_ref, kv_ref, kr_ref, a_ref, b_ref, c_ref, qc_ref, kc_ref, v_ref, kt_ref, vt_ref):
        a, b, c = a_ref[...], b_ref[...], c_ref[...]
        krot = _rope_tile(kr_ref[...], a, b, c)
        krot_b, krot_t = krot.astype(BF16), krot.T.astype(BF16)
        for h in range(MLA_HEADS):
            q0 = h * MLA_QW
            qc_ref[:, q0:q0 + MLA_NOPE] = (q_ref[:, q0:q0 + MLA_NOPE] * QK_LOG2_SCALE).astype(BF16)
            qc_ref[:, q0 + MLA_NOPE:q0 + MLA_QW] = (
                _rope_tile(q_ref[:, q0 + MLA_NOPE:q0 + MLA_QW], a, b, c) * QK_LOG2_SCALE).astype(BF16)
            kn = kv_ref[:, h * MLA_NOPE:(h + 1) * MLA_NOPE]
            kc_ref[:, q0:q0 + MLA_NOPE] = kn.astype(BF16)
            kc_ref[:, q0 + MLA_NOPE:q0 + MLA_QW] = krot_b
            kt_ref[h, :MLA_NOPE, :] = kn.T.astype(BF16)
            kt_ref[h, MLA_NOPE:, :] = krot_t
            vh = kv_ref[:, (MLA_HEADS + h) * MLA_NOPE:(MLA_HEADS + h + 1) * MLA_NOPE]
            v_ref[:, h * MLA_DV:(h + 1) * MLA_DV] = vh.astype(BF16)
            vt_ref[h] = vh.T.astype(BF16)

    return pl.pallas_call(
        body, name="qk_prep", grid=(s // tm,),
        in_specs=[_row(tm, hq), _row(tm, hq), _row(tm, LANES, kr_cb), _row(tm, LANES), _row(tm, LANES), _row(tm, LANES)],
        out_specs=[_row(tm, hq), _row(tm, hq), _row(tm, hv),
                   pl.BlockSpec((MLA_HEADS, None, MLA_QW, tm), lambda i: (0, i // per, 0, i % per)),
                   pl.BlockSpec((MLA_HEADS, None, MLA_DV, tm), lambda i: (0, i // per, 0, i % per))],
        out_shape=[jax.ShapeDtypeStruct((s, hq), BF16), jax.ShapeDtypeStruct((s, hq), BF16),
                   jax.ShapeDtypeStruct((s, hv), BF16),
                   jax.ShapeDtypeStruct((MLA_HEADS, s // blk, MLA_QW, blk), BF16),
                   jax.ShapeDtypeStruct((MLA_HEADS, s // blk, MLA_DV, blk), BF16)],
        compiler_params=_cp("parallel"),
    )(qp, kvp, proj, ta, tb, tc)


def _kv_bwd_prep(dk_cat, dv, ta, tb, tc, dproj, lay):
    s = dk_cat.shape[0]
    hq = MLA_HEADS * MLA_QW
    hv = MLA_HEADS * MLA_DV
    tm = _rtile(s, 256)
    dp_spec, dp_shape = _dproj_out(dproj, s, lay, tm, "kr")

    def body(dk_ref, dv_ref, a_ref, b_ref, c_ref, _, dkv_ref, dp_ref):
        acc = jnp.zeros((tm, LANES), F32)
        for h in range(MLA_HEADS):
            q0 = h * MLA_QW
            dkv_ref[:, h * MLA_NOPE:(h + 1) * MLA_NOPE] = dk_ref[:, q0:q0 + MLA_NOPE].astype(BF16)
            acc = acc + dk_ref[:, q0 + MLA_NOPE:q0 + MLA_QW]
        dkv_ref[:, MLA_HEADS * MLA_NOPE:] = dv_ref[...].astype(BF16)
        dp_ref[...] = _rope_tile_bwd(acc, a_ref[...], b_ref[...], c_ref[...]).astype(BF16)

    return pl.pallas_call(
        body, name="kv_bwd_prep", grid=(s // tm,),
        in_specs=[_row(tm, hq), _row(tm, hv), _row(tm, LANES), _row(tm, LANES), _row(tm, LANES), ANY],
        out_specs=[_row(tm, hq), dp_spec],
        out_shape=[jax.ShapeDtypeStruct((s, hq), BF16), dp_shape],
        input_output_aliases={5: 1},
        compiler_params=_cp("parallel"),
    )(dk_cat, dv, ta, tb, tc, dproj)


def _mla_gate_bwd(du, o, proj, dproj, lay):
    s, vw = du.shape
    tm = _rtile(s, 256)
    cb = lay.off["mg"] // vw
    dp_spec, dp_shape = _dproj_out(dproj, s, lay, tm, "mg")
    assert MLA_HEADS <= LANES

    def body(du_ref, o_ref, g_ref, _, do_ref, dl_ref, dp_ref):
        gv, duv, ov = g_ref[...], du_ref[...], o_ref[...]
        sg = _sigmoid(gv)
        do = (duv * (gv * sg)).astype(BF16)
        do_ref[...] = do
        dp_ref[...] = (duv * ov * (sg + gv * sg * (1.0 - sg))).astype(BF16)
        prod = do.astype(F32) * ov
        lane = lax.broadcasted_iota(jnp.int32, (tm, LANES), 1)
        delta = jnp.zeros((tm, LANES), F32)
        for h in range(MLA_HEADS):
            dh = jnp.sum(prod[:, h * MLA_DV:(h + 1) * MLA_DV], axis=-1, keepdims=True)
            delta = jnp.where(lane == h, dh, delta)
        dl_ref[...] = delta

    return pl.pallas_call(
        body, name="mla_gate_bwd", grid=(s // tm,),
        in_specs=[_row(tm, vw), _row(tm, vw), _row(tm, vw, cb), ANY],
        out_specs=[_row(tm, vw), _row(tm, LANES), dp_spec],
        out_shape=[jax.ShapeDtypeStruct((s, vw), BF16), jax.ShapeDtypeStruct((s, LANES), F32), dp_shape],
        input_output_aliases={3: 2},
        compiler_params=_cp("parallel"),
    )(du, o, proj, dproj)


RET_BLOCK = 256


def _ret_tables(lg, blk):
    ri = lax.broadcasted_iota(jnp.int32, (blk, blk), 0)
    ci = lax.broadcasted_iota(jnp.int32, (blk, blk), 1)
    col = lax.broadcasted_iota(jnp.int32, (blk, 1), 0).astype(F32)
    dist = jnp.abs(ri - ci).astype(F32)
    dmat = jnp.where(ci // CHUNK <= ri // CHUNK, jnp.exp(dist * lg), 0.0)
    xi = jnp.exp((col + 1.0) * lg)
    zeta = jnp.exp((blk - 1.0 - col) * lg)
    decay = jnp.exp(jnp.full((1, 1), blk, F32) * lg)
    return dmat, xi, zeta, decay


def _ret_qkvg(blk, cs, sn):
    dv = RET_DV
    q = blk[:, :RET_DK]
    k = blk[:, RET_DK:2 * RET_DK]
    q = q * cs + _roll(q, RET_DK // 2) * sn
    k = (k * cs + _roll(k, RET_DK // 2) * sn) * (RET_DK ** -0.5)
    return q, k, blk[:, 2 * RET_DK:2 * RET_DK + dv], blk[:, 2 * RET_DK + dv:]


def _group_norm(o):
    mu = jnp.mean(o, axis=-1, keepdims=True)
    oc = o - mu
    rstd = lax.rsqrt(jnp.mean(oc * oc, axis=-1, keepdims=True) + EPS)
    return oc * rstd, rstd


def _ret_fwd(proj, lg, cosr, sinr, lay):
    s = proj.shape[0]
    dv, w = RET_DV, lay.ret_w
    tb = _rtile(s, 512)
    blk = min(RET_BLOCK, tb)
    nb, nch = s // tb, tb // blk
    cb0 = lay.off["ret"] // w

    def body(lg_ref, p_ref, cos_ref, sin_ref, o_ref, u_ref, st_ref, state):
        @pl.when(pl.program_id(1) == 0)
        def _():
            state[...] = jnp.zeros_like(state)

        dmat, xi, zeta, decay = _ret_tables(lg_ref[pl.program_id(0)], blk)
        for c in range(nch):
            rows = slice(c * blk, (c + 1) * blk)
            q, k, v, g = _ret_qkvg(p_ref[rows, :], cos_ref[rows, :], sin_ref[rows, :])
            qb, kb, vb = q.astype(BF16), k.astype(BF16), v.astype(BF16)
            sc = _dot_nt(qb, kb) * dmat
            st = state[...]
            o = _dot(sc.astype(BF16), vb) + _dot((q * xi).astype(BF16), st.astype(BF16))
            st_ref[c] = st.astype(BF16)
            state[...] = st * decay + _dot_tn((k * zeta).astype(BF16), vb)
            o_ref[rows, :] = o
            n, _ = _group_norm(o)
            u_ref[rows, :] = (n * (g * _sigmoid(g))).astype(BF16)

    return pl.pallas_call(
        body, name="ret_fwd", grid=(RET_HEADS, nb),
        in_specs=[pl.BlockSpec(memory_space=pltpu.SMEM),
                  pl.BlockSpec((tb, w), lambda h, b: (b, cb0 + h)),
                  pl.BlockSpec((tb, RET_DK), lambda h, b: (b, 0)),
                  pl.BlockSpec((tb, RET_DK), lambda h, b: (b, 0))],
        out_specs=[pl.BlockSpec((tb, dv), lambda h, b: (b, h)),
                   pl.BlockSpec((tb, dv), lambda h, b: (b, h)),
                   pl.BlockSpec((None, nch, RET_DK, dv), lambda h, b: (h, b, 0, 0))],
        out_shape=[jax.ShapeDtypeStruct((s, RET_HEADS * dv), F32),
                   jax.ShapeDtypeStruct((s, RET_HEADS * dv), BF16),
                   jax.ShapeDtypeStruct((RET_HEADS, s // blk, RET_DK, dv), BF16)],
        scratch_shapes=[pltpu.VMEM((RET_DK, dv), F32)],
        compiler_params=_cp("parallel", "arbitrary"),
    )(lg, proj, cosr, sinr)


def _ret_bwd(proj, lg, cosr, sinr, o, du, states, dproj, lay):
    s = proj.shape[0]
    dv, w = RET_DV, lay.ret_w
    tb = _rtile(s, 512)
    blk = min(RET_BLOCK, tb)
    nb, nch = s // tb, tb // blk
    cb0 = lay.off["ret"] // w

    def body(lg_ref, p_ref, cos_ref, sin_ref, o_ref, du_ref, st_ref, _, dp_ref, dstate):
        @pl.when(pl.program_id(1) == 0)
        def _():
            dstate[...] = jnp.zeros_like(dstate)

        dmat, xi, zeta, decay = _ret_tables(lg_ref[pl.program_id(0)], blk)
        for c in reversed(range(nch)):
            rows = slice(c * blk, (c + 1) * blk)
            cs, sn = cos_ref[rows, :], sin_ref[rows, :]
            q, k, v, g = _ret_qkvg(p_ref[rows, :], cs, sn)
            qb, kb, vb = q.astype(BF16), k.astype(BF16), v.astype(BF16)
            n, rstd = _group_norm(o_ref[rows, :])
            sg = _sigmoid(g)
            duv = du_ref[rows, :]
            dn = duv * (g * sg)
            dg = duv * n * (sg + g * sg * (1.0 - sg))
            do = rstd * (dn - jnp.mean(dn, axis=-1, keepdims=True) - n * jnp.mean(dn * n, axis=-1, keepdims=True))
            dob = do.astype(BF16)
            rb = st_ref[c]
            drb = dstate[...].astype(BF16)
            sc = (_dot_nt(qb, kb) * dmat).astype(BF16)
            dsc = (_dot_nt(dob, vb) * dmat).astype(BF16)
            qx = (q * xi).astype(BF16)
            kz = (k * zeta).astype(BF16)
            dq = _dot(dsc, kb) + _dot_nt(dob, rb) * xi
            dk = (_dot_tn(dsc, qb) + _dot_nt(vb, drb) * zeta) * (RET_DK ** -0.5)
            dvv = _dot_tn(sc, dob) + _dot(kz, drb)
            dstate[...] = dstate[...] * decay + _dot_tn(qx, dob)
            dp_ref[rows, :RET_DK] = (dq * cs + _roll(dq * sn, RET_DK // 2)).astype(BF16)
            dp_ref[rows, RET_DK:2 * RET_DK] = (dk * cs + _roll(dk * sn, RET_DK // 2)).astype(BF16)
            dp_ref[rows, 2 * RET_DK:2 * RET_DK + dv] = dvv.astype(BF16)
            dp_ref[rows, 2 * RET_DK + dv:] = dg.astype(BF16)

    rev = lambda h, b: (nb - 1 - b, h)
    return pl.pallas_call(
        body, name="ret_bwd", grid=(RET_HEADS, nb),
        in_specs=[pl.BlockSpec(memory_space=pltpu.SMEM),
                  pl.BlockSpec((tb, w), lambda h, b: (nb - 1 - b, cb0 + h)),
                  pl.BlockSpec((tb, RET_DK), lambda h, b: (nb - 1 - b, 0)),
                  pl.BlockSpec((tb, RET_DK), lambda h, b: (nb - 1 - b, 0)),
                  pl.BlockSpec((tb, dv), rev),
                  pl.BlockSpec((tb, dv), rev),
                  pl.BlockSpec((None, nch, RET_DK, dv), lambda h, b: (h, nb - 1 - b, 0, 0)),
                  ANY],
        out_specs=pl.BlockSpec((tb, w), lambda h, b: (nb - 1 - b, cb0 + h)),
        out_shape=jax.ShapeDtypeStruct((s, lay.total), BF16),
        input_output_aliases={7: 0},
        scratch_shapes=[pltpu.VMEM((RET_DK, dv), F32)],
        compiler_params=_cp("parallel", "arbitrary"),
    )(lg, proj, cosr, sinr, o, du, states, dproj)


ATTN_HEADS_PER_STEP = 2


def _attn_mask(rows, cols, row0, col0, keys_on_rows):
    ri = (lax.broadcasted_iota(jnp.int32, (rows, cols), 0) + row0) // CHUNK
    ci = (lax.broadcasted_iota(jnp.int32, (rows, cols), 1) + col0) // CHUNK
    return ri <= ci if keys_on_rows else ci <= ri


def _attn_fwd(q_cat, k_cat, v_t, proj, lay):
    s = q_cat.shape[0]
    bq = _attn_block(s)
    nq = s // bq
    mg_cb = lay.off["mg"] // MLA_DV

    hp = ATTN_HEADS_PER_STEP
    assert MLA_HEADS % hp == 0 and mg_cb % hp == 0

    def body(q_ref, k_ref, vt_ref, g_ref, o_ref, u_ref, lse_ref):
        i = pl.program_id(1)
        qs = [q_ref[:, a * MLA_QW:(a + 1) * MLA_QW] for a in range(hp)]

        def scores(a, j):
            r0 = pl.multiple_of(j * bq, bq)
            return _dot_nt(k_ref[pl.ds(r0, bq), a * MLA_QW:(a + 1) * MLA_QW], qs[a])

        def update(a, j, sc, m, l, acc):
            mn = jnp.maximum(m, jnp.max(sc, axis=0, keepdims=True))
            p = jnp.exp2(sc - mn)
            alpha = jnp.exp2(m - mn)
            l = alpha * l + jnp.sum(p, axis=0, keepdims=True)
            acc = alpha * acc + _dot(vt_ref[a, j], p.astype(BF16))
            return mn, l, acc

        def hide(sc):
            return jnp.where(_attn_mask(bq, bq, 0, 0, True), sc, NEG_INF)

        def pair(j, carry, last):
            sa = [scores(a, j) for a in range(hp)]
            sb = [scores(a, j + 1) for a in range(hp)]
            carry = [update(a, j, sa[a], *carry[a]) for a in range(hp)]
            return tuple(update(a, j + 1, hide(sb[a]) if last else sb[a], *carry[a]) for a in range(hp))

        def single(carry):
            return tuple(update(a, i, hide(scores(a, i)), *carry[a]) for a in range(hp))

        init = tuple((jnp.full((1, bq), NEG_INF, F32), jnp.zeros((1, bq), F32), jnp.zeros((MLA_DV, bq), F32))
                     for _ in range(hp))
        carry = lax.fori_loop(0, i // 2, lambda t, c: pair(2 * t, c, False), init)
        carry = lax.cond(i % 2 == 1, lambda c: pair(i - 1, c, True), single, carry)
        for a, (m, l, acc) in enumerate(carry):
            cols = slice(a * MLA_DV, (a + 1) * MLA_DV)
            o = (acc / l).T
            gv = g_ref[:, cols]
            o_ref[:, cols] = o
            u_ref[:, cols] = (o * (gv * _sigmoid(gv))).astype(BF16)
            lse_ref[a] = m + jnp.log2(l)

    return pl.pallas_call(
        body, name="attn_fwd", grid=(MLA_HEADS // hp, nq),
        in_specs=[pl.BlockSpec((bq, hp * MLA_QW), lambda h, i: (i, h)),
                  pl.BlockSpec((s, hp * MLA_QW), lambda h, i: (0, h)),
                  pl.BlockSpec((hp, nq, MLA_DV, bq), lambda h, i: (h, 0, 0, 0)),
                  pl.BlockSpec((bq, hp * MLA_DV), lambda h, i: (i, mg_cb // hp + h))],
        out_specs=[pl.BlockSpec((bq, hp * MLA_DV), lambda h, i: (i, h)),
                   pl.BlockSpec((bq, hp * MLA_DV), lambda h, i: (i, h)),
                   pl.BlockSpec((hp, None, 1, bq), lambda h, i: (h, i, 0, 0))],
        out_shape=[jax.ShapeDtypeStruct((s, MLA_HEADS * MLA_DV), F32),
                   jax.ShapeDtypeStruct((s, MLA_HEADS * MLA_DV), BF16),
                   jax.ShapeDtypeStruct((MLA_HEADS, nq, 1, bq), F32)],
        compiler_params=_cp("parallel", "parallel"),
    )(q_cat, k_cat, v_t, proj)


def _attn_bwd(q_cat, k_cat, v, k_t, do, lse, delta, ta, tb, tc):
    s = q_cat.shape[0]
    blk = _attn_block(s)
    nb = s // blk
    hp = ATTN_HEADS_PER_STEP
    qw, dvw = hp * MLA_QW, hp * MLA_DV

    def body(q_ref, k_ref, v_ref, kt_ref, do_ref, lse_ref, dl_ref, a_ref, b_ref, c_ref,
             dq_ref, dk_ref, dv_ref, dq_acc):
        j = pl.program_id(1)

        @pl.when(j == 0)
        def _():
            dq_acc[...] = jnp.zeros_like(dq_acc)

        def qcols(a):
            return slice(a * MLA_QW, (a + 1) * MLA_QW)

        def vcols(a):
            return slice(a * MLA_DV, (a + 1) * MLA_DV)

        kbs = [k_ref[:, qcols(a)] for a in range(hp)]
        vbs = [v_ref[:, vcols(a)] for a in range(hp)]

        def step(i, carry, masked):
            r0 = pl.multiple_of(i * blk, blk)
            out = []
            for a in range(hp):
                dk, dvv = carry[a]
                q, dob = q_ref[pl.ds(r0, blk), qcols(a)], do_ref[pl.ds(r0, blk), vcols(a)]
                sc = _dot_nt(kbs[a], q)
                if masked:
                    sc = jnp.where(_attn_mask(blk, blk, 0, 0, True), sc, NEG_INF)
                p = jnp.exp2(sc - lse_ref[a, i])
                dvv = dvv + _dot(p.astype(BF16), dob)
                ds = (p * (_dot_nt(vbs[a], dob) - dl_ref[a, i])).astype(BF16)
                dk = dk + _dot(ds, q)
                dq_acc[a, i] = dq_acc[a, i] + _dot(kt_ref[a], ds)
                out.append((dk, dvv))
            return tuple(out)

        init = tuple((jnp.zeros((blk, MLA_QW), F32), jnp.zeros((blk, MLA_DV), F32)) for _ in range(hp))
        carry = step(j, init, True)
        carry = lax.fori_loop(j + 1, nb, lambda i, c: step(i, c, False), carry)
        for a, (dk, dvv) in enumerate(carry):
            dk_ref[:, qcols(a)] = dk * LN2
            dv_ref[:, vcols(a)] = dvv

        @pl.when(j == nb - 1)
        def _():
            for a in range(hp):
                for i in range(nb):
                    rows = slice(i * blk, (i + 1) * blk)
                    dq = dq_acc[a, i].T * QK_SCALE
                    c0 = a * MLA_QW
                    dq_ref[rows, c0:c0 + MLA_NOPE] = dq[:, :MLA_NOPE].astype(BF16)
                    dq_ref[rows, c0 + MLA_NOPE:c0 + MLA_QW] = _rope_tile_bwd(
                        dq[:, MLA_NOPE:], a_ref[rows, :], b_ref[rows, :], c_ref[rows, :]).astype(BF16)

    tab = pl.BlockSpec((s, LANES), lambda h, j: (0, 0), pipeline_mode=pl.Buffered(1))
    rows = pl.BlockSpec((hp, nb, 1, blk), lambda h, j: (h, 0, 0, 0))
    return pl.pallas_call(
        body, name="attn_bwd", grid=(MLA_HEADS // hp, nb),
        in_specs=[pl.BlockSpec((s, qw), lambda h, j: (0, h)),
                  pl.BlockSpec((blk, qw), lambda h, j: (j, h)),
                  pl.BlockSpec((blk, dvw), lambda h, j: (j, h)),
                  pl.BlockSpec((hp, None, MLA_QW, blk), lambda h, j: (h, j, 0, 0)),
                  pl.BlockSpec((s, dvw), lambda h, j: (0, h)),
                  rows, rows, tab, tab, tab],
        out_specs=[pl.BlockSpec((s, qw), lambda h, j: (0, h)),
                   pl.BlockSpec((blk, qw), lambda h, j: (j, h)),
                   pl.BlockSpec((blk, dvw), lambda h, j: (j, h))],
        out_shape=[jax.ShapeDtypeStruct((s, MLA_HEADS * MLA_QW), BF16),
                   jax.ShapeDtypeStruct((s, MLA_HEADS * MLA_QW), F32),
                   jax.ShapeDtypeStruct((s, MLA_HEADS * MLA_DV), F32)],
        scratch_shapes=[pltpu.VMEM((hp, nb, MLA_QW, blk), F32)],
        compiler_params=_cp("parallel", "arbitrary"),
    )(q_cat, k_cat, v, k_t, do, lse, delta, ta, tb, tc)


def _place():
    return lax.axis_index("x"), lax.axis_index("y"), lax.axis_index("c")


def _slot(px, py, pc):
    return 4 * px + 2 * py + pc


def _all_gather(shards, layer, name, vmem=False, deps=()):
    n = len(shards)

    def body(*refs):
        srcs, outs = refs[:n], refs[n + len(deps):2 * n + len(deps)]
        send_sems, recv_sems, local_sems = refs[2 * n + len(deps):]
        x, y, c = _place()
        me, sibling = (x, y, c), (x, y, 1 - c)
        chips = [(1 - x, y), (x, 1 - y), (1 - x, 1 - y)]
        firsts, passes, locals_ = [], [], []

        def copy(a, k, block, to, src=None):
            dst = outs[a].at[_slot(*block)]
            return pltpu.make_async_remote_copy(
                src_ref=dst if src is None else src, dst_ref=dst,
                send_sem=send_sems.at[7 * a + k], recv_sem=recv_sems.at[7 * a + k],
                device_id=to, device_id_type=MESH)

        for a in range(n):
            src = srcs[a] if layer is None else srcs[a].at[layer]
            mine = pltpu.make_async_copy(src, outs[a].at[_slot(*me)], local_sems.at[a])
            mine.start()
            locals_.append(mine)
            first = [copy(a, 0, me, sibling, src=src)]
            first += [copy(a, 1 + j, me, (*chip, c), src=src) for j, chip in enumerate(chips)]
            for cp in first:
                cp.start()
            firsts += first
        for a in range(n):
            for j, chip in enumerate(chips):
                copy(a, 1 + j, (*chip, c), me).wait_recv()
                fwd = copy(a, 4 + j, (*chip, c), sibling)
                fwd.start()
                passes.append(fwd)
        for a in range(n):
            copy(a, 0, sibling, me).wait_recv()
            for j, chip in enumerate(chips):
                copy(a, 4 + j, (*chip, 1 - c), me).wait_recv()
        for cp in firsts + passes:
            cp.wait_send()
        for mine in locals_:
            mine.wait()

    space = pl.BlockSpec(memory_space=pltpu.VMEM) if vmem else ANY
    out_shape = [jax.ShapeDtypeStruct((N_DEV,) + (a.shape if layer is None else a.shape[1:]), a.dtype) for a in shards]
    return pl.pallas_call(
        body, name=name,
        in_specs=[space] * n + [ANY] * len(deps), out_specs=[space] * n, out_shape=out_shape,
        scratch_shapes=[pltpu.SemaphoreType.DMA((7 * n,)), pltpu.SemaphoreType.DMA((7 * n,)),
                        pltpu.SemaphoreType.DMA((n,))],
        compiler_params=pltpu.CompilerParams(has_side_effects=True),
    )(*shards, *deps)


HBM = pl.BlockSpec(memory_space=pltpu.HBM)
SEM = pl.BlockSpec(memory_space=pltpu.SEMAPHORE)
EFFECT = pltpu.SideEffectType.DATAFLOW_SIDE_EFFECTING


def _push_copies(srcs, lands, send_sems, recv_sems, local_sems, by_peer):
    x, y, c = _place()
    me = _slot(x, y, c)
    local, remote = [], []
    for a, (src, land) in enumerate(zip(srcs, lands)):
        local.append(pltpu.make_async_copy(src.at[me] if by_peer else src, land.at[me], local_sems.at[a]))
        for r in range(1, N_DEV):
            peer = (1 - x if r & 4 else x, 1 - y if r & 2 else y, 1 - c if r & 1 else c)
            remote.append(pltpu.make_async_remote_copy(
                src_ref=src.at[_slot(*peer)] if by_peer else src, dst_ref=land.at[me],
                send_sem=send_sems.at[7 * a + r - 1], recv_sem=recv_sems.at[7 * a + r - 1],
                device_id=peer, device_id_type=MESH))
    return local, remote


def _push_start(srcs, by_peer, after, name):
    n = len(srcs)
    srcs = [pltpu.with_memory_space_constraint(a, pltpu.HBM) for a in srcs]
    lands = [pltpu.with_memory_space_constraint(
        lax.empty((N_DEV,) + (a.shape[1:] if by_peer else a.shape), a.dtype), pltpu.HBM) for a in srcs]

    def body(*refs):
        k = 2 * n + len(after)
        local, remote = _push_copies(refs[:n], refs[n:2 * n], refs[k], refs[k + 1], refs[k + 2], by_peer)
        for cp in local + remote:
            cp.start()
        token = refs[k + 3 + 2 * n]
        token[...] = jnp.zeros_like(token)

    outs = pl.pallas_call(
        body, name=name,
        out_shape=(pltpu.SemaphoreType.DMA((7 * n,)), pltpu.SemaphoreType.DMA((7 * n,)),
                   pltpu.SemaphoreType.DMA((n,)),
                   *[pltpu.HBM(a.shape, a.dtype) for a in srcs], *[pltpu.HBM(a.shape, a.dtype) for a in lands],
                   jax.ShapeDtypeStruct((8, LANES), F32)),
        in_specs=[HBM] * (2 * n) + [ANY] * len(after),
        out_specs=(SEM, SEM, SEM, *([HBM] * (2 * n)), pl.BlockSpec(memory_space=pltpu.VMEM)),
        input_output_aliases={i: 3 + i for i in range(2 * n)},
        compiler_params=pltpu.CompilerParams(has_side_effects=EFFECT),
    )(*srcs, *lands, *after)
    return outs[:3], outs[3:3 + n], outs[3 + n:3 + 2 * n], outs[3 + 2 * n]


def _push_wait(sems, srcs, lands, by_peer, after, name):
    n = len(srcs)

    def body(*refs):
        local, remote = _push_copies(refs[:n], refs[n:2 * n], refs[2 * n], refs[2 * n + 1], refs[2 * n + 2], by_peer)
        for cp in local:
            cp.wait()
        for cp in remote:
            cp.wait_send()
            cp.wait_recv()

    outs = pl.pallas_call(
        body, name=name,
        out_shape=[pltpu.HBM(a.shape, a.dtype) for a in list(srcs) + list(lands)],
        in_specs=[HBM] * (2 * n) + [SEM] * 3 + [ANY] * len(after),
        out_specs=[HBM] * (2 * n),
        input_output_aliases={i: i for i in range(2 * n)},
        compiler_params=pltpu.CompilerParams(has_side_effects=EFFECT),
    )(*srcs, *lands, *sems, *after)
    return outs[n:]


def _adam_math(g, w, m, v):
    m = ADAM_B1 * m + (1.0 - ADAM_B1) * g
    v = ADAM_B2 * v + (1.0 - ADAM_B2) * (g * g)
    m_hat = m / (1.0 - ADAM_B1 ** ADAM_STEP)
    v_hat = v / (1.0 - ADAM_B2 ** ADAM_STEP)
    delta = -ADAM_LR * (m_hat / (jnp.sqrt(v_hat) + ADAM_EPS) + ADAM_WD * w)
    return delta, m, v


def _adam_sharded(recvs, first, w, m, v, prev, name):
    nl, r, c = w.shape
    n = len(recvs)
    by_cols = r % 128 != 0
    tr, tc = (r, _tile(c, LANES)) if by_cols else (_rtile(r, 128), c)
    nr = c // tc if by_cols else r // tr
    prev = list(prev) if prev is not None else []

    def tile(t):
        return (0, t) if by_cols else (t, 0)

    def body(*refs):
        g_refs = refs[:n]
        w_ref, m_ref, v_ref = refs[n:n + 3]
        go_ref, d_ref, mo_ref, vo_ref = refs[n + 3 + len(prev):]
        layer = pl.program_id(0)
        for l in range(n):
            @pl.when(layer == l)
            def _(l=l):
                g = g_refs[l][0].astype(F32)
                for i in range(1, N_DEV):
                    g = g + g_refs[l][i].astype(F32)
                delta, mn, vn = _adam_math(g, w_ref[...], m_ref[...], v_ref[...])
                go_ref[...] = g
                d_ref[...] = delta
                mo_ref[...] = mn
                vo_ref[...] = vn

    def recv_spec(l):
        def index(layer, i):
            return (0,) + tile(jnp.where(layer == l, i, jnp.where(layer < l, 0, nr - 1)))
        return pl.BlockSpec((N_DEV, tr, tc), index)

    blk = pl.BlockSpec((None, tr, tc), lambda layer, i: (first + layer,) + tile(i))
    out = jax.ShapeDtypeStruct(w.shape, F32)
    return pl.pallas_call(
        body, name=name, grid=(n, nr),
        in_specs=[recv_spec(l) for l in range(n)] + [blk, blk, blk] + [ANY] * len(prev),
        out_specs=[blk] * 4, out_shape=[out] * 4,
        input_output_aliases={n + 3 + k: k for k in range(len(prev))},
        compiler_params=_cp("arbitrary", "arbitrary"),
    )(*recvs, w, m, v, *prev)


def _adam_mod(c_all_t, dmod, w, m, v):
    nl, d, c = w.shape
    tr = _rtile(d, 128)

    def body(ct_ref, dm_ref, w_ref, m_ref, v_ref, go_ref, d_ref, mo_ref, vo_ref):
        ct = ct_ref[...].astype(BF16).astype(F32)
        dm = dm_ref[...].astype(BF16).astype(F32)
        g = ct[:, 0:1] * dm[0:1, :]
        for b in range(1, N_DEV):
            g = g + ct[:, b:b + 1] * dm[b:b + 1, :]
        delta, mn, vn = _adam_math(g, w_ref[...], m_ref[...], v_ref[...])
        go_ref[...] = g
        d_ref[...] = delta
        mo_ref[...] = mn
        vo_ref[...] = vn

    blk = pl.BlockSpec((None, tr, c), lambda layer, i: (layer, i, 0))
    out = jax.ShapeDtypeStruct(w.shape, F32)
    return pl.pallas_call(
        body, name="adam_mod", grid=(nl, d // tr),
        in_specs=[pl.BlockSpec((tr, N_DEV), lambda layer, i: (i, 0)),
                  pl.BlockSpec((None, N_DEV, c), lambda layer, i: (layer, 0, 0)), blk, blk, blk],
        out_specs=[blk] * 4, out_shape=[out] * 4,
        compiler_params=_cp("parallel", "parallel"),
    )(c_all_t, dmod, w, m, v)


def _adam_small(g, w, m, v, name):
    def body(g_ref, w_ref, m_ref, v_ref, d_ref, mo_ref, vo_ref):
        delta, mn, vn = _adam_math(g_ref[...], w_ref[...], m_ref[...], v_ref[...])
        d_ref[...] = delta
        mo_ref[...] = mn
        vo_ref[...] = vn

    out = jax.ShapeDtypeStruct(w.shape, F32)
    return pl.pallas_call(body, name=name, out_shape=[out] * 3)(g, w, m, v)


def _sum_devices(parts):
    def body(p_ref, o_ref):
        acc = p_ref[0]
        for i in range(1, N_DEV):
            acc = acc + p_ref[i]
        o_ref[...] = acc

    return pl.pallas_call(body, name="sum_devices",
                          out_shape=jax.ShapeDtypeStruct(parts.shape[1:], F32))(parts)


def _rope_tables(positions):
    pos = positions.astype(F32)[:, None]

    def cs(dim):
        inv = 1.0 / (ROPE_BASE ** (jnp.arange(0, dim, 2, dtype=F32) / dim))
        ang = pos * inv
        return jnp.cos(ang), jnp.sin(ang)

    cr, sr = cs(RET_DK)
    cm, sm = cs(MLA_ROPE)
    z = jnp.zeros_like(cm)
    pad = jnp.zeros((pos.shape[0], LANES - MLA_ROPE), F32)
    cosr = jnp.concatenate([cr, cr], axis=1)
    sinr = jnp.concatenate([-sr, sr], axis=1)
    ta = jnp.concatenate([cm, cm, pad], axis=1)
    tb = jnp.concatenate([-sm, z, pad], axis=1)
    tc = jnp.concatenate([z, sm, pad], axis=1)
    return cosr, sinr, ta, tb, tc


def _layer_fwd(x, mod, g_norm, g_cq, g_ckv, wts, tabs, lg, lay, deps):
    d = x.shape[1]
    cosr, sinr, ta, tb, tc = tabs
    shift, scale, gate = mod[:, :d], mod[:, d:2 * d], mod[:, 2 * d:]
    h = _norm_mod_fwd(x, g_norm, scale, shift, deps)
    proj = _matmul(h, wts["in"], name="mm_proj", tn_cap=1920)
    o_ret, u_ret, states = _ret_fwd(proj, lg, cosr, sinr, lay)
    y_ret = _matmul(u_ret, wts["ret"], name="mm_y")
    cqn, ckvn = _mla_prep(proj, g_cq, g_ckv, lay)
    qp = _matmul(cqn, wts["uq"], name="mm_up")
    kvp = _matmul(ckvn, wts["ukv"], name="mm_up")
    q_cat, k_cat, v, k_t, v_t = _qk_prep(qp, kvp, proj, ta, tb, tc, lay)
    o_mla, u_mla, lse = _attn_fwd(q_cat, k_cat, v_t, proj, lay)
    y_mla = _matmul(u_mla, wts["mla"], name="mm_y")
    merged = _merge_fwd(y_ret, y_mla, proj, lay)
    out = _matmul(merged, wts["out"], name="mm_y")
    x_next = _resid_fwd(x, out, gate)
    saved = dict(x=x, h=h, proj=proj, o_ret=o_ret, u_ret=u_ret, states=states, y_ret=y_ret, cqn=cqn,
                 ckvn=ckvn, q_cat=q_cat, k_cat=k_cat, v=v, k_t=k_t, o_mla=o_mla, u_mla=u_mla, lse=lse,
                 y_mla=y_mla, merged=merged, out=out)
    return x_next, saved


def _to_owner_blocks_cols(g, n_local):
    k = g.shape[0]
    return g.reshape(k, N_DEV, n_local).transpose(1, 0, 2)


def _from_owner_blocks_cols(g):
    return g.transpose(1, 0, 2).reshape(g.shape[1], -1)


def _layer_bwd(dxn, sv, mod, g_norm, g_cq, g_ckv, wts, tabs, lg, lay, deps, shard_cols, push):
    d = dxn.shape[1]
    n_in, n_uq, n_ukv = shard_cols
    cosr, sinr, ta, tb, tc = tabs
    scale, gate = mod[:, d:2 * d], mod[:, 2 * d:]
    gdt = BF16
    dout, dgate = _resid_bwd(dxn, sv["out"], gate, deps)
    dmerged = _matmul(dout, wts["out"], tb=True, name="mm_dy")
    dw_out = _matmul(sv["merged"], dout, ta=True, out_dtype=gdt, name="mm_dw")
    dy_ret, dy_mla, dproj = _merge_bwd(dmerged, sv["y_ret"], sv["y_mla"], sv["proj"], lay)
    du_ret = _matmul(dy_ret, wts["ret"], tb=True, name="mm_dy")
    dw_ret = _matmul(sv["u_ret"], dy_ret, ta=True, out_dtype=gdt, name="mm_dw")
    dproj = _ret_bwd(sv["proj"], lg, cosr, sinr, sv["o_ret"], du_ret, sv["states"], dproj, lay)
    du_mla = _matmul(dy_mla, wts["mla"], tb=True, name="mm_dy")
    dw_mla = _matmul(sv["u_mla"], dy_mla, ta=True, out_dtype=gdt, name="mm_dw")
    do_mla, delta, dproj = _mla_gate_bwd(du_mla, sv["o_mla"], sv["proj"], dproj, lay)
    delta = delta[:, :MLA_HEADS].T.reshape(sv["lse"].shape)
    dqp, dk_cat, dv = _attn_bwd(sv["q_cat"], sv["k_cat"], sv["v"], sv["k_t"], do_mla, sv["lse"], delta,
                                ta, tb, tc)
    dkvp, dproj = _kv_bwd_prep(dk_cat, dv, ta, tb, tc, dproj, lay)
    dcqn = _matmul(dqp, wts["uq"], tb=True, name="mm_dlat")
    dckvn = _matmul(dkvp, wts["ukv"], tb=True, name="mm_dlat")
    dw_uq = _matmul(sv["cqn"], dqp, ta=True, out_dtype=gdt, name="mm_dwup")
    dw_ukv = _matmul(sv["ckvn"], dkvp, ta=True, out_dtype=gdt, name="mm_dwup")
    sent = push("a", [_to_owner_blocks_cols(_uq_to_logical(dw_uq), n_uq),
                      _to_owner_blocks_cols(_ukv_to_logical(dw_ukv), n_ukv),
                      dw_ret.reshape(N_DEV, -1, d), dw_mla.reshape(N_DEV, -1, d), dw_out.reshape(N_DEV, -1, d)])
    dproj, dg_cq, dg_ckv = _mla_prep_bwd(sv["proj"], dcqn, dckvn, g_cq, g_ckv, dproj, lay)
    dw_in = _matmul(sv["h"], dproj, ta=True, out_dtype=gdt, name="mm_dwin", tn_cap=1920, deps=sent)
    sent = push("b", [_scatter_dw_in(dw_in, lay, n_in)])
    dh = _matmul(dproj, wts["in"], tb=True, name="mm_dh", deps=sent)
    dx, dshift, dscale, dg_norm = _norm_mod_bwd(sv["x"], g_norm, scale, dh, dxn)
    dmod = jnp.concatenate([dshift, dscale, dgate], axis=1)
    small = dict(dmod=dmod, g_norm=dg_norm, g_cq=dg_cq, g_ckv=dg_ckv)
    return dx, small


def kernel(x, c, positions, w_mod, b_mod, g_norm, w_in, g_cq, g_ckv, w_uq, w_ukv, w_ret_proj, w_mla_proj, w_out, g_final, loss_target, m_w_mod, m_b_mod, m_g_norm, m_w_in, m_g_cq, m_g_ckv, m_w_uq, m_w_ukv, m_w_ret_proj, m_w_mla_proj, m_w_out, m_g_final, v_w_mod, v_b_mod, v_g_norm, v_w_in, v_g_cq, v_g_ckv, v_w_uq, v_w_ukv, v_w_ret_proj, v_w_mla_proj, v_w_out, v_g_final):
    nl, d, _ = w_mod.shape
    s = x.shape[1]
    rank = g_cq.shape[1]
    lay = Layout(d, rank, g_ckv.shape[1])
    me = _slot(*_place())
    x0 = x.reshape(s, d)
    target = loss_target.reshape(s, d)
    tabs = _rope_tables(positions.reshape(s))
    lg = jnp.log(1.0 - 2.0 ** (-5.0 - jnp.arange(RET_HEADS, dtype=F32)))

    c_act = c * _sigmoid(c)
    (c_all,) = _all_gather([c_act.reshape(d // LANES, LANES)], None, "gather_c", vmem=True)
    c_all = c_all.reshape(N_DEV, d)
    n_mod = w_mod.shape[2]
    mod_part = jnp.stack([_matmul(c_all, w_mod[l], name="mm_mod", tm_cap=8) for l in range(nl)])
    (mod_all,) = _all_gather([mod_part.reshape(-1, LANES)], None, "gather_mod", vmem=True)
    mod_all = mod_all.reshape(N_DEV, nl, N_DEV, n_mod)
    mod = lax.dynamic_index_in_dim(mod_all, me, axis=2, keepdims=False)
    mod = mod.transpose(1, 0, 2).reshape(nl, N_DEV * n_mod) + b_mod

    shards = [[w[l].astype(BF16) for w in (w_in, w_uq, w_ukv, w_ret_proj, w_mla_proj, w_out)] for l in range(nl)]
    xl, saved, wts_all = x0, [], []
    gathered = _all_gather(shards[0], None, "gather_w")
    for l in range(nl):
        g_in, g_uq, g_ukv, g_ret, g_mla, g_out = gathered
        deps = []
        if l + 1 < nl:
            sems, srcs, lands, token = _push_start(shards[l + 1], False, [g_out, mod], "gather_start_%d" % (l + 1))
            deps = [token]
        wts = {
            "in": _assemble_w_in(g_in, lay),
            "uq": _uq_to_physical(_from_owner_blocks_cols(g_uq)),
            "ukv": _ukv_to_physical(_from_owner_blocks_cols(g_ukv)),
            "ret": g_ret.reshape(-1, d), "mla": g_mla.reshape(-1, d), "out": g_out.reshape(-1, d),
        }
        wts_all.append(wts)
        xl, sv = _layer_fwd(xl, mod[l:l + 1], g_norm[l:l + 1], g_cq[l:l + 1], g_ckv[l:l + 1], wts, tabs, lg, lay,
                            deps)
        saved.append(sv)
        if l + 1 < nl:
            gathered = _push_wait(sems, srcs, lands, False, [xl], "gather_wait_%d" % (l + 1))
    loss_lanes, dx, dg_final = _final_loss(xl, g_final.reshape(1, d), target)

    small = [None] * nl
    flying = {l: [] for l in range(nl)}
    recv = {}
    shard_cols = (w_in.shape[2], w_uq.shape[2], w_ukv.shape[2])

    def pusher(l):
        def push(group, arrays):
            sems, srcs, lands, token = _push_start(arrays, True, [], "exchange_start_%d%s" % (l, group))
            flying[l].append((group, sems, srcs, lands))
            return [token]
        return push

    def land(l, after):
        got = {}
        for group, sems, srcs, lands in flying[l]:
            got[group] = _push_wait(sems, srcs, lands, True, after, "exchange_wait_%d%s" % (l, group))
        recv[l] = list(got["b"]) + list(got["a"])

    for l in reversed(range(nl)):
        dx, small[l] = _layer_bwd(dx, saved[l], mod[l:l + 1], g_norm[l:l + 1], g_cq[l:l + 1], g_ckv[l:l + 1],
                                  wts_all[l], tabs, lg, lay, [], shard_cols, pusher(l))
        if l + 1 < nl:
            land(l + 1, [dx])
    grad_x = dx.reshape(x.shape)

    out = {}
    w_in_t, m_w_in_t, v_w_in_t = (jnp.swapaxes(a, 1, 2) for a in (w_in, m_w_in, v_w_in))
    sharded = (("w_in", w_in_t, m_w_in_t, v_w_in_t), ("w_uq", w_uq, m_w_uq, v_w_uq), ("w_ukv", w_ukv, m_w_ukv, v_w_ukv),
               ("w_ret_proj", w_ret_proj, m_w_ret_proj, v_w_ret_proj),
               ("w_mla_proj", w_mla_proj, m_w_mla_proj, v_w_mla_proj), ("w_out", w_out, m_w_out, v_w_out))
    if nl > 1:
        for i, (key, w, m, v) in enumerate(sharded):
            out[key] = _adam_sharded([recv[l][i] for l in range(1, nl)], 1, w, m, v, None, "adam_" + key)
    done = [out[key][0] for key, _, _, _ in sharded if key in out]

    pack = jnp.concatenate(
        [jnp.concatenate([sm[k] for sm in small], axis=0).reshape(-1)
         for k in ("dmod", "g_norm", "g_cq", "g_ckv")] + [dg_final.reshape(-1), loss_lanes.reshape(-1)])
    (pack_all,) = _all_gather([pack.reshape(-1, LANES)], None, "gather_small", vmem=True, deps=done)
    tot = _sum_devices(pack_all).reshape(-1)
    sizes = [nl * 3 * d, nl * d, nl * rank, nl * rank, d]
    offs = np.cumsum([0] + sizes)
    grad_b_mod = tot[offs[0]:offs[1]].reshape(nl, 3 * d)
    grad_g_norm = tot[offs[1]:offs[2]].reshape(nl, d)
    grad_g_cq = tot[offs[2]:offs[3]].reshape(nl, rank)
    grad_g_ckv = tot[offs[3]:offs[4]].reshape(nl, rank)
    grad_g_final = tot[offs[4]:offs[5]]
    loss = tot[offs[5]]
    dmod_all = pack_all.reshape(N_DEV, -1)[:, :sizes[0]].reshape(N_DEV, nl, 3 * d)
    dmod_mine = lax.dynamic_slice_in_dim(dmod_all, me * n_mod, n_mod, axis=2).transpose(1, 0, 2)

    out["w_mod"] = _adam_mod(c_all.T, dmod_mine, w_mod, m_w_mod, v_w_mod)
    land(0, [out["w_mod"][0]])
    for i, (key, w, m, v) in enumerate(sharded):
        out[key] = _adam_sharded([recv[0][i]], 0, w, m, v, out.get(key), "adam0_" + key)
    out["w_in"] = tuple(jnp.swapaxes(a, 1, 2) for a in out["w_in"])
    for key, g, w, m, v in (("b_mod", grad_b_mod, b_mod, m_b_mod, v_b_mod),
                            ("g_norm", grad_g_norm, g_norm, m_g_norm, v_g_norm),
                            ("g_cq", grad_g_cq, g_cq, m_g_cq, v_g_cq),
                            ("g_ckv", grad_g_ckv, g_ckv, m_g_ckv, v_g_ckv),
                            ("g_final", grad_g_final.reshape(1, d), g_final.reshape(1, d),
                             m_g_final.reshape(1, d), v_g_final.reshape(1, d))):
        out[key] = (g,) + tuple(_adam_small(g, w, m, v, "adam_" + key))
    out["g_final"] = tuple(a.reshape(d) for a in out["g_final"])

    names = ("w_mod", "b_mod", "g_norm", "w_in", "g_cq", "g_ckv", "w_uq", "w_ukv", "w_ret_proj",
             "w_mla_proj", "w_out", "g_final")
    return (loss, grad_x, *[out[k][0] for k in names], *[out[k][1] for k in names],
            *[out[k][2] for k in names], *[out[k][3] for k in names])
```

```python
import functools
import itertools

import jax
import jax.numpy as jnp
import numpy as np
from jax import lax
from jax.experimental import pallas as pl
from jax.experimental.pallas import tpu as pltpu

F32 = jnp.float32
BF16 = jnp.bfloat16

N_DEV = 8
CHUNK = 64
EPS = 1e-6
NEG_INF = -1e30
ROPE_BASE = 10000.0
LANES = 128

RET_HEADS = 8
RET_DK = 128
RET_DV = 256
MLA_HEADS = 16
MLA_NOPE = 128
MLA_ROPE = 64
MLA_DV = 128
MLA_QW = 256
QK_SCALE = (MLA_NOPE + MLA_ROPE) ** -0.5
QK_LOG2_SCALE = QK_SCALE * 1.4426950408889634
LN2 = 0.6931471805599453

ADAM_LR = 0.001
ADAM_B1 = 0.9
ADAM_B2 = 0.999
ADAM_EPS = 1e-08
ADAM_WD = 0.01
ADAM_STEP = 10

VMEM_LIMIT_BYTES = 56 * 1024 * 1024
MESH = pl.DeviceIdType.MESH
ANY = pl.BlockSpec(memory_space=pl.ANY)


def _cp(*sem):
    return pltpu.CompilerParams(dimension_semantics=sem if sem else None,
                                vmem_limit_bytes=VMEM_LIMIT_BYTES)


def _tile(n, cap):
    best = None
    t = LANES
    while t <= min(n, cap):
        if n % t == 0:
            best = t
        t += LANES
    return best if best is not None else n


def _rtile(n, cap):
    t = cap
    while t > 8 and n % t:
        t //= 2
    return t if n % t == 0 else n


def _sigmoid(x):
    return 1.0 / (1.0 + jnp.exp(-x))


def _dot(a, b):
    return lax.dot_general(a, b, (((1,), (0,)), ((), ())), preferred_element_type=F32)


def _dot_nt(a, b):
    return lax.dot_general(a, b, (((1,), (1,)), ((), ())), preferred_element_type=F32)


def _dot_tn(a, b):
    return lax.dot_general(a, b, (((0,), (0,)), ((), ())), preferred_element_type=F32)


def _roll(x, s):
    return pltpu.roll(x, s, 1)


class Layout:
    def __init__(self, d_model, q_rank, kv_rank):
        assert q_rank == kv_rank
        self.d = d_model
        self.rank = q_rank
        self.ret_w = 2 * RET_DK + 2 * RET_DV
        self.ret_qk = RET_HEADS * RET_DK
        self.ret_v = RET_HEADS * RET_DV
        self.mla_v = MLA_HEADS * MLA_DV
        widths = {"bg": 2 * d_model, "mg": self.mla_v, "ret": RET_HEADS * self.ret_w,
                  "cqkv": 2 * q_rank, "kr": LANES}
        blocks = {"bg": 2 * d_model, "mg": self.mla_v, "ret": self.ret_w,
                  "cqkv": 2 * q_rank, "kr": LANES}
        for order in itertools.permutations(widths):
            off, offs, ok = 0, {}, True
            for name in order:
                if off % blocks[name]:
                    ok = False
                    break
                offs[name] = off
                off += widths[name]
            if ok:
                break
        assert ok, "no aligned layout"
        self.order, self.off, self.width, self.total = order, offs, widths, off
        lo, o = {}, 0
        for name, w in (("rq", self.ret_qk), ("rk", self.ret_qk), ("rv", self.ret_v),
                        ("rg", self.ret_v), ("cq", q_rank), ("ckv", kv_rank), ("kr", MLA_ROPE),
                        ("mg", self.mla_v), ("bg", 2 * d_model)):
            lo[name] = (o, w)
            o += w
        self.logical, self.d_in = lo, o

    def pieces(self):
        lo = self.logical
        out = []
        for name in self.order:
            if name == "bg":
                out.append(lo["bg"])
            elif name == "mg":
                out.append(lo["mg"])
            elif name == "ret":
                for h in range(RET_HEADS):
                    out.append((lo["rq"][0] + h * RET_DK, RET_DK))
                    out.append((lo["rk"][0] + h * RET_DK, RET_DK))
                    out.append((lo["rv"][0] + h * RET_DV, RET_DV))
                    out.append((lo["rg"][0] + h * RET_DV, RET_DV))
            elif name == "cqkv":
                out.append((lo["cq"][0], 2 * self.rank))
            elif name == "kr":
                out.append(lo["kr"])
                out.append((None, LANES - MLA_ROPE))
        return out


RELAYOUT_CHUNK = 512


def _relayout_plan(lay, n_local):
    plan, off = [], 0
    for start, width in lay.pieces():
        done = 0
        while done < width:
            w = min(RELAYOUT_CHUNK, width - done)
            srcs = []
            if start is not None:
                lo, hi = start + done, start + done + w
                while lo < hi:
                    j = lo // n_local
                    end = min(hi, (j + 1) * n_local)
                    srcs.append((j, lo - j * n_local, end - j * n_local))
                    lo = end
            plan.append((off + done, w, srcs))
            done += w
        off += width
    merged = []
    for p, w, srcs in plan:
        if merged and merged[-1][0] % LANES == 0 and (merged[-1][1] % LANES) and p == merged[-1][0] + merged[-1][1]:
            q, qw, qs = merged.pop()
            merged.append((q, qw + w, qs + ([("pad", w)] if not srcs else srcs)))
        else:
            merged.append((p, w, srcs))
    return merged


def _assemble_w_in(g, lay):
    _, d, n_local = g.shape
    tr = _rtile(d, 256)
    plan = _relayout_plan(lay, n_local)

    def body(g_ref, o_ref):
        for p, w, srcs in plan:
            parts = []
            for src in srcs:
                if src[0] == "pad":
                    parts.append(jnp.zeros((tr, src[1]), F32))
                else:
                    j, a, b = src
                    parts.append(g_ref[j, :, a:b].astype(F32))
            if not parts:
                parts = [jnp.zeros((tr, w), F32)]
            val = parts[0] if len(parts) == 1 else jnp.concatenate(parts, axis=1)
            o_ref[:, p:p + w] = val.astype(o_ref.dtype)

    return pl.pallas_call(
        body, name="assemble_w_in", grid=(d // tr,),
        in_specs=[pl.BlockSpec((N_DEV, tr, n_local), lambda i: (0, i, 0))],
        out_specs=pl.BlockSpec((tr, lay.total), lambda i: (i, 0)),
        out_shape=jax.ShapeDtypeStruct((d, lay.total), g.dtype),
        compiler_params=_cp("parallel"),
    )(g)


def _scatter_dw_in(dw, lay, n_local):
    d = dw.shape[0]
    tr = _rtile(d, 256)
    n_pad = -(-n_local // LANES) * LANES
    plan = _relayout_plan(lay, n_local)
    runs = [[] for _ in range(N_DEV)]
    for p, w, srcs in plan:
        at = p
        for src in srcs:
            if src[0] == "pad":
                at += src[1]
                continue
            j, a, b = src
            runs[j].append((a, b, at))
            at += b - a
    for r in runs:
        r.sort()

    def body(dw_ref, o_ref):
        for j in range(N_DEV):
            parts = [dw_ref[:, at:at + (b - a)].astype(F32) for a, b, at in runs[j]]
            if n_pad > n_local:
                parts.append(jnp.zeros((tr, n_pad - n_local), F32))
            val = jnp.concatenate(parts, axis=1).T
            o_ref[j] = val[:n_local, :].astype(BF16)

    return pl.pallas_call(
        body, name="scatter_dw_in", grid=(d // tr,),
        in_specs=[pl.BlockSpec((tr, lay.total), lambda i: (i, 0))],
        out_specs=pl.BlockSpec((N_DEV, n_local, tr), lambda i: (0, 0, i)),
        out_shape=jax.ShapeDtypeStruct((N_DEV, n_local, d), BF16),
        compiler_params=_cp("parallel"),
    )(dw)


def _uq_to_physical(w):
    k = w.shape[0]
    w3 = w.reshape(k, MLA_HEADS, MLA_NOPE + MLA_ROPE)
    pad = jnp.zeros((k, MLA_HEADS, MLA_QW - MLA_NOPE - MLA_ROPE), w.dtype)
    return jnp.concatenate([w3, pad], axis=2).reshape(k, MLA_HEADS * MLA_QW)


def _uq_to_logical(w):
    k = w.shape[0]
    return w.reshape(k, MLA_HEADS, MLA_QW)[:, :, :MLA_NOPE + MLA_ROPE].reshape(k, -1)


def _ukv_to_physical(w):
    k = w.shape[0]
    w3 = w.reshape(k, MLA_HEADS, MLA_NOPE + MLA_DV)
    return jnp.concatenate([w3[:, :, :MLA_NOPE].reshape(k, -1), w3[:, :, MLA_NOPE:].reshape(k, -1)], axis=1)


def _ukv_to_logical(w):
    k = w.shape[0]
    kn = w[:, :MLA_HEADS * MLA_NOPE].reshape(k, MLA_HEADS, MLA_NOPE)
    v = w[:, MLA_HEADS * MLA_NOPE:].reshape(k, MLA_HEADS, MLA_DV)
    return jnp.concatenate([kn, v], axis=2).reshape(k, -1)


def _matmul(a, b, *, ta=False, tb=False, out_dtype=F32, name, tm_cap=1024, tn_cap=1024, tk_cap=2048, deps=()):
    m, k = (a.shape[1], a.shape[0]) if ta else a.shape
    n = b.shape[0] if tb else b.shape[1]
    assert k == (b.shape[1] if tb else b.shape[0])
    tm, tn, tk = _tile(m, tm_cap), _tile(n, tn_cap), _tile(k, tk_cap)
    nk = k // tk

    def body(a_ref, b_ref, *rest):
        dims = (((0 if ta else 1,), (1 if tb else 0,)), ((), ()))
        part = lax.dot_general(a_ref[...].astype(BF16), b_ref[...].astype(BF16), dims, preferred_element_type=F32)
        if nk == 1:
            rest[len(deps)][...] = part.astype(out_dtype)
            return
        o_ref, acc_ref = rest[len(deps):]
        kk = pl.program_id(2)

        @pl.when(kk == 0)
        def _():
            acc_ref[...] = part

        @pl.when(jnp.logical_and(kk > 0, kk < nk - 1))
        def _():
            acc_ref[...] += part

        @pl.when(kk == nk - 1)
        def _():
            o_ref[...] = (acc_ref[...] + part).astype(o_ref.dtype)

    a_spec = pl.BlockSpec((tk, tm), lambda i, j, kk: (kk, i)) if ta else pl.BlockSpec((tm, tk), lambda i, j, kk: (i, kk))
    b_spec = pl.BlockSpec((tn, tk), lambda i, j, kk: (j, kk)) if tb else pl.BlockSpec((tk, tn), lambda i, j, kk: (kk, j))
    return pl.pallas_call(
        body, name=name, grid=(m // tm, n // tn, nk),
        in_specs=[a_spec, b_spec] + [ANY] * len(deps),
        out_specs=pl.BlockSpec((tm, tn), lambda i, j, kk: (i, j)),
        out_shape=jax.ShapeDtypeStruct((m, n), out_dtype),
        scratch_shapes=[pltpu.VMEM((tm, tn), F32)] if nk > 1 else [],
        compiler_params=_cp("parallel", "parallel", "arbitrary"),
    )(a, b, *deps)


def _row(tm, w, cb=0):
    return pl.BlockSpec((tm, w), lambda i, cb=cb: (i, cb))


def _vec(w, cb=0):
    return pl.BlockSpec((1, w), lambda i, cb=cb: (0, cb))


def _dproj_out(dproj, s, lay, tm, name):
    w = lay.width[name]
    cb = lay.off[name] // w
    spec = _row(tm, w, cb)
    shape = jax.ShapeDtypeStruct((s, lay.total), BF16)
    return spec, shape


def _norm_mod_fwd(x, g, scale, shift, deps):
    s, d = x.shape
    tm = _rtile(s, 256)

    def body(x_ref, g_ref, sc_ref, sh_ref, *rest):
        h_ref = rest[len(deps)]
        xv = x_ref[...]
        xh = xv * lax.rsqrt(jnp.mean(xv * xv, axis=-1, keepdims=True) + EPS)
        h_ref[...] = ((xh * g_ref[...]) * (1.0 + sc_ref[...]) + sh_ref[...]).astype(BF16)

    return pl.pallas_call(
        body, name="norm_mod_fwd", grid=(s // tm,),
        in_specs=[_row(tm, d), _vec(d), _vec(d), _vec(d)] + [ANY] * len(deps),
        out_specs=_row(tm, d), out_shape=jax.ShapeDtypeStruct((s, d), BF16),
        compiler_params=_cp("parallel"),
    )(x, g, scale, shift, *deps)


def _norm_mod_bwd(x, g, scale, dh, dres):
    s, d = x.shape
    tm = _rtile(s, 256)

    def body(x_ref, g_ref, sc_ref, dh_ref, dres_ref, dx_ref, dsh_ref, dsc_ref, dg_ref):
        @pl.when(pl.program_id(0) == 0)
        def _():
            dsh_ref[...] = jnp.zeros_like(dsh_ref)
            dsc_ref[...] = jnp.zeros_like(dsc_ref)
            dg_ref[...] = jnp.zeros_like(dg_ref)

        xv, gv, dhv = x_ref[...], g_ref[...], dh_ref[...]
        rstd = lax.rsqrt(jnp.mean(xv * xv, axis=-1, keepdims=True) + EPS)
        xh = xv * rstd
        dy = dhv * (1.0 + sc_ref[...])
        dxh = dy * gv
        dx_ref[...] = dres_ref[...] + rstd * (dxh - xh * jnp.mean(dxh * xh, axis=-1, keepdims=True))
        dsh_ref[...] += jnp.sum(dhv, axis=0, keepdims=True)
        dsc_ref[...] += jnp.sum(dhv * (xh * gv), axis=0, keepdims=True)
        dg_ref[...] += jnp.sum(dy * xh, axis=0, keepdims=True)

    vec = jax.ShapeDtypeStruct((1, d), F32)
    return pl.pallas_call(
        body, name="norm_mod_bwd", grid=(s // tm,),
        in_specs=[_row(tm, d), _vec(d), _vec(d), _row(tm, d), _row(tm, d)],
        out_specs=[_row(tm, d), _vec(d), _vec(d), _vec(d)],
        out_shape=[jax.ShapeDtypeStruct((s, d), F32), vec, vec, vec],
        compiler_params=_cp("arbitrary"),
    )(x, g, scale, dh, dres)


def _final_loss(x, g, target):
    s, d = x.shape
    tm = _rtile(s, 256)

    def body(x_ref, g_ref, t_ref, l_ref, dx_ref, dg_ref):
        @pl.when(pl.program_id(0) == 0)
        def _():
            l_ref[...] = jnp.zeros_like(l_ref)
            dg_ref[...] = jnp.zeros_like(dg_ref)

        xv, gv = x_ref[...], g_ref[...]
        rstd = lax.rsqrt(jnp.mean(xv * xv, axis=-1, keepdims=True) + EPS)
        xh = xv * rstd
        err = xh * gv - t_ref[...]
        row = jnp.mean(err * err, axis=-1, keepdims=True)
        l_ref[...] += 0.5 * jnp.sum(row, axis=0, keepdims=True)
        dy = err / d
        dxh = dy * gv
        dx_ref[...] = rstd * (dxh - xh * jnp.mean(dxh * xh, axis=-1, keepdims=True))
        dg_ref[...] += jnp.sum(dy * xh, axis=0, keepdims=True)

    return pl.pallas_call(
        body, name="final_loss", grid=(s // tm,),
        in_specs=[_row(tm, d), _vec(d), _row(tm, d)],
        out_specs=[_vec(LANES), _row(tm, d), _vec(d)],
        out_shape=[jax.ShapeDtypeStruct((1, LANES), F32), jax.ShapeDtypeStruct((s, d), F32),
                   jax.ShapeDtypeStruct((1, d), F32)],
        compiler_params=_cp("arbitrary"),
    )(x, g, target)


def _resid_fwd(x, out, gate):
    s, d = x.shape
    tm = _rtile(s, 256)

    def body(x_ref, o_ref, g_ref, y_ref):
        y_ref[...] = x_ref[...] + g_ref[...] * o_ref[...]

    return pl.pallas_call(
        body, name="resid_fwd", grid=(s // tm,),
        in_specs=[_row(tm, d), _row(tm, d), _vec(d)],
        out_specs=_row(tm, d), out_shape=jax.ShapeDtypeStruct((s, d), F32),
        compiler_params=_cp("parallel"),
    )(x, out, gate)


def _resid_bwd(dxn, out, gate, deps):
    s, d = dxn.shape
    tm = _rtile(s, 256)

    def body(dx_ref, o_ref, g_ref, *rest):
        do_ref, dg_ref = rest[len(deps):]

        @pl.when(pl.program_id(0) == 0)
        def _():
            dg_ref[...] = jnp.zeros_like(dg_ref)

        dxv = dx_ref[...]
        do_ref[...] = (dxv * g_ref[...]).astype(BF16)
        dg_ref[...] += jnp.sum(dxv * o_ref[...], axis=0, keepdims=True)

    return pl.pallas_call(
        body, name="resid_bwd", grid=(s // tm,),
        in_specs=[_row(tm, d), _row(tm, d), _vec(d)] + [ANY] * len(deps),
        out_specs=[_row(tm, d), _vec(d)],
        out_shape=[jax.ShapeDtypeStruct((s, d), BF16), jax.ShapeDtypeStruct((1, d), F32)],
        compiler_params=_cp("arbitrary"),
    )(dxn, out, gate, *deps)


def _merge_fwd(y_ret, y_mla, proj, lay):
    s, d = y_ret.shape
    tm = _rtile(s, 256)
    cb = lay.off["bg"] // (2 * d)

    def body(a_ref, b_ref, bg_ref, m_ref):
        sg = _sigmoid(bg_ref[...])
        m_ref[...] = (sg[:, :d] * a_ref[...] + sg[:, d:] * b_ref[...]).astype(BF16)

    return pl.pallas_call(
        body, name="merge_fwd", grid=(s // tm,),
        in_specs=[_row(tm, d), _row(tm, d), _row(tm, 2 * d, cb)],
        out_specs=_row(tm, d), out_shape=jax.ShapeDtypeStruct((s, d), BF16),
        compiler_params=_cp("parallel"),
    )(y_ret, y_mla, proj)


def _merge_bwd(dm, y_ret, y_mla, proj, lay):
    s, d = dm.shape
    tm = _rtile(s, 256)
    cb = lay.off["bg"] // (2 * d)
    dp_spec, dp_shape = _dproj_out(None, s, lay, tm, "bg")

    def body(dm_ref, a_ref, b_ref, bg_ref, da_ref, db_ref, dp_ref):
        sg = _sigmoid(bg_ref[...])
        dmv = dm_ref[...]
        ga, gb = sg[:, :d], sg[:, d:]
        da_ref[...] = (dmv * ga).astype(BF16)
        db_ref[...] = (dmv * gb).astype(BF16)
        dp_ref[:, :d] = (dmv * a_ref[...] * ga * (1.0 - ga)).astype(BF16)
        dp_ref[:, d:] = (dmv * b_ref[...] * gb * (1.0 - gb)).astype(BF16)

    act = jax.ShapeDtypeStruct((s, d), BF16)
    return pl.pallas_call(
        body, name="merge_bwd", grid=(s // tm,),
        in_specs=[_row(tm, d), _row(tm, d), _row(tm, d), _row(tm, 2 * d, cb)],
        out_specs=[_row(tm, d), _row(tm, d), dp_spec],
        out_shape=[act, act, dp_shape],
        compiler_params=_cp("parallel"),
    )(dm, y_ret, y_mla, proj)


def _mla_prep(proj, g_cq, g_ckv, lay):
    s = proj.shape[0]
    r = lay.rank
    tm = _rtile(s, 512)
    cb = lay.off["cqkv"] // (2 * r)

    def body(p_ref, gq_ref, gk_ref, q_ref, k_ref):
        pv = p_ref[...]
        for lo, g_ref, o_ref in ((0, gq_ref, q_ref), (r, gk_ref, k_ref)):
            xv = pv[:, lo:lo + r]
            xh = xv * lax.rsqrt(jnp.mean(xv * xv, axis=-1, keepdims=True) + EPS)
            o_ref[...] = (xh * g_ref[...]).astype(BF16)

    act = jax.ShapeDtypeStruct((s, r), BF16)
    return pl.pallas_call(
        body, name="mla_prep", grid=(s // tm,),
        in_specs=[_row(tm, 2 * r, cb), _vec(r), _vec(r)],
        out_specs=[_row(tm, r), _row(tm, r)], out_shape=[act, act],
        compiler_params=_cp("parallel"),
    )(proj, g_cq, g_ckv)


def _mla_prep_bwd(proj, dqn, dkn, g_cq, g_ckv, dproj, lay):
    s = proj.shape[0]
    r = lay.rank
    tm = _rtile(s, 512)
    cb = lay.off["cqkv"] // (2 * r)
    dp_spec, dp_shape = _dproj_out(dproj, s, lay, tm, "cqkv")

    def body(p_ref, dq_ref, dk_ref, gq_ref, gk_ref, _, dp_ref, dgq_ref, dgk_ref):
        @pl.when(pl.program_id(0) == 0)
        def _():
            dgq_ref[...] = jnp.zeros_like(dgq_ref)
            dgk_ref[...] = jnp.zeros_like(dgk_ref)

        pv = p_ref[...]
        for lo, g_ref, d_ref, dg_ref in ((0, gq_ref, dq_ref, dgq_ref), (r, gk_ref, dk_ref, dgk_ref)):
            xv = pv[:, lo:lo + r]
            rstd = lax.rsqrt(jnp.mean(xv * xv, axis=-1, keepdims=True) + EPS)
            xh = xv * rstd
            dy = d_ref[...]
            dxh = dy * g_ref[...]
            dp_ref[:, lo:lo + r] = (rstd * (dxh - xh * jnp.mean(dxh * xh, axis=-1, keepdims=True))).astype(BF16)
            dg_ref[...] += jnp.sum(dy * xh, axis=0, keepdims=True)

    vec = jax.ShapeDtypeStruct((1, r), F32)
    return pl.pallas_call(
        body, name="mla_prep_bwd", grid=(s // tm,),
        in_specs=[_row(tm, 2 * r, cb), _row(tm, r), _row(tm, r), _vec(r), _vec(r), ANY],
        out_specs=[dp_spec, _vec(r), _vec(r)], out_shape=[dp_shape, vec, vec],
        input_output_aliases={5: 0},
        compiler_params=_cp("arbitrary"),
    )(proj, dqn, dkn, g_cq, g_ckv, dproj)


def _rope_tile(t, a, b, c):
    return t * a + _roll(t, 96) * b + _roll(t, 32) * c


def _rope_tile_bwd(dy, a, b, c):
    return dy * a + _roll(dy * b, 32) + _roll(dy * c, 96)


def _attn_block(s):
    return _rtile(s, 512)


def _qk_prep(qp, kvp, proj, ta, tb, tc, lay):
    s = qp.shape[0]
    hq = MLA_HEADS * MLA_QW
    hv = MLA_HEADS * MLA_DV
    blk = _attn_block(s)
    tm = _rtile(blk, 256)
    per = blk // tm
    kr_cb = lay.off["kr"] // LANES

    def body(q_ref, kv_ref, kr_ref, a_ref, b_ref, c_ref, qc_ref, kc_ref, v_ref, kt_ref, vt_ref):
        a, b, c = a_ref[...], b_ref[...], c_ref[...]
        krot = _rope_tile(kr_ref[...], a, b, c)
        krot_b, krot_t = krot.astype(BF16), krot.T.astype(BF16)
        for h in range(MLA_HEADS):
            q0 = h * MLA_QW
            qc_ref[:, q0:q0 + MLA_NOPE] = (q_ref[:, q0:q0 + MLA_NOPE] * QK_LOG2_SCALE).astype(BF16)
            qc_ref[:, q0 + MLA_NOPE:q0 + MLA_QW] = (
                _rope_tile(q_ref[:, q0 + MLA_NOPE:q0 + MLA_QW], a, b, c) * QK_LOG2_SCALE).astype(BF16)
            kn = kv_ref[:, h * MLA_NOPE:(h + 1) * MLA_NOPE]
            kc_ref[:, q0:q0 + MLA_NOPE] = kn.astype(BF16)
            kc_ref[:, q0 + MLA_NOPE:q0 + MLA_QW] = krot_b
            kt_ref[h, :MLA_NOPE, :] = kn.T.astype(BF16)
            kt_ref[h, MLA_NOPE:, :] = krot_t
            vh = kv_ref[:, (MLA_HEADS + h) * MLA_NOPE:(MLA_HEADS + h + 1) * MLA_NOPE]
            v_ref[:, h * MLA_DV:(h + 1) * MLA_DV] = vh.astype(BF16)
            vt_ref[h] = vh.T.astype(BF16)

    return pl.pallas_call(
        body, name="qk_prep", grid=(s // tm,),
        in_specs=[_row(tm, hq), _row(tm, hq), _row(tm, LANES, kr_cb), _row(tm, LANES), _row(tm, LANES), _row(tm, LANES)],
        out_specs=[_row(tm, hq), _row(tm, hq), _row(tm, hv),
                   pl.BlockSpec((MLA_HEADS, None, MLA_QW, tm), lambda i: (0, i // per, 0, i % per)),
                   pl.BlockSpec((MLA_HEADS, None, MLA_DV, tm), lambda i: (0, i // per, 0, i % per))],
        out_shape=[jax.ShapeDtypeStruct((s, hq), BF16), jax.ShapeDtypeStruct((s, hq), BF16),
                   jax.ShapeDtypeStruct((s, hv), BF16),
                   jax.ShapeDtypeStruct((MLA_HEADS, s // blk, MLA_QW, blk), BF16),
                   jax.ShapeDtypeStruct((MLA_HEADS, s // blk, MLA_DV, blk), BF16)],
        compiler_params=_cp("parallel"),
    )(qp, kvp, proj, ta, tb, tc)


def _kv_bwd_prep(dk_cat, dv, ta, tb, tc, dproj, lay):
    s = dk_cat.shape[0]
    hq = MLA_HEADS * MLA_QW
    hv = MLA_HEADS * MLA_DV
    tm = _rtile(s, 256)
    dp_spec, dp_shape = _dproj_out(dproj, s, lay, tm, "kr")

    def body(dk_ref, dv_ref, a_ref, b_ref, c_ref, _, dkv_ref, dp_ref):
        acc = jnp.zeros((tm, LANES), F32)
        for h in range(MLA_HEADS):
            q0 = h * MLA_QW
            dkv_ref[:, h * MLA_NOPE:(h + 1) * MLA_NOPE] = dk_ref[:, q0:q0 + MLA_NOPE].astype(BF16)
            acc = acc + dk_ref[:, q0 + MLA_NOPE:q0 + MLA_QW]
        dkv_ref[:, MLA_HEADS * MLA_NOPE:] = dv_ref[...].astype(BF16)
        dp_ref[...] = _rope_tile_bwd(acc, a_ref[...], b_ref[...], c_ref[...]).astype(BF16)

    return pl.pallas_call(
        body, name="kv_bwd_prep", grid=(s // tm,),
        in_specs=[_row(tm, hq), _row(tm, hv), _row(tm, LANES), _row(tm, LANES), _row(tm, LANES), ANY],
        out_specs=[_row(tm, hq), dp_spec],
        out_shape=[jax.ShapeDtypeStruct((s, hq), BF16), dp_shape],
        input_output_aliases={5: 1},
        compiler_params=_cp("parallel"),
    )(dk_cat, dv, ta, tb, tc, dproj)


def _mla_gate_bwd(du, o, proj, dproj, lay):
    s, vw = du.shape
    tm = _rtile(s, 256)
    cb = lay.off["mg"] // vw
    dp_spec, dp_shape = _dproj_out(dproj, s, lay, tm, "mg")
    assert MLA_HEADS <= LANES

    def body(du_ref, o_ref, g_ref, _, do_ref, dl_ref, dp_ref):
        gv, duv, ov = g_ref[...], du_ref[...], o_ref[...]
        sg = _sigmoid(gv)
        do = (duv * (gv * sg)).astype(BF16)
        do_ref[...] = do
        dp_ref[...] = (duv * ov * (sg + gv * sg * (1.0 - sg))).astype(BF16)
        prod = do.astype(F32) * ov
        lane = lax.broadcasted_iota(jnp.int32, (tm, LANES), 1)
        delta = jnp.zeros((tm, LANES), F32)
        for h in range(MLA_HEADS):
            dh = jnp.sum(prod[:, h * MLA_DV:(h + 1) * MLA_DV], axis=-1, keepdims=True)
            delta = jnp.where(lane == h, dh, delta)
        dl_ref[...] = delta

    return pl.pallas_call(
        body, name="mla_gate_bwd", grid=(s // tm,),
        in_specs=[_row(tm, vw), _row(tm, vw), _row(tm, vw, cb), ANY],
        out_specs=[_row(tm, vw), _row(tm, LANES), dp_spec],
        out_shape=[jax.ShapeDtypeStruct((s, vw), BF16), jax.ShapeDtypeStruct((s, LANES), F32), dp_shape],
        input_output_aliases={3: 2},
        compiler_params=_cp("parallel"),
    )(du, o, proj, dproj)


RET_BLOCK = 256


def _ret_tables(lg, blk):
    ri = lax.broadcasted_iota(jnp.int32, (blk, blk), 0)
    ci = lax.broadcasted_iota(jnp.int32, (blk, blk), 1)
    col = lax.broadcasted_iota(jnp.int32, (blk, 1), 0).astype(F32)
    dist = jnp.abs(ri - ci).astype(F32)
    dmat = jnp.where(ci // CHUNK <= ri // CHUNK, jnp.exp(dist * lg), 0.0)
    xi = jnp.exp((col + 1.0) * lg)
    zeta = jnp.exp((blk - 1.0 - col) * lg)
    decay = jnp.exp(jnp.full((1, 1), blk, F32) * lg)
    return dmat, xi, zeta, decay


def _ret_qkvg(blk, cs, sn):
    dv = RET_DV
    q = blk[:, :RET_DK]
    k = blk[:, RET_DK:2 * RET_DK]
    q = q * cs + _roll(q, RET_DK // 2) * sn
    k = (k * cs + _roll(k, RET_DK // 2) * sn) * (RET_DK ** -0.5)
    return q, k, blk[:, 2 * RET_DK:2 * RET_DK + dv], blk[:, 2 * RET_DK + dv:]


def _group_norm(o):
    mu = jnp.mean(o, axis=-1, keepdims=True)
    oc = o - mu
    rstd = lax.rsqrt(jnp.mean(oc * oc, axis=-1, keepdims=True) + EPS)
    return oc * rstd, rstd


def _ret_fwd(proj, lg, cosr, sinr, lay):
    s = proj.shape[0]
    dv, w = RET_DV, lay.ret_w
    tb = _rtile(s, 512)
    blk = min(RET_BLOCK, tb)
    nb, nch = s // tb, tb // blk
    cb0 = lay.off["ret"] // w

    def body(lg_ref, p_ref, cos_ref, sin_ref, o_ref, u_ref, st_ref, state):
        @pl.when(pl.program_id(1) == 0)
        def _():
            state[...] = jnp.zeros_like(state)

        dmat, xi, zeta, decay = _ret_tables(lg_ref[pl.program_id(0)], blk)
        for c in range(nch):
            rows = slice(c * blk, (c + 1) * blk)
            q, k, v, g = _ret_qkvg(p_ref[rows, :], cos_ref[rows, :], sin_ref[rows, :])
            qb, kb, vb = q.astype(BF16), k.astype(BF16), v.astype(BF16)
            sc = _dot_nt(qb, kb) * dmat
            st = state[...]
            o = _dot(sc.astype(BF16), vb) + _dot((q * xi).astype(BF16), st.astype(BF16))
            st_ref[c] = st.astype(BF16)
            state[...] = st * decay + _dot_tn((k * zeta).astype(BF16), vb)
            o_ref[rows, :] = o
            n, _ = _group_norm(o)
            u_ref[rows, :] = (n * (g * _sigmoid(g))).astype(BF16)

    return pl.pallas_call(
        body, name="ret_fwd", grid=(RET_HEADS, nb),
        in_specs=[pl.BlockSpec(memory_space=pltpu.SMEM),
                  pl.BlockSpec((tb, w), lambda h, b: (b, cb0 + h)),
                  pl.BlockSpec((tb, RET_DK), lambda h, b: (b, 0)),
                  pl.BlockSpec((tb, RET_DK), lambda h, b: (b, 0))],
        out_specs=[pl.BlockSpec((tb, dv), lambda h, b: (b, h)),
                   pl.BlockSpec((tb, dv), lambda h, b: (b, h)),
                   pl.BlockSpec((None, nch, RET_DK, dv), lambda h, b: (h, b, 0, 0))],
        out_shape=[jax.ShapeDtypeStruct((s, RET_HEADS * dv), F32),
                   jax.ShapeDtypeStruct((s, RET_HEADS * dv), BF16),
                   jax.ShapeDtypeStruct((RET_HEADS, s // blk, RET_DK, dv), BF16)],
        scratch_shapes=[pltpu.VMEM((RET_DK, dv), F32)],
        compiler_params=_cp("parallel", "arbitrary"),
    )(lg, proj, cosr, sinr)


def _ret_bwd(proj, lg, cosr, sinr, o, du, states, dproj, lay):
    s = proj.shape[0]
    dv, w = RET_DV, lay.ret_w
    tb = _rtile(s, 512)
    blk = min(RET_BLOCK, tb)
    nb, nch = s // tb, tb // blk
    cb0 = lay.off["ret"] // w

    def body(lg_ref, p_ref, cos_ref, sin_ref, o_ref, du_ref, st_ref, _, dp_ref, dstate):
        @pl.when(pl.program_id(1) == 0)
        def _():
            dstate[...] = jnp.zeros_like(dstate)

        dmat, xi, zeta, decay = _ret_tables(lg_ref[pl.program_id(0)], blk)
        for c in reversed(range(nch)):
            rows = slice(c * blk, (c + 1) * blk)
            cs, sn = cos_ref[rows, :], sin_ref[rows, :]
            q, k, v, g = _ret_qkvg(p_ref[rows, :], cs, sn)
            qb, kb, vb = q.astype(BF16), k.astype(BF16), v.astype(BF16)
            n, rstd = _group_norm(o_ref[rows, :])
            sg = _sigmoid(g)
            duv = du_ref[rows, :]
            dn = duv * (g * sg)
            dg = duv * n * (sg + g * sg * (1.0 - sg))
            do = rstd * (dn - jnp.mean(dn, axis=-1, keepdims=True) - n * jnp.mean(dn * n, axis=-1, keepdims=True))
            dob = do.astype(BF16)
            rb = st_ref[c]
            drb = dstate[...].astype(BF16)
            sc = (_dot_nt(qb, kb) * dmat).astype(BF16)
            dsc = (_dot_nt(dob, vb) * dmat).astype(BF16)
            qx = (q * xi).astype(BF16)
            kz = (k * zeta).astype(BF16)
            dq = _dot(dsc, kb) + _dot_nt(dob, rb) * xi
            dk = (_dot_tn(dsc, qb) + _dot_nt(vb, drb) * zeta) * (RET_DK ** -0.5)
            dvv = _dot_tn(sc, dob) + _dot(kz, drb)
            dstate[...] = dstate[...] * decay + _dot_tn(qx, dob)
            dp_ref[rows, :RET_DK] = (dq * cs + _roll(dq * sn, RET_DK // 2)).astype(BF16)
            dp_ref[rows, RET_DK:2 * RET_DK] = (dk * cs + _roll(dk * sn, RET_DK // 2)).astype(BF16)
            dp_ref[rows, 2 * RET_DK:2 * RET_DK + dv] = dvv.astype(BF16)
            dp_ref[rows, 2 * RET_DK + dv:] = dg.astype(BF16)

    rev = lambda h, b: (nb - 1 - b, h)
    return pl.pallas_call(
        body, name="ret_bwd", grid=(RET_HEADS, nb),
        in_specs=[pl.BlockSpec(memory_space=pltpu.SMEM),
                  pl.BlockSpec((tb, w), lambda h, b: (nb - 1 - b, cb0 + h)),
                  pl.BlockSpec((tb, RET_DK), lambda h, b: (nb - 1 - b, 0)),
                  pl.BlockSpec((tb, RET_DK), lambda h, b: (nb - 1 - b, 0)),
                  pl.BlockSpec((tb, dv), rev),
                  pl.BlockSpec((tb, dv), rev),
                  pl.BlockSpec((None, nch, RET_DK, dv), lambda h, b: (h, nb - 1 - b, 0, 0)),
                  ANY],
        out_specs=pl.BlockSpec((tb, w), lambda h, b: (nb - 1 - b, cb0 + h)),
        out_shape=jax.ShapeDtypeStruct((s, lay.total), BF16),
        input_output_aliases={7: 0},
        scratch_shapes=[pltpu.VMEM((RET_DK, dv), F32)],
        compiler_params=_cp("parallel", "arbitrary"),
    )(lg, proj, cosr, sinr, o, du, states, dproj)


ATTN_HEADS_PER_STEP = 2


def _attn_mask(rows, cols, row0, col0, keys_on_rows):
    ri = (lax.broadcasted_iota(jnp.int32, (rows, cols), 0) + row0) // CHUNK
    ci = (lax.broadcasted_iota(jnp.int32, (rows, cols), 1) + col0) // CHUNK
    return ri <= ci if keys_on_rows else ci <= ri


def _attn_fwd(q_cat, k_cat, v_t, proj, lay):
    s = q_cat.shape[0]
    bq = _attn_block(s)
    nq = s // bq
    mg_cb = lay.off["mg"] // MLA_DV

    hp = ATTN_HEADS_PER_STEP
    assert MLA_HEADS % hp == 0 and mg_cb % hp == 0

    def body(q_ref, k_ref, vt_ref, g_ref, o_ref, u_ref, lse_ref):
        i = pl.program_id(1)
        half = bq // 2
        qs = [q_ref[:, a * MLA_QW:(a + 1) * MLA_QW] for a in range(hp)]

        def scores(a, j):
            r0 = pl.multiple_of(j * bq, bq)
            return _dot_nt(k_ref[pl.ds(r0, bq), a * MLA_QW:(a + 1) * MLA_QW], qs[a])

        def update(a, j, sc, m, l, acc):
            mn = jnp.maximum(m, jnp.max(sc, axis=0, keepdims=True))
            p = jnp.exp2(sc - mn)
            alpha = jnp.exp2(m - mn)
            l = alpha * l + jnp.sum(p, axis=0, keepdims=True)
            acc = alpha * acc + _dot(vt_ref[a, j], p.astype(BF16))
            return mn, l, acc

        def lanes_after(x, fill):
            return jnp.concatenate([jnp.full((x.shape[0], half), fill, F32), x], axis=1)

        def diag(a, m, l, acc):
            r0 = pl.multiple_of(i * bq, bq)
            kc = slice(a * MLA_QW, (a + 1) * MLA_QW)
            top = _dot_nt(k_ref[pl.ds(r0, half), kc], qs[a])
            bot = _dot_nt(k_ref[pl.ds(r0 + half, half), kc], qs[a][half:, :])
            top = jnp.where(_attn_mask(half, bq, 0, 0, True), top, NEG_INF)
            bot = jnp.where(_attn_mask(half, half, 0, 0, True), bot, NEG_INF)
            mn = jnp.maximum(m, jnp.maximum(jnp.max(top, axis=0, keepdims=True),
                                            lanes_after(jnp.max(bot, axis=0, keepdims=True), NEG_INF)))
            pt, pb = jnp.exp2(top - mn), jnp.exp2(bot - mn[:, half:])
            alpha = jnp.exp2(m - mn)
            l = alpha * l + jnp.sum(pt, axis=0, keepdims=True) + lanes_after(jnp.sum(pb, axis=0, keepdims=True), 0.0)
            vt = vt_ref[a, i]
            acc = (alpha * acc + _dot(vt[:, :half], pt.astype(BF16))
                   + lanes_after(_dot(vt[:, half:], pb.astype(BF16)), 0.0))
            return mn, l, acc

        def pair(j, carry, last):
            sa = [scores(a, j) for a in range(hp)]
            if last:
                return tuple(diag(a, *update(a, j, sa[a], *carry[a])) for a in range(hp))
            sb = [scores(a, j + 1) for a in range(hp)]
            carry = [update(a, j, sa[a], *carry[a]) for a in range(hp)]
            return tuple(update(a, j + 1, sb[a], *carry[a]) for a in range(hp))

        def single(carry):
            return tuple(diag(a, *carry[a]) for a in range(hp))

        init = tuple((jnp.full((1, bq), NEG_INF, F32), jnp.zeros((1, bq), F32), jnp.zeros((MLA_DV, bq), F32))
                     for _ in range(hp))
        carry = lax.fori_loop(0, i // 2, lambda t, c: pair(2 * t, c, False), init)
        carry = lax.cond(i % 2 == 1, lambda c: pair(i - 1, c, True), single, carry)
        for a, (m, l, acc) in enumerate(carry):
            cols = slice(a * MLA_DV, (a + 1) * MLA_DV)
            o = (acc / l).T
            gv = g_ref[:, cols]
            o_ref[:, cols] = o
            u_ref[:, cols] = (o * (gv * _sigmoid(gv))).astype(BF16)
            lse_ref[a] = m + jnp.log2(l)

    return pl.pallas_call(
        body, name="attn_fwd", grid=(MLA_HEADS // hp, nq),
        in_specs=[pl.BlockSpec((bq, hp * MLA_QW), lambda h, i: (i, h)),
                  pl.BlockSpec((s, hp * MLA_QW), lambda h, i: (0, h)),
                  pl.BlockSpec((hp, nq, MLA_DV, bq), lambda h, i: (h, 0, 0, 0)),
                  pl.BlockSpec((bq, hp * MLA_DV), lambda h, i: (i, mg_cb // hp + h))],
        out_specs=[pl.BlockSpec((bq, hp * MLA_DV), lambda h, i: (i, h)),
                   pl.BlockSpec((bq, hp * MLA_DV), lambda h, i: (i, h)),
                   pl.BlockSpec((hp, None, 1, bq), lambda h, i: (h, i, 0, 0))],
        out_shape=[jax.ShapeDtypeStruct((s, MLA_HEADS * MLA_DV), F32),
                   jax.ShapeDtypeStruct((s, MLA_HEADS * MLA_DV), BF16),
                   jax.ShapeDtypeStruct((MLA_HEADS, nq, 1, bq), F32)],
        compiler_params=_cp("parallel", "parallel"),
    )(q_cat, k_cat, v_t, proj)


def _attn_bwd(q_cat, k_cat, v, k_t, do, lse, delta, ta, tb, tc):
    s = q_cat.shape[0]
    blk = _attn_block(s)
    nb = s // blk
    hp = ATTN_HEADS_PER_STEP
    qw, dvw = hp * MLA_QW, hp * MLA_DV

    def body(q_ref, k_ref, v_ref, kt_ref, do_ref, lse_ref, dl_ref, a_ref, b_ref, c_ref,
             dq_ref, dk_ref, dv_ref, dq_acc):
        j = pl.program_id(1)

        @pl.when(j == 0)
        def _():
            dq_acc[...] = jnp.zeros_like(dq_acc)

        def qcols(a):
            return slice(a * MLA_QW, (a + 1) * MLA_QW)

        def vcols(a):
            return slice(a * MLA_DV, (a + 1) * MLA_DV)

        kbs = [k_ref[:, qcols(a)] for a in range(hp)]
        vbs = [v_ref[:, vcols(a)] for a in range(hp)]

        half = blk // 2

        def step(i, carry):
            r0 = pl.multiple_of(i * blk, blk)
            out = []
            for a in range(hp):
                dk, dvv = carry[a]
                q, dob = q_ref[pl.ds(r0, blk), qcols(a)], do_ref[pl.ds(r0, blk), vcols(a)]
                sc = _dot_nt(kbs[a], q)
                p = jnp.exp2(sc - lse_ref[a, i])
                dvv = dvv + _dot(p.astype(BF16), dob)
                ds = (p * (_dot_nt(vbs[a], dob) - dl_ref[a, i])).astype(BF16)
                dk = dk + _dot(ds, q)
                dq_acc[a, i] = dq_acc[a, i] + _dot(kt_ref[a], ds)
                out.append((dk, dvv))
            return tuple(out)

        def diag():
            r0 = pl.multiple_of(j * blk, blk)
            out = []
            for a in range(hp):
                q, dob = q_ref[pl.ds(r0, blk), qcols(a)], do_ref[pl.ds(r0, blk), vcols(a)]
                q1, do1 = q[half:, :], dob[half:, :]
                lse, dl, kt = lse_ref[a, j], dl_ref[a, j], kt_ref[a]
                top = jnp.where(_attn_mask(half, blk, 0, 0, True), _dot_nt(kbs[a][:half, :], q), NEG_INF)
                bot = jnp.where(_attn_mask(half, half, 0, 0, True), _dot_nt(kbs[a][half:, :], q1), NEG_INF)
                pt, pb = jnp.exp2(top - lse), jnp.exp2(bot - lse[:, half:])
                dst = (pt * (_dot_nt(vbs[a][:half, :], dob) - dl)).astype(BF16)
                dsb = (pb * (_dot_nt(vbs[a][half:, :], do1) - dl[:, half:])).astype(BF16)
                later = _dot(kt[:, half:], dsb)
                dq_acc[a, j] = (dq_acc[a, j] + _dot(kt[:, :half], dst)
                                + jnp.concatenate([jnp.zeros_like(later), later], axis=1))
                out.append((jnp.concatenate([_dot(dst, q), _dot(dsb, q1)], axis=0),
                            jnp.concatenate([_dot(pt.astype(BF16), dob), _dot(pb.astype(BF16), do1)], axis=0)))
            return tuple(out)

        carry = lax.fori_loop(j + 1, nb, step, diag())
        for a, (dk, dvv) in enumerate(carry):
            dk_ref[:, qcols(a)] = dk * LN2
            dv_ref[:, vcols(a)] = dvv

        @pl.when(j == nb - 1)
        def _():
            for a in range(hp):
                for i in range(nb):
                    rows = slice(i * blk, (i + 1) * blk)
                    dq = dq_acc[a, i].T * QK_SCALE
                    c0 = a * MLA_QW
                    dq_ref[rows, c0:c0 + MLA_NOPE] = dq[:, :MLA_NOPE].astype(BF16)
                    dq_ref[rows, c0 + MLA_NOPE:c0 + MLA_QW] = _rope_tile_bwd(
                        dq[:, MLA_NOPE:], a_ref[rows, :], b_ref[rows, :], c_ref[rows, :]).astype(BF16)

    tab = pl.BlockSpec((s, LANES), lambda h, j: (0, 0), pipeline_mode=pl.Buffered(1))
    rows = pl.BlockSpec((hp, nb, 1, blk), lambda h, j: (h, 0, 0, 0))
    return pl.pallas_call(
        body, name="attn_bwd", grid=(MLA_HEADS // hp, nb),
        in_specs=[pl.BlockSpec((s, qw), lambda h, j: (0, h)),
                  pl.BlockSpec((blk, qw), lambda h, j: (j, h)),
                  pl.BlockSpec((blk, dvw), lambda h, j: (j, h)),
                  pl.BlockSpec((hp, None, MLA_QW, blk), lambda h, j: (h, j, 0, 0)),
                  pl.BlockSpec((s, dvw), lambda h, j: (0, h)),
                  rows, rows, tab, tab, tab],
        out_specs=[pl.BlockSpec((s, qw), lambda h, j: (0, h)),
                   pl.BlockSpec((blk, qw), lambda h, j: (j, h)),
                   pl.BlockSpec((blk, dvw), lambda h, j: (j, h))],
        out_shape=[jax.ShapeDtypeStruct((s, MLA_HEADS * MLA_QW), BF16),
                   jax.ShapeDtypeStruct((s, MLA_HEADS * MLA_QW), F32),
                   jax.ShapeDtypeStruct((s, MLA_HEADS * MLA_DV), F32)],
        scratch_shapes=[pltpu.VMEM((hp, nb, MLA_QW, blk), F32)],
        compiler_params=_cp("parallel", "arbitrary"),
    )(q_cat, k_cat, v, k_t, do, lse, delta, ta, tb, tc)


def _place():
    return lax.axis_index("x"), lax.axis_index("y"), lax.axis_index("c")


def _slot(px, py, pc):
    return 4 * px + 2 * py + pc


def _all_gather(shards, layer, name, vmem=False, deps=()):
    n = len(shards)

    def body(*refs):
        srcs, outs = refs[:n], refs[n + len(deps):2 * n + len(deps)]
        send_sems, recv_sems, local_sems = refs[2 * n + len(deps):]
        x, y, c = _place()
        me, sibling = (x, y, c), (x, y, 1 - c)
        chips = [(1 - x, y), (x, 1 - y), (1 - x, 1 - y)]
        firsts, passes, locals_ = [], [], []

        def copy(a, k, block, to, src=None):
            dst = outs[a].at[_slot(*block)]
            return pltpu.make_async_remote_copy(
                src_ref=dst if src is None else src, dst_ref=dst,
                send_sem=send_sems.at[7 * a + k], recv_sem=recv_sems.at[7 * a + k],
                device_id=to, device_id_type=MESH)

        for a in range(n):
            src = srcs[a] if layer is None else srcs[a].at[layer]
            mine = pltpu.make_async_copy(src, outs[a].at[_slot(*me)], local_sems.at[a])
            mine.start()
            locals_.append(mine)
            first = [copy(a, 0, me, sibling, src=src)]
            first += [copy(a, 1 + j, me, (*chip, c), src=src) for j, chip in enumerate(chips)]
            for cp in first:
                cp.start()
            firsts += first
        for a in range(n):
            for j, chip in enumerate(chips):
                copy(a, 1 + j, (*chip, c), me).wait_recv()
                fwd = copy(a, 4 + j, (*chip, c), sibling)
                fwd.start()
                passes.append(fwd)
        for a in range(n):
            copy(a, 0, sibling, me).wait_recv()
            for j, chip in enumerate(chips):
                copy(a, 4 + j, (*chip, 1 - c), me).wait_recv()
        for cp in firsts + passes:
            cp.wait_send()
        for mine in locals_:
            mine.wait()

    space = pl.BlockSpec(memory_space=pltpu.VMEM) if vmem else ANY
    out_shape = [jax.ShapeDtypeStruct((N_DEV,) + (a.shape if layer is None else a.shape[1:]), a.dtype) for a in shards]
    return pl.pallas_call(
        body, name=name,
        in_specs=[space] * n + [ANY] * len(deps), out_specs=[space] * n, out_shape=out_shape,
        scratch_shapes=[pltpu.SemaphoreType.DMA((7 * n,)), pltpu.SemaphoreType.DMA((7 * n,)),
                        pltpu.SemaphoreType.DMA((n,))],
        compiler_params=pltpu.CompilerParams(has_side_effects=True),
    )(*shards, *deps)


HBM = pl.BlockSpec(memory_space=pltpu.HBM)
SEM = pl.BlockSpec(memory_space=pltpu.SEMAPHORE)
EFFECT = pltpu.SideEffectType.DATAFLOW_SIDE_EFFECTING


def _push_copies(srcs, lands, send_sems, recv_sems, local_sems, by_peer):
    x, y, c = _place()
    me = _slot(x, y, c)
    local, remote = [], []
    for a, (src, land) in enumerate(zip(srcs, lands)):
        local.append(pltpu.make_async_copy(src.at[me] if by_peer else src, land.at[me], local_sems.at[a]))
        for r in range(1, N_DEV):
            peer = (1 - x if r & 4 else x, 1 - y if r & 2 else y, 1 - c if r & 1 else c)
            remote.append(pltpu.make_async_remote_copy(
                src_ref=src.at[_slot(*peer)] if by_peer else src, dst_ref=land.at[me],
                send_sem=send_sems.at[7 * a + r - 1], recv_sem=recv_sems.at[7 * a + r - 1],
                device_id=peer, device_id_type=MESH))
    return local, remote


def _push_start(srcs, by_peer, after, name):
    n = len(srcs)
    srcs = [pltpu.with_memory_space_constraint(a, pltpu.HBM) for a in srcs]
    lands = [pltpu.with_memory_space_constraint(
        lax.empty((N_DEV,) + (a.shape[1:] if by_peer else a.shape), a.dtype), pltpu.HBM) for a in srcs]

    def body(*refs):
        k = 2 * n + len(after)
        local, remote = _push_copies(refs[:n], refs[n:2 * n], refs[k], refs[k + 1], refs[k + 2], by_peer)
        for cp in local + remote:
            cp.start()
        token = refs[k + 3 + 2 * n]
        token[...] = jnp.zeros_like(token)

    outs = pl.pallas_call(
        body, name=name,
        out_shape=(pltpu.SemaphoreType.DMA((7 * n,)), pltpu.SemaphoreType.DMA((7 * n,)),
                   pltpu.SemaphoreType.DMA((n,)),
                   *[pltpu.HBM(a.shape, a.dtype) for a in srcs], *[pltpu.HBM(a.shape, a.dtype) for a in lands],
                   jax.ShapeDtypeStruct((8, LANES), F32)),
        in_specs=[HBM] * (2 * n) + [ANY] * len(after),
        out_specs=(SEM, SEM, SEM, *([HBM] * (2 * n)), pl.BlockSpec(memory_space=pltpu.VMEM)),
        input_output_aliases={i: 3 + i for i in range(2 * n)},
        compiler_params=pltpu.CompilerParams(has_side_effects=EFFECT),
    )(*srcs, *lands, *after)
    return outs[:3], outs[3:3 + n], outs[3 + n:3 + 2 * n], outs[3 + 2 * n]


def _push_wait(sems, srcs, lands, by_peer, after, name):
    n = len(srcs)

    def body(*refs):
        local, remote = _push_copies(refs[:n], refs[n:2 * n], refs[2 * n], refs[2 * n + 1], refs[2 * n + 2], by_peer)
        for cp in local:
            cp.wait()
        for cp in remote:
            cp.wait_send()
            cp.wait_recv()

    outs = pl.pallas_call(
        body, name=name,
        out_shape=[pltpu.HBM(a.shape, a.dtype) for a in list(srcs) + list(lands)],
        in_specs=[HBM] * (2 * n) + [SEM] * 3 + [ANY] * len(after),
        out_specs=[HBM] * (2 * n),
        input_output_aliases={i: i for i in range(2 * n)},
        compiler_params=pltpu.CompilerParams(has_side_effects=EFFECT),
    )(*srcs, *lands, *sems, *after)
    return outs[n:]


def _adam_math(g, w, m, v):
    m = ADAM_B1 * m + (1.0 - ADAM_B1) * g
    v = ADAM_B2 * v + (1.0 - ADAM_B2) * (g * g)
    m_hat = m / (1.0 - ADAM_B1 ** ADAM_STEP)
    v_hat = v / (1.0 - ADAM_B2 ** ADAM_STEP)
    delta = -ADAM_LR * (m_hat / (jnp.sqrt(v_hat) + ADAM_EPS) + ADAM_WD * w)
    return delta, m, v


def _adam_sharded(recvs, first, w, m, v, prev, name):
    nl, r, c = w.shape
    n = len(recvs)
    by_cols = r % 128 != 0
    tr, tc = (r, _tile(c, LANES)) if by_cols else (_rtile(r, 128), c)
    nr = c // tc if by_cols else r // tr
    prev = list(prev) if prev is not None else []

    def tile(t):
        return (0, t) if by_cols else (t, 0)

    def body(*refs):
        g_refs = refs[:n]
        w_ref, m_ref, v_ref = refs[n:n + 3]
        go_ref, d_ref, mo_ref, vo_ref = refs[n + 3 + len(prev):]
        layer = pl.program_id(0)
        for l in range(n):
            @pl.when(layer == l)
            def _(l=l):
                g = g_refs[l][0].astype(F32)
                for i in range(1, N_DEV):
                    g = g + g_refs[l][i].astype(F32)
                delta, mn, vn = _adam_math(g, w_ref[...], m_ref[...], v_ref[...])
                go_ref[...] = g
                d_ref[...] = delta
                mo_ref[...] = mn
                vo_ref[...] = vn

    def recv_spec(l):
        def index(layer, i):
            return (0,) + tile(jnp.where(layer == l, i, jnp.where(layer < l, 0, nr - 1)))
        return pl.BlockSpec((N_DEV, tr, tc), index)

    blk = pl.BlockSpec((None, tr, tc), lambda layer, i: (first + layer,) + tile(i))
    out = jax.ShapeDtypeStruct(w.shape, F32)
    return pl.pallas_call(
        body, name=name, grid=(n, nr),
        in_specs=[recv_spec(l) for l in range(n)] + [blk, blk, blk] + [ANY] * len(prev),
        out_specs=[blk] * 4, out_shape=[out] * 4,
        input_output_aliases={n + 3 + k: k for k in range(len(prev))},
        compiler_params=_cp("arbitrary", "arbitrary"),
    )(*recvs, w, m, v, *prev)


def _adam_mod(c_all_t, dmod, w, m, v):
    nl, d, c = w.shape
    tr = _rtile(d, 128)

    def body(ct_ref, dm_ref, w_ref, m_ref, v_ref, go_ref, d_ref, mo_ref, vo_ref):
        ct = ct_ref[...].astype(BF16).astype(F32)
        dm = dm_ref[...].astype(BF16).astype(F32)
        g = ct[:, 0:1] * dm[0:1, :]
        for b in range(1, N_DEV):
            g = g + ct[:, b:b + 1] * dm[b:b + 1, :]
        delta, mn, vn = _adam_math(g, w_ref[...], m_ref[...], v_ref[...])
        go_ref[...] = g
        d_ref[...] = delta
        mo_ref[...] = mn
        vo_ref[...] = vn

    blk = pl.BlockSpec((None, tr, c), lambda layer, i: (layer, i, 0))
    out = jax.ShapeDtypeStruct(w.shape, F32)
    return pl.pallas_call(
        body, name="adam_mod", grid=(nl, d // tr),
        in_specs=[pl.BlockSpec((tr, N_DEV), lambda layer, i: (i, 0)),
                  pl.BlockSpec((None, N_DEV, c), lambda layer, i: (layer, 0, 0)), blk, blk, blk],
        out_specs=[blk] * 4, out_shape=[out] * 4,
        compiler_params=_cp("parallel", "parallel"),
    )(c_all_t, dmod, w, m, v)


def _adam_small(g, w, m, v, name):
    def body(g_ref, w_ref, m_ref, v_ref, d_ref, mo_ref, vo_ref):
        delta, mn, vn = _adam_math(g_ref[...], w_ref[...], m_ref[...], v_ref[...])
        d_ref[...] = delta
        mo_ref[...] = mn
        vo_ref[...] = vn

    out = jax.ShapeDtypeStruct(w.shape, F32)
    return pl.pallas_call(body, name=name, out_shape=[out] * 3)(g, w, m, v)


def _sum_devices(parts):
    def body(p_ref, o_ref):
        acc = p_ref[0]
        for i in range(1, N_DEV):
            acc = acc + p_ref[i]
        o_ref[...] = acc

    return pl.pallas_call(body, name="sum_devices",
                          out_shape=jax.ShapeDtypeStruct(parts.shape[1:], F32))(parts)


def _rope_tables(positions):
    pos = positions.astype(F32)[:, None]

    def cs(dim):
        inv = 1.0 / (ROPE_BASE ** (jnp.arange(0, dim, 2, dtype=F32) / dim))
        ang = pos * inv
        return jnp.cos(ang), jnp.sin(ang)

    cr, sr = cs(RET_DK)
    cm, sm = cs(MLA_ROPE)
    z = jnp.zeros_like(cm)
    pad = jnp.zeros((pos.shape[0], LANES - MLA_ROPE), F32)
    cosr = jnp.concatenate([cr, cr], axis=1)
    sinr = jnp.concatenate([-sr, sr], axis=1)
    ta = jnp.concatenate([cm, cm, pad], axis=1)
    tb = jnp.concatenate([-sm, z, pad], axis=1)
    tc = jnp.concatenate([z, sm, pad], axis=1)
    return cosr, sinr, ta, tb, tc


def _layer_fwd(x, mod, g_norm, g_cq, g_ckv, w_in, other_weights, tabs, lg, lay, deps):
    d = x.shape[1]
    cosr, sinr, ta, tb, tc = tabs
    shift, scale, gate = mod[:, :d], mod[:, d:2 * d], mod[:, 2 * d:]
    h = _norm_mod_fwd(x, g_norm, scale, shift, deps)
    proj = _matmul(h, w_in, name="mm_proj", tn_cap=1920)
    wts, sent = other_weights(proj)
    wts["in"] = w_in
    o_ret, u_ret, states = _ret_fwd(proj, lg, cosr, sinr, lay)
    y_ret = _matmul(u_ret, wts["ret"], name="mm_y", deps=sent)
    cqn, ckvn = _mla_prep(proj, g_cq, g_ckv, lay)
    qp = _matmul(cqn, wts["uq"], name="mm_up")
    kvp = _matmul(ckvn, wts["ukv"], name="mm_up")
    q_cat, k_cat, v, k_t, v_t = _qk_prep(qp, kvp, proj, ta, tb, tc, lay)
    o_mla, u_mla, lse = _attn_fwd(q_cat, k_cat, v_t, proj, lay)
    y_mla = _matmul(u_mla, wts["mla"], name="mm_y")
    merged = _merge_fwd(y_ret, y_mla, proj, lay)
    out = _matmul(merged, wts["out"], name="mm_y")
    x_next = _resid_fwd(x, out, gate)
    saved = dict(x=x, h=h, proj=proj, o_ret=o_ret, u_ret=u_ret, states=states, y_ret=y_ret, cqn=cqn,
                 ckvn=ckvn, q_cat=q_cat, k_cat=k_cat, v=v, k_t=k_t, o_mla=o_mla, u_mla=u_mla, lse=lse,
                 y_mla=y_mla, merged=merged, out=out, wts=wts)
    return x_next, saved


def _to_owner_blocks_cols(g, n_local):
    k = g.shape[0]
    return g.reshape(k, N_DEV, n_local).transpose(1, 0, 2)


def _from_owner_blocks_cols(g):
    return g.transpose(1, 0, 2).reshape(g.shape[1], -1)


def _layer_bwd(dxn, sv, mod, g_norm, g_cq, g_ckv, wts, tabs, lg, lay, deps, shard_cols, push):
    d = dxn.shape[1]
    n_in, n_uq, n_ukv = shard_cols
    cosr, sinr, ta, tb, tc = tabs
    scale, gate = mod[:, d:2 * d], mod[:, 2 * d:]
    gdt = BF16
    dout, dgate = _resid_bwd(dxn, sv["out"], gate, deps)
    dmerged = _matmul(dout, wts["out"], tb=True, name="mm_dy")
    dw_out = _matmul(sv["merged"], dout, ta=True, out_dtype=gdt, name="mm_dw")
    dy_ret, dy_mla, dproj = _merge_bwd(dmerged, sv["y_ret"], sv["y_mla"], sv["proj"], lay)
    du_ret = _matmul(dy_ret, wts["ret"], tb=True, name="mm_dy")
    dw_ret = _matmul(sv["u_ret"], dy_ret, ta=True, out_dtype=gdt, name="mm_dw")
    dproj = _ret_bwd(sv["proj"], lg, cosr, sinr, sv["o_ret"], du_ret, sv["states"], dproj, lay)
    du_mla = _matmul(dy_mla, wts["mla"], tb=True, name="mm_dy")
    dw_mla = _matmul(sv["u_mla"], dy_mla, ta=True, out_dtype=gdt, name="mm_dw")
    do_mla, delta, dproj = _mla_gate_bwd(du_mla, sv["o_mla"], sv["proj"], dproj, lay)
    delta = delta[:, :MLA_HEADS].T.reshape(sv["lse"].shape)
    dqp, dk_cat, dv = _attn_bwd(sv["q_cat"], sv["k_cat"], sv["v"], sv["k_t"], do_mla, sv["lse"], delta,
                                ta, tb, tc)
    dkvp, dproj = _kv_bwd_prep(dk_cat, dv, ta, tb, tc, dproj, lay)
    dcqn = _matmul(dqp, wts["uq"], tb=True, name="mm_dlat")
    dckvn = _matmul(dkvp, wts["ukv"], tb=True, name="mm_dlat")
    dw_uq = _matmul(sv["cqn"], dqp, ta=True, out_dtype=gdt, name="mm_dwup")
    dw_ukv = _matmul(sv["ckvn"], dkvp, ta=True, out_dtype=gdt, name="mm_dwup")
    sent = push("a", [_to_owner_blocks_cols(_uq_to_logical(dw_uq), n_uq),
                      _to_owner_blocks_cols(_ukv_to_logical(dw_ukv), n_ukv),
                      dw_ret.reshape(N_DEV, -1, d), dw_mla.reshape(N_DEV, -1, d), dw_out.reshape(N_DEV, -1, d)])
    dproj, dg_cq, dg_ckv = _mla_prep_bwd(sv["proj"], dcqn, dckvn, g_cq, g_ckv, dproj, lay)
    dw_in = _matmul(sv["h"], dproj, ta=True, out_dtype=gdt, name="mm_dwin", tn_cap=1920, deps=sent)
    sent = push("b", [_scatter_dw_in(dw_in, lay, n_in)])
    dh = _matmul(dproj, wts["in"], tb=True, name="mm_dh", deps=sent)
    dx, dshift, dscale, dg_norm = _norm_mod_bwd(sv["x"], g_norm, scale, dh, dxn)
    dmod = jnp.concatenate([dshift, dscale, dgate], axis=1)
    small = dict(dmod=dmod, g_norm=dg_norm, g_cq=dg_cq, g_ckv=dg_ckv)
    return dx, small


def kernel(x, c, positions, w_mod, b_mod, g_norm, w_in, g_cq, g_ckv, w_uq, w_ukv, w_ret_proj, w_mla_proj, w_out, g_final, loss_target, m_w_mod, m_b_mod, m_g_norm, m_w_in, m_g_cq, m_g_ckv, m_w_uq, m_w_ukv, m_w_ret_proj, m_w_mla_proj, m_w_out, m_g_final, v_w_mod, v_b_mod, v_g_norm, v_w_in, v_g_cq, v_g_ckv, v_w_uq, v_w_ukv, v_w_ret_proj, v_w_mla_proj, v_w_out, v_g_final):
    nl, d, _ = w_mod.shape
    s = x.shape[1]
    rank = g_cq.shape[1]
    lay = Layout(d, rank, g_ckv.shape[1])
    me = _slot(*_place())
    x0 = x.reshape(s, d)
    target = loss_target.reshape(s, d)
    tabs = _rope_tables(positions.reshape(s))
    lg = jnp.log(1.0 - 2.0 ** (-5.0 - jnp.arange(RET_HEADS, dtype=F32)))

    c_act = c * _sigmoid(c)
    (c_all,) = _all_gather([c_act.reshape(d // LANES, LANES)], None, "gather_c", vmem=True)
    c_all = c_all.reshape(N_DEV, d)
    n_mod = w_mod.shape[2]
    mod_part = jnp.stack([_matmul(c_all, w_mod[l], name="mm_mod", tm_cap=8) for l in range(nl)])
    (mod_all,) = _all_gather([mod_part.reshape(-1, LANES)], None, "gather_mod", vmem=True)
    mod_all = mod_all.reshape(N_DEV, nl, N_DEV, n_mod)
    mod = lax.dynamic_index_in_dim(mod_all, me, axis=2, keepdims=False)
    mod = mod.transpose(1, 0, 2).reshape(nl, N_DEV * n_mod) + b_mod

    shards = [[w[l].astype(BF16) for w in (w_in, w_uq, w_ukv, w_ret_proj, w_mla_proj, w_out)] for l in range(nl)]
    xl, saved = x0, []
    (g_in,) = _all_gather([shards[0][0]], None, "gather_w_in")
    fly = {"late": _push_start(shards[0][1:], False, [g_in, mod], "gather_start_0")}
    deps = [fly["late"][3]]
    for l in range(nl):
        def other_weights(proj, l=l):
            if fly.get("late") is not None:
                sems, srcs, lands, _ = fly.pop("late")
                fly["rest"] = _push_wait(sems, srcs, lands, False, [proj], "gather_wait_%d" % l)
            g_uq, g_ukv, g_ret, g_mla, g_out = fly["rest"]
            sent = []
            if l + 1 < nl:
                fly["next"] = _push_start(shards[l + 1], False, [g_out, proj], "gather_start_%d" % (l + 1))
                sent = [fly["next"][3]]
            return {"uq": _uq_to_physical(_from_owner_blocks_cols(g_uq)),
                    "ukv": _ukv_to_physical(_from_owner_blocks_cols(g_ukv)),
                    "ret": g_ret.reshape(-1, d), "mla": g_mla.reshape(-1, d), "out": g_out.reshape(-1, d)}, sent

        xl, sv = _layer_fwd(xl, mod[l:l + 1], g_norm[l:l + 1], g_cq[l:l + 1], g_ckv[l:l + 1],
                            _assemble_w_in(g_in, lay), other_weights, tabs, lg, lay, deps)
        saved.append(sv)
        deps = []
        if l + 1 < nl:
            sems, srcs, lands, _ = fly.pop("next")
            g_in, *fly["rest"] = _push_wait(sems, srcs, lands, False, [xl], "gather_wait_%d" % (l + 1))
    loss_lanes, dx, dg_final = _final_loss(xl, g_final.reshape(1, d), target)

    small = [None] * nl
    flying = {l: [] for l in range(nl)}
    recv = {}
    shard_cols = (w_in.shape[2], w_uq.shape[2], w_ukv.shape[2])

    def pusher(l):
        def push(group, arrays):
            sems, srcs, lands, token = _push_start(arrays, True, [], "exchange_start_%d%s" % (l, group))
            flying[l].append((group, sems, srcs, lands))
            return [token]
        return push

    def land(l, after):
        got = {}
        for group, sems, srcs, lands in flying[l]:
            got[group] = _push_wait(sems, srcs, lands, True, after, "exchange_wait_%d%s" % (l, group))
        recv[l] = list(got["b"]) + list(got["a"])

    for l in reversed(range(nl)):
        dx, small[l] = _layer_bwd(dx, saved[l], mod[l:l + 1], g_norm[l:l + 1], g_cq[l:l + 1], g_ckv[l:l + 1],
                                  saved[l]["wts"], tabs, lg, lay, [], shard_cols, pusher(l))
        if l + 1 < nl:
            land(l + 1, [dx])
    grad_x = dx.reshape(x.shape)

    out = {}
    w_in_t, m_w_in_t, v_w_in_t = (jnp.swapaxes(a, 1, 2) for a in (w_in, m_w_in, v_w_in))
    sharded = (("w_in", w_in_t, m_w_in_t, v_w_in_t), ("w_uq", w_uq, m_w_uq, v_w_uq), ("w_ukv", w_ukv, m_w_ukv, v_w_ukv),
               ("w_ret_proj", w_ret_proj, m_w_ret_proj, v_w_ret_proj),
               ("w_mla_proj", w_mla_proj, m_w_mla_proj, v_w_mla_proj), ("w_out", w_out, m_w_out, v_w_out))
    if nl > 1:
        for i, (key, w, m, v) in enumerate(sharded):
            out[key] = _adam_sharded([recv[l][i] for l in range(1, nl)], 1, w, m, v, None, "adam_" + key)
    done = [out[key][0] for key, _, _, _ in sharded if key in out]

    pack = jnp.concatenate(
        [jnp.concatenate([sm[k] for sm in small], axis=0).reshape(-1)
         for k in ("dmod", "g_norm", "g_cq", "g_ckv")] + [dg_final.reshape(-1), loss_lanes.reshape(-1)])
    (pack_all,) = _all_gather([pack.reshape(-1, LANES)], None, "gather_small", vmem=True, deps=done)
    tot = _sum_devices(pack_all).reshape(-1)
    sizes = [nl * 3 * d, nl * d, nl * rank, nl * rank, d]
    offs = np.cumsum([0] + sizes)
    grad_b_mod = tot[offs[0]:offs[1]].reshape(nl, 3 * d)
    grad_g_norm = tot[offs[1]:offs[2]].reshape(nl, d)
    grad_g_cq = tot[offs[2]:offs[3]].reshape(nl, rank)
    grad_g_ckv = tot[offs[3]:offs[4]].reshape(nl, rank)
    grad_g_final = tot[offs[4]:offs[5]]
    loss = tot[offs[5]]
    dmod_all = pack_all.reshape(N_DEV, -1)[:, :sizes[0]].reshape(N_DEV, nl, 3 * d)
    dmod_mine = lax.dynamic_slice_in_dim(dmod_all, me * n_mod, n_mod, axis=2).transpose(1, 0, 2)

    out["w_mod"] = _adam_mod(c_all.T, dmod_mine, w_mod, m_w_mod, v_w_mod)
    land(0, [out["w_mod"][0]])
    for i, (key, w, m, v) in enumerate(sharded):
        out[key] = _adam_sharded([recv[0][i]], 0, w, m, v, out.get(key), "adam0_" + key)
    out["w_in"] = tuple(jnp.swapaxes(a, 1, 2) for a in out["w_in"])
    for key, g, w, m, v in (("b_mod", grad_b_mod, b_mod, m_b_mod, v_b_mod),
                            ("g_norm", grad_g_norm, g_norm, m_g_norm, v_g_norm),
                            ("g_cq", grad_g_cq, g_cq, m_g_cq, v_g_cq),
                            ("g_ckv", grad_g_ckv, g_ckv, m_g_ckv, v_g_ckv),
                            ("g_final", grad_g_final.reshape(1, d), g_final.reshape(1, d),
                             m_g_final.reshape(1, d), v_g_final.reshape(1, d))):
        out[key] = (g,) + tuple(_adam_small(g, w, m, v, "adam_" + key))
    out["g_final"] = tuple(a.reshape(d) for a in out["g_final"])

    names = ("w_mod", "b_mod", "g_norm", "w_in", "g_cq", "g_ckv", "w_uq", "w_ukv", "w_ret_proj",
             "w_mla_proj", "w_out", "g_final")
    return (loss, grad_x, *[out[k][0] for k in names], *[out[k][1] for k in names],
            *[out[k][2] for k in names], *[out[k][3] for k in names])
```

```python
import functools
import itertools

import jax
import jax.numpy as jnp
import numpy as np
from jax import lax
from jax.experimental import pallas as pl
from jax.experimental.pallas import tpu as pltpu

F32 = jnp.float32
BF16 = jnp.bfloat16

N_DEV = 8
CHUNK = 64
EPS = 1e-6
NEG_INF = -1e30
ROPE_BASE = 10000.0
LANES = 128

RET_HEADS = 8
RET_DK = 128
RET_DV = 256
MLA_HEADS = 16
MLA_NOPE = 128
MLA_ROPE = 64
MLA_DV = 128
MLA_QW = 256
QK_SCALE = (MLA_NOPE + MLA_ROPE) ** -0.5
QK_LOG2_SCALE = QK_SCALE * 1.4426950408889634
LN2 = 0.6931471805599453

ADAM_LR = 0.001
ADAM_B1 = 0.9
ADAM_B2 = 0.999
ADAM_EPS = 1e-08
ADAM_WD = 0.01
ADAM_STEP = 10

VMEM_LIMIT_BYTES = 56 * 1024 * 1024
MESH = pl.DeviceIdType.MESH
ANY = pl.BlockSpec(memory_space=pl.ANY)


def _cp(*sem):
    return pltpu.CompilerParams(dimension_semantics=sem if sem else None,
                                vmem_limit_bytes=VMEM_LIMIT_BYTES)


def _tile(n, cap):
    best = None
    t = LANES
    while t <= min(n, cap):
        if n % t == 0:
            best = t
        t += LANES
    return best if best is not None else n


def _rtile(n, cap):
    t = cap
    while t > 8 and n % t:
        t //= 2
    return t if n % t == 0 else n


def _sigmoid(x):
    return 1.0 / (1.0 + jnp.exp(-x))


def _dot(a, b):
    return lax.dot_general(a, b, (((1,), (0,)), ((), ())), preferred_element_type=F32)


def _dot_nt(a, b):
    return lax.dot_general(a, b, (((1,), (1,)), ((), ())), preferred_element_type=F32)


def _dot_tn(a, b):
    return lax.dot_general(a, b, (((0,), (0,)), ((), ())), preferred_element_type=F32)


def _roll(x, s):
    return pltpu.roll(x, s, 1)


class Layout:
    def __init__(self, d_model, q_rank, kv_rank):
        assert q_rank == kv_rank
        self.d = d_model
        self.rank = q_rank
        self.ret_w = 2 * RET_DK + 2 * RET_DV
        self.ret_qk = RET_HEADS * RET_DK
        self.ret_v = RET_HEADS * RET_DV
        self.mla_v = MLA_HEADS * MLA_DV
        widths = {"bg": 2 * d_model, "mg": self.mla_v, "ret": RET_HEADS * self.ret_w,
                  "cqkv": 2 * q_rank, "kr": LANES}
        blocks = {"bg": 2 * d_model, "mg": self.mla_v, "ret": self.ret_w,
                  "cqkv": 2 * q_rank, "kr": LANES}
        for order in itertools.permutations(widths):
            off, offs, ok = 0, {}, True
            for name in order:
                if off % blocks[name]:
                    ok = False
                    break
                offs[name] = off
                off += widths[name]
            if ok:
                break
        assert ok, "no aligned layout"
        self.order, self.off, self.width, self.total = order, offs, widths, off
        lo, o = {}, 0
        for name, w in (("rq", self.ret_qk), ("rk", self.ret_qk), ("rv", self.ret_v),
                        ("rg", self.ret_v), ("cq", q_rank), ("ckv", kv_rank), ("kr", MLA_ROPE),
                        ("mg", self.mla_v), ("bg", 2 * d_model)):
            lo[name] = (o, w)
            o += w
        self.logical, self.d_in = lo, o

    def pieces(self):
        lo = self.logical
        out = []
        for name in self.order:
            if name == "bg":
                out.append(lo["bg"])
            elif name == "mg":
                out.append(lo["mg"])
            elif name == "ret":
                for h in range(RET_HEADS):
                    out.append((lo["rq"][0] + h * RET_DK, RET_DK))
                    out.append((lo["rk"][0] + h * RET_DK, RET_DK))
                    out.append((lo["rv"][0] + h * RET_DV, RET_DV))
                    out.append((lo["rg"][0] + h * RET_DV, RET_DV))
            elif name == "cqkv":
                out.append((lo["cq"][0], 2 * self.rank))
            elif name == "kr":
                out.append(lo["kr"])
                out.append((None, LANES - MLA_ROPE))
        return out


RELAYOUT_CHUNK = 512


def _relayout_plan(lay, n_local):
    plan, off = [], 0
    for start, width in lay.pieces():
        done = 0
        while done < width:
            w = min(RELAYOUT_CHUNK, width - done)
            srcs = []
            if start is not None:
                lo, hi = start + done, start + done + w
                while lo < hi:
                    j = lo // n_local
                    end = min(hi, (j + 1) * n_local)
                    srcs.append((j, lo - j * n_local, end - j * n_local))
                    lo = end
            plan.append((off + done, w, srcs))
            done += w
        off += width
    merged = []
    for p, w, srcs in plan:
        if merged and merged[-1][0] % LANES == 0 and (merged[-1][1] % LANES) and p == merged[-1][0] + merged[-1][1]:
            q, qw, qs = merged.pop()
            merged.append((q, qw + w, qs + ([("pad", w)] if not srcs else srcs)))
        else:
            merged.append((p, w, srcs))
    return merged


def _assemble_w_in(g, lay):
    _, d, n_local = g.shape
    tr = _rtile(d, 256)
    plan = _relayout_plan(lay, n_local)

    def body(g_ref, o_ref):
        for p, w, srcs in plan:
            parts = []
            for src in srcs:
                if src[0] == "pad":
                    parts.append(jnp.zeros((tr, src[1]), F32))
                else:
                    j, a, b = src
                    parts.append(g_ref[j, :, a:b].astype(F32))
            if not parts:
                parts = [jnp.zeros((tr, w), F32)]
            val = parts[0] if len(parts) == 1 else jnp.concatenate(parts, axis=1)
            o_ref[:, p:p + w] = val.astype(o_ref.dtype)

    return pl.pallas_call(
        body, name="assemble_w_in", grid=(d // tr,),
        in_specs=[pl.BlockSpec((N_DEV, tr, n_local), lambda i: (0, i, 0))],
        out_specs=pl.BlockSpec((tr, lay.total), lambda i: (i, 0)),
        out_shape=jax.ShapeDtypeStruct((d, lay.total), g.dtype),
        compiler_params=_cp("parallel"),
    )(g)


def _scatter_dw_in(dw, lay, n_local):
    d = dw.shape[0]
    tr = _rtile(d, 256)
    n_pad = -(-n_local // LANES) * LANES
    plan = _relayout_plan(lay, n_local)
    runs = [[] for _ in range(N_DEV)]
    for p, w, srcs in plan:
        at = p
        for src in srcs:
            if src[0] == "pad":
                at += src[1]
                continue
            j, a, b = src
            runs[j].append((a, b, at))
            at += b - a
    for r in runs:
        r.sort()

    def body(dw_ref, o_ref):
        for j in range(N_DEV):
            parts = [dw_ref[:, at:at + (b - a)].astype(F32) for a, b, at in runs[j]]
            if n_pad > n_local:
                parts.append(jnp.zeros((tr, n_pad - n_local), F32))
            val = jnp.concatenate(parts, axis=1).T
            o_ref[j] = val[:n_local, :].astype(BF16)

    return pl.pallas_call(
        body, name="scatter_dw_in", grid=(d // tr,),
        in_specs=[pl.BlockSpec((tr, lay.total), lambda i: (i, 0))],
        out_specs=pl.BlockSpec((N_DEV, n_local, tr), lambda i: (0, 0, i)),
        out_shape=jax.ShapeDtypeStruct((N_DEV, n_local, d), BF16),
        compiler_params=_cp("parallel"),
    )(dw)


def _uq_to_physical(w):
    k = w.shape[0]
    w3 = w.reshape(k, MLA_HEADS, MLA_NOPE + MLA_ROPE)
    pad = jnp.zeros((k, MLA_HEADS, MLA_QW - MLA_NOPE - MLA_ROPE), w.dtype)
    return jnp.concatenate([w3, pad], axis=2).reshape(k, MLA_HEADS * MLA_QW)


def _uq_to_logical(w):
    k = w.shape[0]
    return w.reshape(k, MLA_HEADS, MLA_QW)[:, :, :MLA_NOPE + MLA_ROPE].reshape(k, -1)


def _ukv_to_physical(w):
    k = w.shape[0]
    w3 = w.reshape(k, MLA_HEADS, MLA_NOPE + MLA_DV)
    return jnp.concatenate([w3[:, :, :MLA_NOPE].reshape(k, -1), w3[:, :, MLA_NOPE:].reshape(k, -1)], axis=1)


def _ukv_to_logical(w):
    k = w.shape[0]
    kn = w[:, :MLA_HEADS * MLA_NOPE].reshape(k, MLA_HEADS, MLA_NOPE)
    v = w[:, MLA_HEADS * MLA_NOPE:].reshape(k, MLA_HEADS, MLA_DV)
    return jnp.concatenate([kn, v], axis=2).reshape(k, -1)


def _matmul(a, b, *, ta=False, tb=False, out_dtype=F32, name, tm_cap=1024, tn_cap=1024, tk_cap=2048, deps=()):
    m, k = (a.shape[1], a.shape[0]) if ta else a.shape
    n = b.shape[0] if tb else b.shape[1]
    assert k == (b.shape[1] if tb else b.shape[0])
    tm, tn, tk = _tile(m, tm_cap), _tile(n, tn_cap), _tile(k, tk_cap)
    nk = k // tk

    def body(a_ref, b_ref, *rest):
        dims = (((0 if ta else 1,), (1 if tb else 0,)), ((), ()))
        part = lax.dot_general(a_ref[...].astype(BF16), b_ref[...].astype(BF16), dims, preferred_element_type=F32)
        if nk == 1:
            rest[len(deps)][...] = part.astype(out_dtype)
            return
        o_ref, acc_ref = rest[len(deps):]
        kk = pl.program_id(2)

        @pl.when(kk == 0)
        def _():
            acc_ref[...] = part

        @pl.when(jnp.logical_and(kk > 0, kk < nk - 1))
        def _():
            acc_ref[...] += part

        @pl.when(kk == nk - 1)
        def _():
            o_ref[...] = (acc_ref[...] + part).astype(o_ref.dtype)

    a_spec = pl.BlockSpec((tk, tm), lambda i, j, kk: (kk, i)) if ta else pl.BlockSpec((tm, tk), lambda i, j, kk: (i, kk))
    b_spec = pl.BlockSpec((tn, tk), lambda i, j, kk: (j, kk)) if tb else pl.BlockSpec((tk, tn), lambda i, j, kk: (kk, j))
    return pl.pallas_call(
        body, name=name, grid=(m // tm, n // tn, nk),
        in_specs=[a_spec, b_spec] + [ANY] * len(deps),
        out_specs=pl.BlockSpec((tm, tn), lambda i, j, kk: (i, j)),
        out_shape=jax.ShapeDtypeStruct((m, n), out_dtype),
        scratch_shapes=[pltpu.VMEM((tm, tn), F32)] if nk > 1 else [],
        compiler_params=_cp("parallel", "parallel", "arbitrary"),
    )(a, b, *deps)


def _row(tm, w, cb=0):
    return pl.BlockSpec((tm, w), lambda i, cb=cb: (i, cb))


def _vec(w, cb=0):
    return pl.BlockSpec((1, w), lambda i, cb=cb: (0, cb))


def _dproj_out(dproj, s, lay, tm, name):
    w = lay.width[name]
    cb = lay.off[name] // w
    spec = _row(tm, w, cb)
    shape = jax.ShapeDtypeStruct((s, lay.total), BF16)
    return spec, shape


def _norm_mod_fwd(x, g, scale, shift, deps):
    s, d = x.shape
    tm = _rtile(s, 256)

    def body(x_ref, g_ref, sc_ref, sh_ref, *rest):
        h_ref = rest[len(deps)]
        xv = x_ref[...]
        xh = xv * lax.rsqrt(jnp.mean(xv * xv, axis=-1, keepdims=True) + EPS)
        h_ref[...] = ((xh * g_ref[...]) * (1.0 + sc_ref[...]) + sh_ref[...]).astype(BF16)

    return pl.pallas_call(
        body, name="norm_mod_fwd", grid=(s // tm,),
        in_specs=[_row(tm, d), _vec(d), _vec(d), _vec(d)] + [ANY] * len(deps),
        out_specs=_row(tm, d), out_shape=jax.ShapeDtypeStruct((s, d), BF16),
        compiler_params=_cp("parallel"),
    )(x, g, scale, shift, *deps)


def _norm_mod_bwd(x, g, scale, dh, dres):
    s, d = x.shape
    tm = _rtile(s, 256)

    def body(x_ref, g_ref, sc_ref, dh_ref, dres_ref, dx_ref, dsh_ref, dsc_ref, dg_ref):
        @pl.when(pl.program_id(0) == 0)
        def _():
            dsh_ref[...] = jnp.zeros_like(dsh_ref)
            dsc_ref[...] = jnp.zeros_like(dsc_ref)
            dg_ref[...] = jnp.zeros_like(dg_ref)

        xv, gv, dhv = x_ref[...], g_ref[...], dh_ref[...]
        rstd = lax.rsqrt(jnp.mean(xv * xv, axis=-1, keepdims=True) + EPS)
        xh = xv * rstd
        dy = dhv * (1.0 + sc_ref[...])
        dxh = dy * gv
        dx_ref[...] = dres_ref[...] + rstd * (dxh - xh * jnp.mean(dxh * xh, axis=-1, keepdims=True))
        dsh_ref[...] += jnp.sum(dhv, axis=0, keepdims=True)
        dsc_ref[...] += jnp.sum(dhv * (xh * gv), axis=0, keepdims=True)
        dg_ref[...] += jnp.sum(dy * xh, axis=0, keepdims=True)

    vec = jax.ShapeDtypeStruct((1, d), F32)
    return pl.pallas_call(
        body, name="norm_mod_bwd", grid=(s // tm,),
        in_specs=[_row(tm, d), _vec(d), _vec(d), _row(tm, d), _row(tm, d)],
        out_specs=[_row(tm, d), _vec(d), _vec(d), _vec(d)],
        out_shape=[jax.ShapeDtypeStruct((s, d), F32), vec, vec, vec],
        compiler_params=_cp("arbitrary"),
    )(x, g, scale, dh, dres)


def _final_loss(x, g, target):
    s, d = x.shape
    tm = _rtile(s, 256)

    def body(x_ref, g_ref, t_ref, l_ref, dx_ref, dg_ref):
        @pl.when(pl.program_id(0) == 0)
        def _():
            l_ref[...] = jnp.zeros_like(l_ref)
            dg_ref[...] = jnp.zeros_like(dg_ref)

        xv, gv = x_ref[...], g_ref[...]
        rstd = lax.rsqrt(jnp.mean(xv * xv, axis=-1, keepdims=True) + EPS)
        xh = xv * rstd
        err = xh * gv - t_ref[...]
        row = jnp.mean(err * err, axis=-1, keepdims=True)
        l_ref[...] += 0.5 * jnp.sum(row, axis=0, keepdims=True)
        dy = err / d
        dxh = dy * gv
        dx_ref[...] = rstd * (dxh - xh * jnp.mean(dxh * xh, axis=-1, keepdims=True))
        dg_ref[...] += jnp.sum(dy * xh, axis=0, keepdims=True)

    return pl.pallas_call(
        body, name="final_loss", grid=(s // tm,),
        in_specs=[_row(tm, d), _vec(d), _row(tm, d)],
        out_specs=[_vec(LANES), _row(tm, d), _vec(d)],
        out_shape=[jax.ShapeDtypeStruct((1, LANES), F32), jax.ShapeDtypeStruct((s, d), F32),
                   jax.ShapeDtypeStruct((1, d), F32)],
        compiler_params=_cp("arbitrary"),
    )(x, g, target)


def _resid_fwd(x, out, gate):
    s, d = x.shape
    tm = _rtile(s, 256)

    def body(x_ref, o_ref, g_ref, y_ref):
        y_ref[...] = x_ref[...] + g_ref[...] * o_ref[...]

    return pl.pallas_call(
        body, name="resid_fwd", grid=(s // tm,),
        in_specs=[_row(tm, d), _row(tm, d), _vec(d)],
        out_specs=_row(tm, d), out_shape=jax.ShapeDtypeStruct((s, d), F32),
        compiler_params=_cp("parallel"),
    )(x, out, gate)


def _resid_bwd(dxn, out, gate, deps):
    s, d = dxn.shape
    tm = _rtile(s, 256)

    def body(dx_ref, o_ref, g_ref, *rest):
        do_ref, dg_ref = rest[len(deps):]

        @pl.when(pl.program_id(0) == 0)
        def _():
            dg_ref[...] = jnp.zeros_like(dg_ref)

        dxv = dx_ref[...]
        do_ref[...] = (dxv * g_ref[...]).astype(BF16)
        dg_ref[...] += jnp.sum(dxv * o_ref[...], axis=0, keepdims=True)

    return pl.pallas_call(
        body, name="resid_bwd", grid=(s // tm,),
        in_specs=[_row(tm, d), _row(tm, d), _vec(d)] + [ANY] * len(deps),
        out_specs=[_row(tm, d), _vec(d)],
        out_shape=[jax.ShapeDtypeStruct((s, d), BF16), jax.ShapeDtypeStruct((1, d), F32)],
        compiler_params=_cp("arbitrary"),
    )(dxn, out, gate, *deps)


def _merge_fwd(y_ret, y_mla, proj, lay):
    s, d = y_ret.shape
    tm = _rtile(s, 256)
    cb = lay.off["bg"] // (2 * d)

    def body(a_ref, b_ref, bg_ref, m_ref):
        sg = _sigmoid(bg_ref[...])
        m_ref[...] = (sg[:, :d] * a_ref[...] + sg[:, d:] * b_ref[...]).astype(BF16)

    return pl.pallas_call(
        body, name="merge_fwd", grid=(s // tm,),
        in_specs=[_row(tm, d), _row(tm, d), _row(tm, 2 * d, cb)],
        out_specs=_row(tm, d), out_shape=jax.ShapeDtypeStruct((s, d), BF16),
        compiler_params=_cp("parallel"),
    )(y_ret, y_mla, proj)


def _merge_bwd(dm, y_ret, y_mla, proj, lay):
    s, d = dm.shape
    tm = _rtile(s, 256)
    cb = lay.off["bg"] // (2 * d)
    dp_spec, dp_shape = _dproj_out(None, s, lay, tm, "bg")

    def body(dm_ref, a_ref, b_ref, bg_ref, da_ref, db_ref, dp_ref):
        sg = _sigmoid(bg_ref[...])
        dmv = dm_ref[...]
        ga, gb = sg[:, :d], sg[:, d:]
        da_ref[...] = (dmv * ga).astype(BF16)
        db_ref[...] = (dmv * gb).astype(BF16)
        dp_ref[:, :d] = (dmv * a_ref[...] * ga * (1.0 - ga)).astype(BF16)
        dp_ref[:, d:] = (dmv * b_ref[...] * gb * (1.0 - gb)).astype(BF16)

    act = jax.ShapeDtypeStruct((s, d), BF16)
    return pl.pallas_call(
        body, name="merge_bwd", grid=(s // tm,),
        in_specs=[_row(tm, d), _row(tm, d), _row(tm, d), _row(tm, 2 * d, cb)],
        out_specs=[_row(tm, d), _row(tm, d), dp_spec],
        out_shape=[act, act, dp_shape],
        compiler_params=_cp("parallel"),
    )(dm, y_ret, y_mla, proj)


def _mla_prep(proj, g_cq, g_ckv, lay, deps):
    s = proj.shape[0]
    r = lay.rank
    tm = _rtile(s, 512)
    cb = lay.off["cqkv"] // (2 * r)

    def body(p_ref, gq_ref, gk_ref, *rest):
        q_ref, k_ref = rest[len(deps):]
        pv = p_ref[...]
        for lo, g_ref, o_ref in ((0, gq_ref, q_ref), (r, gk_ref, k_ref)):
            xv = pv[:, lo:lo + r]
            xh = xv * lax.rsqrt(jnp.mean(xv * xv, axis=-1, keepdims=True) + EPS)
            o_ref[...] = (xh * g_ref[...]).astype(BF16)

    act = jax.ShapeDtypeStruct((s, r), BF16)
    return pl.pallas_call(
        body, name="mla_prep", grid=(s // tm,),
        in_specs=[_row(tm, 2 * r, cb), _vec(r), _vec(r)] + [ANY] * len(deps),
        out_specs=[_row(tm, r), _row(tm, r)], out_shape=[act, act],
        compiler_params=_cp("parallel"),
    )(proj, g_cq, g_ckv, *deps)


def _mla_prep_bwd(proj, dqn, dkn, g_cq, g_ckv, dproj, lay):
    s = proj.shape[0]
    r = lay.rank
    tm = _rtile(s, 512)
    cb = lay.off["cqkv"] // (2 * r)
    dp_spec, dp_shape = _dproj_out(dproj, s, lay, tm, "cqkv")

    def body(p_ref, dq_ref, dk_ref, gq_ref, gk_ref, _, dp_ref, dgq_ref, dgk_ref):
        @pl.when(pl.program_id(0) == 0)
        def _():
            dgq_ref[...] = jnp.zeros_like(dgq_ref)
            dgk_ref[...] = jnp.zeros_like(dgk_ref)

        pv = p_ref[...]
        for lo, g_ref, d_ref, dg_ref in ((0, gq_ref, dq_ref, dgq_ref), (r, gk_ref, dk_ref, dgk_ref)):
            xv = pv[:, lo:lo + r]
            rstd = lax.rsqrt(jnp.mean(xv * xv, axis=-1, keepdims=True) + EPS)
            xh = xv * rstd
            dy = d_ref[...]
            dxh = dy * g_ref[...]
            dp_ref[:, lo:lo + r] = (rstd * (dxh - xh * jnp.mean(dxh * xh, axis=-1, keepdims=True))).astype(BF16)
            dg_ref[...] += jnp.sum(dy * xh, axis=0, keepdims=True)

    vec = jax.ShapeDtypeStruct((1, r), F32)
    return pl.pallas_call(
        body, name="mla_prep_bwd", grid=(s // tm,),
        in_specs=[_row(tm, 2 * r, cb), _row(tm, r), _row(tm, r), _vec(r), _vec(r), ANY],
        out_specs=[dp_spec, _vec(r), _vec(r)], out_shape=[dp_shape, vec, vec],
        input_output_aliases={5: 0},
        compiler_params=_cp("arbitrary"),
    )(proj, dqn, dkn, g_cq, g_ckv, dproj)


def _rope_tile(t, a, b, c):
    return t * a + _roll(t, 96) * b + _roll(t, 32) * c


def _rope_tile_bwd(dy, a, b, c):
    return dy * a + _roll(dy * b, 32) + _roll(dy * c, 96)


def _attn_block(s):
    return _rtile(s, 512)


def _qk_prep(qp, kvp, proj, ta, tb, tc, lay):
    s = qp.shape[0]
    hq = MLA_HEADS * MLA_QW
    hv = MLA_HEADS * MLA_DV
    blk = _attn_block(s)
    tm = _rtile(blk, 256)
    per = blk // tm
    kr_cb = lay.off["kr"] // LANES

    def body(q_ref, kv_ref, kr_ref, a_ref, b_ref, c_ref, qc_ref, kc_ref, v_ref, kt_ref, vt_ref):
        a, b, c = a_ref[...], b_ref[...], c_ref[...]
        krot = _rope_tile(kr_ref[...], a, b, c)
        krot_b, krot_t = krot.astype(BF16), krot.T.astype(BF16)
        for h in range(MLA_HEADS):
            q0 = h * MLA_QW
            qc_ref[:, q0:q0 + MLA_NOPE] = (q_ref[:, q0:q0 + MLA_NOPE] * QK_LOG2_SCALE).astype(BF16)
            qc_ref[:, q0 + MLA_NOPE:q0 + MLA_QW] = (
                _rope_tile(q_ref[:, q0 + MLA_NOPE:q0 + MLA_QW], a, b, c) * QK_LOG2_SCALE).astype(BF16)
            kn = kv_ref[:, h * MLA_NOPE:(h + 1) * MLA_NOPE]
            kc_ref[:, q0:q0 + MLA_NOPE] = kn.astype(BF16)
            kc_ref[:, q0 + MLA_NOPE:q0 + MLA_QW] = krot_b
            kt_ref[h, :MLA_NOPE, :] = kn.T.astype(BF16)
            kt_ref[h, MLA_NOPE:, :] = krot_t
            vh = kv_ref[:, (MLA_HEADS + h) * MLA_NOPE:(MLA_HEADS + h + 1) * MLA_NOPE]
            v_ref[:, h * MLA_DV:(h + 1) * MLA_DV] = vh.astype(BF16)
            vt_ref[h] = vh.T.astype(BF16)

    return pl.pallas_call(
        body, name="qk_prep", grid=(s // tm,),
        in_specs=[_row(tm, hq), _row(tm, hq), _row(tm, LANES, kr_cb), _row(tm, LANES), _row(tm, LANES), _row(tm, LANES)],
        out_specs=[_row(tm, hq), _row(tm, hq), _row(tm, hv),
                   pl.BlockSpec((MLA_HEADS, None, MLA_QW, tm), lambda i: (0, i // per, 0, i % per)),
                   pl.BlockSpec((MLA_HEADS, None, MLA_DV, tm), lambda i: (0, i // per, 0, i % per))],
        out_shape=[jax.ShapeDtypeStruct((s, hq), BF16), jax.ShapeDtypeStruct((s, hq), BF16),
                   jax.ShapeDtypeStruct((s, hv), BF16),
                   jax.ShapeDtypeStruct((MLA_HEADS, s // blk, MLA_QW, blk), BF16),
                   jax.ShapeDtypeStruct((MLA_HEADS, s // blk, MLA_DV, blk), BF16)],
        compiler_params=_cp("parallel"),
    )(qp, kvp, proj, ta, tb, tc)


def _kv_bwd_prep(dk_cat, dv, ta, tb, tc, dproj, lay):
    s = dk_cat.shape[0]
    hq = MLA_HEADS * MLA_QW
    hv = MLA_HEADS * MLA_DV
    tm = _rtile(s, 256)
    dp_spec, dp_shape = _dproj_out(dproj, s, lay, tm, "kr")

    def body(dk_ref, dv_ref, a_ref, b_ref, c_ref, _, dkv_ref, dp_ref):
        acc = jnp.zeros((tm, LANES), F32)
        for h in range(MLA_HEADS):
            q0 = h * MLA_QW
            dkv_ref[:, h * MLA_NOPE:(h + 1) * MLA_NOPE] = dk_ref[:, q0:q0 + MLA_NOPE].astype(BF16)
            acc = acc + dk_ref[:, q0 + MLA_NOPE:q0 + MLA_QW]
        dkv_ref[:, MLA_HEADS * MLA_NOPE:] = dv_ref[...].astype(BF16)
        dp_ref[...] = _rope_tile_bwd(acc, a_ref[...], b_ref[...], c_ref[...]).astype(BF16)

    return pl.pallas_call(
        body, name="kv_bwd_prep", grid=(s // tm,),
        in_specs=[_row(tm, hq), _row(tm, hv), _row(tm, LANES), _row(tm, LANES), _row(tm, LANES), ANY],
        out_specs=[_row(tm, hq), dp_spec],
        out_shape=[jax.ShapeDtypeStruct((s, hq), BF16), dp_shape],
        input_output_aliases={5: 1},
        compiler_params=_cp("parallel"),
    )(dk_cat, dv, ta, tb, tc, dproj)


def _mla_gate_bwd(du, o, proj, dproj, lay):
    s, vw = du.shape
    tm = _rtile(s, 256)
    cb = lay.off["mg"] // vw
    dp_spec, dp_shape = _dproj_out(dproj, s, lay, tm, "mg")
    assert MLA_HEADS <= LANES

    def body(du_ref, o_ref, g_ref, _, do_ref, dl_ref, dp_ref):
        gv, duv, ov = g_ref[...], du_ref[...], o_ref[...]
        sg = _sigmoid(gv)
        do = (duv * (gv * sg)).astype(BF16)
        do_ref[...] = do
        dp_ref[...] = (duv * ov * (sg + gv * sg * (1.0 - sg))).astype(BF16)
        prod = do.astype(F32) * ov
        lane = lax.broadcasted_iota(jnp.int32, (tm, LANES), 1)
        delta = jnp.zeros((tm, LANES), F32)
        for h in range(MLA_HEADS):
            dh = jnp.sum(prod[:, h * MLA_DV:(h + 1) * MLA_DV], axis=-1, keepdims=True)
            delta = jnp.where(lane == h, dh, delta)
        dl_ref[...] = delta

    return pl.pallas_call(
        body, name="mla_gate_bwd", grid=(s // tm,),
        in_specs=[_row(tm, vw), _row(tm, vw), _row(tm, vw, cb), ANY],
        out_specs=[_row(tm, vw), _row(tm, LANES), dp_spec],
        out_shape=[jax.ShapeDtypeStruct((s, vw), BF16), jax.ShapeDtypeStruct((s, LANES), F32), dp_shape],
        input_output_aliases={3: 2},
        compiler_params=_cp("parallel"),
    )(du, o, proj, dproj)


RET_BLOCK = 256


def _ret_tables(lg, blk):
    ri = lax.broadcasted_iota(jnp.int32, (blk, blk), 0)
    ci = lax.broadcasted_iota(jnp.int32, (blk, blk), 1)
    col = lax.broadcasted_iota(jnp.int32, (blk, 1), 0).astype(F32)
    dist = jnp.abs(ri - ci).astype(F32)
    dmat = jnp.where(ci // CHUNK <= ri // CHUNK, jnp.exp(dist * lg), 0.0)
    xi = jnp.exp((col + 1.0) * lg)
    zeta = jnp.exp((blk - 1.0 - col) * lg)
    decay = jnp.exp(jnp.full((1, 1), blk, F32) * lg)
    return dmat, xi, zeta, decay


def _ret_qkvg(blk, cs, sn):
    dv = RET_DV
    q = blk[:, :RET_DK]
    k = blk[:, RET_DK:2 * RET_DK]
    q = q * cs + _roll(q, RET_DK // 2) * sn
    k = (k * cs + _roll(k, RET_DK // 2) * sn) * (RET_DK ** -0.5)
    return q, k, blk[:, 2 * RET_DK:2 * RET_DK + dv], blk[:, 2 * RET_DK + dv:]


def _group_norm(o):
    mu = jnp.mean(o, axis=-1, keepdims=True)
    oc = o - mu
    rstd = lax.rsqrt(jnp.mean(oc * oc, axis=-1, keepdims=True) + EPS)
    return oc * rstd, rstd


def _ret_fwd(proj, lg, cosr, sinr, lay, deps):
    s = proj.shape[0]
    dv, w = RET_DV, lay.ret_w
    tb = _rtile(s, 512)
    blk = min(RET_BLOCK, tb)
    nb, nch = s // tb, tb // blk
    cb0 = lay.off["ret"] // w

    def body(lg_ref, p_ref, cos_ref, sin_ref, *rest):
        o_ref, u_ref, st_ref, state = rest[len(deps):]

        @pl.when(pl.program_id(1) == 0)
        def _():
            state[...] = jnp.zeros_like(state)

        dmat, xi, zeta, decay = _ret_tables(lg_ref[pl.program_id(0)], blk)
        for c in range(nch):
            rows = slice(c * blk, (c + 1) * blk)
            q, k, v, g = _ret_qkvg(p_ref[rows, :], cos_ref[rows, :], sin_ref[rows, :])
            qb, kb, vb = q.astype(BF16), k.astype(BF16), v.astype(BF16)
            sc = _dot_nt(qb, kb) * dmat
            st = state[...]
            o = _dot(sc.astype(BF16), vb) + _dot((q * xi).astype(BF16), st.astype(BF16))
            st_ref[c] = st.astype(BF16)
            state[...] = st * decay + _dot_tn((k * zeta).astype(BF16), vb)
            o_ref[rows, :] = o
            n, _ = _group_norm(o)
            u_ref[rows, :] = (n * (g * _sigmoid(g))).astype(BF16)

    return pl.pallas_call(
        body, name="ret_fwd", grid=(RET_HEADS, nb),
        in_specs=[pl.BlockSpec(memory_space=pltpu.SMEM),
                  pl.BlockSpec((tb, w), lambda h, b: (b, cb0 + h)),
                  pl.BlockSpec((tb, RET_DK), lambda h, b: (b, 0)),
                  pl.BlockSpec((tb, RET_DK), lambda h, b: (b, 0))] + [ANY] * len(deps),
        out_specs=[pl.BlockSpec((tb, dv), lambda h, b: (b, h)),
                   pl.BlockSpec((tb, dv), lambda h, b: (b, h)),
                   pl.BlockSpec((None, nch, RET_DK, dv), lambda h, b: (h, b, 0, 0))],
        out_shape=[jax.ShapeDtypeStruct((s, RET_HEADS * dv), F32),
                   jax.ShapeDtypeStruct((s, RET_HEADS * dv), BF16),
                   jax.ShapeDtypeStruct((RET_HEADS, s // blk, RET_DK, dv), BF16)],
        scratch_shapes=[pltpu.VMEM((RET_DK, dv), F32)],
        compiler_params=_cp("parallel", "arbitrary"),
    )(lg, proj, cosr, sinr, *deps)


def _ret_bwd(proj, lg, cosr, sinr, o, du, states, dproj, lay):
    s = proj.shape[0]
    dv, w = RET_DV, lay.ret_w
    tb = _rtile(s, 512)
    blk = min(RET_BLOCK, tb)
    nb, nch = s // tb, tb // blk
    cb0 = lay.off["ret"] // w

    def body(lg_ref, p_ref, cos_ref, sin_ref, o_ref, du_ref, st_ref, _, dp_ref, dstate):
        @pl.when(pl.program_id(1) == 0)
        def _():
            dstate[...] = jnp.zeros_like(dstate)

        dmat, xi, zeta, decay = _ret_tables(lg_ref[pl.program_id(0)], blk)
        for c in reversed(range(nch)):
            rows = slice(c * blk, (c + 1) * blk)
            cs, sn = cos_ref[rows, :], sin_ref[rows, :]
            q, k, v, g = _ret_qkvg(p_ref[rows, :], cs, sn)
            qb, kb, vb = q.astype(BF16), k.astype(BF16), v.astype(BF16)
            n, rstd = _group_norm(o_ref[rows, :])
            sg = _sigmoid(g)
            duv = du_ref[rows, :]
            dn = duv * (g * sg)
            dg = duv * n * (sg + g * sg * (1.0 - sg))
            do = rstd * (dn - jnp.mean(dn, axis=-1, keepdims=True) - n * jnp.mean(dn * n, axis=-1, keepdims=True))
            dob = do.astype(BF16)
            rb = st_ref[c]
            drb = dstate[...].astype(BF16)
            sc = (_dot_nt(qb, kb) * dmat).astype(BF16)
            dsc = (_dot_nt(dob, vb) * dmat).astype(BF16)
            qx = (q * xi).astype(BF16)
            kz = (k * zeta).astype(BF16)
            dq = _dot(dsc, kb) + _dot_nt(dob, rb) * xi
            dk = (_dot_tn(dsc, qb) + _dot_nt(vb, drb) * zeta) * (RET_DK ** -0.5)
            dvv = _dot_tn(sc, dob) + _dot(kz, drb)
            dstate[...] = dstate[...] * decay + _dot_tn(qx, dob)
            dp_ref[rows, :RET_DK] = (dq * cs + _roll(dq * sn, RET_DK // 2)).astype(BF16)
            dp_ref[rows, RET_DK:2 * RET_DK] = (dk * cs + _roll(dk * sn, RET_DK // 2)).astype(BF16)
            dp_ref[rows, 2 * RET_DK:2 * RET_DK + dv] = dvv.astype(BF16)
            dp_ref[rows, 2 * RET_DK + dv:] = dg.astype(BF16)

    rev = lambda h, b: (nb - 1 - b, h)
    return pl.pallas_call(
        body, name="ret_bwd", grid=(RET_HEADS, nb),
        in_specs=[pl.BlockSpec(memory_space=pltpu.SMEM),
                  pl.BlockSpec((tb, w), lambda h, b: (nb - 1 - b, cb0 + h)),
                  pl.BlockSpec((tb, RET_DK), lambda h, b: (nb - 1 - b, 0)),
                  pl.BlockSpec((tb, RET_DK), lambda h, b: (nb - 1 - b, 0)),
                  pl.BlockSpec((tb, dv), rev),
                  pl.BlockSpec((tb, dv), rev),
                  pl.BlockSpec((None, nch, RET_DK, dv), lambda h, b: (h, nb - 1 - b, 0, 0)),
                  ANY],
        out_specs=pl.BlockSpec((tb, w), lambda h, b: (nb - 1 - b, cb0 + h)),
        out_shape=jax.ShapeDtypeStruct((s, lay.total), BF16),
        input_output_aliases={7: 0},
        scratch_shapes=[pltpu.VMEM((RET_DK, dv), F32)],
        compiler_params=_cp("parallel", "arbitrary"),
    )(lg, proj, cosr, sinr, o, du, states, dproj)


ATTN_HEADS_PER_STEP = 2


def _attn_mask(rows, cols, row0, col0, keys_on_rows):
    ri = (lax.broadcasted_iota(jnp.int32, (rows, cols), 0) + row0) // CHUNK
    ci = (lax.broadcasted_iota(jnp.int32, (rows, cols), 1) + col0) // CHUNK
    return ri <= ci if keys_on_rows else ci <= ri


def _attn_fwd(q_cat, k_cat, v_t, proj, lay):
    s = q_cat.shape[0]
    bq = _attn_block(s)
    nq = s // bq
    mg_cb = lay.off["mg"] // MLA_DV

    hp = ATTN_HEADS_PER_STEP
    assert MLA_HEADS % hp == 0 and mg_cb % hp == 0

    def body(q_ref, k_ref, vt_ref, g_ref, o_ref, u_ref, lse_ref):
        i = pl.program_id(1)
        half = bq // 2
        qs = [q_ref[:, a * MLA_QW:(a + 1) * MLA_QW] for a in range(hp)]

        def scores(a, j):
            r0 = pl.multiple_of(j * bq, bq)
            return _dot_nt(k_ref[pl.ds(r0, bq), a * MLA_QW:(a + 1) * MLA_QW], qs[a])

        def update(a, j, sc, m, l, acc):
            mn = jnp.maximum(m, jnp.max(sc, axis=0, keepdims=True))
            p = jnp.exp2(sc - mn)
            alpha = jnp.exp2(m - mn)
            l = alpha * l + jnp.sum(p, axis=0, keepdims=True)
            acc = alpha * acc + _dot(vt_ref[a, j], p.astype(BF16))
            return mn, l, acc

        def lanes_after(x, fill):
            return jnp.concatenate([jnp.full((x.shape[0], half), fill, F32), x], axis=1)

        def diag(a, m, l, acc):
            r0 = pl.multiple_of(i * bq, bq)
            kc = slice(a * MLA_QW, (a + 1) * MLA_QW)
            top = _dot_nt(k_ref[pl.ds(r0, half), kc], qs[a])
            bot = _dot_nt(k_ref[pl.ds(r0 + half, half), kc], qs[a][half:, :])
            top = jnp.where(_attn_mask(half, bq, 0, 0, True), top, NEG_INF)
            bot = jnp.where(_attn_mask(half, half, 0, 0, True), bot, NEG_INF)
            mn = jnp.maximum(m, jnp.maximum(jnp.max(top, axis=0, keepdims=True),
                                            lanes_after(jnp.max(bot, axis=0, keepdims=True), NEG_INF)))
            pt, pb = jnp.exp2(top - mn), jnp.exp2(bot - mn[:, half:])
            alpha = jnp.exp2(m - mn)
            l = alpha * l + jnp.sum(pt, axis=0, keepdims=True) + lanes_after(jnp.sum(pb, axis=0, keepdims=True), 0.0)
            vt = vt_ref[a, i]
            acc = (alpha * acc + _dot(vt[:, :half], pt.astype(BF16))
                   + lanes_after(_dot(vt[:, half:], pb.astype(BF16)), 0.0))
            return mn, l, acc

        def pair(j, carry, last):
            sa = [scores(a, j) for a in range(hp)]
            if last:
                return tuple(diag(a, *update(a, j, sa[a], *carry[a])) for a in range(hp))
            sb = [scores(a, j + 1) for a in range(hp)]
            carry = [update(a, j, sa[a], *carry[a]) for a in range(hp)]
            return tuple(update(a, j + 1, sb[a], *carry[a]) for a in range(hp))

        def single(carry):
            return tuple(diag(a, *carry[a]) for a in range(hp))

        init = tuple((jnp.full((1, bq), NEG_INF, F32), jnp.zeros((1, bq), F32), jnp.zeros((MLA_DV, bq), F32))
                     for _ in range(hp))
        carry = lax.fori_loop(0, i // 2, lambda t, c: pair(2 * t, c, False), init)
        carry = lax.cond(i % 2 == 1, lambda c: pair(i - 1, c, True), single, carry)
        for a, (m, l, acc) in enumerate(carry):
            cols = slice(a * MLA_DV, (a + 1) * MLA_DV)
            o = (acc / l).T
            gv = g_ref[:, cols]
            o_ref[:, cols] = o
            u_ref[:, cols] = (o * (gv * _sigmoid(gv))).astype(BF16)
            lse_ref[a] = m + jnp.log2(l)

    return pl.pallas_call(
        body, name="attn_fwd", grid=(MLA_HEADS // hp, nq),
        in_specs=[pl.BlockSpec((bq, hp * MLA_QW), lambda h, i: (i, h)),
                  pl.BlockSpec((s, hp * MLA_QW), lambda h, i: (0, h)),
                  pl.BlockSpec((hp, nq, MLA_DV, bq), lambda h, i: (h, 0, 0, 0)),
                  pl.BlockSpec((bq, hp * MLA_DV), lambda h, i: (i, mg_cb // hp + h))],
        out_specs=[pl.BlockSpec((bq, hp * MLA_DV), lambda h, i: (i, h)),
                   pl.BlockSpec((bq, hp * MLA_DV), lambda h, i: (i, h)),
                   pl.BlockSpec((hp, None, 1, bq), lambda h, i: (h, i, 0, 0))],
        out_shape=[jax.ShapeDtypeStruct((s, MLA_HEADS * MLA_DV), F32),
                   jax.ShapeDtypeStruct((s, MLA_HEADS * MLA_DV), BF16),
                   jax.ShapeDtypeStruct((MLA_HEADS, nq, 1, bq), F32)],
        compiler_params=_cp("parallel", "parallel"),
    )(q_cat, k_cat, v_t, proj)


def _attn_bwd(q_cat, k_cat, v, k_t, do, lse, delta, ta, tb, tc):
    s = q_cat.shape[0]
    blk = _attn_block(s)
    nb = s // blk
    hp = ATTN_HEADS_PER_STEP
    qw, dvw = hp * MLA_QW, hp * MLA_DV

    def body(q_ref, k_ref, v_ref, kt_ref, do_ref, lse_ref, dl_ref, a_ref, b_ref, c_ref,
             dq_ref, dk_ref, dv_ref, dq_acc):
        j = pl.program_id(1)

        @pl.when(j == 0)
        def _():
            dq_acc[...] = jnp.zeros_like(dq_acc)

        def qcols(a):
            return slice(a * MLA_QW, (a + 1) * MLA_QW)

        def vcols(a):
            return slice(a * MLA_DV, (a + 1) * MLA_DV)

        kbs = [k_ref[:, qcols(a)] for a in range(hp)]
        vbs = [v_ref[:, vcols(a)] for a in range(hp)]

        half = blk // 2

        def step(i, carry):
            r0 = pl.multiple_of(i * blk, blk)
            out = []
            for a in range(hp):
                dk, dvv = carry[a]
                q, dob = q_ref[pl.ds(r0, blk), qcols(a)], do_ref[pl.ds(r0, blk), vcols(a)]
                sc = _dot_nt(kbs[a], q)
                p = jnp.exp2(sc - lse_ref[a, i])
                dvv = dvv + _dot(p.astype(BF16), dob)
                ds = (p * (_dot_nt(vbs[a], dob) - dl_ref[a, i])).astype(BF16)
                dk = dk + _dot(ds, q)
                dq_acc[a, i] = dq_acc[a, i] + _dot(kt_ref[a], ds)
                out.append((dk, dvv))
            return tuple(out)

        def diag():
            r0 = pl.multiple_of(j * blk, blk)
            out = []
            for a in range(hp):
                q, dob = q_ref[pl.ds(r0, blk), qcols(a)], do_ref[pl.ds(r0, blk), vcols(a)]
                q1, do1 = q[half:, :], dob[half:, :]
                lse, dl, kt = lse_ref[a, j], dl_ref[a, j], kt_ref[a]
                top = jnp.where(_attn_mask(half, blk, 0, 0, True), _dot_nt(kbs[a][:half, :], q), NEG_INF)
                bot = jnp.where(_attn_mask(half, half, 0, 0, True), _dot_nt(kbs[a][half:, :], q1), NEG_INF)
                pt, pb = jnp.exp2(top - lse), jnp.exp2(bot - lse[:, half:])
                dst = (pt * (_dot_nt(vbs[a][:half, :], dob) - dl)).astype(BF16)
                dsb = (pb * (_dot_nt(vbs[a][half:, :], do1) - dl[:, half:])).astype(BF16)
                later = _dot(kt[:, half:], dsb)
                dq_acc[a, j] = (dq_acc[a, j] + _dot(kt[:, :half], dst)
                                + jnp.concatenate([jnp.zeros_like(later), later], axis=1))
                out.append((jnp.concatenate([_dot(dst, q), _dot(dsb, q1)], axis=0),
                            jnp.concatenate([_dot(pt.astype(BF16), dob), _dot(pb.astype(BF16), do1)], axis=0)))
            return tuple(out)

        carry = lax.fori_loop(j + 1, nb, step, diag())
        for a, (dk, dvv) in enumerate(carry):
            dk_ref[:, qcols(a)] = dk * LN2
            dv_ref[:, vcols(a)] = dvv

        @pl.when(j == nb - 1)
        def _():
            for a in range(hp):
                for i in range(nb):
                    rows = slice(i * blk, (i + 1) * blk)
                    dq = dq_acc[a, i].T * QK_SCALE
                    c0 = a * MLA_QW
                    dq_ref[rows, c0:c0 + MLA_NOPE] = dq[:, :MLA_NOPE].astype(BF16)
                    dq_ref[rows, c0 + MLA_NOPE:c0 + MLA_QW] = _rope_tile_bwd(
                        dq[:, MLA_NOPE:], a_ref[rows, :], b_ref[rows, :], c_ref[rows, :]).astype(BF16)

    tab = pl.BlockSpec((s, LANES), lambda h, j: (0, 0), pipeline_mode=pl.Buffered(1))
    rows = pl.BlockSpec((hp, nb, 1, blk), lambda h, j: (h, 0, 0, 0))
    return pl.pallas_call(
        body, name="attn_bwd", grid=(MLA_HEADS // hp, nb),
        in_specs=[pl.BlockSpec((s, qw), lambda h, j: (0, h)),
                  pl.BlockSpec((blk, qw), lambda h, j: (j, h)),
                  pl.BlockSpec((blk, dvw), lambda h, j: (j, h)),
                  pl.BlockSpec((hp, None, MLA_QW, blk), lambda h, j: (h, j, 0, 0)),
                  pl.BlockSpec((s, dvw), lambda h, j: (0, h)),
                  rows, rows, tab, tab, tab],
        out_specs=[pl.BlockSpec((s, qw), lambda h, j: (0, h)),
                   pl.BlockSpec((blk, qw), lambda h, j: (j, h)),
                   pl.BlockSpec((blk, dvw), lambda h, j: (j, h))],
        out_shape=[jax.ShapeDtypeStruct((s, MLA_HEADS * MLA_QW), BF16),
                   jax.ShapeDtypeStruct((s, MLA_HEADS * MLA_QW), F32),
                   jax.ShapeDtypeStruct((s, MLA_HEADS * MLA_DV), F32)],
        scratch_shapes=[pltpu.VMEM((hp, nb, MLA_QW, blk), F32)],
        compiler_params=_cp("parallel", "arbitrary"),
    )(q_cat, k_cat, v, k_t, do, lse, delta, ta, tb, tc)


def _place():
    return lax.axis_index("x"), lax.axis_index("y"), lax.axis_index("c")


def _slot(px, py, pc):
    return 4 * px + 2 * py + pc


def _all_gather(shards, layer, name, vmem=False, deps=()):
    n = len(shards)

    def body(*refs):
        srcs, outs = refs[:n], refs[n + len(deps):2 * n + len(deps)]
        send_sems, recv_sems, local_sems = refs[2 * n + len(deps):]
        x, y, c = _place()
        me, sibling = (x, y, c), (x, y, 1 - c)
        chips = [(1 - x, y), (x, 1 - y), (1 - x, 1 - y)]
        firsts, passes, locals_ = [], [], []

        def copy(a, k, block, to, src=None):
            dst = outs[a].at[_slot(*block)]
            return pltpu.make_async_remote_copy(
                src_ref=dst if src is None else src, dst_ref=dst,
                send_sem=send_sems.at[7 * a + k], recv_sem=recv_sems.at[7 * a + k],
                device_id=to, device_id_type=MESH)

        for a in range(n):
            src = srcs[a] if layer is None else srcs[a].at[layer]
            mine = pltpu.make_async_copy(src, outs[a].at[_slot(*me)], local_sems.at[a])
            mine.start()
            locals_.append(mine)
            first = [copy(a, 0, me, sibling, src=src)]
            first += [copy(a, 1 + j, me, (*chip, c), src=src) for j, chip in enumerate(chips)]
            for cp in first:
                cp.start()
            firsts += first
        for a in range(n):
            for j, chip in enumerate(chips):
                copy(a, 1 + j, (*chip, c), me).wait_recv()
                fwd = copy(a, 4 + j, (*chip, c), sibling)
                fwd.start()
                passes.append(fwd)
        for a in range(n):
            copy(a, 0, sibling, me).wait_recv()
            for j, chip in enumerate(chips):
                copy(a, 4 + j, (*chip, 1 - c), me).wait_recv()
        for cp in firsts + passes:
            cp.wait_send()
        for mine in locals_:
            mine.wait()

    space = pl.BlockSpec(memory_space=pltpu.VMEM) if vmem else ANY
    out_shape = [jax.ShapeDtypeStruct((N_DEV,) + (a.shape if layer is None else a.shape[1:]), a.dtype) for a in shards]
    return pl.pallas_call(
        body, name=name,
        in_specs=[space] * n + [ANY] * len(deps), out_specs=[space] * n, out_shape=out_shape,
        scratch_shapes=[pltpu.SemaphoreType.DMA((7 * n,)), pltpu.SemaphoreType.DMA((7 * n,)),
                        pltpu.SemaphoreType.DMA((n,))],
        compiler_params=pltpu.CompilerParams(has_side_effects=True),
    )(*shards, *deps)


HBM = pl.BlockSpec(memory_space=pltpu.HBM)
SEM = pl.BlockSpec(memory_space=pltpu.SEMAPHORE)
EFFECT = pltpu.SideEffectType.DATAFLOW_SIDE_EFFECTING


def _push_copies(srcs, lands, send_sems, recv_sems, local_sems, by_peer):
    x, y, c = _place()
    me = _slot(x, y, c)
    local, remote = [], []
    for a, (src, land) in enumerate(zip(srcs, lands)):
        local.append(pltpu.make_async_copy(src.at[me] if by_peer else src, land.at[me], local_sems.at[a]))
        for r in range(1, N_DEV):
            peer = (1 - x if r & 4 else x, 1 - y if r & 2 else y, 1 - c if r & 1 else c)
            remote.append(pltpu.make_async_remote_copy(
                src_ref=src.at[_slot(*peer)] if by_peer else src, dst_ref=land.at[me],
                send_sem=send_sems.at[7 * a + r - 1], recv_sem=recv_sems.at[7 * a + r - 1],
                device_id=peer, device_id_type=MESH))
    return local, remote


def _push_start(srcs, by_peer, after, name):
    n = len(srcs)
    srcs = [pltpu.with_memory_space_constraint(a, pltpu.HBM) for a in srcs]
    lands = [pltpu.with_memory_space_constraint(
        lax.empty((N_DEV,) + (a.shape[1:] if by_peer else a.shape), a.dtype), pltpu.HBM) for a in srcs]

    def body(*refs):
        k = 2 * n + len(after)
        local, remote = _push_copies(refs[:n], refs[n:2 * n], refs[k], refs[k + 1], refs[k + 2], by_peer)
        for cp in local + remote:
            cp.start()
        token = refs[k + 3 + 2 * n]
        token[...] = jnp.zeros_like(token)

    outs = pl.pallas_call(
        body, name=name,
        out_shape=(pltpu.SemaphoreType.DMA((7 * n,)), pltpu.SemaphoreType.DMA((7 * n,)),
                   pltpu.SemaphoreType.DMA((n,)),
                   *[pltpu.HBM(a.shape, a.dtype) for a in srcs], *[pltpu.HBM(a.shape, a.dtype) for a in lands],
                   jax.ShapeDtypeStruct((8, LANES), F32)),
        in_specs=[HBM] * (2 * n) + [ANY] * len(after),
        out_specs=(SEM, SEM, SEM, *([HBM] * (2 * n)), pl.BlockSpec(memory_space=pltpu.VMEM)),
        input_output_aliases={i: 3 + i for i in range(2 * n)},
        compiler_params=pltpu.CompilerParams(has_side_effects=EFFECT),
    )(*srcs, *lands, *after)
    return outs[:3], outs[3:3 + n], outs[3 + n:3 + 2 * n], outs[3 + 2 * n]


def _push_wait(sems, srcs, lands, by_peer, after, name):
    n = len(srcs)

    def body(*refs):
        local, remote = _push_copies(refs[:n], refs[n:2 * n], refs[2 * n], refs[2 * n + 1], refs[2 * n + 2], by_peer)
        for cp in local:
            cp.wait()
        for cp in remote:
            cp.wait_send()
            cp.wait_recv()

    outs = pl.pallas_call(
        body, name=name,
        out_shape=[pltpu.HBM(a.shape, a.dtype) for a in list(srcs) + list(lands)],
        in_specs=[HBM] * (2 * n) + [SEM] * 3 + [ANY] * len(after),
        out_specs=[HBM] * (2 * n),
        input_output_aliases={i: i for i in range(2 * n)},
        compiler_params=pltpu.CompilerParams(has_side_effects=EFFECT),
    )(*srcs, *lands, *sems, *after)
    return outs[n:]


def _adam_math(g, w, m, v):
    m = ADAM_B1 * m + (1.0 - ADAM_B1) * g
    v = ADAM_B2 * v + (1.0 - ADAM_B2) * (g * g)
    m_hat = m / (1.0 - ADAM_B1 ** ADAM_STEP)
    v_hat = v / (1.0 - ADAM_B2 ** ADAM_STEP)
    delta = -ADAM_LR * (m_hat / (jnp.sqrt(v_hat) + ADAM_EPS) + ADAM_WD * w)
    return delta, m, v


def _adam_sharded(recvs, first, w, m, v, prev, name):
    nl, r, c = w.shape
    n = len(recvs)
    by_cols = r % 128 != 0
    tr, tc = (r, _tile(c, LANES)) if by_cols else (_rtile(r, 128), c)
    nr = c // tc if by_cols else r // tr
    prev = list(prev) if prev is not None else []

    def tile(t):
        return (0, t) if by_cols else (t, 0)

    def body(*refs):
        g_refs = refs[:n]
        w_ref, m_ref, v_ref = refs[n:n + 3]
        go_ref, d_ref, mo_ref, vo_ref = refs[n + 3 + len(prev):]
        layer = pl.program_id(0)
        for l in range(n):
            @pl.when(layer == l)
            def _(l=l):
                g = g_refs[l][0].astype(F32)
                for i in range(1, N_DEV):
                    g = g + g_refs[l][i].astype(F32)
                delta, mn, vn = _adam_math(g, w_ref[...], m_ref[...], v_ref[...])
                go_ref[...] = g
                d_ref[...] = delta
                mo_ref[...] = mn
                vo_ref[...] = vn

    def recv_spec(l):
        def index(layer, i):
            return (0,) + tile(jnp.where(layer == l, i, jnp.where(layer < l, 0, nr - 1)))
        return pl.BlockSpec((N_DEV, tr, tc), index)

    blk = pl.BlockSpec((None, tr, tc), lambda layer, i: (first + layer,) + tile(i))
    out = jax.ShapeDtypeStruct(w.shape, F32)
    return pl.pallas_call(
        body, name=name, grid=(n, nr),
        in_specs=[recv_spec(l) for l in range(n)] + [blk, blk, blk] + [ANY] * len(prev),
        out_specs=[blk] * 4, out_shape=[out] * 4,
        input_output_aliases={n + 3 + k: k for k in range(len(prev))},
        compiler_params=_cp("arbitrary", "arbitrary"),
    )(*recvs, w, m, v, *prev)


def _adam_mod(c_all_t, dmod, w, m, v):
    nl, d, c = w.shape
    tr = _rtile(d, 128)

    def body(ct_ref, dm_ref, w_ref, m_ref, v_ref, go_ref, d_ref, mo_ref, vo_ref):
        ct = ct_ref[...].astype(BF16).astype(F32)
        dm = dm_ref[...].astype(BF16).astype(F32)
        g = ct[:, 0:1] * dm[0:1, :]
        for b in range(1, N_DEV):
            g = g + ct[:, b:b + 1] * dm[b:b + 1, :]
        delta, mn, vn = _adam_math(g, w_ref[...], m_ref[...], v_ref[...])
        go_ref[...] = g
        d_ref[...] = delta
        mo_ref[...] = mn
        vo_ref[...] = vn

    blk = pl.BlockSpec((None, tr, c), lambda layer, i: (layer, i, 0))
    out = jax.ShapeDtypeStruct(w.shape, F32)
    return pl.pallas_call(
        body, name="adam_mod", grid=(nl, d // tr),
        in_specs=[pl.BlockSpec((tr, N_DEV), lambda layer, i: (i, 0)),
                  pl.BlockSpec((None, N_DEV, c), lambda layer, i: (layer, 0, 0)), blk, blk, blk],
        out_specs=[blk] * 4, out_shape=[out] * 4,
        compiler_params=_cp("parallel", "parallel"),
    )(c_all_t, dmod, w, m, v)


def _adam_small(g, w, m, v, name):
    def body(g_ref, w_ref, m_ref, v_ref, d_ref, mo_ref, vo_ref):
        delta, mn, vn = _adam_math(g_ref[...], w_ref[...], m_ref[...], v_ref[...])
        d_ref[...] = delta
        mo_ref[...] = mn
        vo_ref[...] = vn

    out = jax.ShapeDtypeStruct(w.shape, F32)
    return pl.pallas_call(body, name=name, out_shape=[out] * 3)(g, w, m, v)


def _sum_devices(parts):
    def body(p_ref, o_ref):
        acc = p_ref[0]
        for i in range(1, N_DEV):
            acc = acc + p_ref[i]
        o_ref[...] = acc

    return pl.pallas_call(body, name="sum_devices",
                          out_shape=jax.ShapeDtypeStruct(parts.shape[1:], F32))(parts)


def _rope_tables(positions):
    pos = positions.astype(F32)[:, None]

    def cs(dim):
        inv = 1.0 / (ROPE_BASE ** (jnp.arange(0, dim, 2, dtype=F32) / dim))
        ang = pos * inv
        return jnp.cos(ang), jnp.sin(ang)

    cr, sr = cs(RET_DK)
    cm, sm = cs(MLA_ROPE)
    z = jnp.zeros_like(cm)
    pad = jnp.zeros((pos.shape[0], LANES - MLA_ROPE), F32)
    cosr = jnp.concatenate([cr, cr], axis=1)
    sinr = jnp.concatenate([-sr, sr], axis=1)
    ta = jnp.concatenate([cm, cm, pad], axis=1)
    tb = jnp.concatenate([-sm, z, pad], axis=1)
    tc = jnp.concatenate([z, sm, pad], axis=1)
    return cosr, sinr, ta, tb, tc


def _layer_fwd(x, mod, g_norm, g_cq, g_ckv, w_in, other_weights, tabs, lg, lay, deps):
    d = x.shape[1]
    cosr, sinr, ta, tb, tc = tabs
    shift, scale, gate = mod[:, :d], mod[:, d:2 * d], mod[:, 2 * d:]
    h = _norm_mod_fwd(x, g_norm, scale, shift, deps)
    proj = _matmul(h, w_in, name="mm_proj", tn_cap=1920)
    wts, sent = other_weights(proj)
    wts["in"] = w_in
    o_ret, u_ret, states = _ret_fwd(proj, lg, cosr, sinr, lay, sent)
    y_ret = _matmul(u_ret, wts["ret"], name="mm_y")
    cqn, ckvn = _mla_prep(proj, g_cq, g_ckv, lay, sent)
    qp = _matmul(cqn, wts["uq"], name="mm_up")
    kvp = _matmul(ckvn, wts["ukv"], name="mm_up")
    q_cat, k_cat, v, k_t, v_t = _qk_prep(qp, kvp, proj, ta, tb, tc, lay)
    o_mla, u_mla, lse = _attn_fwd(q_cat, k_cat, v_t, proj, lay)
    y_mla = _matmul(u_mla, wts["mla"], name="mm_y")
    merged = _merge_fwd(y_ret, y_mla, proj, lay)
    out = _matmul(merged, wts["out"], name="mm_y")
    x_next = _resid_fwd(x, out, gate)
    saved = dict(x=x, h=h, proj=proj, o_ret=o_ret, u_ret=u_ret, states=states, y_ret=y_ret, cqn=cqn,
                 ckvn=ckvn, q_cat=q_cat, k_cat=k_cat, v=v, k_t=k_t, o_mla=o_mla, u_mla=u_mla, lse=lse,
                 y_mla=y_mla, merged=merged, out=out, wts=wts)
    return x_next, saved


def _to_owner_blocks_cols(g, n_local):
    k = g.shape[0]
    return g.reshape(k, N_DEV, n_local).transpose(1, 0, 2)


def _from_owner_blocks_cols(g):
    return g.transpose(1, 0, 2).reshape(g.shape[1], -1)


def _layer_bwd(dxn, sv, mod, g_norm, g_cq, g_ckv, wts, tabs, lg, lay, deps, shard_cols, push):
    d = dxn.shape[1]
    n_in, n_uq, n_ukv = shard_cols
    cosr, sinr, ta, tb, tc = tabs
    scale, gate = mod[:, d:2 * d], mod[:, 2 * d:]
    gdt = BF16
    dout, dgate = _resid_bwd(dxn, sv["out"], gate, deps)
    dmerged = _matmul(dout, wts["out"], tb=True, name="mm_dy")
    dw_out = _matmul(sv["merged"], dout, ta=True, out_dtype=gdt, name="mm_dw")
    dy_ret, dy_mla, dproj = _merge_bwd(dmerged, sv["y_ret"], sv["y_mla"], sv["proj"], lay)
    du_ret = _matmul(dy_ret, wts["ret"], tb=True, name="mm_dy")
    dw_ret = _matmul(sv["u_ret"], dy_ret, ta=True, out_dtype=gdt, name="mm_dw")
    dproj = _ret_bwd(sv["proj"], lg, cosr, sinr, sv["o_ret"], du_ret, sv["states"], dproj, lay)
    du_mla = _matmul(dy_mla, wts["mla"], tb=True, name="mm_dy")
    dw_mla = _matmul(sv["u_mla"], dy_mla, ta=True, out_dtype=gdt, name="mm_dw")
    do_mla, delta, dproj = _mla_gate_bwd(du_mla, sv["o_mla"], sv["proj"], dproj, lay)
    delta = delta[:, :MLA_HEADS].T.reshape(sv["lse"].shape)
    dqp, dk_cat, dv = _attn_bwd(sv["q_cat"], sv["k_cat"], sv["v"], sv["k_t"], do_mla, sv["lse"], delta,
                                ta, tb, tc)
    dkvp, dproj = _kv_bwd_prep(dk_cat, dv, ta, tb, tc, dproj, lay)
    dcqn = _matmul(dqp, wts["uq"], tb=True, name="mm_dlat")
    dckvn = _matmul(dkvp, wts["ukv"], tb=True, name="mm_dlat")
    dw_uq = _matmul(sv["cqn"], dqp, ta=True, out_dtype=gdt, name="mm_dwup")
    dw_ukv = _matmul(sv["ckvn"], dkvp, ta=True, out_dtype=gdt, name="mm_dwup")
    sent = push("a", [_to_owner_blocks_cols(_uq_to_logical(dw_uq), n_uq),
                      _to_owner_blocks_cols(_ukv_to_logical(dw_ukv), n_ukv),
                      dw_ret.reshape(N_DEV, -1, d), dw_mla.reshape(N_DEV, -1, d), dw_out.reshape(N_DEV, -1, d)])
    dproj, dg_cq, dg_ckv = _mla_prep_bwd(sv["proj"], dcqn, dckvn, g_cq, g_ckv, dproj, lay)
    dw_in = _matmul(sv["h"], dproj, ta=True, out_dtype=gdt, name="mm_dwin", tn_cap=1920, deps=sent)
    sent = push("b", [_scatter_dw_in(dw_in, lay, n_in)])
    dh = _matmul(dproj, wts["in"], tb=True, name="mm_dh", deps=sent)
    dx, dshift, dscale, dg_norm = _norm_mod_bwd(sv["x"], g_norm, scale, dh, dxn)
    dmod = jnp.concatenate([dshift, dscale, dgate], axis=1)
    small = dict(dmod=dmod, g_norm=dg_norm, g_cq=dg_cq, g_ckv=dg_ckv)
    return dx, small


def kernel(x, c, positions, w_mod, b_mod, g_norm, w_in, g_cq, g_ckv, w_uq, w_ukv, w_ret_proj, w_mla_proj, w_out, g_final, loss_target, m_w_mod, m_b_mod, m_g_norm, m_w_in, m_g_cq, m_g_ckv, m_w_uq, m_w_ukv, m_w_ret_proj, m_w_mla_proj, m_w_out, m_g_final, v_w_mod, v_b_mod, v_g_norm, v_w_in, v_g_cq, v_g_ckv, v_w_uq, v_w_ukv, v_w_ret_proj, v_w_mla_proj, v_w_out, v_g_final):
    nl, d, _ = w_mod.shape
    s = x.shape[1]
    rank = g_cq.shape[1]
    lay = Layout(d, rank, g_ckv.shape[1])
    me = _slot(*_place())
    x0 = x.reshape(s, d)
    target = loss_target.reshape(s, d)
    tabs = _rope_tables(positions.reshape(s))
    lg = jnp.log(1.0 - 2.0 ** (-5.0 - jnp.arange(RET_HEADS, dtype=F32)))

    c_act = c * _sigmoid(c)
    (c_all,) = _all_gather([c_act.reshape(d // LANES, LANES)], None, "gather_c", vmem=True)
    c_all = c_all.reshape(N_DEV, d)
    n_mod = w_mod.shape[2]
    mod_part = jnp.stack([_matmul(c_all, w_mod[l], name="mm_mod", tm_cap=8) for l in range(nl)])
    (mod_all,) = _all_gather([mod_part.reshape(-1, LANES)], None, "gather_mod", vmem=True)
    mod_all = mod_all.reshape(N_DEV, nl, N_DEV, n_mod)
    mod = lax.dynamic_index_in_dim(mod_all, me, axis=2, keepdims=False)
    mod = mod.transpose(1, 0, 2).reshape(nl, N_DEV * n_mod) + b_mod

    shards = [[w[l].astype(BF16) for w in (w_in, w_uq, w_ukv, w_ret_proj, w_mla_proj, w_out)] for l in range(nl)]
    xl, saved = x0, []
    (g_in,) = _all_gather([shards[0][0]], None, "gather_w_in")
    fly = {"late": _push_start(shards[0][1:], False, [g_in, mod], "gather_start_0")}
    deps = [fly["late"][3]]
    for l in range(nl):
        def other_weights(proj, l=l):
            if fly.get("late") is not None:
                sems, srcs, lands, _ = fly.pop("late")
                fly["rest"] = _push_wait(sems, srcs, lands, False, [proj], "gather_wait_%d" % l)
            g_uq, g_ukv, g_ret, g_mla, g_out = fly["rest"]
            sent = []
            if l + 1 < nl:
                fly["next"] = _push_start(shards[l + 1], False, [g_out, proj], "gather_start_%d" % (l + 1))
                sent = [fly["next"][3]]
            return {"uq": _uq_to_physical(_from_owner_blocks_cols(g_uq)),
                    "ukv": _ukv_to_physical(_from_owner_blocks_cols(g_ukv)),
                    "ret": g_ret.reshape(-1, d), "mla": g_mla.reshape(-1, d), "out": g_out.reshape(-1, d)}, sent

        xl, sv = _layer_fwd(xl, mod[l:l + 1], g_norm[l:l + 1], g_cq[l:l + 1], g_ckv[l:l + 1],
                            _assemble_w_in(g_in, lay), other_weights, tabs, lg, lay, deps)
        saved.append(sv)
        deps = []
        if l + 1 < nl:
            sems, srcs, lands, _ = fly.pop("next")
            g_in, *fly["rest"] = _push_wait(sems, srcs, lands, False, [xl], "gather_wait_%d" % (l + 1))
    loss_lanes, dx, dg_final = _final_loss(xl, g_final.reshape(1, d), target)

    small = [None] * nl
    flying = {l: [] for l in range(nl)}
    recv = {}
    shard_cols = (w_in.shape[2], w_uq.shape[2], w_ukv.shape[2])

    def pusher(l):
        def push(group, arrays):
            sems, srcs, lands, token = _push_start(arrays, True, [], "exchange_start_%d%s" % (l, group))
            flying[l].append((group, sems, srcs, lands))
            return [token]
        return push

    def land(l, after):
        got = {}
        for group, sems, srcs, lands in flying[l]:
            got[group] = _push_wait(sems, srcs, lands, True, after, "exchange_wait_%d%s" % (l, group))
        recv[l] = list(got["b"]) + list(got["a"])

    for l in reversed(range(nl)):
        dx, small[l] = _layer_bwd(dx, saved[l], mod[l:l + 1], g_norm[l:l + 1], g_cq[l:l + 1], g_ckv[l:l + 1],
                                  saved[l]["wts"], tabs, lg, lay, [], shard_cols, pusher(l))
        if l + 1 < nl:
            land(l + 1, [dx])
    grad_x = dx.reshape(x.shape)

    out = {}
    w_in_t, m_w_in_t, v_w_in_t = (jnp.swapaxes(a, 1, 2) for a in (w_in, m_w_in, v_w_in))
    sharded = (("w_in", w_in_t, m_w_in_t, v_w_in_t), ("w_uq", w_uq, m_w_uq, v_w_uq), ("w_ukv", w_ukv, m_w_ukv, v_w_ukv),
               ("w_ret_proj", w_ret_proj, m_w_ret_proj, v_w_ret_proj),
               ("w_mla_proj", w_mla_proj, m_w_mla_proj, v_w_mla_proj), ("w_out", w_out, m_w_out, v_w_out))
    if nl > 1:
        for i, (key, w, m, v) in enumerate(sharded):
            out[key] = _adam_sharded([recv[l][i] for l in range(1, nl)], 1, w, m, v, None, "adam_" + key)
    done = [out[key][0] for key, _, _, _ in sharded if key in out]

    pack = jnp.concatenate(
        [jnp.concatenate([sm[k] for sm in small], axis=0).reshape(-1)
         for k in ("dmod", "g_norm", "g_cq", "g_ckv")] + [dg_final.reshape(-1), loss_lanes.reshape(-1)])
    (pack_all,) = _all_gather([pack.reshape(-1, LANES)], None, "gather_small", vmem=True, deps=done)
    tot = _sum_devices(pack_all).reshape(-1)
    sizes = [nl * 3 * d, nl * d, nl * rank, nl * rank, d]
    offs = np.cumsum([0] + sizes)
    grad_b_mod = tot[offs[0]:offs[1]].reshape(nl, 3 * d)
    grad_g_norm = tot[offs[1]:offs[2]].reshape(nl, d)
    grad_g_cq = tot[offs[2]:offs[3]].reshape(nl, rank)
    grad_g_ckv = tot[offs[3]:offs[4]].reshape(nl, rank)
    grad_g_final = tot[offs[4]:offs[5]]
    loss = tot[offs[5]]
    dmod_all = pack_all.reshape(N_DEV, -1)[:, :sizes[0]].reshape(N_DEV, nl, 3 * d)
    dmod_mine = lax.dynamic_slice_in_dim(dmod_all, me * n_mod, n_mod, axis=2).transpose(1, 0, 2)

    out["w_mod"] = _adam_mod(c_all.T, dmod_mine, w_mod, m_w_mod, v_w_mod)
    land(0, [out["w_mod"][0]])
    for i, (key, w, m, v) in enumerate(sharded):
        out[key] = _adam_sharded([recv[0][i]], 0, w, m, v, out.get(key), "adam0_" + key)
    out["w_in"] = tuple(jnp.swapaxes(a, 1, 2) for a in out["w_in"])
    for key, g, w, m, v in (("b_mod", grad_b_mod, b_mod, m_b_mod, v_b_mod),
                            ("g_norm", grad_g_norm, g_norm, m_g_norm, v_g_norm),
                            ("g_cq", grad_g_cq, g_cq, m_g_cq, v_g_cq),
                            ("g_ckv", grad_g_ckv, g_ckv, m_g_ckv, v_g_ckv),
                            ("g_final", grad_g_final.reshape(1, d), g_final.reshape(1, d),
                             m_g_final.reshape(1, d), v_g_final.reshape(1, d))):
        out[key] = (g,) + tuple(_adam_small(g, w, m, v, "adam_" + key))
    out["g_final"] = tuple(a.reshape(d) for a in out["g_final"])

    names = ("w_mod", "b_mod", "g_norm", "w_in", "g_cq", "g_ckv", "w_uq", "w_ukv", "w_ret_proj",
             "w_mla_proj", "w_out", "g_final")
    return (loss, grad_x, *[out[k][0] for k in names], *[out[k][1] for k in names],
            *[out[k][2] for k in names], *[out[k][3] for k in names])
```

```python
import functools
import itertools

import jax
import jax.numpy as jnp
import numpy as np
from jax import lax
from jax.experimental import pallas as pl
from jax.experimental.pallas import tpu as pltpu

F32 = jnp.float32
BF16 = jnp.bfloat16

N_DEV = 8
CHUNK = 64
EPS = 1e-6
NEG_INF = -1e30
ROPE_BASE = 10000.0
LANES = 128

RET_HEADS = 8
RET_DK = 128
RET_DV = 256
MLA_HEADS = 16
MLA_NOPE = 128
MLA_ROPE = 64
MLA_DV = 128
MLA_QW = 256
QK_SCALE = (MLA_NOPE + MLA_ROPE) ** -0.5
QK_LOG2_SCALE = QK_SCALE * 1.4426950408889634
LN2 = 0.6931471805599453

ADAM_LR = 0.001
ADAM_B1 = 0.9
ADAM_B2 = 0.999
ADAM_EPS = 1e-08
ADAM_WD = 0.01
ADAM_STEP = 10

VMEM_LIMIT_BYTES = 56 * 1024 * 1024
MESH = pl.DeviceIdType.MESH
ANY = pl.BlockSpec(memory_space=pl.ANY)


def _cp(*sem):
    return pltpu.CompilerParams(dimension_semantics=sem if sem else None,
                                vmem_limit_bytes=VMEM_LIMIT_BYTES)


def _tile(n, cap):
    best = None
    t = LANES
    while t <= min(n, cap):
        if n % t == 0:
            best = t
        t += LANES
    return best if best is not None else n


def _rtile(n, cap):
    t = cap
    while t > 8 and n % t:
        t //= 2
    return t if n % t == 0 else n


def _sigmoid(x):
    return 1.0 / (1.0 + jnp.exp(-x))


def _dot(a, b):
    return lax.dot_general(a, b, (((1,), (0,)), ((), ())), preferred_element_type=F32)


def _dot_nt(a, b):
    return lax.dot_general(a, b, (((1,), (1,)), ((), ())), preferred_element_type=F32)


def _dot_tn(a, b):
    return lax.dot_general(a, b, (((0,), (0,)), ((), ())), preferred_element_type=F32)


def _roll(x, s):
    return pltpu.roll(x, s, 1)


class Layout:
    def __init__(self, d_model, q_rank, kv_rank):
        assert q_rank == kv_rank
        self.d = d_model
        self.rank = q_rank
        self.ret_w = 2 * RET_DK + 2 * RET_DV
        self.ret_qk = RET_HEADS * RET_DK
        self.ret_v = RET_HEADS * RET_DV
        self.mla_v = MLA_HEADS * MLA_DV
        widths = {"bg": 2 * d_model, "mg": self.mla_v, "ret": RET_HEADS * self.ret_w,
                  "cqkv": 2 * q_rank, "kr": LANES}
        blocks = {"bg": 2 * d_model, "mg": self.mla_v, "ret": self.ret_w,
                  "cqkv": 2 * q_rank, "kr": LANES}
        for order in itertools.permutations(widths):
            off, offs, ok = 0, {}, True
            for name in order:
                if off % blocks[name]:
                    ok = False
                    break
                offs[name] = off
                off += widths[name]
            if ok:
                break
        assert ok, "no aligned layout"
        self.order, self.off, self.width, self.total = order, offs, widths, off
        lo, o = {}, 0
        for name, w in (("rq", self.ret_qk), ("rk", self.ret_qk), ("rv", self.ret_v),
                        ("rg", self.ret_v), ("cq", q_rank), ("ckv", kv_rank), ("kr", MLA_ROPE),
                        ("mg", self.mla_v), ("bg", 2 * d_model)):
            lo[name] = (o, w)
            o += w
        self.logical, self.d_in = lo, o

    def pieces(self):
        lo = self.logical
        out = []
        for name in self.order:
            if name == "bg":
                out.append(lo["bg"])
            elif name == "mg":
                out.append(lo["mg"])
            elif name == "ret":
                for h in range(RET_HEADS):
                    out.append((lo["rq"][0] + h * RET_DK, RET_DK))
                    out.append((lo["rk"][0] + h * RET_DK, RET_DK))
                    out.append((lo["rv"][0] + h * RET_DV, RET_DV))
                    out.append((lo["rg"][0] + h * RET_DV, RET_DV))
            elif name == "cqkv":
                out.append((lo["cq"][0], 2 * self.rank))
            elif name == "kr":
                out.append(lo["kr"])
                out.append((None, LANES - MLA_ROPE))
        return out


RELAYOUT_CHUNK = 512


def _relayout_plan(lay, n_local):
    plan, off = [], 0
    for start, width in lay.pieces():
        done = 0
        while done < width:
            w = min(RELAYOUT_CHUNK, width - done)
            srcs = []
            if start is not None:
                lo, hi = start + done, start + done + w
                while lo < hi:
                    j = lo // n_local
                    end = min(hi, (j + 1) * n_local)
                    srcs.append((j, lo - j * n_local, end - j * n_local))
                    lo = end
            plan.append((off + done, w, srcs))
            done += w
        off += width
    merged = []
    for p, w, srcs in plan:
        if merged and merged[-1][0] % LANES == 0 and (merged[-1][1] % LANES) and p == merged[-1][0] + merged[-1][1]:
            q, qw, qs = merged.pop()
            merged.append((q, qw + w, qs + ([("pad", w)] if not srcs else srcs)))
        else:
            merged.append((p, w, srcs))
    return merged


def _assemble_w_in(g, lay):
    _, d, n_local = g.shape
    tr = _rtile(d, 256)
    plan = _relayout_plan(lay, n_local)

    def body(g_ref, o_ref):
        for p, w, srcs in plan:
            parts = []
            for src in srcs:
                if src[0] == "pad":
                    parts.append(jnp.zeros((tr, src[1]), F32))
                else:
                    j, a, b = src
                    parts.append(g_ref[j, :, a:b].astype(F32))
            if not parts:
                parts = [jnp.zeros((tr, w), F32)]
            val = parts[0] if len(parts) == 1 else jnp.concatenate(parts, axis=1)
            o_ref[:, p:p + w] = val.astype(o_ref.dtype)

    return pl.pallas_call(
        body, name="assemble_w_in", grid=(d // tr,),
        in_specs=[pl.BlockSpec((N_DEV, tr, n_local), lambda i: (0, i, 0))],
        out_specs=pl.BlockSpec((tr, lay.total), lambda i: (i, 0)),
        out_shape=jax.ShapeDtypeStruct((d, lay.total), g.dtype),
        compiler_params=_cp("parallel"),
    )(g)


def _scatter_dw_in(dw, lay, n_local):
    d = dw.shape[0]
    tr = _rtile(d, 256)
    n_pad = -(-n_local // LANES) * LANES
    plan = _relayout_plan(lay, n_local)
    runs = [[] for _ in range(N_DEV)]
    for p, w, srcs in plan:
        at = p
        for src in srcs:
            if src[0] == "pad":
                at += src[1]
                continue
            j, a, b = src
            runs[j].append((a, b, at))
            at += b - a
    for r in runs:
        r.sort()

    def body(dw_ref, o_ref):
        for j in range(N_DEV):
            parts = [dw_ref[:, at:at + (b - a)].astype(F32) for a, b, at in runs[j]]
            if n_pad > n_local:
                parts.append(jnp.zeros((tr, n_pad - n_local), F32))
            val = jnp.concatenate(parts, axis=1).T
            o_ref[j] = val[:n_local, :].astype(BF16)

    return pl.pallas_call(
        body, name="scatter_dw_in", grid=(d // tr,),
        in_specs=[pl.BlockSpec((tr, lay.total), lambda i: (i, 0))],
        out_specs=pl.BlockSpec((N_DEV, n_local, tr), lambda i: (0, 0, i)),
        out_shape=jax.ShapeDtypeStruct((N_DEV, n_local, d), BF16),
        compiler_params=_cp("parallel"),
    )(dw)


def _uq_to_physical(w):
    k = w.shape[0]
    w3 = w.reshape(k, MLA_HEADS, MLA_NOPE + MLA_ROPE)
    pad = jnp.zeros((k, MLA_HEADS, MLA_QW - MLA_NOPE - MLA_ROPE), w.dtype)
    return jnp.concatenate([w3, pad], axis=2).reshape(k, MLA_HEADS * MLA_QW)


def _uq_to_logical(w):
    k = w.shape[0]
    return w.reshape(k, MLA_HEADS, MLA_QW)[:, :, :MLA_NOPE + MLA_ROPE].reshape(k, -1)


def _ukv_to_physical(w):
    k = w.shape[0]
    w3 = w.reshape(k, MLA_HEADS, MLA_NOPE + MLA_DV)
    return jnp.concatenate([w3[:, :, :MLA_NOPE].reshape(k, -1), w3[:, :, MLA_NOPE:].reshape(k, -1)], axis=1)


def _ukv_to_logical(w):
    k = w.shape[0]
    kn = w[:, :MLA_HEADS * MLA_NOPE].reshape(k, MLA_HEADS, MLA_NOPE)
    v = w[:, MLA_HEADS * MLA_NOPE:].reshape(k, MLA_HEADS, MLA_DV)
    return jnp.concatenate([kn, v], axis=2).reshape(k, -1)


def _matmul(a, b, *, ta=False, tb=False, out_dtype=F32, name, tm_cap=1024, tn_cap=1024, tk_cap=2048, deps=()):
    m, k = (a.shape[1], a.shape[0]) if ta else a.shape
    n = b.shape[0] if tb else b.shape[1]
    assert k == (b.shape[1] if tb else b.shape[0])
    tm, tn, tk = _tile(m, tm_cap), _tile(n, tn_cap), _tile(k, tk_cap)
    nk = k // tk

    def body(a_ref, b_ref, *rest):
        dims = (((0 if ta else 1,), (1 if tb else 0,)), ((), ()))
        part = lax.dot_general(a_ref[...].astype(BF16), b_ref[...].astype(BF16), dims, preferred_element_type=F32)
        if nk == 1:
            rest[len(deps)][...] = part.astype(out_dtype)
            return
        o_ref, acc_ref = rest[len(deps):]
        kk = pl.program_id(2)

        @pl.when(kk == 0)
        def _():
            acc_ref[...] = part

        @pl.when(jnp.logical_and(kk > 0, kk < nk - 1))
        def _():
            acc_ref[...] += part

        @pl.when(kk == nk - 1)
        def _():
            o_ref[...] = (acc_ref[...] + part).astype(o_ref.dtype)

    a_spec = pl.BlockSpec((tk, tm), lambda i, j, kk: (kk, i)) if ta else pl.BlockSpec((tm, tk), lambda i, j, kk: (i, kk))
    b_spec = pl.BlockSpec((tn, tk), lambda i, j, kk: (j, kk)) if tb else pl.BlockSpec((tk, tn), lambda i, j, kk: (kk, j))
    return pl.pallas_call(
        body, name=name, grid=(m // tm, n // tn, nk),
        in_specs=[a_spec, b_spec] + [ANY] * len(deps),
        out_specs=pl.BlockSpec((tm, tn), lambda i, j, kk: (i, j)),
        out_shape=jax.ShapeDtypeStruct((m, n), out_dtype),
        scratch_shapes=[pltpu.VMEM((tm, tn), F32)] if nk > 1 else [],
        compiler_params=_cp("parallel", "parallel", "arbitrary"),
    )(a, b, *deps)


def _row(tm, w, cb=0):
    return pl.BlockSpec((tm, w), lambda i, cb=cb: (i, cb))


def _vec(w, cb=0):
    return pl.BlockSpec((1, w), lambda i, cb=cb: (0, cb))


def _dproj_out(dproj, s, lay, tm, name):
    w = lay.width[name]
    cb = lay.off[name] // w
    spec = _row(tm, w, cb)
    shape = jax.ShapeDtypeStruct((s, lay.total), BF16)
    return spec, shape


def _norm_mod_fwd(x, g, scale, shift, deps):
    s, d = x.shape
    tm = _rtile(s, 256)

    def body(x_ref, g_ref, sc_ref, sh_ref, *rest):
        h_ref = rest[len(deps)]
        xv = x_ref[...]
        xh = xv * lax.rsqrt(jnp.mean(xv * xv, axis=-1, keepdims=True) + EPS)
        h_ref[...] = ((xh * g_ref[...]) * (1.0 + sc_ref[...]) + sh_ref[...]).astype(BF16)

    return pl.pallas_call(
        body, name="norm_mod_fwd", grid=(s // tm,),
        in_specs=[_row(tm, d), _vec(d), _vec(d), _vec(d)] + [ANY] * len(deps),
        out_specs=_row(tm, d), out_shape=jax.ShapeDtypeStruct((s, d), BF16),
        compiler_params=_cp("parallel"),
    )(x, g, scale, shift, *deps)


def _norm_mod_bwd(x, g, scale, dh, dres):
    s, d = x.shape
    tm = _rtile(s, 256)

    def body(x_ref, g_ref, sc_ref, dh_ref, dres_ref, dx_ref, dsh_ref, dsc_ref, dg_ref):
        @pl.when(pl.program_id(0) == 0)
        def _():
            dsh_ref[...] = jnp.zeros_like(dsh_ref)
            dsc_ref[...] = jnp.zeros_like(dsc_ref)
            dg_ref[...] = jnp.zeros_like(dg_ref)

        xv, gv, dhv = x_ref[...], g_ref[...], dh_ref[...]
        rstd = lax.rsqrt(jnp.mean(xv * xv, axis=-1, keepdims=True) + EPS)
        xh = xv * rstd
        dy = dhv * (1.0 + sc_ref[...])
        dxh = dy * gv
        dx_ref[...] = dres_ref[...] + rstd * (dxh - xh * jnp.mean(dxh * xh, axis=-1, keepdims=True))
        dsh_ref[...] += jnp.sum(dhv, axis=0, keepdims=True)
        dsc_ref[...] += jnp.sum(dhv * (xh * gv), axis=0, keepdims=True)
        dg_ref[...] += jnp.sum(dy * xh, axis=0, keepdims=True)

    vec = jax.ShapeDtypeStruct((1, d), F32)
    return pl.pallas_call(
        body, name="norm_mod_bwd", grid=(s // tm,),
        in_specs=[_row(tm, d), _vec(d), _vec(d), _row(tm, d), _row(tm, d)],
        out_specs=[_row(tm, d), _vec(d), _vec(d), _vec(d)],
        out_shape=[jax.ShapeDtypeStruct((s, d), F32), vec, vec, vec],
        compiler_params=_cp("arbitrary"),
    )(x, g, scale, dh, dres)


def _final_loss(x, g, target):
    s, d = x.shape
    tm = _rtile(s, 256)

    def body(x_ref, g_ref, t_ref, l_ref, dx_ref, dg_ref):
        @pl.when(pl.program_id(0) == 0)
        def _():
            l_ref[...] = jnp.zeros_like(l_ref)
            dg_ref[...] = jnp.zeros_like(dg_ref)

        xv, gv = x_ref[...], g_ref[...]
        rstd = lax.rsqrt(jnp.mean(xv * xv, axis=-1, keepdims=True) + EPS)
        xh = xv * rstd
        err = xh * gv - t_ref[...]
        row = jnp.mean(err * err, axis=-1, keepdims=True)
        l_ref[...] += 0.5 * jnp.sum(row, axis=0, keepdims=True)
        dy = err / d
        dxh = dy * gv
        dx_ref[...] = rstd * (dxh - xh * jnp.mean(dxh * xh, axis=-1, keepdims=True))
        dg_ref[...] += jnp.sum(dy * xh, axis=0, keepdims=True)

    return pl.pallas_call(
        body, name="final_loss", grid=(s // tm,),
        in_specs=[_row(tm, d), _vec(d), _row(tm, d)],
        out_specs=[_vec(LANES), _row(tm, d), _vec(d)],
        out_shape=[jax.ShapeDtypeStruct((1, LANES), F32), jax.ShapeDtypeStruct((s, d), F32),
                   jax.ShapeDtypeStruct((1, d), F32)],
        compiler_params=_cp("arbitrary"),
    )(x, g, target)


def _resid_fwd(x, out, gate):
    s, d = x.shape
    tm = _rtile(s, 256)

    def body(x_ref, o_ref, g_ref, y_ref):
        y_ref[...] = x_ref[...] + g_ref[...] * o_ref[...]

    return pl.pallas_call(
        body, name="resid_fwd", grid=(s // tm,),
        in_specs=[_row(tm, d), _row(tm, d), _vec(d)],
        out_specs=_row(tm, d), out_shape=jax.ShapeDtypeStruct((s, d), F32),
        compiler_params=_cp("parallel"),
    )(x, out, gate)


def _resid_bwd(dxn, out, gate, deps):
    s, d = dxn.shape
    tm = _rtile(s, 256)

    def body(dx_ref, o_ref, g_ref, *rest):
        do_ref, dg_ref = rest[len(deps):]

        @pl.when(pl.program_id(0) == 0)
        def _():
            dg_ref[...] = jnp.zeros_like(dg_ref)

        dxv = dx_ref[...]
        do_ref[...] = (dxv * g_ref[...]).astype(BF16)
        dg_ref[...] += jnp.sum(dxv * o_ref[...], axis=0, keepdims=True)

    return pl.pallas_call(
        body, name="resid_bwd", grid=(s // tm,),
        in_specs=[_row(tm, d), _row(tm, d), _vec(d)] + [ANY] * len(deps),
        out_specs=[_row(tm, d), _vec(d)],
        out_shape=[jax.ShapeDtypeStruct((s, d), BF16), jax.ShapeDtypeStruct((1, d), F32)],
        compiler_params=_cp("arbitrary"),
    )(dxn, out, gate, *deps)


def _merge_fwd(y_ret, y_mla, proj, lay):
    s, d = y_ret.shape
    tm = _rtile(s, 256)
    cb = lay.off["bg"] // (2 * d)

    def body(a_ref, b_ref, bg_ref, m_ref):
        sg = _sigmoid(bg_ref[...])
        m_ref[...] = (sg[:, :d] * a_ref[...] + sg[:, d:] * b_ref[...]).astype(BF16)

    return pl.pallas_call(
        body, name="merge_fwd", grid=(s // tm,),
        in_specs=[_row(tm, d), _row(tm, d), _row(tm, 2 * d, cb)],
        out_specs=_row(tm, d), out_shape=jax.ShapeDtypeStruct((s, d), BF16),
        compiler_params=_cp("parallel"),
    )(y_ret, y_mla, proj)


def _merge_bwd(dm, y_ret, y_mla, proj, lay):
    s, d = dm.shape
    tm = _rtile(s, 256)
    cb = lay.off["bg"] // (2 * d)
    dp_spec, dp_shape = _dproj_out(None, s, lay, tm, "bg")

    def body(dm_ref, a_ref, b_ref, bg_ref, da_ref, db_ref, dp_ref):
        sg = _sigmoid(bg_ref[...])
        dmv = dm_ref[...]
        ga, gb = sg[:, :d], sg[:, d:]
        da_ref[...] = (dmv * ga).astype(BF16)
        db_ref[...] = (dmv * gb).astype(BF16)
        dp_ref[:, :d] = (dmv * a_ref[...] * ga * (1.0 - ga)).astype(BF16)
        dp_ref[:, d:] = (dmv * b_ref[...] * gb * (1.0 - gb)).astype(BF16)

    act = jax.ShapeDtypeStruct((s, d), BF16)
    return pl.pallas_call(
        body, name="merge_bwd", grid=(s // tm,),
        in_specs=[_row(tm, d), _row(tm, d), _row(tm, d), _row(tm, 2 * d, cb)],
        out_specs=[_row(tm, d), _row(tm, d), dp_spec],
        out_shape=[act, act, dp_shape],
        compiler_params=_cp("parallel"),
    )(dm, y_ret, y_mla, proj)


def _mla_prep(proj, g_cq, g_ckv, lay):
    s = proj.shape[0]
    r = lay.rank
    tm = _rtile(s, 512)
    cb = lay.off["cqkv"] // (2 * r)

    def body(p_ref, gq_ref, gk_ref, q_ref, k_ref):
        pv = p_ref[...]
        for lo, g_ref, o_ref in ((0, gq_ref, q_ref), (r, gk_ref, k_ref)):
            xv = pv[:, lo:lo + r]
            xh = xv * lax.rsqrt(jnp.mean(xv * xv, axis=-1, keepdims=True) + EPS)
            o_ref[...] = (xh * g_ref[...]).astype(BF16)

    act = jax.ShapeDtypeStruct((s, r), BF16)
    return pl.pallas_call(
        body, name="mla_prep", grid=(s // tm,),
        in_specs=[_row(tm, 2 * r, cb), _vec(r), _vec(r)],
        out_specs=[_row(tm, r), _row(tm, r)], out_shape=[act, act],
        compiler_params=_cp("parallel"),
    )(proj, g_cq, g_ckv)


def _mla_prep_bwd(proj, dqn, dkn, g_cq, g_ckv, dproj, lay):
    s = proj.shape[0]
    r = lay.rank
    tm = _rtile(s, 512)
    cb = lay.off["cqkv"] // (2 * r)
    dp_spec, dp_shape = _dproj_out(dproj, s, lay, tm, "cqkv")

    def body(p_ref, dq_ref, dk_ref, gq_ref, gk_ref, _, dp_ref, dgq_ref, dgk_ref):
        @pl.when(pl.program_id(0) == 0)
        def _():
            dgq_ref[...] = jnp.zeros_like(dgq_ref)
            dgk_ref[...] = jnp.zeros_like(dgk_ref)

        pv = p_ref[...]
        for lo, g_ref, d_ref, dg_ref in ((0, gq_ref, dq_ref, dgq_ref), (r, gk_ref, dk_ref, dgk_ref)):
            xv = pv[:, lo:lo + r]
            rstd = lax.rsqrt(jnp.mean(xv * xv, axis=-1, keepdims=True) + EPS)
            xh = xv * rstd
            dy = d_ref[...]
            dxh = dy * g_ref[...]
            dp_ref[:, lo:lo + r] = (rstd * (dxh - xh * jnp.mean(dxh * xh, axis=-1, keepdims=True))).astype(BF16)
            dg_ref[...] += jnp.sum(dy * xh, axis=0, keepdims=True)

    vec = jax.ShapeDtypeStruct((1, r), F32)
    return pl.pallas_call(
        body, name="mla_prep_bwd", grid=(s // tm,),
        in_specs=[_row(tm, 2 * r, cb), _row(tm, r), _row(tm, r), _vec(r), _vec(r), ANY],
        out_specs=[dp_spec, _vec(r), _vec(r)], out_shape=[dp_shape, vec, vec],
        input_output_aliases={5: 0},
        compiler_params=_cp("arbitrary"),
    )(proj, dqn, dkn, g_cq, g_ckv, dproj)


def _rope_tile(t, a, b, c):
    return t * a + _roll(t, 96) * b + _roll(t, 32) * c


def _rope_tile_bwd(dy, a, b, c):
    return dy * a + _roll(dy * b, 32) + _roll(dy * c, 96)


def _attn_block(s):
    return _rtile(s, 512)


def _qk_prep(qp, kvp, proj, ta, tb, tc, lay):
    s = qp.shape[0]
    hq = MLA_HEADS * MLA_QW
    hv = MLA_HEADS * MLA_DV
    blk = _attn_block(s)
    tm = _rtile(blk, 256)
    per = blk // tm
    kr_cb = lay.off["kr"] // LANES

    def body(q_ref, kv_ref, kr_ref, a_ref, b_ref, c_ref, qc_ref, kc_ref, v_ref, kt_ref, vt_ref):
        a, b, c = a_ref[...], b_ref[...], c_ref[...]
        krot = _rope_tile(kr_ref[...], a, b, c)
        krot_b, krot_t = krot.astype(BF16), krot.T.astype(BF16)
        for h in range(MLA_HEADS):
            q0 = h * MLA_QW
            qc_ref[:, q0:q0 + MLA_NOPE] = (q_ref[:, q0:q0 + MLA_NOPE] * QK_LOG2_SCALE).astype(BF16)
            qc_ref[:, q0 + MLA_NOPE:q0 + MLA_QW] = (
                _rope_tile(q_ref[:, q0 + MLA_NOPE:q0 + MLA_QW], a, b, c) * QK_LOG2_SCALE).astype(BF16)
            kn = kv_ref[:, h * MLA_NOPE:(h + 1) * MLA_NOPE]
            kc_ref[:, q0:q0 + MLA_NOPE] = kn.astype(BF16)
            kc_ref[:, q0 + MLA_NOPE:q0 + MLA_QW] = krot_b
            kt_ref[h, :MLA_NOPE, :] = kn.T.astype(BF16)
            kt_ref[h, MLA_NOPE:, :] = krot_t
            vh = kv_ref[:, (MLA_HEADS + h) * MLA_NOPE:(MLA_HEADS + h + 1) * MLA_NOPE]
            v_ref[:, h * MLA_DV:(h + 1) * MLA_DV] = vh.astype(BF16)
            vt_ref[h] = vh.T.astype(BF16)

    return pl.pallas_call(
        body, name="qk_prep", grid=(s // tm,),
        in_specs=[_row(tm, hq), _row(tm, hq), _row(tm, LANES, kr_cb), _row(tm, LANES), _row(tm, LANES), _row(tm, LANES)],
        out_specs=[_row(tm, hq), _row(tm, hq), _row(tm, hv),
                   pl.BlockSpec((MLA_HEADS, None, MLA_QW, tm), lambda i: (0, i // per, 0, i % per)),
                   pl.BlockSpec((MLA_HEADS, None, MLA_DV, tm), lambda i: (0, i // per, 0, i % per))],
        out_shape=[jax.ShapeDtypeStruct((s, hq), BF16), jax.ShapeDtypeStruct((s, hq), BF16),
                   jax.ShapeDtypeStruct((s, hv), BF16),
                   jax.ShapeDtypeStruct((MLA_HEADS, s // blk, MLA_QW, blk), BF16),
                   jax.ShapeDtypeStruct((MLA_HEADS, s // blk, MLA_DV, blk), BF16)],
        compiler_params=_cp("parallel"),
    )(qp, kvp, proj, ta, tb, tc)


def _kv_bwd_prep(dk_cat, dv, ta, tb, tc, dproj, lay):
    s = dk_cat.shape[0]
    hq = MLA_HEADS * MLA_QW
    hv = MLA_HEADS * MLA_DV
    tm = _rtile(s, 256)
    dp_spec, dp_shape = _dproj_out(dproj, s, lay, tm, "kr")

    def body(dk_ref, dv_ref, a_ref, b_ref, c_ref, _, dkv_ref, dp_ref):
        acc = jnp.zeros((tm, LANES), F32)
        for h in range(MLA_HEADS):
            q0 = h * MLA_QW
            dkv_ref[:, h * MLA_NOPE:(h + 1) * MLA_NOPE] = dk_ref[:, q0:q0 + MLA_NOPE].astype(BF16)
            acc = acc + dk_ref[:, q0 + MLA_NOPE:q0 + MLA_QW]
        dkv_ref[:, MLA_HEADS * MLA_NOPE:] = dv_ref[...].astype(BF16)
        dp_ref[...] = _rope_tile_bwd(acc, a_ref[...], b_ref[...], c_ref[...]).astype(BF16)

    return pl.pallas_call(
        body, name="kv_bwd_prep", grid=(s // tm,),
        in_specs=[_row(tm, hq), _row(tm, hv), _row(tm, LANES), _row(tm, LANES), _row(tm, LANES), ANY],
        out_specs=[_row(tm, hq), dp_spec],
        out_shape=[jax.ShapeDtypeStruct((s, hq), BF16), dp_shape],
        input_output_aliases={5: 1},
        compiler_params=_cp("parallel"),
    )(dk_cat, dv, ta, tb, tc, dproj)


def _mla_gate_bwd(du, o, proj, dproj, lay):
    s, vw = du.shape
    tm = _rtile(s, 256)
    cb = lay.off["mg"] // vw
    dp_spec, dp_shape = _dproj_out(dproj, s, lay, tm, "mg")
    assert MLA_HEADS <= LANES

    def body(du_ref, o_ref, g_ref, _, do_ref, dl_ref, dp_ref):
        gv, duv, ov = g_ref[...], du_ref[...], o_ref[...]
        sg = _sigmoid(gv)
        do = (duv * (gv * sg)).astype(BF16)
        do_ref[...] = do
        dp_ref[...] = (duv * ov * (sg + gv * sg * (1.0 - sg))).astype(BF16)
        prod = do.astype(F32) * ov
        lane = lax.broadcasted_iota(jnp.int32, (tm, LANES), 1)
        delta = jnp.zeros((tm, LANES), F32)
        for h in range(MLA_HEADS):
            dh = jnp.sum(prod[:, h * MLA_DV:(h + 1) * MLA_DV], axis=-1, keepdims=True)
            delta = jnp.where(lane == h, dh, delta)
        dl_ref[...] = delta

    return pl.pallas_call(
        body, name="mla_gate_bwd", grid=(s // tm,),
        in_specs=[_row(tm, vw), _row(tm, vw), _row(tm, vw, cb), ANY],
        out_specs=[_row(tm, vw), _row(tm, LANES), dp_spec],
        out_shape=[jax.ShapeDtypeStruct((s, vw), BF16), jax.ShapeDtypeStruct((s, LANES), F32), dp_shape],
        input_output_aliases={3: 2},
        compiler_params=_cp("parallel"),
    )(du, o, proj, dproj)


RET_BLOCK = 256


def _ret_tables(lg, blk):
    ri = lax.broadcasted_iota(jnp.int32, (blk, blk), 0)
    ci = lax.broadcasted_iota(jnp.int32, (blk, blk), 1)
    col = lax.broadcasted_iota(jnp.int32, (blk, 1), 0).astype(F32)
    dist = jnp.abs(ri - ci).astype(F32)
    dmat = jnp.where(ci // CHUNK <= ri // CHUNK, jnp.exp(dist * lg), 0.0)
    xi = jnp.exp((col + 1.0) * lg)
    zeta = jnp.exp((blk - 1.0 - col) * lg)
    decay = jnp.exp(jnp.full((1, 1), blk, F32) * lg)
    return dmat, xi, zeta, decay


def _ret_qkvg(blk, cs, sn):
    dv = RET_DV
    q = blk[:, :RET_DK]
    k = blk[:, RET_DK:2 * RET_DK]
    q = q * cs + _roll(q, RET_DK // 2) * sn
    k = (k * cs + _roll(k, RET_DK // 2) * sn) * (RET_DK ** -0.5)
    return q, k, blk[:, 2 * RET_DK:2 * RET_DK + dv], blk[:, 2 * RET_DK + dv:]


def _group_norm(o):
    mu = jnp.mean(o, axis=-1, keepdims=True)
    oc = o - mu
    rstd = lax.rsqrt(jnp.mean(oc * oc, axis=-1, keepdims=True) + EPS)
    return oc * rstd, rstd


def _ret_fwd(proj, lg, cosr, sinr, lay):
    s = proj.shape[0]
    dv, w = RET_DV, lay.ret_w
    tb = _rtile(s, 512)
    blk = min(RET_BLOCK, tb)
    nb, nch = s // tb, tb // blk
    cb0 = lay.off["ret"] // w

    def body(lg_ref, p_ref, cos_ref, sin_ref, o_ref, u_ref, st_ref, state):
        @pl.when(pl.program_id(1) == 0)
        def _():
            state[...] = jnp.zeros_like(state)

        dmat, xi, zeta, decay = _ret_tables(lg_ref[pl.program_id(0)], blk)
        for c in range(nch):
            rows = slice(c * blk, (c + 1) * blk)
            q, k, v, g = _ret_qkvg(p_ref[rows, :], cos_ref[rows, :], sin_ref[rows, :])
            qb, kb, vb = q.astype(BF16), k.astype(BF16), v.astype(BF16)
            sc = _dot_nt(qb, kb) * dmat
            st = state[...]
            o = _dot(sc.astype(BF16), vb) + _dot((q * xi).astype(BF16), st.astype(BF16))
            st_ref[c] = st.astype(BF16)
            state[...] = st * decay + _dot_tn((k * zeta).astype(BF16), vb)
            o_ref[rows, :] = o
            n, _ = _group_norm(o)
            u_ref[rows, :] = (n * (g * _sigmoid(g))).astype(BF16)

    return pl.pallas_call(
        body, name="ret_fwd", grid=(RET_HEADS, nb),
        in_specs=[pl.BlockSpec(memory_space=pltpu.SMEM),
                  pl.BlockSpec((tb, w), lambda h, b: (b, cb0 + h)),
                  pl.BlockSpec((tb, RET_DK), lambda h, b: (b, 0)),
                  pl.BlockSpec((tb, RET_DK), lambda h, b: (b, 0))],
        out_specs=[pl.BlockSpec((tb, dv), lambda h, b: (b, h)),
                   pl.BlockSpec((tb, dv), lambda h, b: (b, h)),
                   pl.BlockSpec((None, nch, RET_DK, dv), lambda h, b: (h, b, 0, 0))],
        out_shape=[jax.ShapeDtypeStruct((s, RET_HEADS * dv), F32),
                   jax.ShapeDtypeStruct((s, RET_HEADS * dv), BF16),
                   jax.ShapeDtypeStruct((RET_HEADS, s // blk, RET_DK, dv), BF16)],
        scratch_shapes=[pltpu.VMEM((RET_DK, dv), F32)],
        compiler_params=_cp("parallel", "arbitrary"),
    )(lg, proj, cosr, sinr)


def _ret_bwd(proj, lg, cosr, sinr, o, du, states, dproj, lay):
    s = proj.shape[0]
    dv, w = RET_DV, lay.ret_w
    tb = _rtile(s, 512)
    blk = min(RET_BLOCK, tb)
    nb, nch = s // tb, tb // blk
    cb0 = lay.off["ret"] // w

    def body(lg_ref, p_ref, cos_ref, sin_ref, o_ref, du_ref, st_ref, _, dp_ref, dstate):
        @pl.when(pl.program_id(1) == 0)
        def _():
            dstate[...] = jnp.zeros_like(dstate)

        dmat, xi, zeta, decay = _ret_tables(lg_ref[pl.program_id(0)], blk)
        for c in reversed(range(nch)):
            rows = slice(c * blk, (c + 1) * blk)
            cs, sn = cos_ref[rows, :], sin_ref[rows, :]
            q, k, v, g = _ret_qkvg(p_ref[rows, :], cs, sn)
            qb, kb, vb = q.astype(BF16), k.astype(BF16), v.astype(BF16)
            n, rstd = _group_norm(o_ref[rows, :])
            sg = _sigmoid(g)
            duv = du_ref[rows, :]
            dn = duv * (g * sg)
            dg = duv * n * (sg + g * sg * (1.0 - sg))
            do = rstd * (dn - jnp.mean(dn, axis=-1, keepdims=True) - n * jnp.mean(dn * n, axis=-1, keepdims=True))
            dob = do.astype(BF16)
            rb = st_ref[c]
            drb = dstate[...].astype(BF16)
            sc = (_dot_nt(qb, kb) * dmat).astype(BF16)
            dsc = (_dot_nt(dob, vb) * dmat).astype(BF16)
            qx = (q * xi).astype(BF16)
            kz = (k * zeta).astype(BF16)
            dq = _dot(dsc, kb) + _dot_nt(dob, rb) * xi
            dk = (_dot_tn(dsc, qb) + _dot_nt(vb, drb) * zeta) * (RET_DK ** -0.5)
            dvv = _dot_tn(sc, dob) + _dot(kz, drb)
            dstate[...] = dstate[...] * decay + _dot_tn(qx, dob)
            dp_ref[rows, :RET_DK] = (dq * cs + _roll(dq * sn, RET_DK // 2)).astype(BF16)
            dp_ref[rows, RET_DK:2 * RET_DK] = (dk * cs + _roll(dk * sn, RET_DK // 2)).astype(BF16)
            dp_ref[rows, 2 * RET_DK:2 * RET_DK + dv] = dvv.astype(BF16)
            dp_ref[rows, 2 * RET_DK + dv:] = dg.astype(BF16)

    rev = lambda h, b: (nb - 1 - b, h)
    return pl.pallas_call(
        body, name="ret_bwd", grid=(RET_HEADS, nb),
        in_specs=[pl.BlockSpec(memory_space=pltpu.SMEM),
                  pl.BlockSpec((tb, w), lambda h, b: (nb - 1 - b, cb0 + h)),
                  pl.BlockSpec((tb, RET_DK), lambda h, b: (nb - 1 - b, 0)),
                  pl.BlockSpec((tb, RET_DK), lambda h, b: (nb - 1 - b, 0)),
                  pl.BlockSpec((tb, dv), rev),
                  pl.BlockSpec((tb, dv), rev),
                  pl.BlockSpec((None, nch, RET_DK, dv), lambda h, b: (h, nb - 1 - b, 0, 0)),
                  ANY],
        out_specs=pl.BlockSpec((tb, w), lambda h, b: (nb - 1 - b, cb0 + h)),
        out_shape=jax.ShapeDtypeStruct((s, lay.total), BF16),
        input_output_aliases={7: 0},
        scratch_shapes=[pltpu.VMEM((RET_DK, dv), F32)],
        compiler_params=_cp("parallel", "arbitrary"),
    )(lg, proj, cosr, sinr, o, du, states, dproj)


ATTN_HEADS_PER_STEP = 2


def _attn_mask(rows, cols, row0, col0, keys_on_rows):
    ri = (lax.broadcasted_iota(jnp.int32, (rows, cols), 0) + row0) // CHUNK
    ci = (lax.broadcasted_iota(jnp.int32, (rows, cols), 1) + col0) // CHUNK
    return ri <= ci if keys_on_rows else ci <= ri


def _attn_fwd(q_cat, k_cat, v_t, proj, lay):
    s = q_cat.shape[0]
    bq = _attn_block(s)
    nq = s // bq
    mg_cb = lay.off["mg"] // MLA_DV

    hp = ATTN_HEADS_PER_STEP
    assert MLA_HEADS % hp == 0 and mg_cb % hp == 0

    def body(q_ref, k_ref, vt_ref, g_ref, o_ref, u_ref, lse_ref):
        i = pl.program_id(1)
        half = bq // 2
        qs = [q_ref[:, a * MLA_QW:(a + 1) * MLA_QW] for a in range(hp)]

        def scores(a, j):
            r0 = pl.multiple_of(j * bq, bq)
            return _dot_nt(k_ref[pl.ds(r0, bq), a * MLA_QW:(a + 1) * MLA_QW], qs[a])

        def update(a, j, sc, m, l, acc):
            mn = jnp.maximum(m, jnp.max(sc, axis=0, keepdims=True))
            p = jnp.exp2(sc - mn)
            alpha = jnp.exp2(m - mn)
            l = alpha * l + jnp.sum(p, axis=0, keepdims=True)
            acc = alpha * acc + _dot(vt_ref[a, j], p.astype(BF16))
            return mn, l, acc

        def lanes_after(x, fill):
            return jnp.concatenate([jnp.full((x.shape[0], half), fill, F32), x], axis=1)

        def diag(a, m, l, acc):
            r0 = pl.multiple_of(i * bq, bq)
            kc = slice(a * MLA_QW, (a + 1) * MLA_QW)
            top = _dot_nt(k_ref[pl.ds(r0, half), kc], qs[a])
            bot = _dot_nt(k_ref[pl.ds(r0 + half, half), kc], qs[a][half:, :])
            top = jnp.where(_attn_mask(half, bq, 0, 0, True), top, NEG_INF)
            bot = jnp.where(_attn_mask(half, half, 0, 0, True), bot, NEG_INF)
            mn = jnp.maximum(m, jnp.maximum(jnp.max(top, axis=0, keepdims=True),
                                            lanes_after(jnp.max(bot, axis=0, keepdims=True), NEG_INF)))
            pt, pb = jnp.exp2(top - mn), jnp.exp2(bot - mn[:, half:])
            alpha = jnp.exp2(m - mn)
            l = alpha * l + jnp.sum(pt, axis=0, keepdims=True) + lanes_after(jnp.sum(pb, axis=0, keepdims=True), 0.0)
            vt = vt_ref[a, i]
            acc = (alpha * acc + _dot(vt[:, :half], pt.astype(BF16))
                   + lanes_after(_dot(vt[:, half:], pb.astype(BF16)), 0.0))
            return mn, l, acc

        def pair(j, carry, last):
            sa = [scores(a, j) for a in range(hp)]
            if last:
                return tuple(diag(a, *update(a, j, sa[a], *carry[a])) for a in range(hp))
            sb = [scores(a, j + 1) for a in range(hp)]
            carry = [update(a, j, sa[a], *carry[a]) for a in range(hp)]
            return tuple(update(a, j + 1, sb[a], *carry[a]) for a in range(hp))

        def single(carry):
            return tuple(diag(a, *carry[a]) for a in range(hp))

        init = tuple((jnp.full((1, bq), NEG_INF, F32), jnp.zeros((1, bq), F32), jnp.zeros((MLA_DV, bq), F32))
                     for _ in range(hp))
        carry = lax.fori_loop(0, i // 2, lambda t, c: pair(2 * t, c, False), init)
        carry = lax.cond(i % 2 == 1, lambda c: pair(i - 1, c, True), single, carry)
        for a, (m, l, acc) in enumerate(carry):
            cols = slice(a * MLA_DV, (a + 1) * MLA_DV)
            o = (acc / l).T
            gv = g_ref[:, cols]
            o_ref[:, cols] = o
            u_ref[:, cols] = (o * (gv * _sigmoid(gv))).astype(BF16)
            lse_ref[a] = m + jnp.log2(l)

    return pl.pallas_call(
        body, name="attn_fwd", grid=(MLA_HEADS // hp, nq),
        in_specs=[pl.BlockSpec((bq, hp * MLA_QW), lambda h, i: (i, h)),
                  pl.BlockSpec((s, hp * MLA_QW), lambda h, i: (0, h)),
                  pl.BlockSpec((hp, nq, MLA_DV, bq), lambda h, i: (h, 0, 0, 0)),
                  pl.BlockSpec((bq, hp * MLA_DV), lambda h, i: (i, mg_cb // hp + h))],
        out_specs=[pl.BlockSpec((bq, hp * MLA_DV), lambda h, i: (i, h)),
                   pl.BlockSpec((bq, hp * MLA_DV), lambda h, i: (i, h)),
                   pl.BlockSpec((hp, None, 1, bq), lambda h, i: (h, i, 0, 0))],
        out_shape=[jax.ShapeDtypeStruct((s, MLA_HEADS * MLA_DV), F32),
                   jax.ShapeDtypeStruct((s, MLA_HEADS * MLA_DV), BF16),
                   jax.ShapeDtypeStruct((MLA_HEADS, nq, 1, bq), F32)],
        compiler_params=_cp("parallel", "parallel"),
    )(q_cat, k_cat, v_t, proj)


def _attn_bwd(q_cat, k_cat, v, k_t, do, lse, delta, ta, tb, tc):
    s = q_cat.shape[0]
    blk = _attn_block(s)
    nb = s // blk
    hp = ATTN_HEADS_PER_STEP
    qw, dvw = hp * MLA_QW, hp * MLA_DV

    def body(q_ref, k_ref, v_ref, kt_ref, do_ref, lse_ref, dl_ref, a_ref, b_ref, c_ref,
             dq_ref, dk_ref, dv_ref, dq_acc):
        j = pl.program_id(1)

        @pl.when(j == 0)
        def _():
            dq_acc[...] = jnp.zeros_like(dq_acc)

        def qcols(a):
            return slice(a * MLA_QW, (a + 1) * MLA_QW)

        def vcols(a):
            return slice(a * MLA_DV, (a + 1) * MLA_DV)

        kbs = [k_ref[:, qcols(a)] for a in range(hp)]
        vbs = [v_ref[:, vcols(a)] for a in range(hp)]

        half = blk // 2

        def step(i, carry):
            r0 = pl.multiple_of(i * blk, blk)
            out = []
            for a in range(hp):
                dk, dvv = carry[a]
                q, dob = q_ref[pl.ds(r0, blk), qcols(a)], do_ref[pl.ds(r0, blk), vcols(a)]
                sc = _dot_nt(kbs[a], q)
                p = jnp.exp2(sc - lse_ref[a, i])
                dvv = dvv + _dot(p.astype(BF16), dob)
                ds = (p * (_dot_nt(vbs[a], dob) - dl_ref[a, i])).astype(BF16)
                dk = dk + _dot(ds, q)
                dq_acc[a, i] = dq_acc[a, i] + _dot(kt_ref[a], ds)
                out.append((dk, dvv))
            return tuple(out)

        def diag():
            r0 = pl.multiple_of(j * blk, blk)
            out = []
            for a in range(hp):
                q, dob = q_ref[pl.ds(r0, blk), qcols(a)], do_ref[pl.ds(r0, blk), vcols(a)]
                q1, do1 = q[half:, :], dob[half:, :]
                lse, dl, kt = lse_ref[a, j], dl_ref[a, j], kt_ref[a]
                top = jnp.where(_attn_mask(half, blk, 0, 0, True), _dot_nt(kbs[a][:half, :], q), NEG_INF)
                bot = jnp.where(_attn_mask(half, half, 0, 0, True), _dot_nt(kbs[a][half:, :], q1), NEG_INF)
                pt, pb = jnp.exp2(top - lse), jnp.exp2(bot - lse[:, half:])
                dst = (pt * (_dot_nt(vbs[a][:half, :], dob) - dl)).astype(BF16)
                dsb = (pb * (_dot_nt(vbs[a][half:, :], do1) - dl[:, half:])).astype(BF16)
                later = _dot(kt[:, half:], dsb)
                dq_acc[a, j] = (dq_acc[a, j] + _dot(kt[:, :half], dst)
                                + jnp.concatenate([jnp.zeros_like(later), later], axis=1))
                out.append((jnp.concatenate([_dot(dst, q), _dot(dsb, q1)], axis=0),
                            jnp.concatenate([_dot(pt.astype(BF16), dob), _dot(pb.astype(BF16), do1)], axis=0)))
            return tuple(out)

        carry = lax.fori_loop(j + 1, nb, step, diag())
        for a, (dk, dvv) in enumerate(carry):
            dk_ref[:, qcols(a)] = dk * LN2
            dv_ref[:, vcols(a)] = dvv

        @pl.when(j == nb - 1)
        def _():
            for a in range(hp):
                for i in range(nb):
                    rows = slice(i * blk, (i + 1) * blk)
                    dq = dq_acc[a, i].T * QK_SCALE
                    c0 = a * MLA_QW
                    dq_ref[rows, c0:c0 + MLA_NOPE] = dq[:, :MLA_NOPE].astype(BF16)
                    dq_ref[rows, c0 + MLA_NOPE:c0 + MLA_QW] = _rope_tile_bwd(
                        dq[:, MLA_NOPE:], a_ref[rows, :], b_ref[rows, :], c_ref[rows, :]).astype(BF16)

    tab = pl.BlockSpec((s, LANES), lambda h, j: (0, 0), pipeline_mode=pl.Buffered(1))
    rows = pl.BlockSpec((hp, nb, 1, blk), lambda h, j: (h, 0, 0, 0))
    return pl.pallas_call(
        body, name="attn_bwd", grid=(MLA_HEADS // hp, nb),
        in_specs=[pl.BlockSpec((s, qw), lambda h, j: (0, h)),
                  pl.BlockSpec((blk, qw), lambda h, j: (j, h)),
                  pl.BlockSpec((blk, dvw), lambda h, j: (j, h)),
                  pl.BlockSpec((hp, None, MLA_QW, blk), lambda h, j: (h, j, 0, 0)),
                  pl.BlockSpec((s, dvw), lambda h, j: (0, h)),
                  rows, rows, tab, tab, tab],
        out_specs=[pl.BlockSpec((s, qw), lambda h, j: (0, h)),
                   pl.BlockSpec((blk, qw), lambda h, j: (j, h)),
                   pl.BlockSpec((blk, dvw), lambda h, j: (j, h))],
        out_shape=[jax.ShapeDtypeStruct((s, MLA_HEADS * MLA_QW), BF16),
                   jax.ShapeDtypeStruct((s, MLA_HEADS * MLA_QW), F32),
                   jax.ShapeDtypeStruct((s, MLA_HEADS * MLA_DV), F32)],
        scratch_shapes=[pltpu.VMEM((hp, nb, MLA_QW, blk), F32)],
        compiler_params=_cp("parallel", "arbitrary"),
    )(q_cat, k_cat, v, k_t, do, lse, delta, ta, tb, tc)


def _place():
    return lax.axis_index("x"), lax.axis_index("y"), lax.axis_index("c")


def _slot(px, py, pc):
    return 4 * px + 2 * py + pc


def _all_gather(shards, layer, name, vmem=False, deps=()):
    n = len(shards)

    def body(*refs):
        srcs, outs = refs[:n], refs[n + len(deps):2 * n + len(deps)]
        send_sems, recv_sems, local_sems = refs[2 * n + len(deps):]
        x, y, c = _place()
        me, sibling = (x, y, c), (x, y, 1 - c)
        chips = [(1 - x, y), (x, 1 - y), (1 - x, 1 - y)]
        firsts, passes, locals_ = [], [], []

        def copy(a, k, block, to, src=None):
            dst = outs[a].at[_slot(*block)]
            return pltpu.make_async_remote_copy(
                src_ref=dst if src is None else src, dst_ref=dst,
                send_sem=send_sems.at[7 * a + k], recv_sem=recv_sems.at[7 * a + k],
                device_id=to, device_id_type=MESH)

        for a in range(n):
            src = srcs[a] if layer is None else srcs[a].at[layer]
            mine = pltpu.make_async_copy(src, outs[a].at[_slot(*me)], local_sems.at[a])
            mine.start()
            locals_.append(mine)
            first = [copy(a, 0, me, sibling, src=src)]
            first += [copy(a, 1 + j, me, (*chip, c), src=src) for j, chip in enumerate(chips)]
            for cp in first:
                cp.start()
            firsts += first
        for a in range(n):
            for j, chip in enumerate(chips):
                copy(a, 1 + j, (*chip, c), me).wait_recv()
                fwd = copy(a, 4 + j, (*chip, c), sibling)
                fwd.start()
                passes.append(fwd)
        for a in range(n):
            copy(a, 0, sibling, me).wait_recv()
            for j, chip in enumerate(chips):
                copy(a, 4 + j, (*chip, 1 - c), me).wait_recv()
        for cp in firsts + passes:
            cp.wait_send()
        for mine in locals_:
            mine.wait()

    space = pl.BlockSpec(memory_space=pltpu.VMEM) if vmem else ANY
    out_shape = [jax.ShapeDtypeStruct((N_DEV,) + (a.shape if layer is None else a.shape[1:]), a.dtype) for a in shards]
    return pl.pallas_call(
        body, name=name,
        in_specs=[space] * n + [ANY] * len(deps), out_specs=[space] * n, out_shape=out_shape,
        scratch_shapes=[pltpu.SemaphoreType.DMA((7 * n,)), pltpu.SemaphoreType.DMA((7 * n,)),
                        pltpu.SemaphoreType.DMA((n,))],
        compiler_params=pltpu.CompilerParams(has_side_effects=True),
    )(*shards, *deps)


HBM = pl.BlockSpec(memory_space=pltpu.HBM)
SEM = pl.BlockSpec(memory_space=pltpu.SEMAPHORE)
EFFECT = pltpu.SideEffectType.DATAFLOW_SIDE_EFFECTING


def _push_copies(srcs, lands, send_sems, recv_sems, local_sems, by_peer):
    x, y, c = _place()
    me = _slot(x, y, c)
    local, remote = [], []
    for a, (src, land) in enumerate(zip(srcs, lands)):
        local.append(pltpu.make_async_copy(src.at[me] if by_peer else src, land.at[me], local_sems.at[a]))
        for r in range(1, N_DEV):
            peer = (1 - x if r & 4 else x, 1 - y if r & 2 else y, 1 - c if r & 1 else c)
            remote.append(pltpu.make_async_remote_copy(
                src_ref=src.at[_slot(*peer)] if by_peer else src, dst_ref=land.at[me],
                send_sem=send_sems.at[7 * a + r - 1], recv_sem=recv_sems.at[7 * a + r - 1],
                device_id=peer, device_id_type=MESH))
    return local, remote


def _push_start(srcs, by_peer, after, name):
    n = len(srcs)
    srcs = [pltpu.with_memory_space_constraint(a, pltpu.HBM) for a in srcs]
    lands = [pltpu.with_memory_space_constraint(
        lax.empty((N_DEV,) + (a.shape[1:] if by_peer else a.shape), a.dtype), pltpu.HBM) for a in srcs]

    def body(*refs):
        k = 2 * n + len(after)
        local, remote = _push_copies(refs[:n], refs[n:2 * n], refs[k], refs[k + 1], refs[k + 2], by_peer)
        for cp in local + remote:
            cp.start()
        token = refs[k + 3 + 2 * n]
        token[...] = jnp.zeros_like(token)

    outs = pl.pallas_call(
        body, name=name,
        out_shape=(pltpu.SemaphoreType.DMA((7 * n,)), pltpu.SemaphoreType.DMA((7 * n,)),
                   pltpu.SemaphoreType.DMA((n,)),
                   *[pltpu.HBM(a.shape, a.dtype) for a in srcs], *[pltpu.HBM(a.shape, a.dtype) for a in lands],
                   jax.ShapeDtypeStruct((8, LANES), F32)),
        in_specs=[HBM] * (2 * n) + [ANY] * len(after),
        out_specs=(SEM, SEM, SEM, *([HBM] * (2 * n)), pl.BlockSpec(memory_space=pltpu.VMEM)),
        input_output_aliases={i: 3 + i for i in range(2 * n)},
        compiler_params=pltpu.CompilerParams(has_side_effects=EFFECT),
    )(*srcs, *lands, *after)
    return outs[:3], outs[3:3 + n], outs[3 + n:3 + 2 * n], outs[3 + 2 * n]


def _push_wait(sems, srcs, lands, by_peer, after, name):
    n = len(srcs)

    def body(*refs):
        local, remote = _push_copies(refs[:n], refs[n:2 * n], refs[2 * n], refs[2 * n + 1], refs[2 * n + 2], by_peer)
        for cp in local:
            cp.wait()
        for cp in remote:
            cp.wait_send()
            cp.wait_recv()

    outs = pl.pallas_call(
        body, name=name,
        out_shape=[pltpu.HBM(a.shape, a.dtype) for a in list(srcs) + list(lands)],
        in_specs=[HBM] * (2 * n) + [SEM] * 3 + [ANY] * len(after),
        out_specs=[HBM] * (2 * n),
        input_output_aliases={i: i for i in range(2 * n)},
        compiler_params=pltpu.CompilerParams(has_side_effects=EFFECT),
    )(*srcs, *lands, *sems, *after)
    return outs[n:]


def _adam_math(g, w, m, v):
    m = ADAM_B1 * m + (1.0 - ADAM_B1) * g
    v = ADAM_B2 * v + (1.0 - ADAM_B2) * (g * g)
    m_hat = m / (1.0 - ADAM_B1 ** ADAM_STEP)
    v_hat = v / (1.0 - ADAM_B2 ** ADAM_STEP)
    delta = -ADAM_LR * (m_hat / (jnp.sqrt(v_hat) + ADAM_EPS) + ADAM_WD * w)
    return delta, m, v


def _adam_sharded(recvs, first, w, m, v, prev, name):
    nl, r, c = w.shape
    n = len(recvs)
    by_cols = r % 128 != 0
    tr, tc = (r, _tile(c, LANES)) if by_cols else (_rtile(r, 128), c)
    nr = c // tc if by_cols else r // tr
    prev = list(prev) if prev is not None else []

    def tile(t):
        return (0, t) if by_cols else (t, 0)

    def body(*refs):
        g_refs = refs[:n]
        w_ref, m_ref, v_ref = refs[n:n + 3]
        go_ref, d_ref, mo_ref, vo_ref = refs[n + 3 + len(prev):]
        layer = pl.program_id(0)
        for l in range(n):
            @pl.when(layer == l)
            def _(l=l):
                g = g_refs[l][0].astype(F32)
                for i in range(1, N_DEV):
                    g = g + g_refs[l][i].astype(F32)
                delta, mn, vn = _adam_math(g, w_ref[...], m_ref[...], v_ref[...])
                go_ref[...] = g
                d_ref[...] = delta
                mo_ref[...] = mn
                vo_ref[...] = vn

    def recv_spec(l):
        def index(layer, i):
            return (0,) + tile(jnp.where(layer == l, i, jnp.where(layer < l, 0, nr - 1)))
        return pl.BlockSpec((N_DEV, tr, tc), index)

    blk = pl.BlockSpec((None, tr, tc), lambda layer, i: (first + layer,) + tile(i))
    out = jax.ShapeDtypeStruct(w.shape, F32)
    return pl.pallas_call(
        body, name=name, grid=(n, nr),
        in_specs=[recv_spec(l) for l in range(n)] + [blk, blk, blk] + [ANY] * len(prev),
        out_specs=[blk] * 4, out_shape=[out] * 4,
        input_output_aliases={n + 3 + k: k for k in range(len(prev))},
        compiler_params=_cp("arbitrary", "arbitrary"),
    )(*recvs, w, m, v, *prev)


def _adam_mod(c_all_t, dmod, w, m, v):
    nl, d, c = w.shape
    tr = _rtile(d, 128)

    def body(ct_ref, dm_ref, w_ref, m_ref, v_ref, go_ref, d_ref, mo_ref, vo_ref):
        ct = ct_ref[...].astype(BF16).astype(F32)
        dm = dm_ref[...].astype(BF16).astype(F32)
        g = ct[:, 0:1] * dm[0:1, :]
        for b in range(1, N_DEV):
            g = g + ct[:, b:b + 1] * dm[b:b + 1, :]
        delta, mn, vn = _adam_math(g, w_ref[...], m_ref[...], v_ref[...])
        go_ref[...] = g
        d_ref[...] = delta
        mo_ref[...] = mn
        vo_ref[...] = vn

    blk = pl.BlockSpec((None, tr, c), lambda layer, i: (layer, i, 0))
    out = jax.ShapeDtypeStruct(w.shape, F32)
    return pl.pallas_call(
        body, name="adam_mod", grid=(nl, d // tr),
        in_specs=[pl.BlockSpec((tr, N_DEV), lambda layer, i: (i, 0)),
                  pl.BlockSpec((None, N_DEV, c), lambda layer, i: (layer, 0, 0)), blk, blk, blk],
        out_specs=[blk] * 4, out_shape=[out] * 4,
        compiler_params=_cp("parallel", "parallel"),
    )(c_all_t, dmod, w, m, v)


def _adam_small(g, w, m, v, name):
    def body(g_ref, w_ref, m_ref, v_ref, d_ref, mo_ref, vo_ref):
        delta, mn, vn = _adam_math(g_ref[...], w_ref[...], m_ref[...], v_ref[...])
        d_ref[...] = delta
        mo_ref[...] = mn
        vo_ref[...] = vn

    out = jax.ShapeDtypeStruct(w.shape, F32)
    return pl.pallas_call(body, name=name, out_shape=[out] * 3)(g, w, m, v)


def _sum_devices(parts):
    def body(p_ref, o_ref):
        acc = p_ref[0]
        for i in range(1, N_DEV):
            acc = acc + p_ref[i]
        o_ref[...] = acc

    return pl.pallas_call(body, name="sum_devices",
                          out_shape=jax.ShapeDtypeStruct(parts.shape[1:], F32))(parts)


def _rope_tables(positions):
    pos = positions.astype(F32)[:, None]

    def cs(dim):
        inv = 1.0 / (ROPE_BASE ** (jnp.arange(0, dim, 2, dtype=F32) / dim))
        ang = pos * inv
        return jnp.cos(ang), jnp.sin(ang)

    cr, sr = cs(RET_DK)
    cm, sm = cs(MLA_ROPE)
    z = jnp.zeros_like(cm)
    pad = jnp.zeros((pos.shape[0], LANES - MLA_ROPE), F32)
    cosr = jnp.concatenate([cr, cr], axis=1)
    sinr = jnp.concatenate([-sr, sr], axis=1)
    ta = jnp.concatenate([cm, cm, pad], axis=1)
    tb = jnp.concatenate([-sm, z, pad], axis=1)
    tc = jnp.concatenate([z, sm, pad], axis=1)
    return cosr, sinr, ta, tb, tc


def _layer_fwd(x, mod, g_norm, g_cq, g_ckv, wts, tabs, lg, lay, deps):
    d = x.shape[1]
    cosr, sinr, ta, tb, tc = tabs
    shift, scale, gate = mod[:, :d], mod[:, d:2 * d], mod[:, 2 * d:]
    h = _norm_mod_fwd(x, g_norm, scale, shift, deps)
    proj = _matmul(h, wts["in"], name="mm_proj", tn_cap=1920)
    o_ret, u_ret, states = _ret_fwd(proj, lg, cosr, sinr, lay)
    y_ret = _matmul(u_ret, wts["ret"], name="mm_y")
    cqn, ckvn = _mla_prep(proj, g_cq, g_ckv, lay)
    qp = _matmul(cqn, wts["uq"], name="mm_up")
    kvp = _matmul(ckvn, wts["ukv"], name="mm_up")
    q_cat, k_cat, v, k_t, v_t = _qk_prep(qp, kvp, proj, ta, tb, tc, lay)
    o_mla, u_mla, lse = _attn_fwd(q_cat, k_cat, v_t, proj, lay)
    y_mla = _matmul(u_mla, wts["mla"], name="mm_y")
    merged = _merge_fwd(y_ret, y_mla, proj, lay)
    out = _matmul(merged, wts["out"], name="mm_y")
    x_next = _resid_fwd(x, out, gate)
    saved = dict(x=x, h=h, proj=proj, o_ret=o_ret, u_ret=u_ret, states=states, y_ret=y_ret, cqn=cqn,
                 ckvn=ckvn, q_cat=q_cat, k_cat=k_cat, v=v, k_t=k_t, o_mla=o_mla, u_mla=u_mla, lse=lse,
                 y_mla=y_mla, merged=merged, out=out, wts=wts)
    return x_next, saved


def _to_owner_blocks_cols(g, n_local):
    k = g.shape[0]
    return g.reshape(k, N_DEV, n_local).transpose(1, 0, 2)


def _from_owner_blocks_cols(g):
    return g.transpose(1, 0, 2).reshape(g.shape[1], -1)


def _layer_bwd(dxn, sv, mod, g_norm, g_cq, g_ckv, wts, tabs, lg, lay, deps, shard_cols, push):
    d = dxn.shape[1]
    n_in, n_uq, n_ukv = shard_cols
    cosr, sinr, ta, tb, tc = tabs
    scale, gate = mod[:, d:2 * d], mod[:, 2 * d:]
    gdt = BF16
    dout, dgate = _resid_bwd(dxn, sv["out"], gate, deps)
    dmerged = _matmul(dout, wts["out"], tb=True, name="mm_dy")
    dw_out = _matmul(sv["merged"], dout, ta=True, out_dtype=gdt, name="mm_dw")
    dy_ret, dy_mla, dproj = _merge_bwd(dmerged, sv["y_ret"], sv["y_mla"], sv["proj"], lay)
    du_ret = _matmul(dy_ret, wts["ret"], tb=True, name="mm_dy")
    dw_ret = _matmul(sv["u_ret"], dy_ret, ta=True, out_dtype=gdt, name="mm_dw")
    dproj = _ret_bwd(sv["proj"], lg, cosr, sinr, sv["o_ret"], du_ret, sv["states"], dproj, lay)
    du_mla = _matmul(dy_mla, wts["mla"], tb=True, name="mm_dy")
    dw_mla = _matmul(sv["u_mla"], dy_mla, ta=True, out_dtype=gdt, name="mm_dw")
    do_mla, delta, dproj = _mla_gate_bwd(du_mla, sv["o_mla"], sv["proj"], dproj, lay)
    delta = delta[:, :MLA_HEADS].T.reshape(sv["lse"].shape)
    dqp, dk_cat, dv = _attn_bwd(sv["q_cat"], sv["k_cat"], sv["v"], sv["k_t"], do_mla, sv["lse"], delta,
                                ta, tb, tc)
    dkvp, dproj = _kv_bwd_prep(dk_cat, dv, ta, tb, tc, dproj, lay)
    dcqn = _matmul(dqp, wts["uq"], tb=True, name="mm_dlat")
    dckvn = _matmul(dkvp, wts["ukv"], tb=True, name="mm_dlat")
    dw_uq = _matmul(sv["cqn"], dqp, ta=True, out_dtype=gdt, name="mm_dwup")
    dw_ukv = _matmul(sv["ckvn"], dkvp, ta=True, out_dtype=gdt, name="mm_dwup")
    sent = push("a", [_to_owner_blocks_cols(_uq_to_logical(dw_uq), n_uq),
                      _to_owner_blocks_cols(_ukv_to_logical(dw_ukv), n_ukv),
                      dw_ret.reshape(N_DEV, -1, d), dw_mla.reshape(N_DEV, -1, d), dw_out.reshape(N_DEV, -1, d)])
    dproj, dg_cq, dg_ckv = _mla_prep_bwd(sv["proj"], dcqn, dckvn, g_cq, g_ckv, dproj, lay)
    dw_in = _matmul(sv["h"], dproj, ta=True, out_dtype=gdt, name="mm_dwin", tn_cap=1920, deps=sent)
    sent = push("b", [_scatter_dw_in(dw_in, lay, n_in)])
    dh = _matmul(dproj, wts["in"], tb=True, name="mm_dh", deps=sent)
    dx, dshift, dscale, dg_norm = _norm_mod_bwd(sv["x"], g_norm, scale, dh, dxn)
    dmod = jnp.concatenate([dshift, dscale, dgate], axis=1)
    small = dict(dmod=dmod, g_norm=dg_norm, g_cq=dg_cq, g_ckv=dg_ckv)
    return dx, small


def kernel(x, c, positions, w_mod, b_mod, g_norm, w_in, g_cq, g_ckv, w_uq, w_ukv, w_ret_proj, w_mla_proj, w_out, g_final, loss_target, m_w_mod, m_b_mod, m_g_norm, m_w_in, m_g_cq, m_g_ckv, m_w_uq, m_w_ukv, m_w_ret_proj, m_w_mla_proj, m_w_out, m_g_final, v_w_mod, v_b_mod, v_g_norm, v_w_in, v_g_cq, v_g_ckv, v_w_uq, v_w_ukv, v_w_ret_proj, v_w_mla_proj, v_w_out, v_g_final):
    nl, d, _ = w_mod.shape
    s = x.shape[1]
    rank = g_cq.shape[1]
    lay = Layout(d, rank, g_ckv.shape[1])
    me = _slot(*_place())
    x0 = x.reshape(s, d)
    target = loss_target.reshape(s, d)
    tabs = _rope_tables(positions.reshape(s))
    lg = jnp.log(1.0 - 2.0 ** (-5.0 - jnp.arange(RET_HEADS, dtype=F32)))

    c_act = c * _sigmoid(c)
    (c_all,) = _all_gather([c_act.reshape(d // LANES, LANES)], None, "gather_c", vmem=True)
    c_all = c_all.reshape(N_DEV, d)
    n_mod = w_mod.shape[2]
    mod_part = jnp.stack([_matmul(c_all, w_mod[l], name="mm_mod", tm_cap=8) for l in range(nl)])
    (mod_all,) = _all_gather([mod_part.reshape(-1, LANES)], None, "gather_mod", vmem=True)
    mod_all = mod_all.reshape(N_DEV, nl, N_DEV, n_mod)
    mod = lax.dynamic_index_in_dim(mod_all, me, axis=2, keepdims=False)
    mod = mod.transpose(1, 0, 2).reshape(nl, N_DEV * n_mod) + b_mod

    shards = [[w[l].astype(BF16) for w in (w_in, w_uq, w_ukv, w_ret_proj, w_mla_proj, w_out)] for l in range(nl)]
    xl, saved = x0, []
    gathered = _all_gather(shards[0], None, "gather_w")
    for l in range(nl):
        g_in, g_uq, g_ukv, g_ret, g_mla, g_out = gathered
        deps = []
        if l + 1 < nl:
            sems, srcs, lands, token = _push_start(shards[l + 1], False, [g_out, mod], "gather_start_%d" % (l + 1))
            deps = [token]
        wts = {
            "in": _assemble_w_in(g_in, lay),
            "uq": _uq_to_physical(_from_owner_blocks_cols(g_uq)),
            "ukv": _ukv_to_physical(_from_owner_blocks_cols(g_ukv)),
            "ret": g_ret.reshape(-1, d), "mla": g_mla.reshape(-1, d), "out": g_out.reshape(-1, d),
        }
        xl, sv = _layer_fwd(xl, mod[l:l + 1], g_norm[l:l + 1], g_cq[l:l + 1], g_ckv[l:l + 1], wts, tabs, lg, lay,
                            deps)
        saved.append(sv)
        if l + 1 < nl:
            gathered = _push_wait(sems, srcs, lands, False, [xl], "gather_wait_%d" % (l + 1))
    loss_lanes, dx, dg_final = _final_loss(xl, g_final.reshape(1, d), target)

    small = [None] * nl
    flying = {l: [] for l in range(nl)}
    recv = {}
    shard_cols = (w_in.shape[2], w_uq.shape[2], w_ukv.shape[2])

    def pusher(l):
        def push(group, arrays):
            sems, srcs, lands, token = _push_start(arrays, True, [], "exchange_start_%d%s" % (l, group))
            flying[l].append((group, sems, srcs, lands))
            return [token]
        return push

    def land(l, after):
        got = {}
        for group, sems, srcs, lands in flying[l]:
            got[group] = _push_wait(sems, srcs, lands, True, after, "exchange_wait_%d%s" % (l, group))
        recv[l] = list(got["b"]) + list(got["a"])

    for l in reversed(range(nl)):
        dx, small[l] = _layer_bwd(dx, saved[l], mod[l:l + 1], g_norm[l:l + 1], g_cq[l:l + 1], g_ckv[l:l + 1],
                                  saved[l]["wts"], tabs, lg, lay, [], shard_cols, pusher(l))
        if l + 1 < nl:
            land(l + 1, [dx])
    grad_x = dx.reshape(x.shape)

    out = {}
    w_in_t, m_w_in_t, v_w_in_t = (jnp.swapaxes(a, 1, 2) for a in (w_in, m_w_in, v_w_in))
    sharded = (("w_in", w_in_t, m_w_in_t, v_w_in_t), ("w_uq", w_uq, m_w_uq, v_w_uq), ("w_ukv", w_ukv, m_w_ukv, v_w_ukv),
               ("w_ret_proj", w_ret_proj, m_w_ret_proj, v_w_ret_proj),
               ("w_mla_proj", w_mla_proj, m_w_mla_proj, v_w_mla_proj), ("w_out", w_out, m_w_out, v_w_out))
    if nl > 1:
        for i, (key, w, m, v) in enumerate(sharded):
            out[key] = _adam_sharded([recv[l][i] for l in range(1, nl)], 1, w, m, v, None, "adam_" + key)
    done = [out[key][0] for key, _, _, _ in sharded if key in out]

    pack = jnp.concatenate(
        [jnp.concatenate([sm[k] for sm in small], axis=0).reshape(-1)
         for k in ("dmod", "g_norm", "g_cq", "g_ckv")] + [dg_final.reshape(-1), loss_lanes.reshape(-1)])
    (pack_all,) = _all_gather([pack.reshape(-1, LANES)], None, "gather_small", vmem=True, deps=done)
    tot = _sum_devices(pack_all).reshape(-1)
    sizes = [nl * 3 * d, nl * d, nl * rank, nl * rank, d]
    offs = np.cumsum([0] + sizes)
    grad_b_mod = tot[offs[0]:offs[1]].reshape(nl, 3 * d)
    grad_g_norm = tot[offs[1]:offs[2]].reshape(nl, d)
    grad_g_cq = tot[offs[2]:offs[3]].reshape(nl, rank)
    grad_g_ckv = tot[offs[3]:offs[4]].reshape(nl, rank)
    grad_g_final = tot[offs[4]:offs[5]]
    loss = tot[offs[5]]
    dmod_all = pack_all.reshape(N_DEV, -1)[:, :sizes[0]].reshape(N_DEV, nl, 3 * d)
    dmod_mine = lax.dynamic_slice_in_dim(dmod_all, me * n_mod, n_mod, axis=2).transpose(1, 0, 2)

    out["w_mod"] = _adam_mod(c_all.T, dmod_mine, w_mod, m_w_mod, v_w_mod)
    land(0, [out["w_mod"][0]])
    for i, (key, w, m, v) in enumerate(sharded):
        out[key] = _adam_sharded([recv[0][i]], 0, w, m, v, out.get(key), "adam0_" + key)
    out["w_in"] = tuple(jnp.swapaxes(a, 1, 2) for a in out["w_in"])
    for key, g, w, m, v in (("b_mod", grad_b_mod, b_mod, m_b_mod, v_b_mod),
                            ("g_norm", grad_g_norm, g_norm, m_g_norm, v_g_norm),
                            ("g_cq", grad_g_cq, g_cq, m_g_cq, v_g_cq),
                            ("g_ckv", grad_g_ckv, g_ckv, m_g_ckv, v_g_ckv),
                            ("g_final", grad_g_final.reshape(1, d), g_final.reshape(1, d),
                             m_g_final.reshape(1, d), v_g_final.reshape(1, d))):
        out[key] = (g,) + tuple(_adam_small(g, w, m, v, "adam_" + key))
    out["g_final"] = tuple(a.reshape(d) for a in out["g_final"])

    names = ("w_mod", "b_mod", "g_norm", "w_in", "g_cq", "g_ckv", "w_uq", "w_ukv", "w_ret_proj",
             "w_mla_proj", "w_out", "g_final")
    return (loss, grad_x, *[out[k][0] for k in names], *[out[k][1] for k in names],
            *[out[k][2] for k in names], *[out[k][3] for k in names])
```

```python
import functools
import itertools

import jax
import jax.numpy as jnp
import numpy as np
from jax import lax
from jax.experimental import pallas as pl
from jax.experimental.pallas import tpu as pltpu

F32 = jnp.float32
BF16 = jnp.bfloat16

N_DEV = 8
CHUNK = 64
EPS = 1e-6
NEG_INF = -1e30
ROPE_BASE = 10000.0
LANES = 128

RET_HEADS = 8
RET_DK = 128
RET_DV = 256
MLA_HEADS = 16
MLA_NOPE = 128
MLA_ROPE = 64
MLA_DV = 128
MLA_QW = 256
QK_SCALE = (MLA_NOPE + MLA_ROPE) ** -0.5
QK_LOG2_SCALE = QK_SCALE * 1.4426950408889634
LN2 = 0.6931471805599453

ADAM_LR = 0.001
ADAM_B1 = 0.9
ADAM_B2 = 0.999
ADAM_EPS = 1e-08
ADAM_WD = 0.01
ADAM_STEP = 10

VMEM_LIMIT_BYTES = 56 * 1024 * 1024
MESH = pl.DeviceIdType.MESH
ANY = pl.BlockSpec(memory_space=pl.ANY)


def _cp(*sem):
    return pltpu.CompilerParams(dimension_semantics=sem if sem else None,
                                vmem_limit_bytes=VMEM_LIMIT_BYTES)


def _tile(n, cap):
    best = None
    t = LANES
    while t <= min(n, cap):
        if n % t == 0:
            best = t
        t += LANES
    return best if best is not None else n


def _rtile(n, cap):
    t = cap
    while t > 8 and n % t:
        t //= 2
    return t if n % t == 0 else n


def _sigmoid(x):
    return 1.0 / (1.0 + jnp.exp(-x))


def _dot(a, b):
    return lax.dot_general(a, b, (((1,), (0,)), ((), ())), preferred_element_type=F32)


def _dot_nt(a, b):
    return lax.dot_general(a, b, (((1,), (1,)), ((), ())), preferred_element_type=F32)


def _dot_tn(a, b):
    return lax.dot_general(a, b, (((0,), (0,)), ((), ())), preferred_element_type=F32)


def _roll(x, s):
    return pltpu.roll(x, s, 1)


class Layout:
    def __init__(self, d_model, q_rank, kv_rank):
        assert q_rank == kv_rank
        self.d = d_model
        self.rank = q_rank
        self.ret_w = 2 * RET_DK + 2 * RET_DV
        self.ret_qk = RET_HEADS * RET_DK
        self.ret_v = RET_HEADS * RET_DV
        self.mla_v = MLA_HEADS * MLA_DV
        widths = {"bg": 2 * d_model, "mg": self.mla_v, "ret": RET_HEADS * self.ret_w,
                  "cqkv": 2 * q_rank, "kr": LANES}
        blocks = {"bg": 2 * d_model, "mg": self.mla_v, "ret": self.ret_w,
                  "cqkv": 2 * q_rank, "kr": LANES}
        for order in itertools.permutations(widths):
            off, offs, ok = 0, {}, True
            for name in order:
                if off % blocks[name]:
                    ok = False
                    break
                offs[name] = off
                off += widths[name]
            if ok:
                break
        assert ok, "no aligned layout"
        self.order, self.off, self.width, self.total = order, offs, widths, off
        lo, o = {}, 0
        for name, w in (("rq", self.ret_qk), ("rk", self.ret_qk), ("rv", self.ret_v),
                        ("rg", self.ret_v), ("cq", q_rank), ("ckv", kv_rank), ("kr", MLA_ROPE),
                        ("mg", self.mla_v), ("bg", 2 * d_model)):
            lo[name] = (o, w)
            o += w
        self.logical, self.d_in = lo, o

    def pieces(self):
        lo = self.logical
        out = []
        for name in self.order:
            if name == "bg":
                out.append(lo["bg"])
            elif name == "mg":
                out.append(lo["mg"])
            elif name == "ret":
                for h in range(RET_HEADS):
                    out.append((lo["rq"][0] + h * RET_DK, RET_DK))
                    out.append((lo["rk"][0] + h * RET_DK, RET_DK))
                    out.append((lo["rv"][0] + h * RET_DV, RET_DV))
                    out.append((lo["rg"][0] + h * RET_DV, RET_DV))
            elif name == "cqkv":
                out.append((lo["cq"][0], 2 * self.rank))
            elif name == "kr":
                out.append(lo["kr"])
                out.append((None, LANES - MLA_ROPE))
        return out


RELAYOUT_CHUNK = 512


def _relayout_plan(lay, n_local):
    plan, off = [], 0
    for start, width in lay.pieces():
        done = 0
        while done < width:
            w = min(RELAYOUT_CHUNK, width - done)
            srcs = []
            if start is not None:
                lo, hi = start + done, start + done + w
                while lo < hi:
                    j = lo // n_local
                    end = min(hi, (j + 1) * n_local)
                    srcs.append((j, lo - j * n_local, end - j * n_local))
                    lo = end
            plan.append((off + done, w, srcs))
            done += w
        off += width
    merged = []
    for p, w, srcs in plan:
        if merged and merged[-1][0] % LANES == 0 and (merged[-1][1] % LANES) and p == merged[-1][0] + merged[-1][1]:
            q, qw, qs = merged.pop()
            merged.append((q, qw + w, qs + ([("pad", w)] if not srcs else srcs)))
        else:
            merged.append((p, w, srcs))
    return merged


def _assemble_w_in(g, lay):
    _, d, n_local = g.shape
    tr = _rtile(d, 256)
    plan = _relayout_plan(lay, n_local)

    def body(g_ref, o_ref):
        for p, w, srcs in plan:
            parts = []
            for src in srcs:
                if src[0] == "pad":
                    parts.append(jnp.zeros((tr, src[1]), F32))
                else:
                    j, a, b = src
                    parts.append(g_ref[j, :, a:b].astype(F32))
            if not parts:
                parts = [jnp.zeros((tr, w), F32)]
            val = parts[0] if len(parts) == 1 else jnp.concatenate(parts, axis=1)
            o_ref[:, p:p + w] = val.astype(o_ref.dtype)

    return pl.pallas_call(
        body, name="assemble_w_in", grid=(d // tr,),
        in_specs=[pl.BlockSpec((N_DEV, tr, n_local), lambda i: (0, i, 0))],
        out_specs=pl.BlockSpec((tr, lay.total), lambda i: (i, 0)),
        out_shape=jax.ShapeDtypeStruct((d, lay.total), g.dtype),
        compiler_params=_cp("parallel"),
    )(g)


def _scatter_dw_in(dw, lay, n_local):
    d = dw.shape[0]
    tr = _rtile(d, 256)
    n_pad = -(-n_local // LANES) * LANES
    plan = _relayout_plan(lay, n_local)
    runs = [[] for _ in range(N_DEV)]
    for p, w, srcs in plan:
        at = p
        for src in srcs:
            if src[0] == "pad":
                at += src[1]
                continue
            j, a, b = src
            runs[j].append((a, b, at))
            at += b - a
    for r in runs:
        r.sort()

    def body(dw_ref, o_ref):
        for j in range(N_DEV):
            parts = [dw_ref[:, at:at + (b - a)].astype(F32) for a, b, at in runs[j]]
            if n_pad > n_local:
                parts.append(jnp.zeros((tr, n_pad - n_local), F32))
            val = jnp.concatenate(parts, axis=1).T
            o_ref[j] = val[:n_local, :].astype(BF16)

    return pl.pallas_call(
        body, name="scatter_dw_in", grid=(d // tr,),
        in_specs=[pl.BlockSpec((tr, lay.total), lambda i: (i, 0))],
        out_specs=pl.BlockSpec((N_DEV, n_local, tr), lambda i: (0, 0, i)),
        out_shape=jax.ShapeDtypeStruct((N_DEV, n_local, d), BF16),
        compiler_params=_cp("parallel"),
    )(dw)


def _uq_to_physical(w):
    k = w.shape[0]
    w3 = w.reshape(k, MLA_HEADS, MLA_NOPE + MLA_ROPE)
    pad = jnp.zeros((k, MLA_HEADS, MLA_QW - MLA_NOPE - MLA_ROPE), w.dtype)
    return jnp.concatenate([w3, pad], axis=2).reshape(k, MLA_HEADS * MLA_QW)


def _uq_to_logical(w):
    k = w.shape[0]
    return w.reshape(k, MLA_HEADS, MLA_QW)[:, :, :MLA_NOPE + MLA_ROPE].reshape(k, -1)


def _ukv_to_physical(w):
    k = w.shape[0]
    w3 = w.reshape(k, MLA_HEADS, MLA_NOPE + MLA_DV)
    return jnp.concatenate([w3[:, :, :MLA_NOPE].reshape(k, -1), w3[:, :, MLA_NOPE:].reshape(k, -1)], axis=1)


def _ukv_to_logical(w):
    k = w.shape[0]
    kn = w[:, :MLA_HEADS * MLA_NOPE].reshape(k, MLA_HEADS, MLA_NOPE)
    v = w[:, MLA_HEADS * MLA_NOPE:].reshape(k, MLA_HEADS, MLA_DV)
    return jnp.concatenate([kn, v], axis=2).reshape(k, -1)


def _matmul(a, b, *, ta=False, tb=False, out_dtype=F32, name, tm_cap=1024, tn_cap=2048, tk_cap=2048, deps=()):
    m, k = (a.shape[1], a.shape[0]) if ta else a.shape
    n = b.shape[0] if tb else b.shape[1]
    assert k == (b.shape[1] if tb else b.shape[0])
    tm, tn, tk = _tile(m, tm_cap), _tile(n, tn_cap), _tile(k, tk_cap)
    nk = k // tk

    def body(a_ref, b_ref, *rest):
        dims = (((0 if ta else 1,), (1 if tb else 0,)), ((), ()))
        part = lax.dot_general(a_ref[...].astype(BF16), b_ref[...].astype(BF16), dims, preferred_element_type=F32)
        if nk == 1:
            rest[len(deps)][...] = part.astype(out_dtype)
            return
        o_ref, acc_ref = rest[len(deps):]
        kk = pl.program_id(2)

        @pl.when(kk == 0)
        def _():
            acc_ref[...] = part

        @pl.when(jnp.logical_and(kk > 0, kk < nk - 1))
        def _():
            acc_ref[...] += part

        @pl.when(kk == nk - 1)
        def _():
            o_ref[...] = (acc_ref[...] + part).astype(o_ref.dtype)

    a_spec = pl.BlockSpec((tk, tm), lambda i, j, kk: (kk, i)) if ta else pl.BlockSpec((tm, tk), lambda i, j, kk: (i, kk))
    b_spec = pl.BlockSpec((tn, tk), lambda i, j, kk: (j, kk)) if tb else pl.BlockSpec((tk, tn), lambda i, j, kk: (kk, j))
    return pl.pallas_call(
        body, name=name, grid=(m // tm, n // tn, nk),
        in_specs=[a_spec, b_spec] + [ANY] * len(deps),
        out_specs=pl.BlockSpec((tm, tn), lambda i, j, kk: (i, j)),
        out_shape=jax.ShapeDtypeStruct((m, n), out_dtype),
        scratch_shapes=[pltpu.VMEM((tm, tn), F32)] if nk > 1 else [],
        compiler_params=_cp("parallel", "parallel", "arbitrary"),
    )(a, b, *deps)


def _row(tm, w, cb=0):
    return pl.BlockSpec((tm, w), lambda i, cb=cb: (i, cb))


def _vec(w, cb=0):
    return pl.BlockSpec((1, w), lambda i, cb=cb: (0, cb))


def _dproj_out(dproj, s, lay, tm, name):
    w = lay.width[name]
    cb = lay.off[name] // w
    spec = _row(tm, w, cb)
    shape = jax.ShapeDtypeStruct((s, lay.total), BF16)
    return spec, shape


def _norm_mod_fwd(x, g, scale, shift, deps):
    s, d = x.shape
    tm = _rtile(s, 256)

    def body(x_ref, g_ref, sc_ref, sh_ref, *rest):
        h_ref = rest[len(deps)]
        xv = x_ref[...]
        xh = xv * lax.rsqrt(jnp.mean(xv * xv, axis=-1, keepdims=True) + EPS)
        h_ref[...] = ((xh * g_ref[...]) * (1.0 + sc_ref[...]) + sh_ref[...]).astype(BF16)

    return pl.pallas_call(
        body, name="norm_mod_fwd", grid=(s // tm,),
        in_specs=[_row(tm, d), _vec(d), _vec(d), _vec(d)] + [ANY] * len(deps),
        out_specs=_row(tm, d), out_shape=jax.ShapeDtypeStruct((s, d), BF16),
        compiler_params=_cp("parallel"),
    )(x, g, scale, shift, *deps)


def _norm_mod_bwd(x, g, scale, dh, dres):
    s, d = x.shape
    tm = _rtile(s, 256)

    def body(x_ref, g_ref, sc_ref, dh_ref, dres_ref, dx_ref, dsh_ref, dsc_ref, dg_ref):
        @pl.when(pl.program_id(0) == 0)
        def _():
            dsh_ref[...] = jnp.zeros_like(dsh_ref)
            dsc_ref[...] = jnp.zeros_like(dsc_ref)
            dg_ref[...] = jnp.zeros_like(dg_ref)

        xv, gv, dhv = x_ref[...], g_ref[...], dh_ref[...]
        rstd = lax.rsqrt(jnp.mean(xv * xv, axis=-1, keepdims=True) + EPS)
        xh = xv * rstd
        dy = dhv * (1.0 + sc_ref[...])
        dxh = dy * gv
        dx_ref[...] = dres_ref[...] + rstd * (dxh - xh * jnp.mean(dxh * xh, axis=-1, keepdims=True))
        dsh_ref[...] += jnp.sum(dhv, axis=0, keepdims=True)
        dsc_ref[...] += jnp.sum(dhv * (xh * gv), axis=0, keepdims=True)
        dg_ref[...] += jnp.sum(dy * xh, axis=0, keepdims=True)

    vec = jax.ShapeDtypeStruct((1, d), F32)
    return pl.pallas_call(
        body, name="norm_mod_bwd", grid=(s // tm,),
        in_specs=[_row(tm, d), _vec(d), _vec(d), _row(tm, d), _row(tm, d)],
        out_specs=[_row(tm, d), _vec(d), _vec(d), _vec(d)],
        out_shape=[jax.ShapeDtypeStruct((s, d), F32), vec, vec, vec],
        compiler_params=_cp("arbitrary"),
    )(x, g, scale, dh, dres)


def _final_loss(x, g, target):
    s, d = x.shape
    tm = _rtile(s, 256)

    def body(x_ref, g_ref, t_ref, l_ref, dx_ref, dg_ref):
        @pl.when(pl.program_id(0) == 0)
        def _():
            l_ref[...] = jnp.zeros_like(l_ref)
            dg_ref[...] = jnp.zeros_like(dg_ref)

        xv, gv = x_ref[...], g_ref[...]
        rstd = lax.rsqrt(jnp.mean(xv * xv, axis=-1, keepdims=True) + EPS)
        xh = xv * rstd
        err = xh * gv - t_ref[...]
        row = jnp.mean(err * err, axis=-1, keepdims=True)
        l_ref[...] += 0.5 * jnp.sum(row, axis=0, keepdims=True)
        dy = err / d
        dxh = dy * gv
        dx_ref[...] = rstd * (dxh - xh * jnp.mean(dxh * xh, axis=-1, keepdims=True))
        dg_ref[...] += jnp.sum(dy * xh, axis=0, keepdims=True)

    return pl.pallas_call(
        body, name="final_loss", grid=(s // tm,),
        in_specs=[_row(tm, d), _vec(d), _row(tm, d)],
        out_specs=[_vec(LANES), _row(tm, d), _vec(d)],
        out_shape=[jax.ShapeDtypeStruct((1, LANES), F32), jax.ShapeDtypeStruct((s, d), F32),
                   jax.ShapeDtypeStruct((1, d), F32)],
        compiler_params=_cp("arbitrary"),
    )(x, g, target)


def _resid_fwd(x, out, gate):
    s, d = x.shape
    tm = _rtile(s, 256)

    def body(x_ref, o_ref, g_ref, y_ref):
        y_ref[...] = x_ref[...] + g_ref[...] * o_ref[...]

    return pl.pallas_call(
        body, name="resid_fwd", grid=(s // tm,),
        in_specs=[_row(tm, d), _row(tm, d), _vec(d)],
        out_specs=_row(tm, d), out_shape=jax.ShapeDtypeStruct((s, d), F32),
        compiler_params=_cp("parallel"),
    )(x, out, gate)


def _resid_bwd(dxn, out, gate, deps):
    s, d = dxn.shape
    tm = _rtile(s, 256)

    def body(dx_ref, o_ref, g_ref, *rest):
        do_ref, dg_ref = rest[len(deps):]

        @pl.when(pl.program_id(0) == 0)
        def _():
            dg_ref[...] = jnp.zeros_like(dg_ref)

        dxv = dx_ref[...]
        do_ref[...] = (dxv * g_ref[...]).astype(BF16)
        dg_ref[...] += jnp.sum(dxv * o_ref[...], axis=0, keepdims=True)

    return pl.pallas_call(
        body, name="resid_bwd", grid=(s // tm,),
        in_specs=[_row(tm, d), _row(tm, d), _vec(d)] + [ANY] * len(deps),
        out_specs=[_row(tm, d), _vec(d)],
        out_shape=[jax.ShapeDtypeStruct((s, d), BF16), jax.ShapeDtypeStruct((1, d), F32)],
        compiler_params=_cp("arbitrary"),
    )(dxn, out, gate, *deps)


def _merge_fwd(y_ret, y_mla, proj, lay):
    s, d = y_ret.shape
    tm = _rtile(s, 256)
    cb = lay.off["bg"] // (2 * d)

    def body(a_ref, b_ref, bg_ref, m_ref):
        sg = _sigmoid(bg_ref[...])
        m_ref[...] = (sg[:, :d] * a_ref[...] + sg[:, d:] * b_ref[...]).astype(BF16)

    return pl.pallas_call(
        body, name="merge_fwd", grid=(s // tm,),
        in_specs=[_row(tm, d), _row(tm, d), _row(tm, 2 * d, cb)],
        out_specs=_row(tm, d), out_shape=jax.ShapeDtypeStruct((s, d), BF16),
        compiler_params=_cp("parallel"),
    )(y_ret, y_mla, proj)


def _merge_bwd(dm, y_ret, y_mla, proj, lay):
    s, d = dm.shape
    tm = _rtile(s, 256)
    cb = lay.off["bg"] // (2 * d)
    dp_spec, dp_shape = _dproj_out(None, s, lay, tm, "bg")

    def body(dm_ref, a_ref, b_ref, bg_ref, da_ref, db_ref, dp_ref):
        sg = _sigmoid(bg_ref[...])
        dmv = dm_ref[...]
        ga, gb = sg[:, :d], sg[:, d:]
        da_ref[...] = (dmv * ga).astype(BF16)
        db_ref[...] = (dmv * gb).astype(BF16)
        dp_ref[:, :d] = (dmv * a_ref[...] * ga * (1.0 - ga)).astype(BF16)
        dp_ref[:, d:] = (dmv * b_ref[...] * gb * (1.0 - gb)).astype(BF16)

    act = jax.ShapeDtypeStruct((s, d), BF16)
    return pl.pallas_call(
        body, name="merge_bwd", grid=(s // tm,),
        in_specs=[_row(tm, d), _row(tm, d), _row(tm, d), _row(tm, 2 * d, cb)],
        out_specs=[_row(tm, d), _row(tm, d), dp_spec],
        out_shape=[act, act, dp_shape],
        compiler_params=_cp("parallel"),
    )(dm, y_ret, y_mla, proj)


def _mla_prep(proj, g_cq, g_ckv, lay):
    s = proj.shape[0]
    r = lay.rank
    tm = _rtile(s, 512)
    cb = lay.off["cqkv"] // (2 * r)

    def body(p_ref, gq_ref, gk_ref, q_ref, k_ref):
        pv = p_ref[...]
        for lo, g_ref, o_ref in ((0, gq_ref, q_ref), (r, gk_ref, k_ref)):
            xv = pv[:, lo:lo + r]
            xh = xv * lax.rsqrt(jnp.mean(xv * xv, axis=-1, keepdims=True) + EPS)
            o_ref[...] = (xh * g_ref[...]).astype(BF16)

    act = jax.ShapeDtypeStruct((s, r), BF16)
    return pl.pallas_call(
        body, name="mla_prep", grid=(s // tm,),
        in_specs=[_row(tm, 2 * r, cb), _vec(r), _vec(r)],
        out_specs=[_row(tm, r), _row(tm, r)], out_shape=[act, act],
        compiler_params=_cp("parallel"),
    )(proj, g_cq, g_ckv)


def _mla_prep_bwd(proj, dqn, dkn, g_cq, g_ckv, dproj, lay):
    s = proj.shape[0]
    r = lay.rank
    tm = _rtile(s, 512)
    cb = lay.off["cqkv"] // (2 * r)
    dp_spec, dp_shape = _dproj_out(dproj, s, lay, tm, "cqkv")

    def body(p_ref, dq_ref, dk_ref, gq_ref, gk_ref, _, dp_ref, dgq_ref, dgk_ref):
        @pl.when(pl.program_id(0) == 0)
        def _():
            dgq_ref[...] = jnp.zeros_like(dgq_ref)
            dgk_ref[...] = jnp.zeros_like(dgk_ref)

        pv = p_ref[...]
        for lo, g_ref, d_ref, dg_ref in ((0, gq_ref, dq_ref, dgq_ref), (r, gk_ref, dk_ref, dgk_ref)):
            xv = pv[:, lo:lo + r]
            rstd = lax.rsqrt(jnp.mean(xv * xv, axis=-1, keepdims=True) + EPS)
            xh = xv * rstd
            dy = d_ref[...]
            dxh = dy * g_ref[...]
            dp_ref[:, lo:lo + r] = (rstd * (dxh - xh * jnp.mean(dxh * xh, axis=-1, keepdims=True))).astype(BF16)
            dg_ref[...] += jnp.sum(dy * xh, axis=0, keepdims=True)

    vec = jax.ShapeDtypeStruct((1, r), F32)
    return pl.pallas_call(
        body, name="mla_prep_bwd", grid=(s // tm,),
        in_specs=[_row(tm, 2 * r, cb), _row(tm, r), _row(tm, r), _vec(r), _vec(r), ANY],
        out_specs=[dp_spec, _vec(r), _vec(r)], out_shape=[dp_shape, vec, vec],
        input_output_aliases={5: 0},
        compiler_params=_cp("arbitrary"),
    )(proj, dqn, dkn, g_cq, g_ckv, dproj)


def _rope_tile(t, a, b, c):
    return t * a + _roll(t, 96) * b + _roll(t, 32) * c


def _rope_tile_bwd(dy, a, b, c):
    return dy * a + _roll(dy * b, 32) + _roll(dy * c, 96)


def _attn_block(s):
    return _rtile(s, 512)


def _qk_prep(qp, kvp, proj, ta, tb, tc, lay):
    s = qp.shape[0]
    hq = MLA_HEADS * MLA_QW
    hv = MLA_HEADS * MLA_DV
    blk = _attn_block(s)
    tm = _rtile(blk, 256)
    per = blk // tm
    kr_cb = lay.off["kr"] // LANES

    def body(q_ref, kv_ref, kr_ref, a_ref, b_ref, c_ref, qc_ref, kc_ref, v_ref, kt_ref, vt_ref):
        a, b, c = a_ref[...], b_ref[...], c_ref[...]
        krot = _rope_tile(kr_ref[...], a, b, c)
        krot_b, krot_t = krot.astype(BF16), krot.T.astype(BF16)
        for h in range(MLA_HEADS):
            q0 = h * MLA_QW
            qc_ref[:, q0:q0 + MLA_NOPE] = (q_ref[:, q0:q0 + MLA_NOPE] * QK_LOG2_SCALE).astype(BF16)
            qc_ref[:, q0 + MLA_NOPE:q0 + MLA_QW] = (
                _rope_tile(q_ref[:, q0 + MLA_NOPE:q0 + MLA_QW], a, b, c) * QK_LOG2_SCALE).astype(BF16)
            kn = kv_ref[:, h * MLA_NOPE:(h + 1) * MLA_NOPE]
            kc_ref[:, q0:q0 + MLA_NOPE] = kn.astype(BF16)
            kc_ref[:, q0 + MLA_NOPE:q0 + MLA_QW] = krot_b
            kt_ref[h, :MLA_NOPE, :] = kn.T.astype(BF16)
            kt_ref[h, MLA_NOPE:, :] = krot_t
            vh = kv_ref[:, (MLA_HEADS + h) * MLA_NOPE:(MLA_HEADS + h + 1) * MLA_NOPE]
            v_ref[:, h * MLA_DV:(h + 1) * MLA_DV] = vh.astype(BF16)
            vt_ref[h] = vh.T.astype(BF16)

    return pl.pallas_call(
        body, name="qk_prep", grid=(s // tm,),
        in_specs=[_row(tm, hq), _row(tm, hq), _row(tm, LANES, kr_cb), _row(tm, LANES), _row(tm, LANES), _row(tm, LANES)],
        out_specs=[_row(tm, hq), _row(tm, hq), _row(tm, hv),
                   pl.BlockSpec((MLA_HEADS, None, MLA_QW, tm), lambda i: (0, i // per, 0, i % per)),
                   pl.BlockSpec((MLA_HEADS, None, MLA_DV, tm), lambda i: (0, i // per, 0, i % per))],
        out_shape=[jax.ShapeDtypeStruct((s, hq), BF16), jax.ShapeDtypeStruct((s, hq), BF16),
                   jax.ShapeDtypeStruct((s, hv), BF16),
                   jax.ShapeDtypeStruct((MLA_HEADS, s // blk, MLA_QW, blk), BF16),
                   jax.ShapeDtypeStruct((MLA_HEADS, s // blk, MLA_DV, blk), BF16)],
        compiler_params=_cp("parallel"),
    )(qp, kvp, proj, ta, tb, tc)


def _kv_bwd_prep(dk_cat, dv, ta, tb, tc, dproj, lay):
    s = dk_cat.shape[0]
    hq = MLA_HEADS * MLA_QW
    hv = MLA_HEADS * MLA_DV
    tm = _rtile(s, 256)
    dp_spec, dp_shape = _dproj_out(dproj, s, lay, tm, "kr")

    def body(dk_ref, dv_ref, a_ref, b_ref, c_ref, _, dkv_ref, dp_ref):
        acc = jnp.zeros((tm, LANES), F32)
        for h in range(MLA_HEADS):
            q0 = h * MLA_QW
            dkv_ref[:, h * MLA_NOPE:(h + 1) * MLA_NOPE] = dk_ref[:, q0:q0 + MLA_NOPE].astype(BF16)
            acc = acc + dk_ref[:, q0 + MLA_NOPE:q0 + MLA_QW]
        dkv_ref[:, MLA_HEADS * MLA_NOPE:] = dv_ref[...].astype(BF16)
        dp_ref[...] = _rope_tile_bwd(acc, a_ref[...], b_ref[...], c_ref[...]).astype(BF16)

    return pl.pallas_call(
        body, name="kv_bwd_prep", grid=(s // tm,),
        in_specs=[_row(tm, hq), _row(tm, hv), _row(tm, LANES), _row(tm, LANES), _row(tm, LANES), ANY],
        out_specs=[_row(tm, hq), dp_spec],
        out_shape=[jax.ShapeDtypeStruct((s, hq), BF16), dp_shape],
        input_output_aliases={5: 1},
        compiler_params=_cp("parallel"),
    )(dk_cat, dv, ta, tb, tc, dproj)


def _mla_gate_bwd(du, o, proj, dproj, lay):
    s, vw = du.shape
    tm = _rtile(s, 256)
    cb = lay.off["mg"] // vw
    dp_spec, dp_shape = _dproj_out(dproj, s, lay, tm, "mg")
    assert MLA_HEADS <= LANES

    def body(du_ref, o_ref, g_ref, _, do_ref, dl_ref, dp_ref):
        gv, duv, ov = g_ref[...], du_ref[...], o_ref[...]
        sg = _sigmoid(gv)
        do = (duv * (gv * sg)).astype(BF16)
        do_ref[...] = do
        dp_ref[...] = (duv * ov * (sg + gv * sg * (1.0 - sg))).astype(BF16)
        prod = do.astype(F32) * ov
        lane = lax.broadcasted_iota(jnp.int32, (tm, LANES), 1)
        delta = jnp.zeros((tm, LANES), F32)
        for h in range(MLA_HEADS):
            dh = jnp.sum(prod[:, h * MLA_DV:(h + 1) * MLA_DV], axis=-1, keepdims=True)
            delta = jnp.where(lane == h, dh, delta)
        dl_ref[...] = delta

    return pl.pallas_call(
        body, name="mla_gate_bwd", grid=(s // tm,),
        in_specs=[_row(tm, vw), _row(tm, vw), _row(tm, vw, cb), ANY],
        out_specs=[_row(tm, vw), _row(tm, LANES), dp_spec],
        out_shape=[jax.ShapeDtypeStruct((s, vw), BF16), jax.ShapeDtypeStruct((s, LANES), F32), dp_shape],
        input_output_aliases={3: 2},
        compiler_params=_cp("parallel"),
    )(du, o, proj, dproj)


RET_BLOCK = 256


def _ret_tables(lg, blk):
    ri = lax.broadcasted_iota(jnp.int32, (blk, blk), 0)
    ci = lax.broadcasted_iota(jnp.int32, (blk, blk), 1)
    col = lax.broadcasted_iota(jnp.int32, (blk, 1), 0).astype(F32)
    dist = jnp.abs(ri - ci).astype(F32)
    dmat = jnp.where(ci // CHUNK <= ri // CHUNK, jnp.exp(dist * lg), 0.0)
    xi = jnp.exp((col + 1.0) * lg)
    zeta = jnp.exp((blk - 1.0 - col) * lg)
    decay = jnp.exp(jnp.full((1, 1), blk, F32) * lg)
    return dmat, xi, zeta, decay


def _ret_qkvg(blk, cs, sn):
    dv = RET_DV
    q = blk[:, :RET_DK]
    k = blk[:, RET_DK:2 * RET_DK]
    q = q * cs + _roll(q, RET_DK // 2) * sn
    k = (k * cs + _roll(k, RET_DK // 2) * sn) * (RET_DK ** -0.5)
    return q, k, blk[:, 2 * RET_DK:2 * RET_DK + dv], blk[:, 2 * RET_DK + dv:]


def _group_norm(o):
    mu = jnp.mean(o, axis=-1, keepdims=True)
    oc = o - mu
    rstd = lax.rsqrt(jnp.mean(oc * oc, axis=-1, keepdims=True) + EPS)
    return oc * rstd, rstd


def _ret_fwd(proj, lg, cosr, sinr, lay):
    s = proj.shape[0]
    dv, w = RET_DV, lay.ret_w
    tb = _rtile(s, 512)
    blk = min(RET_BLOCK, tb)
    nb, nch = s // tb, tb // blk
    cb0 = lay.off["ret"] // w

    def body(lg_ref, p_ref, cos_ref, sin_ref, o_ref, u_ref, st_ref, state):
        @pl.when(pl.program_id(1) == 0)
        def _():
            state[...] = jnp.zeros_like(state)

        dmat, xi, zeta, decay = _ret_tables(lg_ref[pl.program_id(0)], blk)
        for c in range(nch):
            rows = slice(c * blk, (c + 1) * blk)
            q, k, v, g = _ret_qkvg(p_ref[rows, :], cos_ref[rows, :], sin_ref[rows, :])
            qb, kb, vb = q.astype(BF16), k.astype(BF16), v.astype(BF16)
            sc = _dot_nt(qb, kb) * dmat
            st = state[...]
            o = _dot(sc.astype(BF16), vb) + _dot((q * xi).astype(BF16), st.astype(BF16))
            st_ref[c] = st.astype(BF16)
            state[...] = st * decay + _dot_tn((k * zeta).astype(BF16), vb)
            o_ref[rows, :] = o
            n, _ = _group_norm(o)
            u_ref[rows, :] = (n * (g * _sigmoid(g))).astype(BF16)

    return pl.pallas_call(
        body, name="ret_fwd", grid=(RET_HEADS, nb),
        in_specs=[pl.BlockSpec(memory_space=pltpu.SMEM),
                  pl.BlockSpec((tb, w), lambda h, b: (b, cb0 + h)),
                  pl.BlockSpec((tb, RET_DK), lambda h, b: (b, 0)),
                  pl.BlockSpec((tb, RET_DK), lambda h, b: (b, 0))],
        out_specs=[pl.BlockSpec((tb, dv), lambda h, b: (b, h)),
                   pl.BlockSpec((tb, dv), lambda h, b: (b, h)),
                   pl.BlockSpec((None, nch, RET_DK, dv), lambda h, b: (h, b, 0, 0))],
        out_shape=[jax.ShapeDtypeStruct((s, RET_HEADS * dv), F32),
                   jax.ShapeDtypeStruct((s, RET_HEADS * dv), BF16),
                   jax.ShapeDtypeStruct((RET_HEADS, s // blk, RET_DK, dv), BF16)],
        scratch_shapes=[pltpu.VMEM((RET_DK, dv), F32)],
        compiler_params=_cp("parallel", "arbitrary"),
    )(lg, proj, cosr, sinr)


def _ret_bwd(proj, lg, cosr, sinr, o, du, states, dproj, lay):
    s = proj.shape[0]
    dv, w = RET_DV, lay.ret_w
    tb = _rtile(s, 512)
    blk = min(RET_BLOCK, tb)
    nb, nch = s // tb, tb // blk
    cb0 = lay.off["ret"] // w

    def body(lg_ref, p_ref, cos_ref, sin_ref, o_ref, du_ref, st_ref, _, dp_ref, dstate):
        @pl.when(pl.program_id(1) == 0)
        def _():
            dstate[...] = jnp.zeros_like(dstate)

        dmat, xi, zeta, decay = _ret_tables(lg_ref[pl.program_id(0)], blk)
        for c in reversed(range(nch)):
            rows = slice(c * blk, (c + 1) * blk)
            cs, sn = cos_ref[rows, :], sin_ref[rows, :]
            q, k, v, g = _ret_qkvg(p_ref[rows, :], cs, sn)
            qb, kb, vb = q.astype(BF16), k.astype(BF16), v.astype(BF16)
            n, rstd = _group_norm(o_ref[rows, :])
            sg = _sigmoid(g)
            duv = du_ref[rows, :]
            dn = duv * (g * sg)
            dg = duv * n * (sg + g * sg * (1.0 - sg))
            do = rstd * (dn - jnp.mean(dn, axis=-1, keepdims=True) - n * jnp.mean(dn * n, axis=-1, keepdims=True))
            dob = do.astype(BF16)
            rb = st_ref[c]
            drb = dstate[...].astype(BF16)
            sc = (_dot_nt(qb, kb) * dmat).astype(BF16)
            dsc = (_dot_nt(dob, vb) * dmat).astype(BF16)
            qx = (q * xi).astype(BF16)
            kz = (k * zeta).astype(BF16)
            dq = _dot(dsc, kb) + _dot_nt(dob, rb) * xi
            dk = (_dot_tn(dsc, qb) + _dot_nt(vb, drb) * zeta) * (RET_DK ** -0.5)
            dvv = _dot_tn(sc, dob) + _dot(kz, drb)
            dstate[...] = dstate[...] * decay + _dot_tn(qx, dob)
            dp_ref[rows, :RET_DK] = (dq * cs + _roll(dq * sn, RET_DK // 2)).astype(BF16)
            dp_ref[rows, RET_DK:2 * RET_DK] = (dk * cs + _roll(dk * sn, RET_DK // 2)).astype(BF16)
            dp_ref[rows, 2 * RET_DK:2 * RET_DK + dv] = dvv.astype(BF16)
            dp_ref[rows, 2 * RET_DK + dv:] = dg.astype(BF16)

    rev = lambda h, b: (nb - 1 - b, h)
    return pl.pallas_call(
        body, name="ret_bwd", grid=(RET_HEADS, nb),
        in_specs=[pl.BlockSpec(memory_space=pltpu.SMEM),
                  pl.BlockSpec((tb, w), lambda h, b: (nb - 1 - b, cb0 + h)),
                  pl.BlockSpec((tb, RET_DK), lambda h, b: (nb - 1 - b, 0)),
                  pl.BlockSpec((tb, RET_DK), lambda h, b: (nb - 1 - b, 0)),
                  pl.BlockSpec((tb, dv), rev),
                  pl.BlockSpec((tb, dv), rev),
                  pl.BlockSpec((None, nch, RET_DK, dv), lambda h, b: (h, nb - 1 - b, 0, 0)),
                  ANY],
        out_specs=pl.BlockSpec((tb, w), lambda h, b: (nb - 1 - b, cb0 + h)),
        out_shape=jax.ShapeDtypeStruct((s, lay.total), BF16),
        input_output_aliases={7: 0},
        scratch_shapes=[pltpu.VMEM((RET_DK, dv), F32)],
        compiler_params=_cp("parallel", "arbitrary"),
    )(lg, proj, cosr, sinr, o, du, states, dproj)


ATTN_HEADS_PER_STEP = 2


def _attn_mask(rows, cols, row0, col0, keys_on_rows):
    ri = (lax.broadcasted_iota(jnp.int32, (rows, cols), 0) + row0) // CHUNK
    ci = (lax.broadcasted_iota(jnp.int32, (rows, cols), 1) + col0) // CHUNK
    return ri <= ci if keys_on_rows else ci <= ri


def _attn_fwd(q_cat, k_cat, v_t, proj, lay):
    s = q_cat.shape[0]
    bq = _attn_block(s)
    nq = s // bq
    mg_cb = lay.off["mg"] // MLA_DV

    hp = ATTN_HEADS_PER_STEP
    assert MLA_HEADS % hp == 0 and mg_cb % hp == 0

    def body(q_ref, k_ref, vt_ref, g_ref, o_ref, u_ref, lse_ref):
        i = pl.program_id(1)
        qs = [q_ref[:, a * MLA_QW:(a + 1) * MLA_QW] for a in range(hp)]

        def scores(a, j):
            r0 = pl.multiple_of(j * bq, bq)
            return _dot_nt(k_ref[pl.ds(r0, bq), a * MLA_QW:(a + 1) * MLA_QW], qs[a])

        def update(a, j, sc, m, l, acc):
            mn = jnp.maximum(m, jnp.max(sc, axis=0, keepdims=True))
            p = jnp.exp2(sc - mn)
            alpha = jnp.exp2(m - mn)
            l = alpha * l + jnp.sum(p, axis=0, keepdims=True)
            acc = alpha * acc + _dot(vt_ref[a, j], p.astype(BF16))
            return mn, l, acc

        def hide(sc):
            return jnp.where(_attn_mask(bq, bq, 0, 0, True), sc, NEG_INF)

        def pair(j, carry, last):
            sa = [scores(a, j) for a in range(hp)]
            sb = [scores(a, j + 1) for a in range(hp)]
            carry = [update(a, j, sa[a], *carry[a]) for a in range(hp)]
            return tuple(update(a, j + 1, hide(sb[a]) if last else sb[a], *carry[a]) for a in range(hp))

        def single(carry):
            return tuple(update(a, i, hide(scores(a, i)), *carry[a]) for a in range(hp))

        init = tuple((jnp.full((1, bq), NEG_INF, F32), jnp.zeros((1, bq), F32), jnp.zeros((MLA_DV, bq), F32))
                     for _ in range(hp))
        carry = lax.fori_loop(0, i // 2, lambda t, c: pair(2 * t, c, False), init)
        carry = lax.cond(i % 2 == 1, lambda c: pair(i - 1, c, True), single, carry)
        for a, (m, l, acc) in enumerate(carry):
            cols = slice(a * MLA_DV, (a + 1) * MLA_DV)
            o = (acc / l).T
            gv = g_ref[:, cols]
            o_ref[:, cols] = o
            u_ref[:, cols] = (o * (gv * _sigmoid(gv))).astype(BF16)
            lse_ref[a] = m + jnp.log2(l)

    return pl.pallas_call(
        body, name="attn_fwd", grid=(MLA_HEADS // hp, nq),
        in_specs=[pl.BlockSpec((bq, hp * MLA_QW), lambda h, i: (i, h)),
                  pl.BlockSpec((s, hp * MLA_QW), lambda h, i: (0, h)),
                  pl.BlockSpec((hp, nq, MLA_DV, bq), lambda h, i: (h, 0, 0, 0)),
                  pl.BlockSpec((bq, hp * MLA_DV), lambda h, i: (i, mg_cb // hp + h))],
        out_specs=[pl.BlockSpec((bq, hp * MLA_DV), lambda h, i: (i, h)),
                   pl.BlockSpec((bq, hp * MLA_DV), lambda h, i: (i, h)),
                   pl.BlockSpec((hp, None, 1, bq), lambda h, i: (h, i, 0, 0))],
        out_shape=[jax.ShapeDtypeStruct((s, MLA_HEADS * MLA_DV), F32),
                   jax.ShapeDtypeStruct((s, MLA_HEADS * MLA_DV), BF16),
                   jax.ShapeDtypeStruct((MLA_HEADS, nq, 1, bq), F32)],
        compiler_params=_cp("parallel", "parallel"),
    )(q_cat, k_cat, v_t, proj)


def _attn_bwd(q_cat, k_cat, v, k_t, do, lse, delta, ta, tb, tc):
    s = q_cat.shape[0]
    blk = _attn_block(s)
    nb = s // blk
    hp = ATTN_HEADS_PER_STEP
    qw, dvw = hp * MLA_QW, hp * MLA_DV

    def body(q_ref, k_ref, v_ref, kt_ref, do_ref, lse_ref, dl_ref, a_ref, b_ref, c_ref,
             dq_ref, dk_ref, dv_ref, dq_acc):
        j = pl.program_id(1)

        @pl.when(j == 0)
        def _():
            dq_acc[...] = jnp.zeros_like(dq_acc)

        def qcols(a):
            return slice(a * MLA_QW, (a + 1) * MLA_QW)

        def vcols(a):
            return slice(a * MLA_DV, (a + 1) * MLA_DV)

        kbs = [k_ref[:, qcols(a)] for a in range(hp)]
        vbs = [v_ref[:, vcols(a)] for a in range(hp)]

        def step(i, carry, masked):
            r0 = pl.multiple_of(i * blk, blk)
            out = []
            for a in range(hp):
                dk, dvv = carry[a]
                q, dob = q_ref[pl.ds(r0, blk), qcols(a)], do_ref[pl.ds(r0, blk), vcols(a)]
                sc = _dot_nt(kbs[a], q)
                if masked:
                    sc = jnp.where(_attn_mask(blk, blk, 0, 0, True), sc, NEG_INF)
                p = jnp.exp2(sc - lse_ref[a, i])
                dvv = dvv + _dot(p.astype(BF16), dob)
                ds = (p * (_dot_nt(vbs[a], dob) - dl_ref[a, i])).astype(BF16)
                dk = dk + _dot(ds, q)
                dq_acc[a, i] = dq_acc[a, i] + _dot(kt_ref[a], ds)
                out.append((dk, dvv))
            return tuple(out)

        init = tuple((jnp.zeros((blk, MLA_QW), F32), jnp.zeros((blk, MLA_DV), F32)) for _ in range(hp))
        carry = step(j, init, True)
        carry = lax.fori_loop(j + 1, nb, lambda i, c: step(i, c, False), carry)
        for a, (dk, dvv) in enumerate(carry):
            dk_ref[:, qcols(a)] = dk * LN2
            dv_ref[:, vcols(a)] = dvv

        @pl.when(j == nb - 1)
        def _():
            for a in range(hp):
                for i in range(nb):
                    rows = slice(i * blk, (i + 1) * blk)
                    dq = dq_acc[a, i].T * QK_SCALE
                    c0 = a * MLA_QW
                    dq_ref[rows, c0:c0 + MLA_NOPE] = dq[:, :MLA_NOPE].astype(BF16)
                    dq_ref[rows, c0 + MLA_NOPE:c0 + MLA_QW] = _rope_tile_bwd(
                        dq[:, MLA_NOPE:], a_ref[rows, :], b_ref[rows, :], c_ref[rows, :]).astype(BF16)

    tab = pl.BlockSpec((s, LANES), lambda h, j: (0, 0), pipeline_mode=pl.Buffered(1))
    rows = pl.BlockSpec((hp, nb, 1, blk), lambda h, j: (h, 0, 0, 0))
    return pl.pallas_call(
        body, name="attn_bwd", grid=(MLA_HEADS // hp, nb),
        in_specs=[pl.BlockSpec((s, qw), lambda h, j: (0, h)),
                  pl.BlockSpec((blk, qw), lambda h, j: (j, h)),
                  pl.BlockSpec((blk, dvw), lambda h, j: (j, h)),
                  pl.BlockSpec((hp, None, MLA_QW, blk), lambda h, j: (h, j, 0, 0)),
                  pl.BlockSpec((s, dvw), lambda h, j: (0, h)),
                  rows, rows, tab, tab, tab],
        out_specs=[pl.BlockSpec((s, qw), lambda h, j: (0, h)),
                   pl.BlockSpec((blk, qw), lambda h, j: (j, h)),
                   pl.BlockSpec((blk, dvw), lambda h, j: (j, h))],
        out_shape=[jax.ShapeDtypeStruct((s, MLA_HEADS * MLA_QW), BF16),
                   jax.ShapeDtypeStruct((s, MLA_HEADS * MLA_QW), F32),
                   jax.ShapeDtypeStruct((s, MLA_HEADS * MLA_DV), F32)],
        scratch_shapes=[pltpu.VMEM((hp, nb, MLA_QW, blk), F32)],
        compiler_params=_cp("parallel", "arbitrary"),
    )(q_cat, k_cat, v, k_t, do, lse, delta, ta, tb, tc)


def _place():
    return lax.axis_index("x"), lax.axis_index("y"), lax.axis_index("c")


def _slot(px, py, pc):
    return 4 * px + 2 * py + pc


def _all_gather(shards, layer, name, vmem=False, deps=()):
    n = len(shards)

    def body(*refs):
        srcs, outs = refs[:n], refs[n + len(deps):2 * n + len(deps)]
        send_sems, recv_sems, local_sems = refs[2 * n + len(deps):]
        x, y, c = _place()
        me, sibling = (x, y, c), (x, y, 1 - c)
        chips = [(1 - x, y), (x, 1 - y), (1 - x, 1 - y)]
        firsts, passes, locals_ = [], [], []

        def copy(a, k, block, to, src=None):
            dst = outs[a].at[_slot(*block)]
            return pltpu.make_async_remote_copy(
                src_ref=dst if src is None else src, dst_ref=dst,
                send_sem=send_sems.at[7 * a + k], recv_sem=recv_sems.at[7 * a + k],
                device_id=to, device_id_type=MESH)

        for a in range(n):
            src = srcs[a] if layer is None else srcs[a].at[layer]
            mine = pltpu.make_async_copy(src, outs[a].at[_slot(*me)], local_sems.at[a])
            mine.start()
            locals_.append(mine)
            first = [copy(a, 0, me, sibling, src=src)]
            first += [copy(a, 1 + j, me, (*chip, c), src=src) for j, chip in enumerate(chips)]
            for cp in first:
                cp.start()
            firsts += first
        for a in range(n):
            for j, chip in enumerate(chips):
                copy(a, 1 + j, (*chip, c), me).wait_recv()
                fwd = copy(a, 4 + j, (*chip, c), sibling)
                fwd.start()
                passes.append(fwd)
        for a in range(n):
            copy(a, 0, sibling, me).wait_recv()
            for j, chip in enumerate(chips):
                copy(a, 4 + j, (*chip, 1 - c), me).wait_recv()
        for cp in firsts + passes:
            cp.wait_send()
        for mine in locals_:
            mine.wait()

    space = pl.BlockSpec(memory_space=pltpu.VMEM) if vmem else ANY
    out_shape = [jax.ShapeDtypeStruct((N_DEV,) + (a.shape if layer is None else a.shape[1:]), a.dtype) for a in shards]
    return pl.pallas_call(
        body, name=name,
        in_specs=[space] * n + [ANY] * len(deps), out_specs=[space] * n, out_shape=out_shape,
        scratch_shapes=[pltpu.SemaphoreType.DMA((7 * n,)), pltpu.SemaphoreType.DMA((7 * n,)),
                        pltpu.SemaphoreType.DMA((n,))],
        compiler_params=pltpu.CompilerParams(has_side_effects=True),
    )(*shards, *deps)


HBM = pl.BlockSpec(memory_space=pltpu.HBM)
SEM = pl.BlockSpec(memory_space=pltpu.SEMAPHORE)
EFFECT = pltpu.SideEffectType.DATAFLOW_SIDE_EFFECTING


def _push_copies(srcs, lands, send_sems, recv_sems, local_sems, by_peer):
    x, y, c = _place()
    me = _slot(x, y, c)
    local, remote = [], []
    for a, (src, land) in enumerate(zip(srcs, lands)):
        local.append(pltpu.make_async_copy(src.at[me] if by_peer else src, land.at[me], local_sems.at[a]))
        for r in range(1, N_DEV):
            peer = (1 - x if r & 4 else x, 1 - y if r & 2 else y, 1 - c if r & 1 else c)
            remote.append(pltpu.make_async_remote_copy(
                src_ref=src.at[_slot(*peer)] if by_peer else src, dst_ref=land.at[me],
                send_sem=send_sems.at[7 * a + r - 1], recv_sem=recv_sems.at[7 * a + r - 1],
                device_id=peer, device_id_type=MESH))
    return local, remote


def _push_start(srcs, by_peer, after, name):
    n = len(srcs)
    srcs = [pltpu.with_memory_space_constraint(a, pltpu.HBM) for a in srcs]
    lands = [pltpu.with_memory_space_constraint(
        lax.empty((N_DEV,) + (a.shape[1:] if by_peer else a.shape), a.dtype), pltpu.HBM) for a in srcs]

    def body(*refs):
        k = 2 * n + len(after)
        local, remote = _push_copies(refs[:n], refs[n:2 * n], refs[k], refs[k + 1], refs[k + 2], by_peer)
        for cp in local + remote:
            cp.start()
        token = refs[k + 3 + 2 * n]
        token[...] = jnp.zeros_like(token)

    outs = pl.pallas_call(
        body, name=name,
        out_shape=(pltpu.SemaphoreType.DMA((7 * n,)), pltpu.SemaphoreType.DMA((7 * n,)),
                   pltpu.SemaphoreType.DMA((n,)),
                   *[pltpu.HBM(a.shape, a.dtype) for a in srcs], *[pltpu.HBM(a.shape, a.dtype) for a in lands],
                   jax.ShapeDtypeStruct((8, LANES), F32)),
        in_specs=[HBM] * (2 * n) + [ANY] * len(after),
        out_specs=(SEM, SEM, SEM, *([HBM] * (2 * n)), pl.BlockSpec(memory_space=pltpu.VMEM)),
        input_output_aliases={i: 3 + i for i in range(2 * n)},
        compiler_params=pltpu.CompilerParams(has_side_effects=EFFECT),
    )(*srcs, *lands, *after)
    return outs[:3], outs[3:3 + n], outs[3 + n:3 + 2 * n], outs[3 + 2 * n]


def _push_wait(sems, srcs, lands, by_peer, after, name):
    n = len(srcs)

    def body(*refs):
        local, remote = _push_copies(refs[:n], refs[n:2 * n], refs[2 * n], refs[2 * n + 1], refs[2 * n + 2], by_peer)
        for cp in local:
            cp.wait()
        for cp in remote:
            cp.wait_send()
            cp.wait_recv()

    outs = pl.pallas_call(
        body, name=name,
        out_shape=[pltpu.HBM(a.shape, a.dtype) for a in list(srcs) + list(lands)],
        in_specs=[HBM] * (2 * n) + [SEM] * 3 + [ANY] * len(after),
        out_specs=[HBM] * (2 * n),
        input_output_aliases={i: i for i in range(2 * n)},
        compiler_params=pltpu.CompilerParams(has_side_effects=EFFECT),
    )(*srcs, *lands, *sems, *after)
    return outs[n:]


def _adam_math(g, w, m, v):
    m = ADAM_B1 * m + (1.0 - ADAM_B1) * g
    v = ADAM_B2 * v + (1.0 - ADAM_B2) * (g * g)
    m_hat = m / (1.0 - ADAM_B1 ** ADAM_STEP)
    v_hat = v / (1.0 - ADAM_B2 ** ADAM_STEP)
    delta = -ADAM_LR * (m_hat / (jnp.sqrt(v_hat) + ADAM_EPS) + ADAM_WD * w)
    return delta, m, v


def _adam_sharded(recvs, first, w, m, v, prev, name):
    nl, r, c = w.shape
    n = len(recvs)
    by_cols = r % 128 != 0
    tr, tc = (r, _tile(c, LANES)) if by_cols else (_rtile(r, 128), c)
    nr = c // tc if by_cols else r // tr
    prev = list(prev) if prev is not None else []

    def tile(t):
        return (0, t) if by_cols else (t, 0)

    def body(*refs):
        g_refs = refs[:n]
        w_ref, m_ref, v_ref = refs[n:n + 3]
        go_ref, d_ref, mo_ref, vo_ref = refs[n + 3 + len(prev):]
        layer = pl.program_id(0)
        for l in range(n):
            @pl.when(layer == l)
            def _(l=l):
                g = g_refs[l][0].astype(F32)
                for i in range(1, N_DEV):
                    g = g + g_refs[l][i].astype(F32)
                delta, mn, vn = _adam_math(g, w_ref[...], m_ref[...], v_ref[...])
                go_ref[...] = g
                d_ref[...] = delta
                mo_ref[...] = mn
                vo_ref[...] = vn

    def recv_spec(l):
        def index(layer, i):
            return (0,) + tile(jnp.where(layer == l, i, jnp.where(layer < l, 0, nr - 1)))
        return pl.BlockSpec((N_DEV, tr, tc), index)

    blk = pl.BlockSpec((None, tr, tc), lambda layer, i: (first + layer,) + tile(i))
    out = jax.ShapeDtypeStruct(w.shape, F32)
    return pl.pallas_call(
        body, name=name, grid=(n, nr),
        in_specs=[recv_spec(l) for l in range(n)] + [blk, blk, blk] + [ANY] * len(prev),
        out_specs=[blk] * 4, out_shape=[out] * 4,
        input_output_aliases={n + 3 + k: k for k in range(len(prev))},
        compiler_params=_cp("arbitrary", "arbitrary"),
    )(*recvs, w, m, v, *prev)


def _adam_mod(c_all_t, dmod, w, m, v):
    nl, d, c = w.shape
    tr = _rtile(d, 128)

    def body(ct_ref, dm_ref, w_ref, m_ref, v_ref, go_ref, d_ref, mo_ref, vo_ref):
        ct = ct_ref[...].astype(BF16).astype(F32)
        dm = dm_ref[...].astype(BF16).astype(F32)
        g = ct[:, 0:1] * dm[0:1, :]
        for b in range(1, N_DEV):
            g = g + ct[:, b:b + 1] * dm[b:b + 1, :]
        delta, mn, vn = _adam_math(g, w_ref[...], m_ref[...], v_ref[...])
        go_ref[...] = g
        d_ref[...] = delta
        mo_ref[...] = mn
        vo_ref[...] = vn

    blk = pl.BlockSpec((None, tr, c), lambda layer, i: (layer, i, 0))
    out = jax.ShapeDtypeStruct(w.shape, F32)
    return pl.pallas_call(
        body, name="adam_mod", grid=(nl, d // tr),
        in_specs=[pl.BlockSpec((tr, N_DEV), lambda layer, i: (i, 0)),
                  pl.BlockSpec((None, N_DEV, c), lambda layer, i: (layer, 0, 0)), blk, blk, blk],
        out_specs=[blk] * 4, out_shape=[out] * 4,
        compiler_params=_cp("parallel", "parallel"),
    )(c_all_t, dmod, w, m, v)


def _adam_small(g, w, m, v, name):
    def body(g_ref, w_ref, m_ref, v_ref, d_ref, mo_ref, vo_ref):
        delta, mn, vn = _adam_math(g_ref[...], w_ref[...], m_ref[...], v_ref[...])
        d_ref[...] = delta
        mo_ref[...] = mn
        vo_ref[...] = vn

    out = jax.ShapeDtypeStruct(w.shape, F32)
    return pl.pallas_call(body, name=name, out_shape=[out] * 3)(g, w, m, v)


def _sum_devices(parts):
    def body(p_ref, o_ref):
        acc = p_ref[0]
        for i in range(1, N_DEV):
            acc = acc + p_ref[i]
        o_ref[...] = acc

    return pl.pallas_call(body, name="sum_devices",
                          out_shape=jax.ShapeDtypeStruct(parts.shape[1:], F32))(parts)


def _rope_tables(positions):
    pos = positions.astype(F32)[:, None]

    def cs(dim):
        inv = 1.0 / (ROPE_BASE ** (jnp.arange(0, dim, 2, dtype=F32) / dim))
        ang = pos * inv
        return jnp.cos(ang), jnp.sin(ang)

    cr, sr = cs(RET_DK)
    cm, sm = cs(MLA_ROPE)
    z = jnp.zeros_like(cm)
    pad = jnp.zeros((pos.shape[0], LANES - MLA_ROPE), F32)
    cosr = jnp.concatenate([cr, cr], axis=1)
    sinr = jnp.concatenate([-sr, sr], axis=1)
    ta = jnp.concatenate([cm, cm, pad], axis=1)
    tb = jnp.concatenate([-sm, z, pad], axis=1)
    tc = jnp.concatenate([z, sm, pad], axis=1)
    return cosr, sinr, ta, tb, tc


def _layer_fwd(x, mod, g_norm, g_cq, g_ckv, wts, tabs, lg, lay, deps):
    d = x.shape[1]
    cosr, sinr, ta, tb, tc = tabs
    shift, scale, gate = mod[:, :d], mod[:, d:2 * d], mod[:, 2 * d:]
    h = _norm_mod_fwd(x, g_norm, scale, shift, deps)
    proj = _matmul(h, wts["in"], name="mm_proj", tn_cap=1920)
    o_ret, u_ret, states = _ret_fwd(proj, lg, cosr, sinr, lay)
    y_ret = _matmul(u_ret, wts["ret"], name="mm_y")
    cqn, ckvn = _mla_prep(proj, g_cq, g_ckv, lay)
    qp = _matmul(cqn, wts["uq"], name="mm_up")
    kvp = _matmul(ckvn, wts["ukv"], name="mm_up")
    q_cat, k_cat, v, k_t, v_t = _qk_prep(qp, kvp, proj, ta, tb, tc, lay)
    o_mla, u_mla, lse = _attn_fwd(q_cat, k_cat, v_t, proj, lay)
    y_mla = _matmul(u_mla, wts["mla"], name="mm_y")
    merged = _merge_fwd(y_ret, y_mla, proj, lay)
    out = _matmul(merged, wts["out"], name="mm_y")
    x_next = _resid_fwd(x, out, gate)
    saved = dict(x=x, h=h, proj=proj, o_ret=o_ret, u_ret=u_ret, states=states, y_ret=y_ret, cqn=cqn,
                 ckvn=ckvn, q_cat=q_cat, k_cat=k_cat, v=v, k_t=k_t, o_mla=o_mla, u_mla=u_mla, lse=lse,
                 y_mla=y_mla, merged=merged, out=out)
    return x_next, saved


def _to_owner_blocks_cols(g, n_local):
    k = g.shape[0]
    return g.reshape(k, N_DEV, n_local).transpose(1, 0, 2)


def _from_owner_blocks_cols(g):
    return g.transpose(1, 0, 2).reshape(g.shape[1], -1)


def _layer_bwd(dxn, sv, mod, g_norm, g_cq, g_ckv, wts, tabs, lg, lay, deps, shard_cols, push):
    d = dxn.shape[1]
    n_in, n_uq, n_ukv = shard_cols
    cosr, sinr, ta, tb, tc = tabs
    scale, gate = mod[:, d:2 * d], mod[:, 2 * d:]
    gdt = BF16
    dout, dgate = _resid_bwd(dxn, sv["out"], gate, deps)
    dmerged = _matmul(dout, wts["out"], tb=True, name="mm_dy")
    dw_out = _matmul(sv["merged"], dout, ta=True, out_dtype=gdt, name="mm_dw")
    dy_ret, dy_mla, dproj = _merge_bwd(dmerged, sv["y_ret"], sv["y_mla"], sv["proj"], lay)
    du_ret = _matmul(dy_ret, wts["ret"], tb=True, name="mm_dy")
    dw_ret = _matmul(sv["u_ret"], dy_ret, ta=True, out_dtype=gdt, name="mm_dw")
    dproj = _ret_bwd(sv["proj"], lg, cosr, sinr, sv["o_ret"], du_ret, sv["states"], dproj, lay)
    du_mla = _matmul(dy_mla, wts["mla"], tb=True, name="mm_dy")
    dw_mla = _matmul(sv["u_mla"], dy_mla, ta=True, out_dtype=gdt, name="mm_dw")
    do_mla, delta, dproj = _mla_gate_bwd(du_mla, sv["o_mla"], sv["proj"], dproj, lay)
    delta = delta[:, :MLA_HEADS].T.reshape(sv["lse"].shape)
    dqp, dk_cat, dv = _attn_bwd(sv["q_cat"], sv["k_cat"], sv["v"], sv["k_t"], do_mla, sv["lse"], delta,
                                ta, tb, tc)
    dkvp, dproj = _kv_bwd_prep(dk_cat, dv, ta, tb, tc, dproj, lay)
    dcqn = _matmul(dqp, wts["uq"], tb=True, name="mm_dlat")
    dckvn = _matmul(dkvp, wts["ukv"], tb=True, name="mm_dlat")
    dw_uq = _matmul(sv["cqn"], dqp, ta=True, out_dtype=gdt, name="mm_dwup")
    dw_ukv = _matmul(sv["ckvn"], dkvp, ta=True, out_dtype=gdt, name="mm_dwup")
    sent = push("a", [_to_owner_blocks_cols(_uq_to_logical(dw_uq), n_uq),
                      _to_owner_blocks_cols(_ukv_to_logical(dw_ukv), n_ukv),
                      dw_ret.reshape(N_DEV, -1, d), dw_mla.reshape(N_DEV, -1, d), dw_out.reshape(N_DEV, -1, d)])
    dproj, dg_cq, dg_ckv = _mla_prep_bwd(sv["proj"], dcqn, dckvn, g_cq, g_ckv, dproj, lay)
    dw_in = _matmul(sv["h"], dproj, ta=True, out_dtype=gdt, name="mm_dwin", tn_cap=1920, deps=sent)
    sent = push("b", [_scatter_dw_in(dw_in, lay, n_in)])
    dh = _matmul(dproj, wts["in"], tb=True, name="mm_dh", deps=sent)
    dx, dshift, dscale, dg_norm = _norm_mod_bwd(sv["x"], g_norm, scale, dh, dxn)
    dmod = jnp.concatenate([dshift, dscale, dgate], axis=1)
    small = dict(dmod=dmod, g_norm=dg_norm, g_cq=dg_cq, g_ckv=dg_ckv)
    return dx, small


def kernel(x, c, positions, w_mod, b_mod, g_norm, w_in, g_cq, g_ckv, w_uq, w_ukv, w_ret_proj, w_mla_proj, w_out, g_final, loss_target, m_w_mod, m_b_mod, m_g_norm, m_w_in, m_g_cq, m_g_ckv, m_w_uq, m_w_ukv, m_w_ret_proj, m_w_mla_proj, m_w_out, m_g_final, v_w_mod, v_b_mod, v_g_norm, v_w_in, v_g_cq, v_g_ckv, v_w_uq, v_w_ukv, v_w_ret_proj, v_w_mla_proj, v_w_out, v_g_final):
    nl, d, _ = w_mod.shape
    s = x.shape[1]
    rank = g_cq.shape[1]
    lay = Layout(d, rank, g_ckv.shape[1])
    me = _slot(*_place())
    x0 = x.reshape(s, d)
    target = loss_target.reshape(s, d)
    tabs = _rope_tables(positions.reshape(s))
    lg = jnp.log(1.0 - 2.0 ** (-5.0 - jnp.arange(RET_HEADS, dtype=F32)))

    c_act = c * _sigmoid(c)
    (c_all,) = _all_gather([c_act.reshape(d // LANES, LANES)], None, "gather_c", vmem=True)
    c_all = c_all.reshape(N_DEV, d)
    n_mod = w_mod.shape[2]
    mod_part = jnp.stack([_matmul(c_all, w_mod[l], name="mm_mod", tm_cap=8) for l in range(nl)])
    (mod_all,) = _all_gather([mod_part.reshape(-1, LANES)], None, "gather_mod", vmem=True)
    mod_all = mod_all.reshape(N_DEV, nl, N_DEV, n_mod)
    mod = lax.dynamic_index_in_dim(mod_all, me, axis=2, keepdims=False)
    mod = mod.transpose(1, 0, 2).reshape(nl, N_DEV * n_mod) + b_mod

    shards = [[w[l].astype(BF16) for w in (w_in, w_uq, w_ukv, w_ret_proj, w_mla_proj, w_out)] for l in range(nl)]
    xl, saved, wts_all = x0, [], []
    gathered = _all_gather(shards[0], None, "gather_w")
    for l in range(nl):
        g_in, g_uq, g_ukv, g_ret, g_mla, g_out = gathered
        deps = []
        if l + 1 < nl:
            sems, srcs, lands, token = _push_start(shards[l + 1], False, [g_out, mod], "gather_start_%d" % (l + 1))
            deps = [token]
        wts = {
            "in": _assemble_w_in(g_in, lay),
            "uq": _uq_to_physical(_from_owner_blocks_cols(g_uq)),
            "ukv": _ukv_to_physical(_from_owner_blocks_cols(g_ukv)),
            "ret": g_ret.reshape(-1, d), "mla": g_mla.reshape(-1, d), "out": g_out.reshape(-1, d),
        }
        wts_all.append(wts)
        xl, sv = _layer_fwd(xl, mod[l:l + 1], g_norm[l:l + 1], g_cq[l:l + 1], g_ckv[l:l + 1], wts, tabs, lg, lay,
                            deps)
        saved.append(sv)
        if l + 1 < nl:
            gathered = _push_wait(sems, srcs, lands, False, [xl], "gather_wait_%d" % (l + 1))
    loss_lanes, dx, dg_final = _final_loss(xl, g_final.reshape(1, d), target)

    small = [None] * nl
    flying = {l: [] for l in range(nl)}
    recv = {}
    shard_cols = (w_in.shape[2], w_uq.shape[2], w_ukv.shape[2])

    def pusher(l):
        def push(group, arrays):
            sems, srcs, lands, token = _push_start(arrays, True, [], "exchange_start_%d%s" % (l, group))
            flying[l].append((group, sems, srcs, lands))
            return [token]
        return push

    def land(l, after):
        got = {}
        for group, sems, srcs, lands in flying[l]:
            got[group] = _push_wait(sems, srcs, lands, True, after, "exchange_wait_%d%s" % (l, group))
        recv[l] = list(got["b"]) + list(got["a"])

    for l in reversed(range(nl)):
        dx, small[l] = _layer_bwd(dx, saved[l], mod[l:l + 1], g_norm[l:l + 1], g_cq[l:l + 1], g_ckv[l:l + 1],
                                  wts_all[l], tabs, lg, lay, [], shard_cols, pusher(l))
        if l + 1 < nl:
            land(l + 1, [dx])
    grad_x = dx.reshape(x.shape)

    out = {}
    w_in_t, m_w_in_t, v_w_in_t = (jnp.swapaxes(a, 1, 2) for a in (w_in, m_w_in, v_w_in))
    sharded = (("w_in", w_in_t, m_w_in_t, v_w_in_t), ("w_uq", w_uq, m_w_uq, v_w_uq), ("w_ukv", w_ukv, m_w_ukv, v_w_ukv),
               ("w_ret_proj", w_ret_proj, m_w_ret_proj, v_w_ret_proj),
               ("w_mla_proj", w_mla_proj, m_w_mla_proj, v_w_mla_proj), ("w_out", w_out, m_w_out, v_w_out))
    if nl > 1:
        for i, (key, w, m, v) in enumerate(sharded):
            out[key] = _adam_sharded([recv[l][i] for l in range(1, nl)], 1, w, m, v, None, "adam_" + key)
    done = [out[key][0] for key, _, _, _ in sharded if key in out]

    pack = jnp.concatenate(
        [jnp.concatenate([sm[k] for sm in small], axis=0).reshape(-1)
         for k in ("dmod", "g_norm", "g_cq", "g_ckv")] + [dg_final.reshape(-1), loss_lanes.reshape(-1)])
    (pack_all,) = _all_gather([pack.reshape(-1, LANES)], None, "gather_small", vmem=True, deps=done)
    tot = _sum_devices(pack_all).reshape(-1)
    sizes = [nl * 3 * d, nl * d, nl * rank, nl * rank, d]
    offs = np.cumsum([0] + sizes)
    grad_b_mod = tot[offs[0]:offs[1]].reshape(nl, 3 * d)
    grad_g_norm = tot[offs[1]:offs[2]].reshape(nl, d)
    grad_g_cq = tot[offs[2]:offs[3]].reshape(nl, rank)
    grad_g_ckv = tot[offs[3]:offs[4]].reshape(nl, rank)
    grad_g_final = tot[offs[4]:offs[5]]
    loss = tot[offs[5]]
    dmod_all = pack_all.reshape(N_DEV, -1)[:, :sizes[0]].reshape(N_DEV, nl, 3 * d)
    dmod_mine = lax.dynamic_slice_in_dim(dmod_all, me * n_mod, n_mod, axis=2).transpose(1, 0, 2)

    out["w_mod"] = _adam_mod(c_all.T, dmod_mine, w_mod, m_w_mod, v_w_mod)
    land(0, [out["w_mod"][0]])
    for i, (key, w, m, v) in enumerate(sharded):
        out[key] = _adam_sharded([recv[0][i]], 0, w, m, v, out.get(key), "adam0_" + key)
    out["w_in"] = tuple(jnp.swapaxes(a, 1, 2) for a in out["w_in"])
    for key, g, w, m, v in (("b_mod", grad_b_mod, b_mod, m_b_mod, v_b_mod),
                            ("g_norm", grad_g_norm, g_norm, m_g_norm, v_g_norm),
                            ("g_cq", grad_g_cq, g_cq, m_g_cq, v_g_cq),
                            ("g_ckv", grad_g_ckv, g_ckv, m_g_ckv, v_g_ckv),
                            ("g_final", grad_g_final.reshape(1, d), g_final.reshape(1, d),
                             m_g_final.reshape(1, d), v_g_final.reshape(1, d))):
        out[key] = (g,) + tuple(_adam_small(g, w, m, v, "adam_" + key))
    out["g_final"] = tuple(a.reshape(d) for a in out["g_final"])

    names = ("w_mod", "b_mod", "g_norm", "w_in", "g_cq", "g_ckv", "w_uq", "w_ukv", "w_ret_proj",
             "w_mla_proj", "w_out", "g_final")
    return (loss, grad_x, *[out[k][0] for k in names], *[out[k][1] for k in names],
            *[out[k][2] for k in names], *[out[k][3] for k in names])
```

```python
import functools
import itertools

import jax
import jax.numpy as jnp
import numpy as np
from jax import lax
from jax.experimental import pallas as pl
from jax.experimental.pallas import tpu as pltpu

F32 = jnp.float32
BF16 = jnp.bfloat16

N_DEV = 8
CHUNK = 64
EPS = 1e-6
NEG_INF = -1e30
ROPE_BASE = 10000.0
LANES = 128

RET_HEADS = 8
RET_DK = 128
RET_DV = 256
MLA_HEADS = 16
MLA_NOPE = 128
MLA_ROPE = 64
MLA_DV = 128
MLA_QW = 256
QK_SCALE = (MLA_NOPE + MLA_ROPE) ** -0.5
QK_LOG2_SCALE = QK_SCALE * 1.4426950408889634
LN2 = 0.6931471805599453

ADAM_LR = 0.001
ADAM_B1 = 0.9
ADAM_B2 = 0.999
ADAM_EPS = 1e-08
ADAM_WD = 0.01
ADAM_STEP = 10

VMEM_LIMIT_BYTES = 56 * 1024 * 1024
MESH = pl.DeviceIdType.MESH
ANY = pl.BlockSpec(memory_space=pl.ANY)


def _cp(*sem):
    return pltpu.CompilerParams(dimension_semantics=sem if sem else None,
                                vmem_limit_bytes=VMEM_LIMIT_BYTES)


def _tile(n, cap):
    best = None
    t = LANES
    while t <= min(n, cap):
        if n % t == 0:
            best = t
        t += LANES
    return best if best is not None else n


def _rtile(n, cap):
    t = cap
    while t > 8 and n % t:
        t //= 2
    return t if n % t == 0 else n


def _sigmoid(x):
    return 1.0 / (1.0 + jnp.exp(-x))


def _dot(a, b):
    return lax.dot_general(a, b, (((1,), (0,)), ((), ())), preferred_element_type=F32)


def _dot_nt(a, b):
    return lax.dot_general(a, b, (((1,), (1,)), ((), ())), preferred_element_type=F32)


def _dot_tn(a, b):
    return lax.dot_general(a, b, (((0,), (0,)), ((), ())), preferred_element_type=F32)


def _roll(x, s):
    return pltpu.roll(x, s, 1)


class Layout:
    def __init__(self, d_model, q_rank, kv_rank):
        assert q_rank == kv_rank
        self.d = d_model
        self.rank = q_rank
        self.ret_w = 2 * RET_DK + 2 * RET_DV
        self.ret_qk = RET_HEADS * RET_DK
        self.ret_v = RET_HEADS * RET_DV
        self.mla_v = MLA_HEADS * MLA_DV
        widths = {"bg": 2 * d_model, "mg": self.mla_v, "ret": RET_HEADS * self.ret_w,
                  "cqkv": 2 * q_rank, "kr": LANES}
        blocks = {"bg": 2 * d_model, "mg": self.mla_v, "ret": self.ret_w,
                  "cqkv": 2 * q_rank, "kr": LANES}
        for order in itertools.permutations(widths):
            off, offs, ok = 0, {}, True
            for name in order:
                if off % blocks[name]:
                    ok = False
                    break
                offs[name] = off
                off += widths[name]
            if ok:
                break
        assert ok, "no aligned layout"
        self.order, self.off, self.width, self.total = order, offs, widths, off
        lo, o = {}, 0
        for name, w in (("rq", self.ret_qk), ("rk", self.ret_qk), ("rv", self.ret_v),
                        ("rg", self.ret_v), ("cq", q_rank), ("ckv", kv_rank), ("kr", MLA_ROPE),
                        ("mg", self.mla_v), ("bg", 2 * d_model)):
            lo[name] = (o, w)
            o += w
        self.logical, self.d_in = lo, o

    def pieces(self):
        lo = self.logical
        out = []
        for name in self.order:
            if name == "bg":
                out.append(lo["bg"])
            elif name == "mg":
                out.append(lo["mg"])
            elif name == "ret":
                for h in range(RET_HEADS):
                    out.append((lo["rq"][0] + h * RET_DK, RET_DK))
                    out.append((lo["rk"][0] + h * RET_DK, RET_DK))
                    out.append((lo["rv"][0] + h * RET_DV, RET_DV))
                    out.append((lo["rg"][0] + h * RET_DV, RET_DV))
            elif name == "cqkv":
                out.append((lo["cq"][0], 2 * self.rank))
            elif name == "kr":
                out.append(lo["kr"])
                out.append((None, LANES - MLA_ROPE))
        return out


RELAYOUT_CHUNK = 512


def _relayout_plan(lay, n_local):
    plan, off = [], 0
    for start, width in lay.pieces():
        done = 0
        while done < width:
            w = min(RELAYOUT_CHUNK, width - done)
            srcs = []
            if start is not None:
                lo, hi = start + done, start + done + w
                while lo < hi:
                    j = lo // n_local
                    end = min(hi, (j + 1) * n_local)
                    srcs.append((j, lo - j * n_local, end - j * n_local))
                    lo = end
            plan.append((off + done, w, srcs))
            done += w
        off += width
    merged = []
    for p, w, srcs in plan:
        if merged and merged[-1][0] % LANES == 0 and (merged[-1][1] % LANES) and p == merged[-1][0] + merged[-1][1]:
            q, qw, qs = merged.pop()
            merged.append((q, qw + w, qs + ([("pad", w)] if not srcs else srcs)))
        else:
            merged.append((p, w, srcs))
    return merged


def _assemble_w_in(g, lay):
    _, d, n_local = g.shape
    tr = _rtile(d, 256)
    plan = _relayout_plan(lay, n_local)

    def body(g_ref, o_ref):
        for p, w, srcs in plan:
            parts = []
            for src in srcs:
                if src[0] == "pad":
                    parts.append(jnp.zeros((tr, src[1]), F32))
                else:
                    j, a, b = src
                    parts.append(g_ref[j, :, a:b].astype(F32))
            if not parts:
                parts = [jnp.zeros((tr, w), F32)]
            val = parts[0] if len(parts) == 1 else jnp.concatenate(parts, axis=1)
            o_ref[:, p:p + w] = val.astype(o_ref.dtype)

    return pl.pallas_call(
        body, name="assemble_w_in", grid=(d // tr,),
        in_specs=[pl.BlockSpec((N_DEV, tr, n_local), lambda i: (0, i, 0))],
        out_specs=pl.BlockSpec((tr, lay.total), lambda i: (i, 0)),
        out_shape=jax.ShapeDtypeStruct((d, lay.total), g.dtype),
        compiler_params=_cp("parallel"),
    )(g)


def _scatter_dw_in(dw, lay, n_local):
    d = dw.shape[0]
    tr = _rtile(d, 256)
    n_pad = -(-n_local // LANES) * LANES
    plan = _relayout_plan(lay, n_local)
    runs = [[] for _ in range(N_DEV)]
    for p, w, srcs in plan:
        at = p
        for src in srcs:
            if src[0] == "pad":
                at += src[1]
                continue
            j, a, b = src
            runs[j].append((a, b, at))
            at += b - a
    for r in runs:
        r.sort()

    def body(dw_ref, o_ref):
        for j in range(N_DEV):
            parts = [dw_ref[:, at:at + (b - a)].astype(F32) for a, b, at in runs[j]]
            if n_pad > n_local:
                parts.append(jnp.zeros((tr, n_pad - n_local), F32))
            val = jnp.concatenate(parts, axis=1).T
            o_ref[j] = val[:n_local, :].astype(BF16)

    return pl.pallas_call(
        body, name="scatter_dw_in", grid=(d // tr,),
        in_specs=[pl.BlockSpec((tr, lay.total), lambda i: (i, 0))],
        out_specs=pl.BlockSpec((N_DEV, n_local, tr), lambda i: (0, 0, i)),
        out_shape=jax.ShapeDtypeStruct((N_DEV, n_local, d), BF16),
        compiler_params=_cp("parallel"),
    )(dw)


def _uq_to_physical(w):
    k = w.shape[0]
    w3 = w.reshape(k, MLA_HEADS, MLA_NOPE + MLA_ROPE)
    pad = jnp.zeros((k, MLA_HEADS, MLA_QW - MLA_NOPE - MLA_ROPE), w.dtype)
    return jnp.concatenate([w3, pad], axis=2).reshape(k, MLA_HEADS * MLA_QW)


def _uq_to_logical(w):
    k = w.shape[0]
    return w.reshape(k, MLA_HEADS, MLA_QW)[:, :, :MLA_NOPE + MLA_ROPE].reshape(k, -1)


def _ukv_to_physical(w):
    k = w.shape[0]
    w3 = w.reshape(k, MLA_HEADS, MLA_NOPE + MLA_DV)
    return jnp.concatenate([w3[:, :, :MLA_NOPE].reshape(k, -1), w3[:, :, MLA_NOPE:].reshape(k, -1)], axis=1)


def _ukv_to_logical(w):
    k = w.shape[0]
    kn = w[:, :MLA_HEADS * MLA_NOPE].reshape(k, MLA_HEADS, MLA_NOPE)
    v = w[:, MLA_HEADS * MLA_NOPE:].reshape(k, MLA_HEADS, MLA_DV)
    return jnp.concatenate([kn, v], axis=2).reshape(k, -1)


def _matmul(a, b, *, ta=False, tb=False, out_dtype=F32, name, tm_cap=1024, tn_cap=2048, tk_cap=2048, deps=()):
    m, k = (a.shape[1], a.shape[0]) if ta else a.shape
    n = b.shape[0] if tb else b.shape[1]
    assert k == (b.shape[1] if tb else b.shape[0])
    tm, tn, tk = _tile(m, tm_cap), _tile(n, tn_cap), _tile(k, tk_cap)
    nk = k // tk

    def body(a_ref, b_ref, *rest):
        dims = (((0 if ta else 1,), (1 if tb else 0,)), ((), ()))
        part = lax.dot_general(a_ref[...].astype(BF16), b_ref[...].astype(BF16), dims, preferred_element_type=F32)
        if nk == 1:
            rest[len(deps)][...] = part.astype(out_dtype)
            return
        o_ref, acc_ref = rest[len(deps):]
        kk = pl.program_id(2)

        @pl.when(kk == 0)
        def _():
            acc_ref[...] = part

        @pl.when(jnp.logical_and(kk > 0, kk < nk - 1))
        def _():
            acc_ref[...] += part

        @pl.when(kk == nk - 1)
        def _():
            o_ref[...] = (acc_ref[...] + part).astype(o_ref.dtype)

    a_spec = pl.BlockSpec((tk, tm), lambda i, j, kk: (kk, i)) if ta else pl.BlockSpec((tm, tk), lambda i, j, kk: (i, kk))
    b_spec = pl.BlockSpec((tn, tk), lambda i, j, kk: (j, kk)) if tb else pl.BlockSpec((tk, tn), lambda i, j, kk: (kk, j))
    return pl.pallas_call(
        body, name=name, grid=(m // tm, n // tn, nk),
        in_specs=[a_spec, b_spec] + [ANY] * len(deps),
        out_specs=pl.BlockSpec((tm, tn), lambda i, j, kk: (i, j)),
        out_shape=jax.ShapeDtypeStruct((m, n), out_dtype),
        scratch_shapes=[pltpu.VMEM((tm, tn), F32)] if nk > 1 else [],
        compiler_params=_cp("parallel", "parallel", "arbitrary"),
    )(a, b, *deps)


def _out_proj_resid(merged, w_out, x, gate):
    m, k = merged.shape
    n = w_out.shape[1]
    tm, tn = _tile(m, 1024), _tile(n, 1024)

    def body(a_ref, b_ref, x_ref, g_ref, o_ref, xn_ref):
        out = _dot(a_ref[...], b_ref[...])
        o_ref[...] = out
        xn_ref[...] = x_ref[...] + g_ref[...] * out

    tile = pl.BlockSpec((tm, tn), lambda i, j: (i, j))
    shape = jax.ShapeDtypeStruct((m, n), F32)
    return pl.pallas_call(
        body, name="mm_out_resid", grid=(m // tm, n // tn),
        in_specs=[pl.BlockSpec((tm, k), lambda i, j: (i, 0)), pl.BlockSpec((k, tn), lambda i, j: (0, j)),
                  tile, pl.BlockSpec((1, tn), lambda i, j: (0, j))],
        out_specs=[tile, tile], out_shape=[shape, shape],
        compiler_params=_cp("parallel", "parallel"),
    )(merged, w_out, x, gate)


def _row(tm, w, cb=0):
    return pl.BlockSpec((tm, w), lambda i, cb=cb: (i, cb))


def _vec(w, cb=0):
    return pl.BlockSpec((1, w), lambda i, cb=cb: (0, cb))


def _dproj_out(dproj, s, lay, tm, name):
    w = lay.width[name]
    cb = lay.off[name] // w
    spec = _row(tm, w, cb)
    shape = jax.ShapeDtypeStruct((s, lay.total), BF16)
    return spec, shape


def _norm_mod_fwd(x, g, scale, shift, deps):
    s, d = x.shape
    tm = _rtile(s, 256)

    def body(x_ref, g_ref, sc_ref, sh_ref, *rest):
        h_ref = rest[len(deps)]
        xv = x_ref[...]
        xh = xv * lax.rsqrt(jnp.mean(xv * xv, axis=-1, keepdims=True) + EPS)
        h_ref[...] = ((xh * g_ref[...]) * (1.0 + sc_ref[...]) + sh_ref[...]).astype(BF16)

    return pl.pallas_call(
        body, name="norm_mod_fwd", grid=(s // tm,),
        in_specs=[_row(tm, d), _vec(d), _vec(d), _vec(d)] + [ANY] * len(deps),
        out_specs=_row(tm, d), out_shape=jax.ShapeDtypeStruct((s, d), BF16),
        compiler_params=_cp("parallel"),
    )(x, g, scale, shift, *deps)


def _norm_mod_bwd(x, g, scale, dh, dres):
    s, d = x.shape
    tm = _rtile(s, 256)

    def body(x_ref, g_ref, sc_ref, dh_ref, dres_ref, dx_ref, dsh_ref, dsc_ref, dg_ref):
        @pl.when(pl.program_id(0) == 0)
        def _():
            dsh_ref[...] = jnp.zeros_like(dsh_ref)
            dsc_ref[...] = jnp.zeros_like(dsc_ref)
            dg_ref[...] = jnp.zeros_like(dg_ref)

        xv, gv, dhv = x_ref[...], g_ref[...], dh_ref[...]
        rstd = lax.rsqrt(jnp.mean(xv * xv, axis=-1, keepdims=True) + EPS)
        xh = xv * rstd
        dy = dhv * (1.0 + sc_ref[...])
        dxh = dy * gv
        dx_ref[...] = dres_ref[...] + rstd * (dxh - xh * jnp.mean(dxh * xh, axis=-1, keepdims=True))
        dsh_ref[...] += jnp.sum(dhv, axis=0, keepdims=True)
        dsc_ref[...] += jnp.sum(dhv * (xh * gv), axis=0, keepdims=True)
        dg_ref[...] += jnp.sum(dy * xh, axis=0, keepdims=True)

    vec = jax.ShapeDtypeStruct((1, d), F32)
    return pl.pallas_call(
        body, name="norm_mod_bwd", grid=(s // tm,),
        in_specs=[_row(tm, d), _vec(d), _vec(d), _row(tm, d), _row(tm, d)],
        out_specs=[_row(tm, d), _vec(d), _vec(d), _vec(d)],
        out_shape=[jax.ShapeDtypeStruct((s, d), F32), vec, vec, vec],
        compiler_params=_cp("arbitrary"),
    )(x, g, scale, dh, dres)


def _final_loss(x, g, target):
    s, d = x.shape
    tm = _rtile(s, 256)

    def body(x_ref, g_ref, t_ref, l_ref, dx_ref, dg_ref):
        @pl.when(pl.program_id(0) == 0)
        def _():
            l_ref[...] = jnp.zeros_like(l_ref)
            dg_ref[...] = jnp.zeros_like(dg_ref)

        xv, gv = x_ref[...], g_ref[...]
        rstd = lax.rsqrt(jnp.mean(xv * xv, axis=-1, keepdims=True) + EPS)
        xh = xv * rstd
        err = xh * gv - t_ref[...]
        row = jnp.mean(err * err, axis=-1, keepdims=True)
        l_ref[...] += 0.5 * jnp.sum(row, axis=0, keepdims=True)
        dy = err / d
        dxh = dy * gv
        dx_ref[...] = rstd * (dxh - xh * jnp.mean(dxh * xh, axis=-1, keepdims=True))
        dg_ref[...] += jnp.sum(dy * xh, axis=0, keepdims=True)

    return pl.pallas_call(
        body, name="final_loss", grid=(s // tm,),
        in_specs=[_row(tm, d), _vec(d), _row(tm, d)],
        out_specs=[_vec(LANES), _row(tm, d), _vec(d)],
        out_shape=[jax.ShapeDtypeStruct((1, LANES), F32), jax.ShapeDtypeStruct((s, d), F32),
                   jax.ShapeDtypeStruct((1, d), F32)],
        compiler_params=_cp("arbitrary"),
    )(x, g, target)


def _resid_bwd(dxn, out, gate, deps):
    s, d = dxn.shape
    tm = _rtile(s, 256)

    def body(dx_ref, o_ref, g_ref, *rest):
        do_ref, dg_ref = rest[len(deps):]

        @pl.when(pl.program_id(0) == 0)
        def _():
            dg_ref[...] = jnp.zeros_like(dg_ref)

        dxv = dx_ref[...]
        do_ref[...] = (dxv * g_ref[...]).astype(BF16)
        dg_ref[...] += jnp.sum(dxv * o_ref[...], axis=0, keepdims=True)

    return pl.pallas_call(
        body, name="resid_bwd", grid=(s // tm,),
        in_specs=[_row(tm, d), _row(tm, d), _vec(d)] + [ANY] * len(deps),
        out_specs=[_row(tm, d), _vec(d)],
        out_shape=[jax.ShapeDtypeStruct((s, d), BF16), jax.ShapeDtypeStruct((1, d), F32)],
        compiler_params=_cp("arbitrary"),
    )(dxn, out, gate, *deps)


def _merge_fwd(y_ret, y_mla, proj, lay):
    s, d = y_ret.shape
    tm = _rtile(s, 256)
    cb = lay.off["bg"] // (2 * d)

    def body(a_ref, b_ref, bg_ref, m_ref):
        sg = _sigmoid(bg_ref[...])
        m_ref[...] = (sg[:, :d] * a_ref[...] + sg[:, d:] * b_ref[...]).astype(BF16)

    return pl.pallas_call(
        body, name="merge_fwd", grid=(s // tm,),
        in_specs=[_row(tm, d), _row(tm, d), _row(tm, 2 * d, cb)],
        out_specs=_row(tm, d), out_shape=jax.ShapeDtypeStruct((s, d), BF16),
        compiler_params=_cp("parallel"),
    )(y_ret, y_mla, proj)


def _merge_bwd(dm, y_ret, y_mla, proj, lay):
    s, d = dm.shape
    tm = _rtile(s, 256)
    cb = lay.off["bg"] // (2 * d)
    dp_spec, dp_shape = _dproj_out(None, s, lay, tm, "bg")

    def body(dm_ref, a_ref, b_ref, bg_ref, da_ref, db_ref, dp_ref):
        sg = _sigmoid(bg_ref[...])
        dmv = dm_ref[...]
        ga, gb = sg[:, :d], sg[:, d:]
        da_ref[...] = (dmv * ga).astype(BF16)
        db_ref[...] = (dmv * gb).astype(BF16)
        dp_ref[:, :d] = (dmv * a_ref[...] * ga * (1.0 - ga)).astype(BF16)
        dp_ref[:, d:] = (dmv * b_ref[...] * gb * (1.0 - gb)).astype(BF16)

    act = jax.ShapeDtypeStruct((s, d), BF16)
    return pl.pallas_call(
        body, name="merge_bwd", grid=(s // tm,),
        in_specs=[_row(tm, d), _row(tm, d), _row(tm, d), _row(tm, 2 * d, cb)],
        out_specs=[_row(tm, d), _row(tm, d), dp_spec],
        out_shape=[act, act, dp_shape],
        compiler_params=_cp("parallel"),
    )(dm, y_ret, y_mla, proj)


def _mla_prep(proj, g_cq, g_ckv, lay):
    s = proj.shape[0]
    r = lay.rank
    tm = _rtile(s, 512)
    cb = lay.off["cqkv"] // (2 * r)

    def body(p_ref, gq_ref, gk_ref, q_ref, k_ref):
        pv = p_ref[...]
        for lo, g_ref, o_ref in ((0, gq_ref, q_ref), (r, gk_ref, k_ref)):
            xv = pv[:, lo:lo + r]
            xh = xv * lax.rsqrt(jnp.mean(xv * xv, axis=-1, keepdims=True) + EPS)
            o_ref[...] = (xh * g_ref[...]).astype(BF16)

    act = jax.ShapeDtypeStruct((s, r), BF16)
    return pl.pallas_call(
        body, name="mla_prep", grid=(s // tm,),
        in_specs=[_row(tm, 2 * r, cb), _vec(r), _vec(r)],
        out_specs=[_row(tm, r), _row(tm, r)], out_shape=[act, act],
        compiler_params=_cp("parallel"),
    )(proj, g_cq, g_ckv)


def _mla_prep_bwd(proj, dqn, dkn, g_cq, g_ckv, dproj, lay):
    s = proj.shape[0]
    r = lay.rank
    tm = _rtile(s, 512)
    cb = lay.off["cqkv"] // (2 * r)
    dp_spec, dp_shape = _dproj_out(dproj, s, lay, tm, "cqkv")

    def body(p_ref, dq_ref, dk_ref, gq_ref, gk_ref, _, dp_ref, dgq_ref, dgk_ref):
        @pl.when(pl.program_id(0) == 0)
        def _():
            dgq_ref[...] = jnp.zeros_like(dgq_ref)
            dgk_ref[...] = jnp.zeros_like(dgk_ref)

        pv = p_ref[...]
        for lo, g_ref, d_ref, dg_ref in ((0, gq_ref, dq_ref, dgq_ref), (r, gk_ref, dk_ref, dgk_ref)):
            xv = pv[:, lo:lo + r]
            rstd = lax.rsqrt(jnp.mean(xv * xv, axis=-1, keepdims=True) + EPS)
            xh = xv * rstd
            dy = d_ref[...]
            dxh = dy * g_ref[...]
            dp_ref[:, lo:lo + r] = (rstd * (dxh - xh * jnp.mean(dxh * xh, axis=-1, keepdims=True))).astype(BF16)
            dg_ref[...] += jnp.sum(dy * xh, axis=0, keepdims=True)

    vec = jax.ShapeDtypeStruct((1, r), F32)
    return pl.pallas_call(
        body, name="mla_prep_bwd", grid=(s // tm,),
        in_specs=[_row(tm, 2 * r, cb), _row(tm, r), _row(tm, r), _vec(r), _vec(r), ANY],
        out_specs=[dp_spec, _vec(r), _vec(r)], out_shape=[dp_shape, vec, vec],
        input_output_aliases={5: 0},
        compiler_params=_cp("arbitrary"),
    )(proj, dqn, dkn, g_cq, g_ckv, dproj)


def _rope_tile(t, a, b, c):
    return t * a + _roll(t, 96) * b + _roll(t, 32) * c


def _rope_tile_bwd(dy, a, b, c):
    return dy * a + _roll(dy * b, 32) + _roll(dy * c, 96)


def _attn_block(s):
    return _rtile(s, 512)


def _qk_prep(qp, kvp, proj, ta, tb, tc, lay):
    s = qp.shape[0]
    hq = MLA_HEADS * MLA_QW
    hv = MLA_HEADS * MLA_DV
    blk = _attn_block(s)
    tm = _rtile(blk, 256)
    per = blk // tm
    kr_cb = lay.off["kr"] // LANES

    def body(q_ref, kv_ref, kr_ref, a_ref, b_ref, c_ref, qc_ref, kc_ref, v_ref, kt_ref, vt_ref):
        a, b, c = a_ref[...], b_ref[...], c_ref[...]
        krot = _rope_tile(kr_ref[...], a, b, c)
        krot_b, krot_t = krot.astype(BF16), krot.T.astype(BF16)
        for h in range(MLA_HEADS):
            q0 = h * MLA_QW
            qc_ref[:, q0:q0 + MLA_NOPE] = (q_ref[:, q0:q0 + MLA_NOPE] * QK_LOG2_SCALE).astype(BF16)
            qc_ref[:, q0 + MLA_NOPE:q0 + MLA_QW] = (
                _rope_tile(q_ref[:, q0 + MLA_NOPE:q0 + MLA_QW], a, b, c) * QK_LOG2_SCALE).astype(BF16)
            kn = kv_ref[:, h * MLA_NOPE:(h + 1) * MLA_NOPE]
            kc_ref[:, q0:q0 + MLA_NOPE] = kn.astype(BF16)
            kc_ref[:, q0 + MLA_NOPE:q0 + MLA_QW] = krot_b
            kt_ref[h, :MLA_NOPE, :] = kn.T.astype(BF16)
            kt_ref[h, MLA_NOPE:, :] = krot_t
            vh = kv_ref[:, (MLA_HEADS + h) * MLA_NOPE:(MLA_HEADS + h + 1) * MLA_NOPE]
            v_ref[:, h * MLA_DV:(h + 1) * MLA_DV] = vh.astype(BF16)
            vt_ref[h] = vh.T.astype(BF16)

    return pl.pallas_call(
        body, name="qk_prep", grid=(s // tm,),
        in_specs=[_row(tm, hq), _row(tm, hq), _row(tm, LANES, kr_cb), _row(tm, LANES), _row(tm, LANES), _row(tm, LANES)],
        out_specs=[_row(tm, hq), _row(tm, hq), _row(tm, hv),
                   pl.BlockSpec((MLA_HEADS, None, MLA_QW, tm), lambda i: (0, i // per, 0, i % per)),
                   pl.BlockSpec((MLA_HEADS, None, MLA_DV, tm), lambda i: (0, i // per, 0, i % per))],
        out_shape=[jax.ShapeDtypeStruct((s, hq), BF16), jax.ShapeDtypeStruct((s, hq), BF16),
                   jax.ShapeDtypeStruct((s, hv), BF16),
                   jax.ShapeDtypeStruct((MLA_HEADS, s // blk, MLA_QW, blk), BF16),
                   jax.ShapeDtypeStruct((MLA_HEADS, s // blk, MLA_DV, blk), BF16)],
        compiler_params=_cp("parallel"),
    )(qp, kvp, proj, ta, tb, tc)


def _kv_bwd_prep(dk_cat, dv, ta, tb, tc, dproj, lay):
    s = dk_cat.shape[0]
    hq = MLA_HEADS * MLA_QW
    hv = MLA_HEADS * MLA_DV
    tm = _rtile(s, 256)
    dp_spec, dp_shape = _dproj_out(dproj, s, lay, tm, "kr")

    def body(dk_ref, dv_ref, a_ref, b_ref, c_ref, _, dkv_ref, dp_ref):
        acc = jnp.zeros((tm, LANES), F32)
        for h in range(MLA_HEADS):
            q0 = h * MLA_QW
            dkv_ref[:, h * MLA_NOPE:(h + 1) * MLA_NOPE] = dk_ref[:, q0:q0 + MLA_NOPE].astype(BF16)
            acc = acc + dk_ref[:, q0 + MLA_NOPE:q0 + MLA_QW]
        dkv_ref[:, MLA_HEADS * MLA_NOPE:] = dv_ref[...].astype(BF16)
        dp_ref[...] = _rope_tile_bwd(acc, a_ref[...], b_ref[...], c_ref[...]).astype(BF16)

    return pl.pallas_call(
        body, name="kv_bwd_prep", grid=(s // tm,),
        in_specs=[_row(tm, hq), _row(tm, hv), _row(tm, LANES), _row(tm, LANES), _row(tm, LANES), ANY],
        out_specs=[_row(tm, hq), dp_spec],
        out_shape=[jax.ShapeDtypeStruct((s, hq), BF16), dp_shape],
        input_output_aliases={5: 1},
        compiler_params=_cp("parallel"),
    )(dk_cat, dv, ta, tb, tc, dproj)


def _mla_gate_bwd(du, o, proj, dproj, lay):
    s, vw = du.shape
    tm = _rtile(s, 256)
    cb = lay.off["mg"] // vw
    dp_spec, dp_shape = _dproj_out(dproj, s, lay, tm, "mg")
    assert MLA_HEADS <= LANES

    def body(du_ref, o_ref, g_ref, _, do_ref, dl_ref, dp_ref):
        gv, duv, ov = g_ref[...], du_ref[...], o_ref[...]
        sg = _sigmoid(gv)
        do = (duv * (gv * sg)).astype(BF16)
        do_ref[...] = do
        dp_ref[...] = (duv * ov * (sg + gv * sg * (1.0 - sg))).astype(BF16)
        prod = do.astype(F32) * ov
        lane = lax.broadcasted_iota(jnp.int32, (tm, LANES), 1)
        delta = jnp.zeros((tm, LANES), F32)
        for h in range(MLA_HEADS):
            dh = jnp.sum(prod[:, h * MLA_DV:(h + 1) * MLA_DV], axis=-1, keepdims=True)
            delta = jnp.where(lane == h, dh, delta)
        dl_ref[...] = delta

    return pl.pallas_call(
        body, name="mla_gate_bwd", grid=(s // tm,),
        in_specs=[_row(tm, vw), _row(tm, vw), _row(tm, vw, cb), ANY],
        out_specs=[_row(tm, vw), _row(tm, LANES), dp_spec],
        out_shape=[jax.ShapeDtypeStruct((s, vw), BF16), jax.ShapeDtypeStruct((s, LANES), F32), dp_shape],
        input_output_aliases={3: 2},
        compiler_params=_cp("parallel"),
    )(du, o, proj, dproj)


RET_BLOCK = 256


def _ret_tables(lg, blk):
    ri = lax.broadcasted_iota(jnp.int32, (blk, blk), 0)
    ci = lax.broadcasted_iota(jnp.int32, (blk, blk), 1)
    col = lax.broadcasted_iota(jnp.int32, (blk, 1), 0).astype(F32)
    dist = jnp.abs(ri - ci).astype(F32)
    dmat = jnp.where(ci // CHUNK <= ri // CHUNK, jnp.exp(dist * lg), 0.0)
    xi = jnp.exp((col + 1.0) * lg)
    zeta = jnp.exp((blk - 1.0 - col) * lg)
    decay = jnp.exp(jnp.full((1, 1), blk, F32) * lg)
    return dmat, xi, zeta, decay


def _ret_qkvg(blk, cs, sn):
    dv = RET_DV
    q = blk[:, :RET_DK]
    k = blk[:, RET_DK:2 * RET_DK]
    q = q * cs + _roll(q, RET_DK // 2) * sn
    k = (k * cs + _roll(k, RET_DK // 2) * sn) * (RET_DK ** -0.5)
    return q, k, blk[:, 2 * RET_DK:2 * RET_DK + dv], blk[:, 2 * RET_DK + dv:]


def _group_norm(o):
    mu = jnp.mean(o, axis=-1, keepdims=True)
    oc = o - mu
    rstd = lax.rsqrt(jnp.mean(oc * oc, axis=-1, keepdims=True) + EPS)
    return oc * rstd, rstd


def _ret_fwd(proj, lg, cosr, sinr, lay):
    s = proj.shape[0]
    dv, w = RET_DV, lay.ret_w
    tb = _rtile(s, 512)
    blk = min(RET_BLOCK, tb)
    nb, nch = s // tb, tb // blk
    cb0 = lay.off["ret"] // w

    def body(lg_ref, p_ref, cos_ref, sin_ref, o_ref, u_ref, st_ref, state):
        @pl.when(pl.program_id(1) == 0)
        def _():
            state[...] = jnp.zeros_like(state)

        dmat, xi, zeta, decay = _ret_tables(lg_ref[pl.program_id(0)], blk)
        for c in range(nch):
            rows = slice(c * blk, (c + 1) * blk)
            q, k, v, g = _ret_qkvg(p_ref[rows, :], cos_ref[rows, :], sin_ref[rows, :])
            qb, kb, vb = q.astype(BF16), k.astype(BF16), v.astype(BF16)
            sc = _dot_nt(qb, kb) * dmat
            st = state[...]
            o = _dot(sc.astype(BF16), vb) + _dot((q * xi).astype(BF16), st.astype(BF16))
            st_ref[c] = st.astype(BF16)
            state[...] = st * decay + _dot_tn((k * zeta).astype(BF16), vb)
            o_ref[rows, :] = o
            n, _ = _group_norm(o)
            u_ref[rows, :] = (n * (g * _sigmoid(g))).astype(BF16)

    return pl.pallas_call(
        body, name="ret_fwd", grid=(RET_HEADS, nb),
        in_specs=[pl.BlockSpec(memory_space=pltpu.SMEM),
                  pl.BlockSpec((tb, w), lambda h, b: (b, cb0 + h)),
                  pl.BlockSpec((tb, RET_DK), lambda h, b: (b, 0)),
                  pl.BlockSpec((tb, RET_DK), lambda h, b: (b, 0))],
        out_specs=[pl.BlockSpec((tb, dv), lambda h, b: (b, h)),
                   pl.BlockSpec((tb, dv), lambda h, b: (b, h)),
                   pl.BlockSpec((None, nch, RET_DK, dv), lambda h, b: (h, b, 0, 0))],
        out_shape=[jax.ShapeDtypeStruct((s, RET_HEADS * dv), F32),
                   jax.ShapeDtypeStruct((s, RET_HEADS * dv), BF16),
                   jax.ShapeDtypeStruct((RET_HEADS, s // blk, RET_DK, dv), BF16)],
        scratch_shapes=[pltpu.VMEM((RET_DK, dv), F32)],
        compiler_params=_cp("parallel", "arbitrary"),
    )(lg, proj, cosr, sinr)


def _ret_bwd(proj, lg, cosr, sinr, o, du, states, dproj, lay):
    s = proj.shape[0]
    dv, w = RET_DV, lay.ret_w
    tb = _rtile(s, 512)
    blk = min(RET_BLOCK, tb)
    nb, nch = s // tb, tb // blk
    cb0 = lay.off["ret"] // w

    def body(lg_ref, p_ref, cos_ref, sin_ref, o_ref, du_ref, st_ref, _, dp_ref, dstate):
        @pl.when(pl.program_id(1) == 0)
        def _():
            dstate[...] = jnp.zeros_like(dstate)

        dmat, xi, zeta, decay = _ret_tables(lg_ref[pl.program_id(0)], blk)
        for c in reversed(range(nch)):
            rows = slice(c * blk, (c + 1) * blk)
            cs, sn = cos_ref[rows, :], sin_ref[rows, :]
            q, k, v, g = _ret_qkvg(p_ref[rows, :], cs, sn)
            qb, kb, vb = q.astype(BF16), k.astype(BF16), v.astype(BF16)
            n, rstd = _group_norm(o_ref[rows, :])
            sg = _sigmoid(g)
            duv = du_ref[rows, :]
            dn = duv * (g * sg)
            dg = duv * n * (sg + g * sg * (1.0 - sg))
            do = rstd * (dn - jnp.mean(dn, axis=-1, keepdims=True) - n * jnp.mean(dn * n, axis=-1, keepdims=True))
            dob = do.astype(BF16)
            rb = st_ref[c]
            drb = dstate[...].astype(BF16)
            sc = (_dot_nt(qb, kb) * dmat).astype(BF16)
            dsc = (_dot_nt(dob, vb) * dmat).astype(BF16)
            qx = (q * xi).astype(BF16)
            kz = (k * zeta).astype(BF16)
            dq = _dot(dsc, kb) + _dot_nt(dob, rb) * xi
            dk = (_dot_tn(dsc, qb) + _dot_nt(vb, drb) * zeta) * (RET_DK ** -0.5)
            dvv = _dot_tn(sc, dob) + _dot(kz, drb)
            dstate[...] = dstate[...] * decay + _dot_tn(qx, dob)
            dp_ref[rows, :RET_DK] = (dq * cs + _roll(dq * sn, RET_DK // 2)).astype(BF16)
            dp_ref[rows, RET_DK:2 * RET_DK] = (dk * cs + _roll(dk * sn, RET_DK // 2)).astype(BF16)
            dp_ref[rows, 2 * RET_DK:2 * RET_DK + dv] = dvv.astype(BF16)
            dp_ref[rows, 2 * RET_DK + dv:] = dg.astype(BF16)

    rev = lambda h, b: (nb - 1 - b, h)
    return pl.pallas_call(
        body, name="ret_bwd", grid=(RET_HEADS, nb),
        in_specs=[pl.BlockSpec(memory_space=pltpu.SMEM),
                  pl.BlockSpec((tb, w), lambda h, b: (nb - 1 - b, cb0 + h)),
                  pl.BlockSpec((tb, RET_DK), lambda h, b: (nb - 1 - b, 0)),
                  pl.BlockSpec((tb, RET_DK), lambda h, b: (nb - 1 - b, 0)),
                  pl.BlockSpec((tb, dv), rev),
                  pl.BlockSpec((tb, dv), rev),
                  pl.BlockSpec((None, nch, RET_DK, dv), lambda h, b: (h, nb - 1 - b, 0, 0)),
                  ANY],
        out_specs=pl.BlockSpec((tb, w), lambda h, b: (nb - 1 - b, cb0 + h)),
        out_shape=jax.ShapeDtypeStruct((s, lay.total), BF16),
        input_output_aliases={7: 0},
        scratch_shapes=[pltpu.VMEM((RET_DK, dv), F32)],
        compiler_params=_cp("parallel", "arbitrary"),
    )(lg, proj, cosr, sinr, o, du, states, dproj)


ATTN_HEADS_PER_STEP = 2


def _attn_mask(rows, cols, row0, col0, keys_on_rows):
    ri = (lax.broadcasted_iota(jnp.int32, (rows, cols), 0) + row0) // CHUNK
    ci = (lax.broadcasted_iota(jnp.int32, (rows, cols), 1) + col0) // CHUNK
    return ri <= ci if keys_on_rows else ci <= ri


def _attn_fwd(q_cat, k_cat, v_t, proj, lay):
    s = q_cat.shape[0]
    bq = _attn_block(s)
    nq = s // bq
    mg_cb = lay.off["mg"] // MLA_DV

    hp = ATTN_HEADS_PER_STEP
    assert MLA_HEADS % hp == 0 and mg_cb % hp == 0

    def body(q_ref, k_ref, vt_ref, g_ref, o_ref, u_ref, lse_ref):
        i = pl.program_id(1)
        qs = [q_ref[:, a * MLA_QW:(a + 1) * MLA_QW] for a in range(hp)]

        def scores(a, j):
            r0 = pl.multiple_of(j * bq, bq)
            return _dot_nt(k_ref[pl.ds(r0, bq), a * MLA_QW:(a + 1) * MLA_QW], qs[a])

        def update(a, j, sc, m, l, acc):
            mn = jnp.maximum(m, jnp.max(sc, axis=0, keepdims=True))
            p = jnp.exp2(sc - mn)
            alpha = jnp.exp2(m - mn)
            l = alpha * l + jnp.sum(p, axis=0, keepdims=True)
            acc = alpha * acc + _dot(vt_ref[a, j], p.astype(BF16))
            return mn, l, acc

        def hide(sc):
            return jnp.where(_attn_mask(bq, bq, 0, 0, True), sc, NEG_INF)

        def pair(j, carry, last):
            sa = [scores(a, j) for a in range(hp)]
            sb = [scores(a, j + 1) for a in range(hp)]
            carry = [update(a, j, sa[a], *carry[a]) for a in range(hp)]
            return tuple(update(a, j + 1, hide(sb[a]) if last else sb[a], *carry[a]) for a in range(hp))

        def single(carry):
            return tuple(update(a, i, hide(scores(a, i)), *carry[a]) for a in range(hp))

        init = tuple((jnp.full((1, bq), NEG_INF, F32), jnp.zeros((1, bq), F32), jnp.zeros((MLA_DV, bq), F32))
                     for _ in range(hp))
        carry = lax.fori_loop(0, i // 2, lambda t, c: pair(2 * t, c, False), init)
        carry = lax.cond(i % 2 == 1, lambda c: pair(i - 1, c, True), single, carry)
        for a, (m, l, acc) in enumerate(carry):
            cols = slice(a * MLA_DV, (a + 1) * MLA_DV)
            o = (acc / l).T
            gv = g_ref[:, cols]
            o_ref[:, cols] = o
            u_ref[:, cols] = (o * (gv * _sigmoid(gv))).astype(BF16)
            lse_ref[a] = m + jnp.log2(l)

    return pl.pallas_call(
        body, name="attn_fwd", grid=(MLA_HEADS // hp, nq),
        in_specs=[pl.BlockSpec((bq, hp * MLA_QW), lambda h, i: (i, h)),
                  pl.BlockSpec((s, hp * MLA_QW), lambda h, i: (0, h)),
                  pl.BlockSpec((hp, nq, MLA_DV, bq), lambda h, i: (h, 0, 0, 0)),
                  pl.BlockSpec((bq, hp * MLA_DV), lambda h, i: (i, mg_cb // hp + h))],
        out_specs=[pl.BlockSpec((bq, hp * MLA_DV), lambda h, i: (i, h)),
                   pl.BlockSpec((bq, hp * MLA_DV), lambda h, i: (i, h)),
                   pl.BlockSpec((hp, None, 1, bq), lambda h, i: (h, i, 0, 0))],
        out_shape=[jax.ShapeDtypeStruct((s, MLA_HEADS * MLA_DV), F32),
                   jax.ShapeDtypeStruct((s, MLA_HEADS * MLA_DV), BF16),
                   jax.ShapeDtypeStruct((MLA_HEADS, nq, 1, bq), F32)],
        compiler_params=_cp("parallel", "parallel"),
    )(q_cat, k_cat, v_t, proj)


def _attn_bwd(q_cat, k_cat, v, k_t, do, lse, delta, ta, tb, tc):
    s = q_cat.shape[0]
    blk = _attn_block(s)
    nb = s // blk
    hp = ATTN_HEADS_PER_STEP
    qw, dvw = hp * MLA_QW, hp * MLA_DV

    def body(q_ref, k_ref, v_ref, kt_ref, do_ref, lse_ref, dl_ref, a_ref, b_ref, c_ref,
             dq_ref, dk_ref, dv_ref, dq_acc):
        j = pl.program_id(1)

        @pl.when(j == 0)
        def _():
            dq_acc[...] = jnp.zeros_like(dq_acc)

        def qcols(a):
            return slice(a * MLA_QW, (a + 1) * MLA_QW)

        def vcols(a):
            return slice(a * MLA_DV, (a + 1) * MLA_DV)

        kbs = [k_ref[:, qcols(a)] for a in range(hp)]
        vbs = [v_ref[:, vcols(a)] for a in range(hp)]

        def step(i, carry, masked):
            r0 = pl.multiple_of(i * blk, blk)
            out = []
            for a in range(hp):
                dk, dvv = carry[a]
                q, dob = q_ref[pl.ds(r0, blk), qcols(a)], do_ref[pl.ds(r0, blk), vcols(a)]
                sc = _dot_nt(kbs[a], q)
                if masked:
                    sc = jnp.where(_attn_mask(blk, blk, 0, 0, True), sc, NEG_INF)
                p = jnp.exp2(sc - lse_ref[a, i])
                dvv = dvv + _dot(p.astype(BF16), dob)
                ds = (p * (_dot_nt(vbs[a], dob) - dl_ref[a, i])).astype(BF16)
                dk = dk + _dot(ds, q)
                dq_acc[a, i] = dq_acc[a, i] + _dot(kt_ref[a], ds)
                out.append((dk, dvv))
            return tuple(out)

        init = tuple((jnp.zeros((blk, MLA_QW), F32), jnp.zeros((blk, MLA_DV), F32)) for _ in range(hp))
        carry = step(j, init, True)
        carry = lax.fori_loop(j + 1, nb, lambda i, c: step(i, c, False), carry)
        for a, (dk, dvv) in enumerate(carry):
            dk_ref[:, qcols(a)] = dk * LN2
            dv_ref[:, vcols(a)] = dvv

        @pl.when(j == nb - 1)
        def _():
            for a in range(hp):
                for i in range(nb):
                    rows = slice(i * blk, (i + 1) * blk)
                    dq = dq_acc[a, i].T * QK_SCALE
                    c0 = a * MLA_QW
                    dq_ref[rows, c0:c0 + MLA_NOPE] = dq[:, :MLA_NOPE].astype(BF16)
                    dq_ref[rows, c0 + MLA_NOPE:c0 + MLA_QW] = _rope_tile_bwd(
                        dq[:, MLA_NOPE:], a_ref[rows, :], b_ref[rows, :], c_ref[rows, :]).astype(BF16)

    tab = pl.BlockSpec((s, LANES), lambda h, j: (0, 0), pipeline_mode=pl.Buffered(1))
    rows = pl.BlockSpec((hp, nb, 1, blk), lambda h, j: (h, 0, 0, 0))
    return pl.pallas_call(
        body, name="attn_bwd", grid=(MLA_HEADS // hp, nb),
        in_specs=[pl.BlockSpec((s, qw), lambda h, j: (0, h)),
                  pl.BlockSpec((blk, qw), lambda h, j: (j, h)),
                  pl.BlockSpec((blk, dvw), lambda h, j: (j, h)),
                  pl.BlockSpec((hp, None, MLA_QW, blk), lambda h, j: (h, j, 0, 0)),
                  pl.BlockSpec((s, dvw), lambda h, j: (0, h)),
                  rows, rows, tab, tab, tab],
        out_specs=[pl.BlockSpec((s, qw), lambda h, j: (0, h)),
                   pl.BlockSpec((blk, qw), lambda h, j: (j, h)),
                   pl.BlockSpec((blk, dvw), lambda h, j: (j, h))],
        out_shape=[jax.ShapeDtypeStruct((s, MLA_HEADS * MLA_QW), BF16),
                   jax.ShapeDtypeStruct((s, MLA_HEADS * MLA_QW), F32),
                   jax.ShapeDtypeStruct((s, MLA_HEADS * MLA_DV), F32)],
        scratch_shapes=[pltpu.VMEM((hp, nb, MLA_QW, blk), F32)],
        compiler_params=_cp("parallel", "arbitrary"),
    )(q_cat, k_cat, v, k_t, do, lse, delta, ta, tb, tc)


def _place():
    return lax.axis_index("x"), lax.axis_index("y"), lax.axis_index("c")


def _slot(px, py, pc):
    return 4 * px + 2 * py + pc


def _all_gather(shards, layer, name, vmem=False, deps=()):
    n = len(shards)

    def body(*refs):
        srcs, outs = refs[:n], refs[n + len(deps):2 * n + len(deps)]
        send_sems, recv_sems, local_sems = refs[2 * n + len(deps):]
        x, y, c = _place()
        me, sibling = (x, y, c), (x, y, 1 - c)
        chips = [(1 - x, y), (x, 1 - y), (1 - x, 1 - y)]
        firsts, passes, locals_ = [], [], []

        def copy(a, k, block, to, src=None):
            dst = outs[a].at[_slot(*block)]
            return pltpu.make_async_remote_copy(
                src_ref=dst if src is None else src, dst_ref=dst,
                send_sem=send_sems.at[7 * a + k], recv_sem=recv_sems.at[7 * a + k],
                device_id=to, device_id_type=MESH)

        for a in range(n):
            src = srcs[a] if layer is None else srcs[a].at[layer]
            mine = pltpu.make_async_copy(src, outs[a].at[_slot(*me)], local_sems.at[a])
            mine.start()
            locals_.append(mine)
            first = [copy(a, 0, me, sibling, src=src)]
            first += [copy(a, 1 + j, me, (*chip, c), src=src) for j, chip in enumerate(chips)]
            for cp in first:
                cp.start()
            firsts += first
        for a in range(n):
            for j, chip in enumerate(chips):
                copy(a, 1 + j, (*chip, c), me).wait_recv()
                fwd = copy(a, 4 + j, (*chip, c), sibling)
                fwd.start()
                passes.append(fwd)
        for a in range(n):
            copy(a, 0, sibling, me).wait_recv()
            for j, chip in enumerate(chips):
                copy(a, 4 + j, (*chip, 1 - c), me).wait_recv()
        for cp in firsts + passes:
            cp.wait_send()
        for mine in locals_:
            mine.wait()

    space = pl.BlockSpec(memory_space=pltpu.VMEM) if vmem else ANY
    out_shape = [jax.ShapeDtypeStruct((N_DEV,) + (a.shape if layer is None else a.shape[1:]), a.dtype) for a in shards]
    return pl.pallas_call(
        body, name=name,
        in_specs=[space] * n + [ANY] * len(deps), out_specs=[space] * n, out_shape=out_shape,
        scratch_shapes=[pltpu.SemaphoreType.DMA((7 * n,)), pltpu.SemaphoreType.DMA((7 * n,)),
                        pltpu.SemaphoreType.DMA((n,))],
        compiler_params=pltpu.CompilerParams(has_side_effects=True),
    )(*shards, *deps)


HBM = pl.BlockSpec(memory_space=pltpu.HBM)
SEM = pl.BlockSpec(memory_space=pltpu.SEMAPHORE)
EFFECT = pltpu.SideEffectType.DATAFLOW_SIDE_EFFECTING


def _push_copies(srcs, lands, send_sems, recv_sems, local_sems, by_peer):
    x, y, c = _place()
    me = _slot(x, y, c)
    local, remote = [], []
    for a, (src, land) in enumerate(zip(srcs, lands)):
        local.append(pltpu.make_async_copy(src.at[me] if by_peer else src, land.at[me], local_sems.at[a]))
        for r in range(1, N_DEV):
            peer = (1 - x if r & 4 else x, 1 - y if r & 2 else y, 1 - c if r & 1 else c)
            remote.append(pltpu.make_async_remote_copy(
                src_ref=src.at[_slot(*peer)] if by_peer else src, dst_ref=land.at[me],
                send_sem=send_sems.at[7 * a + r - 1], recv_sem=recv_sems.at[7 * a + r - 1],
                device_id=peer, device_id_type=MESH))
    return local, remote


def _push_start(srcs, by_peer, after, name):
    n = len(srcs)
    srcs = [pltpu.with_memory_space_constraint(a, pltpu.HBM) for a in srcs]
    lands = [pltpu.with_memory_space_constraint(
        lax.empty((N_DEV,) + (a.shape[1:] if by_peer else a.shape), a.dtype), pltpu.HBM) for a in srcs]

    def body(*refs):
        k = 2 * n + len(after)
        local, remote = _push_copies(refs[:n], refs[n:2 * n], refs[k], refs[k + 1], refs[k + 2], by_peer)
        for cp in local + remote:
            cp.start()
        token = refs[k + 3 + 2 * n]
        token[...] = jnp.zeros_like(token)

    outs = pl.pallas_call(
        body, name=name,
        out_shape=(pltpu.SemaphoreType.DMA((7 * n,)), pltpu.SemaphoreType.DMA((7 * n,)),
                   pltpu.SemaphoreType.DMA((n,)),
                   *[pltpu.HBM(a.shape, a.dtype) for a in srcs], *[pltpu.HBM(a.shape, a.dtype) for a in lands],
                   jax.ShapeDtypeStruct((8, LANES), F32)),
        in_specs=[HBM] * (2 * n) + [ANY] * len(after),
        out_specs=(SEM, SEM, SEM, *([HBM] * (2 * n)), pl.BlockSpec(memory_space=pltpu.VMEM)),
        input_output_aliases={i: 3 + i for i in range(2 * n)},
        compiler_params=pltpu.CompilerParams(has_side_effects=EFFECT),
    )(*srcs, *lands, *after)
    return outs[:3], outs[3:3 + n], outs[3 + n:3 + 2 * n], outs[3 + 2 * n]


def _push_wait(sems, srcs, lands, by_peer, after, name):
    n = len(srcs)

    def body(*refs):
        local, remote = _push_copies(refs[:n], refs[n:2 * n], refs[2 * n], refs[2 * n + 1], refs[2 * n + 2], by_peer)
        for cp in local:
            cp.wait()
        for cp in remote:
            cp.wait_send()
            cp.wait_recv()

    outs = pl.pallas_call(
        body, name=name,
        out_shape=[pltpu.HBM(a.shape, a.dtype) for a in list(srcs) + list(lands)],
        in_specs=[HBM] * (2 * n) + [SEM] * 3 + [ANY] * len(after),
        out_specs=[HBM] * (2 * n),
        input_output_aliases={i: i for i in range(2 * n)},
        compiler_params=pltpu.CompilerParams(has_side_effects=EFFECT),
    )(*srcs, *lands, *sems, *after)
    return outs[n:]


def _adam_math(g, w, m, v):
    m = ADAM_B1 * m + (1.0 - ADAM_B1) * g
    v = ADAM_B2 * v + (1.0 - ADAM_B2) * (g * g)
    m_hat = m / (1.0 - ADAM_B1 ** ADAM_STEP)
    v_hat = v / (1.0 - ADAM_B2 ** ADAM_STEP)
    delta = -ADAM_LR * (m_hat / (jnp.sqrt(v_hat) + ADAM_EPS) + ADAM_WD * w)
    return delta, m, v


def _adam_sharded(recvs, first, w, m, v, prev, name):
    nl, r, c = w.shape
    n = len(recvs)
    by_cols = r % 128 != 0
    tr, tc = (r, _tile(c, LANES)) if by_cols else (_rtile(r, 128), c)
    nr = c // tc if by_cols else r // tr
    prev = list(prev) if prev is not None else []

    def tile(t):
        return (0, t) if by_cols else (t, 0)

    def body(*refs):
        g_refs = refs[:n]
        w_ref, m_ref, v_ref = refs[n:n + 3]
        go_ref, d_ref, mo_ref, vo_ref = refs[n + 3 + len(prev):]
        layer = pl.program_id(0)
        for l in range(n):
            @pl.when(layer == l)
            def _(l=l):
                g = g_refs[l][0].astype(F32)
                for i in range(1, N_DEV):
                    g = g + g_refs[l][i].astype(F32)
                delta, mn, vn = _adam_math(g, w_ref[...], m_ref[...], v_ref[...])
                go_ref[...] = g
                d_ref[...] = delta
                mo_ref[...] = mn
                vo_ref[...] = vn

    def recv_spec(l):
        def index(layer, i):
            return (0,) + tile(jnp.where(layer == l, i, jnp.where(layer < l, 0, nr - 1)))
        return pl.BlockSpec((N_DEV, tr, tc), index)

    blk = pl.BlockSpec((None, tr, tc), lambda layer, i: (first + layer,) + tile(i))
    out = jax.ShapeDtypeStruct(w.shape, F32)
    return pl.pallas_call(
        body, name=name, grid=(n, nr),
        in_specs=[recv_spec(l) for l in range(n)] + [blk, blk, blk] + [ANY] * len(prev),
        out_specs=[blk] * 4, out_shape=[out] * 4,
        input_output_aliases={n + 3 + k: k for k in range(len(prev))},
        compiler_params=_cp("arbitrary", "arbitrary"),
    )(*recvs, w, m, v, *prev)


def _adam_mod(c_all_t, dmod, w, m, v):
    nl, d, c = w.shape
    tr = _rtile(d, 128)

    def body(ct_ref, dm_ref, w_ref, m_ref, v_ref, go_ref, d_ref, mo_ref, vo_ref):
        ct = ct_ref[...].astype(BF16).astype(F32)
        dm = dm_ref[...].astype(BF16).astype(F32)
        g = ct[:, 0:1] * dm[0:1, :]
        for b in range(1, N_DEV):
            g = g + ct[:, b:b + 1] * dm[b:b + 1, :]
        delta, mn, vn = _adam_math(g, w_ref[...], m_ref[...], v_ref[...])
        go_ref[...] = g
        d_ref[...] = delta
        mo_ref[...] = mn
        vo_ref[...] = vn

    blk = pl.BlockSpec((None, tr, c), lambda layer, i: (layer, i, 0))
    out = jax.ShapeDtypeStruct(w.shape, F32)
    return pl.pallas_call(
        body, name="adam_mod", grid=(nl, d // tr),
        in_specs=[pl.BlockSpec((tr, N_DEV), lambda layer, i: (i, 0)),
                  pl.BlockSpec((None, N_DEV, c), lambda layer, i: (layer, 0, 0)), blk, blk, blk],
        out_specs=[blk] * 4, out_shape=[out] * 4,
        compiler_params=_cp("parallel", "parallel"),
    )(c_all_t, dmod, w, m, v)


def _adam_small(g, w, m, v, name):
    def body(g_ref, w_ref, m_ref, v_ref, d_ref, mo_ref, vo_ref):
        delta, mn, vn = _adam_math(g_ref[...], w_ref[...], m_ref[...], v_ref[...])
        d_ref[...] = delta
        mo_ref[...] = mn
        vo_ref[...] = vn

    out = jax.ShapeDtypeStruct(w.shape, F32)
    return pl.pallas_call(body, name=name, out_shape=[out] * 3)(g, w, m, v)


def _sum_devices(parts):
    def body(p_ref, o_ref):
        acc = p_ref[0]
        for i in range(1, N_DEV):
            acc = acc + p_ref[i]
        o_ref[...] = acc

    return pl.pallas_call(body, name="sum_devices",
                          out_shape=jax.ShapeDtypeStruct(parts.shape[1:], F32))(parts)


def _rope_tables(positions):
    pos = positions.astype(F32)[:, None]

    def cs(dim):
        inv = 1.0 / (ROPE_BASE ** (jnp.arange(0, dim, 2, dtype=F32) / dim))
        ang = pos * inv
        return jnp.cos(ang), jnp.sin(ang)

    cr, sr = cs(RET_DK)
    cm, sm = cs(MLA_ROPE)
    z = jnp.zeros_like(cm)
    pad = jnp.zeros((pos.shape[0], LANES - MLA_ROPE), F32)
    cosr = jnp.concatenate([cr, cr], axis=1)
    sinr = jnp.concatenate([-sr, sr], axis=1)
    ta = jnp.concatenate([cm, cm, pad], axis=1)
    tb = jnp.concatenate([-sm, z, pad], axis=1)
    tc = jnp.concatenate([z, sm, pad], axis=1)
    return cosr, sinr, ta, tb, tc


def _layer_fwd(x, mod, g_norm, g_cq, g_ckv, wts, tabs, lg, lay, deps):
    d = x.shape[1]
    cosr, sinr, ta, tb, tc = tabs
    shift, scale, gate = mod[:, :d], mod[:, d:2 * d], mod[:, 2 * d:]
    h = _norm_mod_fwd(x, g_norm, scale, shift, deps)
    proj = _matmul(h, wts["in"], name="mm_proj", tn_cap=1920)
    o_ret, u_ret, states = _ret_fwd(proj, lg, cosr, sinr, lay)
    y_ret = _matmul(u_ret, wts["ret"], name="mm_y")
    cqn, ckvn = _mla_prep(proj, g_cq, g_ckv, lay)
    qp = _matmul(cqn, wts["uq"], name="mm_up")
    kvp = _matmul(ckvn, wts["ukv"], name="mm_up")
    q_cat, k_cat, v, k_t, v_t = _qk_prep(qp, kvp, proj, ta, tb, tc, lay)
    o_mla, u_mla, lse = _attn_fwd(q_cat, k_cat, v_t, proj, lay)
    y_mla = _matmul(u_mla, wts["mla"], name="mm_y")
    merged = _merge_fwd(y_ret, y_mla, proj, lay)
    out, x_next = _out_proj_resid(merged, wts["out"], x, gate)
    saved = dict(x=x, h=h, proj=proj, o_ret=o_ret, u_ret=u_ret, states=states, y_ret=y_ret, cqn=cqn,
                 ckvn=ckvn, q_cat=q_cat, k_cat=k_cat, v=v, k_t=k_t, o_mla=o_mla, u_mla=u_mla, lse=lse,
                 y_mla=y_mla, merged=merged, out=out)
    return x_next, saved


def _to_owner_blocks_cols(g, n_local):
    k = g.shape[0]
    return g.reshape(k, N_DEV, n_local).transpose(1, 0, 2)


def _from_owner_blocks_cols(g):
    return g.transpose(1, 0, 2).reshape(g.shape[1], -1)


def _layer_bwd(dxn, sv, mod, g_norm, g_cq, g_ckv, wts, tabs, lg, lay, deps, shard_cols, push):
    d = dxn.shape[1]
    n_in, n_uq, n_ukv = shard_cols
    cosr, sinr, ta, tb, tc = tabs
    scale, gate = mod[:, d:2 * d], mod[:, 2 * d:]
    gdt = BF16
    dout, dgate = _resid_bwd(dxn, sv["out"], gate, deps)
    dmerged = _matmul(dout, wts["out"], tb=True, name="mm_dy")
    dw_out = _matmul(sv["merged"], dout, ta=True, out_dtype=gdt, name="mm_dw")
    dy_ret, dy_mla, dproj = _merge_bwd(dmerged, sv["y_ret"], sv["y_mla"], sv["proj"], lay)
    du_ret = _matmul(dy_ret, wts["ret"], tb=True, name="mm_dy")
    dw_ret = _matmul(sv["u_ret"], dy_ret, ta=True, out_dtype=gdt, name="mm_dw")
    dproj = _ret_bwd(sv["proj"], lg, cosr, sinr, sv["o_ret"], du_ret, sv["states"], dproj, lay)
    du_mla = _matmul(dy_mla, wts["mla"], tb=True, name="mm_dy")
    dw_mla = _matmul(sv["u_mla"], dy_mla, ta=True, out_dtype=gdt, name="mm_dw")
    do_mla, delta, dproj = _mla_gate_bwd(du_mla, sv["o_mla"], sv["proj"], dproj, lay)
    delta = delta[:, :MLA_HEADS].T.reshape(sv["lse"].shape)
    dqp, dk_cat, dv = _attn_bwd(sv["q_cat"], sv["k_cat"], sv["v"], sv["k_t"], do_mla, sv["lse"], delta,
                                ta, tb, tc)
    dkvp, dproj = _kv_bwd_prep(dk_cat, dv, ta, tb, tc, dproj, lay)
    dcqn = _matmul(dqp, wts["uq"], tb=True, name="mm_dlat")
    dckvn = _matmul(dkvp, wts["ukv"], tb=True, name="mm_dlat")
    dw_uq = _matmul(sv["cqn"], dqp, ta=True, out_dtype=gdt, name="mm_dwup")
    dw_ukv = _matmul(sv["ckvn"], dkvp, ta=True, out_dtype=gdt, name="mm_dwup")
    sent = push("a", [_to_owner_blocks_cols(_uq_to_logical(dw_uq), n_uq),
                      _to_owner_blocks_cols(_ukv_to_logical(dw_ukv), n_ukv),
                      dw_ret.reshape(N_DEV, -1, d), dw_mla.reshape(N_DEV, -1, d), dw_out.reshape(N_DEV, -1, d)])
    dproj, dg_cq, dg_ckv = _mla_prep_bwd(sv["proj"], dcqn, dckvn, g_cq, g_ckv, dproj, lay)
    dw_in = _matmul(sv["h"], dproj, ta=True, out_dtype=gdt, name="mm_dwin", tn_cap=1920, deps=sent)
    sent = push("b", [_scatter_dw_in(dw_in, lay, n_in)])
    dh = _matmul(dproj, wts["in"], tb=True, name="mm_dh", deps=sent)
    dx, dshift, dscale, dg_norm = _norm_mod_bwd(sv["x"], g_norm, scale, dh, dxn)
    dmod = jnp.concatenate([dshift, dscale, dgate], axis=1)
    small = dict(dmod=dmod, g_norm=dg_norm, g_cq=dg_cq, g_ckv=dg_ckv)
    return dx, small


def kernel(x, c, positions, w_mod, b_mod, g_norm, w_in, g_cq, g_ckv, w_uq, w_ukv, w_ret_proj, w_mla_proj, w_out, g_final, loss_target, m_w_mod, m_b_mod, m_g_norm, m_w_in, m_g_cq, m_g_ckv, m_w_uq, m_w_ukv, m_w_ret_proj, m_w_mla_proj, m_w_out, m_g_final, v_w_mod, v_b_mod, v_g_norm, v_w_in, v_g_cq, v_g_ckv, v_w_uq, v_w_ukv, v_w_ret_proj, v_w_mla_proj, v_w_out, v_g_final):
    nl, d, _ = w_mod.shape
    s = x.shape[1]
    rank = g_cq.shape[1]
    lay = Layout(d, rank, g_ckv.shape[1])
    me = _slot(*_place())
    x0 = x.reshape(s, d)
    target = loss_target.reshape(s, d)
    tabs = _rope_tables(positions.reshape(s))
    lg = jnp.log(1.0 - 2.0 ** (-5.0 - jnp.arange(RET_HEADS, dtype=F32)))

    c_act = c * _sigmoid(c)
    (c_all,) = _all_gather([c_act.reshape(d // LANES, LANES)], None, "gather_c", vmem=True)
    c_all = c_all.reshape(N_DEV, d)
    n_mod = w_mod.shape[2]
    mod_part = jnp.stack([_matmul(c_all, w_mod[l], name="mm_mod", tm_cap=8) for l in range(nl)])
    (mod_all,) = _all_gather([mod_part.reshape(-1, LANES)], None, "gather_mod", vmem=True)
    mod_all = mod_all.reshape(N_DEV, nl, N_DEV, n_mod)
    mod = lax.dynamic_index_in_dim(mod_all, me, axis=2, keepdims=False)
    mod = mod.transpose(1, 0, 2).reshape(nl, N_DEV * n_mod) + b_mod

    shards = [[w[l].astype(BF16) for w in (w_in, w_uq, w_ukv, w_ret_proj, w_mla_proj, w_out)] for l in range(nl)]
    xl, saved, wts_all = x0, [], []
    gathered = _all_gather(shards[0], None, "gather_w")
    for l in range(nl):
        g_in, g_uq, g_ukv, g_ret, g_mla, g_out = gathered
        deps = []
        if l + 1 < nl:
            sems, srcs, lands, token = _push_start(shards[l + 1], False, [g_out, mod], "gather_start_%d" % (l + 1))
            deps = [token]
        wts = {
            "in": _assemble_w_in(g_in, lay),
            "uq": _uq_to_physical(_from_owner_blocks_cols(g_uq)),
            "ukv": _ukv_to_physical(_from_owner_blocks_cols(g_ukv)),
            "ret": g_ret.reshape(-1, d), "mla": g_mla.reshape(-1, d), "out": g_out.reshape(-1, d),
        }
        wts_all.append(wts)
        xl, sv = _layer_fwd(xl, mod[l:l + 1], g_norm[l:l + 1], g_cq[l:l + 1], g_ckv[l:l + 1], wts, tabs, lg, lay,
                            deps)
        saved.append(sv)
        if l + 1 < nl:
            gathered = _push_wait(sems, srcs, lands, False, [xl], "gather_wait_%d" % (l + 1))
    loss_lanes, dx, dg_final = _final_loss(xl, g_final.reshape(1, d), target)

    small = [None] * nl
    flying = {l: [] for l in range(nl)}
    recv = {}
    shard_cols = (w_in.shape[2], w_uq.shape[2], w_ukv.shape[2])

    def pusher(l):
        def push(group, arrays):
            sems, srcs, lands, token = _push_start(arrays, True, [], "exchange_start_%d%s" % (l, group))
            flying[l].append((group, sems, srcs, lands))
            return [token]
        return push

    def land(l, after):
        got = {}
        for group, sems, srcs, lands in flying[l]:
            got[group] = _push_wait(sems, srcs, lands, True, after, "exchange_wait_%d%s" % (l, group))
        recv[l] = list(got["b"]) + list(got["a"])

    for l in reversed(range(nl)):
        dx, small[l] = _layer_bwd(dx, saved[l], mod[l:l + 1], g_norm[l:l + 1], g_cq[l:l + 1], g_ckv[l:l + 1],
                                  wts_all[l], tabs, lg, lay, [], shard_cols, pusher(l))
        if l + 1 < nl:
            land(l + 1, [dx])
    grad_x = dx.reshape(x.shape)

    out = {}
    w_in_t, m_w_in_t, v_w_in_t = (jnp.swapaxes(a, 1, 2) for a in (w_in, m_w_in, v_w_in))
    sharded = (("w_in", w_in_t, m_w_in_t, v_w_in_t), ("w_uq", w_uq, m_w_uq, v_w_uq), ("w_ukv", w_ukv, m_w_ukv, v_w_ukv),
               ("w_ret_proj", w_ret_proj, m_w_ret_proj, v_w_ret_proj),
               ("w_mla_proj", w_mla_proj, m_w_mla_proj, v_w_mla_proj), ("w_out", w_out, m_w_out, v_w_out))
    if nl > 1:
        for i, (key, w, m, v) in enumerate(sharded):
            out[key] = _adam_sharded([recv[l][i] for l in range(1, nl)], 1, w, m, v, None, "adam_" + key)
    done = [out[key][0] for key, _, _, _ in sharded if key in out]

    pack = jnp.concatenate(
        [jnp.concatenate([sm[k] for sm in small], axis=0).reshape(-1)
         for k in ("dmod", "g_norm", "g_cq", "g_ckv")] + [dg_final.reshape(-1), loss_lanes.reshape(-1)])
    (pack_all,) = _all_gather([pack.reshape(-1, LANES)], None, "gather_small", vmem=True, deps=done)
    tot = _sum_devices(pack_all).reshape(-1)
    sizes = [nl * 3 * d, nl * d, nl * rank, nl * rank, d]
    offs = np.cumsum([0] + sizes)
    grad_b_mod = tot[offs[0]:offs[1]].reshape(nl, 3 * d)
    grad_g_norm = tot[offs[1]:offs[2]].reshape(nl, d)
    grad_g_cq = tot[offs[2]:offs[3]].reshape(nl, rank)
    grad_g_ckv = tot[offs[3]:offs[4]].reshape(nl, rank)
    grad_g_final = tot[offs[4]:offs[5]]
    loss = tot[offs[5]]
    dmod_all = pack_all.reshape(N_DEV, -1)[:, :sizes[0]].reshape(N_DEV, nl, 3 * d)
    dmod_mine = lax.dynamic_slice_in_dim(dmod_all, me * n_mod, n_mod, axis=2).transpose(1, 0, 2)

    out["w_mod"] = _adam_mod(c_all.T, dmod_mine, w_mod, m_w_mod, v_w_mod)
    land(0, [out["w_mod"][0]])
    for i, (key, w, m, v) in enumerate(sharded):
        out[key] = _adam_sharded([recv[0][i]], 0, w, m, v, out.get(key), "adam0_" + key)
    out["w_in"] = tuple(jnp.swapaxes(a, 1, 2) for a in out["w_in"])
    for key, g, w, m, v in (("b_mod", grad_b_mod, b_mod, m_b_mod, v_b_mod),
                            ("g_norm", grad_g_norm, g_norm, m_g_norm, v_g_norm),
                            ("g_cq", grad_g_cq, g_cq, m_g_cq, v_g_cq),
                            ("g_ckv", grad_g_ckv, g_ckv, m_g_ckv, v_g_ckv),
                            ("g_final", grad_g_final.reshape(1, d), g_final.reshape(1, d),
                             m_g_final.reshape(1, d), v_g_final.reshape(1, d))):
        out[key] = (g,) + tuple(_adam_small(g, w, m, v, "adam_" + key))
    out["g_final"] = tuple(a.reshape(d) for a in out["g_final"])

    names = ("w_mod", "b_mod", "g_norm", "w_in", "g_cq", "g_ckv", "w_uq", "w_ukv", "w_ret_proj",
             "w_mla_proj", "w_out", "g_final")
    return (loss, grad_x, *[out[k][0] for k in names], *[out[k][1] for k in names],
            *[out[k][2] for k in names], *[out[k][3] for k in names])
```

```python
import functools
import itertools

import jax
import jax.numpy as jnp
import numpy as np
from jax import lax
from jax.experimental import pallas as pl
from jax.experimental.pallas import tpu as pltpu

F32 = jnp.float32
BF16 = jnp.bfloat16

N_DEV = 8
CHUNK = 64
EPS = 1e-6
NEG_INF = -1e30
ROPE_BASE = 10000.0
LANES = 128

RET_HEADS = 8
RET_DK = 128
RET_DV = 256
MLA_HEADS = 16
MLA_NOPE = 128
MLA_ROPE = 64
MLA_DV = 128
MLA_QW = 256
QK_SCALE = (MLA_NOPE + MLA_ROPE) ** -0.5
QK_LOG2_SCALE = QK_SCALE * 1.4426950408889634
LN2 = 0.6931471805599453

ADAM_LR = 0.001
ADAM_B1 = 0.9
ADAM_B2 = 0.999
ADAM_EPS = 1e-08
ADAM_WD = 0.01
ADAM_STEP = 10

VMEM_LIMIT_BYTES = 56 * 1024 * 1024
MESH = pl.DeviceIdType.MESH
ANY = pl.BlockSpec(memory_space=pl.ANY)


def _cp(*sem):
    return pltpu.CompilerParams(dimension_semantics=sem if sem else None,
                                vmem_limit_bytes=VMEM_LIMIT_BYTES)


def _tile(n, cap):
    best = None
    t = LANES
    while t <= min(n, cap):
        if n % t == 0:
            best = t
        t += LANES
    return best if best is not None else n


def _rtile(n, cap):
    t = cap
    while t > 8 and n % t:
        t //= 2
    return t if n % t == 0 else n


def _sigmoid(x):
    return 1.0 / (1.0 + jnp.exp(-x))


def _dot(a, b):
    return lax.dot_general(a, b, (((1,), (0,)), ((), ())), preferred_element_type=F32)


def _dot_nt(a, b):
    return lax.dot_general(a, b, (((1,), (1,)), ((), ())), preferred_element_type=F32)


def _dot_tn(a, b):
    return lax.dot_general(a, b, (((0,), (0,)), ((), ())), preferred_element_type=F32)


def _roll(x, s):
    return pltpu.roll(x, s, 1)


class Layout:
    def __init__(self, d_model, q_rank, kv_rank):
        assert q_rank == kv_rank
        self.d = d_model
        self.rank = q_rank
        self.ret_w = 2 * RET_DK + 2 * RET_DV
        self.ret_qk = RET_HEADS * RET_DK
        self.ret_v = RET_HEADS * RET_DV
        self.mla_v = MLA_HEADS * MLA_DV
        widths = {"bg": 2 * d_model, "mg": self.mla_v, "ret": RET_HEADS * self.ret_w,
                  "cqkv": 2 * q_rank, "kr": LANES}
        blocks = {"bg": 2 * d_model, "mg": self.mla_v, "ret": self.ret_w,
                  "cqkv": 2 * q_rank, "kr": LANES}
        for order in itertools.permutations(widths):
            off, offs, ok = 0, {}, True
            for name in order:
                if off % blocks[name]:
                    ok = False
                    break
                offs[name] = off
                off += widths[name]
            if ok:
                break
        assert ok, "no aligned layout"
        self.order, self.off, self.width, self.total = order, offs, widths, off
        lo, o = {}, 0
        for name, w in (("rq", self.ret_qk), ("rk", self.ret_qk), ("rv", self.ret_v),
                        ("rg", self.ret_v), ("cq", q_rank), ("ckv", kv_rank), ("kr", MLA_ROPE),
                        ("mg", self.mla_v), ("bg", 2 * d_model)):
            lo[name] = (o, w)
            o += w
        self.logical, self.d_in = lo, o

    def pieces(self):
        lo = self.logical
        out = []
        for name in self.order:
            if name == "bg":
                out.append(lo["bg"])
            elif name == "mg":
                out.append(lo["mg"])
            elif name == "ret":
                for h in range(RET_HEADS):
                    out.append((lo["rq"][0] + h * RET_DK, RET_DK))
                    out.append((lo["rk"][0] + h * RET_DK, RET_DK))
                    out.append((lo["rv"][0] + h * RET_DV, RET_DV))
                    out.append((lo["rg"][0] + h * RET_DV, RET_DV))
            elif name == "cqkv":
                out.append((lo["cq"][0], 2 * self.rank))
            elif name == "kr":
                out.append(lo["kr"])
                out.append((None, LANES - MLA_ROPE))
        return out


RELAYOUT_CHUNK = 512


def _relayout_plan(lay, n_local):
    plan, off = [], 0
    for start, width in lay.pieces():
        done = 0
        while done < width:
            w = min(RELAYOUT_CHUNK, width - done)
            srcs = []
            if start is not None:
                lo, hi = start + done, start + done + w
                while lo < hi:
                    j = lo // n_local
                    end = min(hi, (j + 1) * n_local)
                    srcs.append((j, lo - j * n_local, end - j * n_local))
                    lo = end
            plan.append((off + done, w, srcs))
            done += w
        off += width
    merged = []
    for p, w, srcs in plan:
        if merged and merged[-1][0] % LANES == 0 and (merged[-1][1] % LANES) and p == merged[-1][0] + merged[-1][1]:
            q, qw, qs = merged.pop()
            merged.append((q, qw + w, qs + ([("pad", w)] if not srcs else srcs)))
        else:
            merged.append((p, w, srcs))
    return merged


def _assemble_w_in(g, lay):
    _, d, n_local = g.shape
    tr = _rtile(d, 256)
    plan = _relayout_plan(lay, n_local)

    def body(g_ref, o_ref):
        for p, w, srcs in plan:
            parts = []
            for src in srcs:
                if src[0] == "pad":
                    parts.append(jnp.zeros((tr, src[1]), F32))
                else:
                    j, a, b = src
                    parts.append(g_ref[j, :, a:b].astype(F32))
            if not parts:
                parts = [jnp.zeros((tr, w), F32)]
            val = parts[0] if len(parts) == 1 else jnp.concatenate(parts, axis=1)
            o_ref[:, p:p + w] = val.astype(o_ref.dtype)

    return pl.pallas_call(
        body, name="assemble_w_in", grid=(d // tr,),
        in_specs=[pl.BlockSpec((N_DEV, tr, n_local), lambda i: (0, i, 0))],
        out_specs=pl.BlockSpec((tr, lay.total), lambda i: (i, 0)),
        out_shape=jax.ShapeDtypeStruct((d, lay.total), g.dtype),
        compiler_params=_cp("parallel"),
    )(g)


def _scatter_dw_in(dw, lay, n_local):
    d = dw.shape[0]
    tr = _rtile(d, 256)
    n_pad = -(-n_local // LANES) * LANES
    plan = _relayout_plan(lay, n_local)
    runs = [[] for _ in range(N_DEV)]
    for p, w, srcs in plan:
        at = p
        for src in srcs:
            if src[0] == "pad":
                at += src[1]
                continue
            j, a, b = src
            runs[j].append((a, b, at))
            at += b - a
    for r in runs:
        r.sort()

    def body(dw_ref, o_ref):
        for j in range(N_DEV):
            parts = [dw_ref[:, at:at + (b - a)].astype(F32) for a, b, at in runs[j]]
            if n_pad > n_local:
                parts.append(jnp.zeros((tr, n_pad - n_local), F32))
            val = jnp.concatenate(parts, axis=1).T
            o_ref[j] = val[:n_local, :].astype(BF16)

    return pl.pallas_call(
        body, name="scatter_dw_in", grid=(d // tr,),
        in_specs=[pl.BlockSpec((tr, lay.total), lambda i: (i, 0))],
        out_specs=pl.BlockSpec((N_DEV, n_local, tr), lambda i: (0, 0, i)),
        out_shape=jax.ShapeDtypeStruct((N_DEV, n_local, d), BF16),
        compiler_params=_cp("parallel"),
    )(dw)


def _uq_to_physical(w):
    k = w.shape[0]
    w3 = w.reshape(k, MLA_HEADS, MLA_NOPE + MLA_ROPE)
    pad = jnp.zeros((k, MLA_HEADS, MLA_QW - MLA_NOPE - MLA_ROPE), w.dtype)
    return jnp.concatenate([w3, pad], axis=2).reshape(k, MLA_HEADS * MLA_QW)


def _uq_to_logical(w):
    k = w.shape[0]
    return w.reshape(k, MLA_HEADS, MLA_QW)[:, :, :MLA_NOPE + MLA_ROPE].reshape(k, -1)


def _ukv_to_physical(w):
    k = w.shape[0]
    w3 = w.reshape(k, MLA_HEADS, MLA_NOPE + MLA_DV)
    return jnp.concatenate([w3[:, :, :MLA_NOPE].reshape(k, -1), w3[:, :, MLA_NOPE:].reshape(k, -1)], axis=1)


def _ukv_to_logical(w):
    k = w.shape[0]
    kn = w[:, :MLA_HEADS * MLA_NOPE].reshape(k, MLA_HEADS, MLA_NOPE)
    v = w[:, MLA_HEADS * MLA_NOPE:].reshape(k, MLA_HEADS, MLA_DV)
    return jnp.concatenate([kn, v], axis=2).reshape(k, -1)


def _matmul(a, b, *, ta=False, tb=False, out_dtype=F32, name, tm_cap=1024, tn_cap=2048, tk_cap=2048, deps=()):
    m, k = (a.shape[1], a.shape[0]) if ta else a.shape
    n = b.shape[0] if tb else b.shape[1]
    assert k == (b.shape[1] if tb else b.shape[0])
    tm, tn, tk = _tile(m, tm_cap), _tile(n, tn_cap), _tile(k, tk_cap)
    nk = k // tk

    def body(a_ref, b_ref, *rest):
        dims = (((0 if ta else 1,), (1 if tb else 0,)), ((), ()))
        part = lax.dot_general(a_ref[...].astype(BF16), b_ref[...].astype(BF16), dims, preferred_element_type=F32)
        if nk == 1:
            rest[len(deps)][...] = part.astype(out_dtype)
            return
        o_ref, acc_ref = rest[len(deps):]
        kk = pl.program_id(2)

        @pl.when(kk == 0)
        def _():
            acc_ref[...] = part

        @pl.when(jnp.logical_and(kk > 0, kk < nk - 1))
        def _():
            acc_ref[...] += part

        @pl.when(kk == nk - 1)
        def _():
            o_ref[...] = (acc_ref[...] + part).astype(o_ref.dtype)

    a_spec = pl.BlockSpec((tk, tm), lambda i, j, kk: (kk, i)) if ta else pl.BlockSpec((tm, tk), lambda i, j, kk: (i, kk))
    b_spec = pl.BlockSpec((tn, tk), lambda i, j, kk: (j, kk)) if tb else pl.BlockSpec((tk, tn), lambda i, j, kk: (kk, j))
    return pl.pallas_call(
        body, name=name, grid=(m // tm, n // tn, nk),
        in_specs=[a_spec, b_spec] + [ANY] * len(deps),
        out_specs=pl.BlockSpec((tm, tn), lambda i, j, kk: (i, j)),
        out_shape=jax.ShapeDtypeStruct((m, n), out_dtype),
        scratch_shapes=[pltpu.VMEM((tm, tn), F32)] if nk > 1 else [],
        compiler_params=_cp("parallel", "parallel", "arbitrary"),
    )(a, b, *deps)


def _out_proj_resid(merged, w_out, x, gate):
    m, k = merged.shape
    n = w_out.shape[1]
    tm, tn = _tile(m, 1024), _tile(n, 1024)

    def body(a_ref, b_ref, x_ref, g_ref, o_ref, xn_ref):
        out = _dot(a_ref[...], b_ref[...])
        o_ref[...] = out
        xn_ref[...] = x_ref[...] + g_ref[...] * out

    tile = pl.BlockSpec((tm, tn), lambda i, j: (i, j))
    shape = jax.ShapeDtypeStruct((m, n), F32)
    return pl.pallas_call(
        body, name="mm_out_resid", grid=(m // tm, n // tn),
        in_specs=[pl.BlockSpec((tm, k), lambda i, j: (i, 0)), pl.BlockSpec((k, tn), lambda i, j: (0, j)),
                  tile, pl.BlockSpec((1, tn), lambda i, j: (0, j))],
        out_specs=[tile, tile], out_shape=[shape, shape],
        compiler_params=_cp("parallel", "parallel"),
    )(merged, w_out, x, gate)


def _row(tm, w, cb=0):
    return pl.BlockSpec((tm, w), lambda i, cb=cb: (i, cb))


def _vec(w, cb=0):
    return pl.BlockSpec((1, w), lambda i, cb=cb: (0, cb))


def _dproj_out(dproj, s, lay, tm, name):
    w = lay.width[name]
    cb = lay.off[name] // w
    spec = _row(tm, w, cb)
    shape = jax.ShapeDtypeStruct((s, lay.total), BF16)
    return spec, shape


def _norm_mod_fwd(x, g, scale, shift, deps):
    s, d = x.shape
    tm = _rtile(s, 256)

    def body(x_ref, g_ref, sc_ref, sh_ref, *rest):
        h_ref = rest[len(deps)]
        xv = x_ref[...]
        xh = xv * lax.rsqrt(jnp.mean(xv * xv, axis=-1, keepdims=True) + EPS)
        h_ref[...] = ((xh * g_ref[...]) * (1.0 + sc_ref[...]) + sh_ref[...]).astype(BF16)

    return pl.pallas_call(
        body, name="norm_mod_fwd", grid=(s // tm,),
        in_specs=[_row(tm, d), _vec(d), _vec(d), _vec(d)] + [ANY] * len(deps),
        out_specs=_row(tm, d), out_shape=jax.ShapeDtypeStruct((s, d), BF16),
        compiler_params=_cp("parallel"),
    )(x, g, scale, shift, *deps)


def _norm_mod_bwd(x, g, scale, dh, dres):
    s, d = x.shape
    tm = _rtile(s, 256)

    def body(x_ref, g_ref, sc_ref, dh_ref, dres_ref, dx_ref, dsh_ref, dsc_ref, dg_ref):
        @pl.when(pl.program_id(0) == 0)
        def _():
            dsh_ref[...] = jnp.zeros_like(dsh_ref)
            dsc_ref[...] = jnp.zeros_like(dsc_ref)
            dg_ref[...] = jnp.zeros_like(dg_ref)

        xv, gv, dhv = x_ref[...], g_ref[...], dh_ref[...]
        rstd = lax.rsqrt(jnp.mean(xv * xv, axis=-1, keepdims=True) + EPS)
        xh = xv * rstd
        dy = dhv * (1.0 + sc_ref[...])
        dxh = dy * gv
        dx_ref[...] = dres_ref[...] + rstd * (dxh - xh * jnp.mean(dxh * xh, axis=-1, keepdims=True))
        dsh_ref[...] += jnp.sum(dhv, axis=0, keepdims=True)
        dsc_ref[...] += jnp.sum(dhv * (xh * gv), axis=0, keepdims=True)
        dg_ref[...] += jnp.sum(dy * xh, axis=0, keepdims=True)

    vec = jax.ShapeDtypeStruct((1, d), F32)
    return pl.pallas_call(
        body, name="norm_mod_bwd", grid=(s // tm,),
        in_specs=[_row(tm, d), _vec(d), _vec(d), _row(tm, d), _row(tm, d)],
        out_specs=[_row(tm, d), _vec(d), _vec(d), _vec(d)],
        out_shape=[jax.ShapeDtypeStruct((s, d), F32), vec, vec, vec],
        compiler_params=_cp("arbitrary"),
    )(x, g, scale, dh, dres)


def _final_loss(x, g, target):
    s, d = x.shape
    tm = _rtile(s, 256)

    def body(x_ref, g_ref, t_ref, l_ref, dx_ref, dg_ref):
        @pl.when(pl.program_id(0) == 0)
        def _():
            l_ref[...] = jnp.zeros_like(l_ref)
            dg_ref[...] = jnp.zeros_like(dg_ref)

        xv, gv = x_ref[...], g_ref[...]
        rstd = lax.rsqrt(jnp.mean(xv * xv, axis=-1, keepdims=True) + EPS)
        xh = xv * rstd
        err = xh * gv - t_ref[...]
        row = jnp.mean(err * err, axis=-1, keepdims=True)
        l_ref[...] += 0.5 * jnp.sum(row, axis=0, keepdims=True)
        dy = err / d
        dxh = dy * gv
        dx_ref[...] = rstd * (dxh - xh * jnp.mean(dxh * xh, axis=-1, keepdims=True))
        dg_ref[...] += jnp.sum(dy * xh, axis=0, keepdims=True)

    return pl.pallas_call(
        body, name="final_loss", grid=(s // tm,),
        in_specs=[_row(tm, d), _vec(d), _row(tm, d)],
        out_specs=[_vec(LANES), _row(tm, d), _vec(d)],
        out_shape=[jax.ShapeDtypeStruct((1, LANES), F32), jax.ShapeDtypeStruct((s, d), F32),
                   jax.ShapeDtypeStruct((1, d), F32)],
        compiler_params=_cp("arbitrary"),
    )(x, g, target)


def _resid_bwd(dxn, out, gate, deps):
    s, d = dxn.shape
    tm = _rtile(s, 256)

    def body(dx_ref, o_ref, g_ref, *rest):
        do_ref, dg_ref = rest[len(deps):]

        @pl.when(pl.program_id(0) == 0)
        def _():
            dg_ref[...] = jnp.zeros_like(dg_ref)

        dxv = dx_ref[...]
        do_ref[...] = (dxv * g_ref[...]).astype(BF16)
        dg_ref[...] += jnp.sum(dxv * o_ref[...], axis=0, keepdims=True)

    return pl.pallas_call(
        body, name="resid_bwd", grid=(s // tm,),
        in_specs=[_row(tm, d), _row(tm, d), _vec(d)] + [ANY] * len(deps),
        out_specs=[_row(tm, d), _vec(d)],
        out_shape=[jax.ShapeDtypeStruct((s, d), BF16), jax.ShapeDtypeStruct((1, d), F32)],
        compiler_params=_cp("arbitrary"),
    )(dxn, out, gate, *deps)


def _merge_fwd(y_ret, y_mla, proj, lay):
    s, d = y_ret.shape
    tm = _rtile(s, 256)
    cb = lay.off["bg"] // (2 * d)

    def body(a_ref, b_ref, bg_ref, m_ref):
        sg = _sigmoid(bg_ref[...])
        m_ref[...] = (sg[:, :d] * a_ref[...] + sg[:, d:] * b_ref[...]).astype(BF16)

    return pl.pallas_call(
        body, name="merge_fwd", grid=(s // tm,),
        in_specs=[_row(tm, d), _row(tm, d), _row(tm, 2 * d, cb)],
        out_specs=_row(tm, d), out_shape=jax.ShapeDtypeStruct((s, d), BF16),
        compiler_params=_cp("parallel"),
    )(y_ret, y_mla, proj)


def _merge_bwd(dm, y_ret, y_mla, proj, lay):
    s, d = dm.shape
    tm = _rtile(s, 256)
    cb = lay.off["bg"] // (2 * d)
    dp_spec, dp_shape = _dproj_out(None, s, lay, tm, "bg")

    def body(dm_ref, a_ref, b_ref, bg_ref, da_ref, db_ref, dp_ref):
        sg = _sigmoid(bg_ref[...])
        dmv = dm_ref[...]
        ga, gb = sg[:, :d], sg[:, d:]
        da_ref[...] = (dmv * ga).astype(BF16)
        db_ref[...] = (dmv * gb).astype(BF16)
        dp_ref[:, :d] = (dmv * a_ref[...] * ga * (1.0 - ga)).astype(BF16)
        dp_ref[:, d:] = (dmv * b_ref[...] * gb * (1.0 - gb)).astype(BF16)

    act = jax.ShapeDtypeStruct((s, d), BF16)
    return pl.pallas_call(
        body, name="merge_bwd", grid=(s // tm,),
        in_specs=[_row(tm, d), _row(tm, d), _row(tm, d), _row(tm, 2 * d, cb)],
        out_specs=[_row(tm, d), _row(tm, d), dp_spec],
        out_shape=[act, act, dp_shape],
        compiler_params=_cp("parallel"),
    )(dm, y_ret, y_mla, proj)


def _mla_prep(proj, g_cq, g_ckv, lay):
    s = proj.shape[0]
    r = lay.rank
    tm = _rtile(s, 512)
    cb = lay.off["cqkv"] // (2 * r)

    def body(p_ref, gq_ref, gk_ref, q_ref, k_ref):
        pv = p_ref[...]
        for lo, g_ref, o_ref in ((0, gq_ref, q_ref), (r, gk_ref, k_ref)):
            xv = pv[:, lo:lo + r]
            xh = xv * lax.rsqrt(jnp.mean(xv * xv, axis=-1, keepdims=True) + EPS)
            o_ref[...] = (xh * g_ref[...]).astype(BF16)

    act = jax.ShapeDtypeStruct((s, r), BF16)
    return pl.pallas_call(
        body, name="mla_prep", grid=(s // tm,),
        in_specs=[_row(tm, 2 * r, cb), _vec(r), _vec(r)],
        out_specs=[_row(tm, r), _row(tm, r)], out_shape=[act, act],
        compiler_params=_cp("parallel"),
    )(proj, g_cq, g_ckv)


def _mla_prep_bwd(proj, dqn, dkn, g_cq, g_ckv, dproj, lay):
    s = proj.shape[0]
    r = lay.rank
    tm = _rtile(s, 512)
    cb = lay.off["cqkv"] // (2 * r)
    dp_spec, dp_shape = _dproj_out(dproj, s, lay, tm, "cqkv")

    def body(p_ref, dq_ref, dk_ref, gq_ref, gk_ref, _, dp_ref, dgq_ref, dgk_ref):
        @pl.when(pl.program_id(0) == 0)
        def _():
            dgq_ref[...] = jnp.zeros_like(dgq_ref)
            dgk_ref[...] = jnp.zeros_like(dgk_ref)

        pv = p_ref[...]
        for lo, g_ref, d_ref, dg_ref in ((0, gq_ref, dq_ref, dgq_ref), (r, gk_ref, dk_ref, dgk_ref)):
            xv = pv[:, lo:lo + r]
            rstd = lax.rsqrt(jnp.mean(xv * xv, axis=-1, keepdims=True) + EPS)
            xh = xv * rstd
            dy = d_ref[...]
            dxh = dy * g_ref[...]
            dp_ref[:, lo:lo + r] = (rstd * (dxh - xh * jnp.mean(dxh * xh, axis=-1, keepdims=True))).astype(BF16)
            dg_ref[...] += jnp.sum(dy * xh, axis=0, keepdims=True)

    vec = jax.ShapeDtypeStruct((1, r), F32)
    return pl.pallas_call(
        body, name="mla_prep_bwd", grid=(s // tm,),
        in_specs=[_row(tm, 2 * r, cb), _row(tm, r), _row(tm, r), _vec(r), _vec(r), ANY],
        out_specs=[dp_spec, _vec(r), _vec(r)], out_shape=[dp_shape, vec, vec],
        input_output_aliases={5: 0},
        compiler_params=_cp("arbitrary"),
    )(proj, dqn, dkn, g_cq, g_ckv, dproj)


def _rope_tile(t, a, b, c):
    return t * a + _roll(t, 96) * b + _roll(t, 32) * c


def _rope_tile_bwd(dy, a, b, c):
    return dy * a + _roll(dy * b, 32) + _roll(dy * c, 96)


def _attn_block(s):
    return _rtile(s, 512)


def _qk_prep(qp, kvp, proj, ta, tb, tc, lay):
    s = qp.shape[0]
    hq = MLA_HEADS * MLA_QW
    hv = MLA_HEADS * MLA_DV
    blk = _attn_block(s)
    tm = _rtile(blk, 256)
    per = blk // tm
    kr_cb = lay.off["kr"] // LANES

    def body(q_ref, kv_ref, kr_ref, a_ref, b_ref, c_ref, qc_ref, kc_ref, v_ref, kt_ref, vt_ref):
        a, b, c = a_ref[...], b_ref[...], c_ref[...]
        krot = _rope_tile(kr_ref[...], a, b, c)
        krot_b, krot_t = krot.astype(BF16), krot.T.astype(BF16)
        for h in range(MLA_HEADS):
            q0 = h * MLA_QW
            qc_ref[:, q0:q0 + MLA_NOPE] = (q_ref[:, q0:q0 + MLA_NOPE] * QK_LOG2_SCALE).astype(BF16)
            qc_ref[:, q0 + MLA_NOPE:q0 + MLA_QW] = (
                _rope_tile(q_ref[:, q0 + MLA_NOPE:q0 + MLA_QW], a, b, c) * QK_LOG2_SCALE).astype(BF16)
            kn = kv_ref[:, h * MLA_NOPE:(h + 1) * MLA_NOPE]
            kc_ref[:, q0:q0 + MLA_NOPE] = kn.astype(BF16)
            kc_ref[:, q0 + MLA_NOPE:q0 + MLA_QW] = krot_b
            kt_ref[h, :MLA_NOPE, :] = kn.T.astype(BF16)
            kt_ref[h, MLA_NOPE:, :] = krot_t
            vh = kv_ref[:, (MLA_HEADS + h) * MLA_NOPE:(MLA_HEADS + h + 1) * MLA_NOPE]
            v_ref[:, h * MLA_DV:(h + 1) * MLA_DV] = vh.astype(BF16)
            vt_ref[h] = vh.T.astype(BF16)

    return pl.pallas_call(
        body, name="qk_prep", grid=(s // tm,),
        in_specs=[_row(tm, hq), _row(tm, hq), _row(tm, LANES, kr_cb), _row(tm, LANES), _row(tm, LANES), _row(tm, LANES)],
        out_specs=[_row(tm, hq), _row(tm, hq), _row(tm, hv),
                   pl.BlockSpec((MLA_HEADS, None, MLA_QW, tm), lambda i: (0, i // per, 0, i % per)),
                   pl.BlockSpec((MLA_HEADS, None, MLA_DV, tm), lambda i: (0, i // per, 0, i % per))],
        out_shape=[jax.ShapeDtypeStruct((s, hq), BF16), jax.ShapeDtypeStruct((s, hq), BF16),
                   jax.ShapeDtypeStruct((s, hv), BF16),
                   jax.ShapeDtypeStruct((MLA_HEADS, s // blk, MLA_QW, blk), BF16),
                   jax.ShapeDtypeStruct((MLA_HEADS, s // blk, MLA_DV, blk), BF16)],
        compiler_params=_cp("parallel"),
    )(qp, kvp, proj, ta, tb, tc)


def _kv_bwd_prep(dk_cat, dv, ta, tb, tc, dproj, lay):
    s = dk_cat.shape[0]
    hq = MLA_HEADS * MLA_QW
    hv = MLA_HEADS * MLA_DV
    tm = _rtile(s, 256)
    dp_spec, dp_shape = _dproj_out(dproj, s, lay, tm, "kr")

    def body(dk_ref, dv_ref, a_ref, b_ref, c_ref, _, dkv_ref, dp_ref):
        acc = jnp.zeros((tm, LANES), F32)
        for h in range(MLA_HEADS):
            q0 = h * MLA_QW
            dkv_ref[:, h * MLA_NOPE:(h + 1) * MLA_NOPE] = dk_ref[:, q0:q0 + MLA_NOPE].astype(BF16)
            acc = acc + dk_ref[:, q0 + MLA_NOPE:q0 + MLA_QW]
        dkv_ref[:, MLA_HEADS * MLA_NOPE:] = dv_ref[...].astype(BF16)
        dp_ref[...] = _rope_tile_bwd(acc, a_ref[...], b_ref[...], c_ref[...]).astype(BF16)

    return pl.pallas_call(
        body, name="kv_bwd_prep", grid=(s // tm,),
        in_specs=[_row(tm, hq), _row(tm, hv), _row(tm, LANES), _row(tm, LANES), _row(tm, LANES), ANY],
        out_specs=[_row(tm, hq), dp_spec],
        out_shape=[jax.ShapeDtypeStruct((s, hq), BF16), dp_shape],
        input_output_aliases={5: 1},
        compiler_params=_cp("parallel"),
    )(dk_cat, dv, ta, tb, tc, dproj)


def _mla_gate_bwd(du, o, proj, dproj, lay):
    s, vw = du.shape
    tm = _rtile(s, 256)
    cb = lay.off["mg"] // vw
    dp_spec, dp_shape = _dproj_out(dproj, s, lay, tm, "mg")
    assert MLA_HEADS <= LANES

    def body(du_ref, o_ref, g_ref, _, do_ref, dl_ref, dp_ref):
        gv, duv, ov = g_ref[...], du_ref[...], o_ref[...]
        sg = _sigmoid(gv)
        do = (duv * (gv * sg)).astype(BF16)
        do_ref[...] = do
        dp_ref[...] = (duv * ov * (sg + gv * sg * (1.0 - sg))).astype(BF16)
        prod = do.astype(F32) * ov
        lane = lax.broadcasted_iota(jnp.int32, (tm, LANES), 1)
        delta = jnp.zeros((tm, LANES), F32)
        for h in range(MLA_HEADS):
            dh = jnp.sum(prod[:, h * MLA_DV:(h + 1) * MLA_DV], axis=-1, keepdims=True)
            delta = jnp.where(lane == h, dh, delta)
        dl_ref[...] = delta

    return pl.pallas_call(
        body, name="mla_gate_bwd", grid=(s // tm,),
        in_specs=[_row(tm, vw), _row(tm, vw), _row(tm, vw, cb), ANY],
        out_specs=[_row(tm, vw), _row(tm, LANES), dp_spec],
        out_shape=[jax.ShapeDtypeStruct((s, vw), BF16), jax.ShapeDtypeStruct((s, LANES), F32), dp_shape],
        input_output_aliases={3: 2},
        compiler_params=_cp("parallel"),
    )(du, o, proj, dproj)


RET_BLOCK = 256


def _ret_tables(lg, blk):
    ri = lax.broadcasted_iota(jnp.int32, (blk, blk), 0)
    ci = lax.broadcasted_iota(jnp.int32, (blk, blk), 1)
    col = lax.broadcasted_iota(jnp.int32, (blk, 1), 0).astype(F32)
    dist = jnp.abs(ri - ci).astype(F32)
    dmat = jnp.where(ci // CHUNK <= ri // CHUNK, jnp.exp(dist * lg), 0.0)
    xi = jnp.exp((col + 1.0) * lg)
    zeta = jnp.exp((blk - 1.0 - col) * lg)
    decay = jnp.exp(jnp.full((1, 1), blk, F32) * lg)
    return dmat, xi, zeta, decay


def _ret_qkvg(blk, cs, sn):
    dv = RET_DV
    q = blk[:, :RET_DK]
    k = blk[:, RET_DK:2 * RET_DK]
    q = q * cs + _roll(q, RET_DK // 2) * sn
    k = (k * cs + _roll(k, RET_DK // 2) * sn) * (RET_DK ** -0.5)
    return q, k, blk[:, 2 * RET_DK:2 * RET_DK + dv], blk[:, 2 * RET_DK + dv:]


def _group_norm(o):
    mu = jnp.mean(o, axis=-1, keepdims=True)
    oc = o - mu
    rstd = lax.rsqrt(jnp.mean(oc * oc, axis=-1, keepdims=True) + EPS)
    return oc * rstd, rstd


def _ret_fwd(proj, lg, cosr, sinr, lay):
    s = proj.shape[0]
    dv, w = RET_DV, lay.ret_w
    tb = _rtile(s, 512)
    blk = min(RET_BLOCK, tb)
    nb, nch = s // tb, tb // blk
    cb0 = lay.off["ret"] // w
    hp = 2
    assert RET_HEADS % hp == 0 and cb0 % hp == 0

    def body(lg_ref, p_ref, cos_ref, sin_ref, o_ref, u_ref, st_ref, state):
        @pl.when(pl.program_id(1) == 0)
        def _():
            state[...] = jnp.zeros_like(state)

        tables = [_ret_tables(lg_ref[hp * pl.program_id(0) + a], blk) for a in range(hp)]
        for c in range(nch):
            rows = slice(c * blk, (c + 1) * blk)
            for a in range(hp):
                dmat, xi, zeta, decay = tables[a]
                q, k, v, g = _ret_qkvg(p_ref[rows, a * w:(a + 1) * w], cos_ref[rows, :], sin_ref[rows, :])
                qb, kb, vb = q.astype(BF16), k.astype(BF16), v.astype(BF16)
                sc = _dot_nt(qb, kb) * dmat
                st = state[a]
                o = _dot(sc.astype(BF16), vb) + _dot((q * xi).astype(BF16), st.astype(BF16))
                st_ref[a, c] = st.astype(BF16)
                state[a] = st * decay + _dot_tn((k * zeta).astype(BF16), vb)
                o_ref[rows, a * dv:(a + 1) * dv] = o
                n, _ = _group_norm(o)
                u_ref[rows, a * dv:(a + 1) * dv] = (n * (g * _sigmoid(g))).astype(BF16)

    return pl.pallas_call(
        body, name="ret_fwd", grid=(RET_HEADS // hp, nb),
        in_specs=[pl.BlockSpec(memory_space=pltpu.SMEM),
                  pl.BlockSpec((tb, hp * w), lambda h, b: (b, cb0 // hp + h)),
                  pl.BlockSpec((tb, RET_DK), lambda h, b: (b, 0)),
                  pl.BlockSpec((tb, RET_DK), lambda h, b: (b, 0))],
        out_specs=[pl.BlockSpec((tb, hp * dv), lambda h, b: (b, h)),
                   pl.BlockSpec((tb, hp * dv), lambda h, b: (b, h)),
                   pl.BlockSpec((hp, nch, RET_DK, dv), lambda h, b: (h, b, 0, 0))],
        out_shape=[jax.ShapeDtypeStruct((s, RET_HEADS * dv), F32),
                   jax.ShapeDtypeStruct((s, RET_HEADS * dv), BF16),
                   jax.ShapeDtypeStruct((RET_HEADS, s // blk, RET_DK, dv), BF16)],
        scratch_shapes=[pltpu.VMEM((hp, RET_DK, dv), F32)],
        compiler_params=_cp("parallel", "arbitrary"),
    )(lg, proj, cosr, sinr)


def _ret_bwd(proj, lg, cosr, sinr, o, du, states, dproj, lay):
    s = proj.shape[0]
    dv, w = RET_DV, lay.ret_w
    tb = _rtile(s, 512)
    blk = min(RET_BLOCK, tb)
    nb, nch = s // tb, tb // blk
    cb0 = lay.off["ret"] // w

    def body(lg_ref, p_ref, cos_ref, sin_ref, o_ref, du_ref, st_ref, _, dp_ref, dstate):
        @pl.when(pl.program_id(1) == 0)
        def _():
            dstate[...] = jnp.zeros_like(dstate)

        dmat, xi, zeta, decay = _ret_tables(lg_ref[pl.program_id(0)], blk)
        for c in reversed(range(nch)):
            rows = slice(c * blk, (c + 1) * blk)
            cs, sn = cos_ref[rows, :], sin_ref[rows, :]
            q, k, v, g = _ret_qkvg(p_ref[rows, :], cs, sn)
            qb, kb, vb = q.astype(BF16), k.astype(BF16), v.astype(BF16)
            n, rstd = _group_norm(o_ref[rows, :])
            sg = _sigmoid(g)
            duv = du_ref[rows, :]
            dn = duv * (g * sg)
            dg = duv * n * (sg + g * sg * (1.0 - sg))
            do = rstd * (dn - jnp.mean(dn, axis=-1, keepdims=True) - n * jnp.mean(dn * n, axis=-1, keepdims=True))
            dob = do.astype(BF16)
            rb = st_ref[c]
            drb = dstate[...].astype(BF16)
            sc = (_dot_nt(qb, kb) * dmat).astype(BF16)
            dsc = (_dot_nt(dob, vb) * dmat).astype(BF16)
            qx = (q * xi).astype(BF16)
            kz = (k * zeta).astype(BF16)
            dq = _dot(dsc, kb) + _dot_nt(dob, rb) * xi
            dk = (_dot_tn(dsc, qb) + _dot_nt(vb, drb) * zeta) * (RET_DK ** -0.5)
            dvv = _dot_tn(sc, dob) + _dot(kz, drb)
            dstate[...] = dstate[...] * decay + _dot_tn(qx, dob)
            dp_ref[rows, :RET_DK] = (dq * cs + _roll(dq * sn, RET_DK // 2)).astype(BF16)
            dp_ref[rows, RET_DK:2 * RET_DK] = (dk * cs + _roll(dk * sn, RET_DK // 2)).astype(BF16)
            dp_ref[rows, 2 * RET_DK:2 * RET_DK + dv] = dvv.astype(BF16)
            dp_ref[rows, 2 * RET_DK + dv:] = dg.astype(BF16)

    rev = lambda h, b: (nb - 1 - b, h)
    return pl.pallas_call(
        body, name="ret_bwd", grid=(RET_HEADS, nb),
        in_specs=[pl.BlockSpec(memory_space=pltpu.SMEM),
                  pl.BlockSpec((tb, w), lambda h, b: (nb - 1 - b, cb0 + h)),
                  pl.BlockSpec((tb, RET_DK), lambda h, b: (nb - 1 - b, 0)),
                  pl.BlockSpec((tb, RET_DK), lambda h, b: (nb - 1 - b, 0)),
                  pl.BlockSpec((tb, dv), rev),
                  pl.BlockSpec((tb, dv), rev),
                  pl.BlockSpec((None, nch, RET_DK, dv), lambda h, b: (h, nb - 1 - b, 0, 0)),
                  ANY],
        out_specs=pl.BlockSpec((tb, w), lambda h, b: (nb - 1 - b, cb0 + h)),
        out_shape=jax.ShapeDtypeStruct((s, lay.total), BF16),
        input_output_aliases={7: 0},
        scratch_shapes=[pltpu.VMEM((RET_DK, dv), F32)],
        compiler_params=_cp("parallel", "arbitrary"),
    )(lg, proj, cosr, sinr, o, du, states, dproj)


ATTN_HEADS_PER_STEP = 2


def _attn_mask(rows, cols, row0, col0, keys_on_rows):
    ri = (lax.broadcasted_iota(jnp.int32, (rows, cols), 0) + row0) // CHUNK
    ci = (lax.broadcasted_iota(jnp.int32, (rows, cols), 1) + col0) // CHUNK
    return ri <= ci if keys_on_rows else ci <= ri


def _attn_fwd(q_cat, k_cat, v_t, proj, lay):
    s = q_cat.shape[0]
    bq = _attn_block(s)
    nq = s // bq
    mg_cb = lay.off["mg"] // MLA_DV

    hp = ATTN_HEADS_PER_STEP
    assert MLA_HEADS % hp == 0 and mg_cb % hp == 0

    def body(q_ref, k_ref, vt_ref, g_ref, o_ref, u_ref, lse_ref):
        i = pl.program_id(1)
        qs = [q_ref[:, a * MLA_QW:(a + 1) * MLA_QW] for a in range(hp)]

        def scores(a, j):
            r0 = pl.multiple_of(j * bq, bq)
            return _dot_nt(k_ref[pl.ds(r0, bq), a * MLA_QW:(a + 1) * MLA_QW], qs[a])

        def update(a, j, sc, m, l, acc):
            mn = jnp.maximum(m, jnp.max(sc, axis=0, keepdims=True))
            p = jnp.exp2(sc - mn)
            alpha = jnp.exp2(m - mn)
            l = alpha * l + jnp.sum(p, axis=0, keepdims=True)
            acc = alpha * acc + _dot(vt_ref[a, j], p.astype(BF16))
            return mn, l, acc

        def hide(sc):
            return jnp.where(_attn_mask(bq, bq, 0, 0, True), sc, NEG_INF)

        def pair(j, carry, last):
            sa = [scores(a, j) for a in range(hp)]
            sb = [scores(a, j + 1) for a in range(hp)]
            carry = [update(a, j, sa[a], *carry[a]) for a in range(hp)]
            return tuple(update(a, j + 1, hide(sb[a]) if last else sb[a], *carry[a]) for a in range(hp))

        def single(carry):
            return tuple(update(a, i, hide(scores(a, i)), *carry[a]) for a in range(hp))

        init = tuple((jnp.full((1, bq), NEG_INF, F32), jnp.zeros((1, bq), F32), jnp.zeros((MLA_DV, bq), F32))
                     for _ in range(hp))
        carry = lax.fori_loop(0, i // 2, lambda t, c: pair(2 * t, c, False), init)
        carry = lax.cond(i % 2 == 1, lambda c: pair(i - 1, c, True), single, carry)
        for a, (m, l, acc) in enumerate(carry):
            cols = slice(a * MLA_DV, (a + 1) * MLA_DV)
            o = (acc / l).T
            gv = g_ref[:, cols]
            o_ref[:, cols] = o
            u_ref[:, cols] = (o * (gv * _sigmoid(gv))).astype(BF16)
            lse_ref[a] = m + jnp.log2(l)

    return pl.pallas_call(
        body, name="attn_fwd", grid=(MLA_HEADS // hp, nq),
        in_specs=[pl.BlockSpec((bq, hp * MLA_QW), lambda h, i: (i, h)),
                  pl.BlockSpec((s, hp * MLA_QW), lambda h, i: (0, h)),
                  pl.BlockSpec((hp, nq, MLA_DV, bq), lambda h, i: (h, 0, 0, 0)),
                  pl.BlockSpec((bq, hp * MLA_DV), lambda h, i: (i, mg_cb // hp + h))],
        out_specs=[pl.BlockSpec((bq, hp * MLA_DV), lambda h, i: (i, h)),
                   pl.BlockSpec((bq, hp * MLA_DV), lambda h, i: (i, h)),
                   pl.BlockSpec((hp, None, 1, bq), lambda h, i: (h, i, 0, 0))],
        out_shape=[jax.ShapeDtypeStruct((s, MLA_HEADS * MLA_DV), F32),
                   jax.ShapeDtypeStruct((s, MLA_HEADS * MLA_DV), BF16),
                   jax.ShapeDtypeStruct((MLA_HEADS, nq, 1, bq), F32)],
        compiler_params=_cp("parallel", "parallel"),
    )(q_cat, k_cat, v_t, proj)


def _attn_bwd(q_cat, k_cat, v, k_t, do, lse, delta, ta, tb, tc):
    s = q_cat.shape[0]
    blk = _attn_block(s)
    nb = s // blk
    hp = ATTN_HEADS_PER_STEP
    qw, dvw = hp * MLA_QW, hp * MLA_DV

    def body(q_ref, k_ref, v_ref, kt_ref, do_ref, lse_ref, dl_ref, a_ref, b_ref, c_ref,
             dq_ref, dk_ref, dv_ref, dq_acc):
        j = pl.program_id(1)

        @pl.when(j == 0)
        def _():
            dq_acc[...] = jnp.zeros_like(dq_acc)

        def qcols(a):
            return slice(a * MLA_QW, (a + 1) * MLA_QW)

        def vcols(a):
            return slice(a * MLA_DV, (a + 1) * MLA_DV)

        kbs = [k_ref[:, qcols(a)] for a in range(hp)]
        vbs = [v_ref[:, vcols(a)] for a in range(hp)]

        def step(i, carry, masked):
            r0 = pl.multiple_of(i * blk, blk)
            out = []
            for a in range(hp):
                dk, dvv = carry[a]
                q, dob = q_ref[pl.ds(r0, blk), qcols(a)], do_ref[pl.ds(r0, blk), vcols(a)]
                sc = _dot_nt(kbs[a], q)
                if masked:
                    sc = jnp.where(_attn_mask(blk, blk, 0, 0, True), sc, NEG_INF)
                p = jnp.exp2(sc - lse_ref[a, i])
                dvv = dvv + _dot(p.astype(BF16), dob)
                ds = (p * (_dot_nt(vbs[a], dob) - dl_ref[a, i])).astype(BF16)
                dk = dk + _dot(ds, q)
                dq_acc[a, i] = dq_acc[a, i] + _dot(kt_ref[a], ds)
                out.append((dk, dvv))
            return tuple(out)

        init = tuple((jnp.zeros((blk, MLA_QW), F32), jnp.zeros((blk, MLA_DV), F32)) for _ in range(hp))
        carry = step(j, init, True)
        carry = lax.fori_loop(j + 1, nb, lambda i, c: step(i, c, False), carry)
        for a, (dk, dvv) in enumerate(carry):
            dk_ref[:, qcols(a)] = dk * LN2
            dv_ref[:, vcols(a)] = dvv

        @pl.when(j == nb - 1)
        def _():
            for a in range(hp):
                for i in range(nb):
                    rows = slice(i * blk, (i + 1) * blk)
                    dq = dq_acc[a, i].T * QK_SCALE
                    c0 = a * MLA_QW
                    dq_ref[rows, c0:c0 + MLA_NOPE] = dq[:, :MLA_NOPE].astype(BF16)
                    dq_ref[rows, c0 + MLA_NOPE:c0 + MLA_QW] = _rope_tile_bwd(
                        dq[:, MLA_NOPE:], a_ref[rows, :], b_ref[rows, :], c_ref[rows, :]).astype(BF16)

    tab = pl.BlockSpec((s, LANES), lambda h, j: (0, 0), pipeline_mode=pl.Buffered(1))
    rows = pl.BlockSpec((hp, nb, 1, blk), lambda h, j: (h, 0, 0, 0))
    return pl.pallas_call(
        body, name="attn_bwd", grid=(MLA_HEADS // hp, nb),
        in_specs=[pl.BlockSpec((s, qw), lambda h, j: (0, h)),
                  pl.BlockSpec((blk, qw), lambda h, j: (j, h)),
                  pl.BlockSpec((blk, dvw), lambda h, j: (j, h)),
                  pl.BlockSpec((hp, None, MLA_QW, blk), lambda h, j: (h, j, 0, 0)),
                  pl.BlockSpec((s, dvw), lambda h, j: (0, h)),
                  rows, rows, tab, tab, tab],
        out_specs=[pl.BlockSpec((s, qw), lambda h, j: (0, h)),
                   pl.BlockSpec((blk, qw), lambda h, j: (j, h)),
                   pl.BlockSpec((blk, dvw), lambda h, j: (j, h))],
        out_shape=[jax.ShapeDtypeStruct((s, MLA_HEADS * MLA_QW), BF16),
                   jax.ShapeDtypeStruct((s, MLA_HEADS * MLA_QW), F32),
                   jax.ShapeDtypeStruct((s, MLA_HEADS * MLA_DV), F32)],
        scratch_shapes=[pltpu.VMEM((hp, nb, MLA_QW, blk), F32)],
        compiler_params=_cp("parallel", "arbitrary"),
    )(q_cat, k_cat, v, k_t, do, lse, delta, ta, tb, tc)


def _place():
    return lax.axis_index("x"), lax.axis_index("y"), lax.axis_index("c")


def _slot(px, py, pc):
    return 4 * px + 2 * py + pc


def _all_gather(shards, layer, name, vmem=False, deps=()):
    n = len(shards)

    def body(*refs):
        srcs, outs = refs[:n], refs[n + len(deps):2 * n + len(deps)]
        send_sems, recv_sems, local_sems = refs[2 * n + len(deps):]
        x, y, c = _place()
        me, sibling = (x, y, c), (x, y, 1 - c)
        chips = [(1 - x, y), (x, 1 - y), (1 - x, 1 - y)]
        firsts, passes, locals_ = [], [], []

        def copy(a, k, block, to, src=None):
            dst = outs[a].at[_slot(*block)]
            return pltpu.make_async_remote_copy(
                src_ref=dst if src is None else src, dst_ref=dst,
                send_sem=send_sems.at[7 * a + k], recv_sem=recv_sems.at[7 * a + k],
                device_id=to, device_id_type=MESH)

        for a in range(n):
            src = srcs[a] if layer is None else srcs[a].at[layer]
            mine = pltpu.make_async_copy(src, outs[a].at[_slot(*me)], local_sems.at[a])
            mine.start()
            locals_.append(mine)
            first = [copy(a, 0, me, sibling, src=src)]
            first += [copy(a, 1 + j, me, (*chip, c), src=src) for j, chip in enumerate(chips)]
            for cp in first:
                cp.start()
            firsts += first
        for a in range(n):
            for j, chip in enumerate(chips):
                copy(a, 1 + j, (*chip, c), me).wait_recv()
                fwd = copy(a, 4 + j, (*chip, c), sibling)
                fwd.start()
                passes.append(fwd)
        for a in range(n):
            copy(a, 0, sibling, me).wait_recv()
            for j, chip in enumerate(chips):
                copy(a, 4 + j, (*chip, 1 - c), me).wait_recv()
        for cp in firsts + passes:
            cp.wait_send()
        for mine in locals_:
            mine.wait()

    space = pl.BlockSpec(memory_space=pltpu.VMEM) if vmem else ANY
    out_shape = [jax.ShapeDtypeStruct((N_DEV,) + (a.shape if layer is None else a.shape[1:]), a.dtype) for a in shards]
    return pl.pallas_call(
        body, name=name,
        in_specs=[space] * n + [ANY] * len(deps), out_specs=[space] * n, out_shape=out_shape,
        scratch_shapes=[pltpu.SemaphoreType.DMA((7 * n,)), pltpu.SemaphoreType.DMA((7 * n,)),
                        pltpu.SemaphoreType.DMA((n,))],
        compiler_params=pltpu.CompilerParams(has_side_effects=True),
    )(*shards, *deps)


HBM = pl.BlockSpec(memory_space=pltpu.HBM)
SEM = pl.BlockSpec(memory_space=pltpu.SEMAPHORE)
EFFECT = pltpu.SideEffectType.DATAFLOW_SIDE_EFFECTING


def _push_copies(srcs, lands, send_sems, recv_sems, local_sems, by_peer):
    x, y, c = _place()
    me = _slot(x, y, c)
    local, remote = [], []
    for a, (src, land) in enumerate(zip(srcs, lands)):
        local.append(pltpu.make_async_copy(src.at[me] if by_peer else src, land.at[me], local_sems.at[a]))
        for r in range(1, N_DEV):
            peer = (1 - x if r & 4 else x, 1 - y if r & 2 else y, 1 - c if r & 1 else c)
            remote.append(pltpu.make_async_remote_copy(
                src_ref=src.at[_slot(*peer)] if by_peer else src, dst_ref=land.at[me],
                send_sem=send_sems.at[7 * a + r - 1], recv_sem=recv_sems.at[7 * a + r - 1],
                device_id=peer, device_id_type=MESH))
    return local, remote


def _push_start(srcs, by_peer, after, name):
    n = len(srcs)
    srcs = [pltpu.with_memory_space_constraint(a, pltpu.HBM) for a in srcs]
    lands = [pltpu.with_memory_space_constraint(
        lax.empty((N_DEV,) + (a.shape[1:] if by_peer else a.shape), a.dtype), pltpu.HBM) for a in srcs]

    def body(*refs):
        k = 2 * n + len(after)
        local, remote = _push_copies(refs[:n], refs[n:2 * n], refs[k], refs[k + 1], refs[k + 2], by_peer)
        for cp in local + remote:
            cp.start()
        token = refs[k + 3 + 2 * n]
        token[...] = jnp.zeros_like(token)

    outs = pl.pallas_call(
        body, name=name,
        out_shape=(pltpu.SemaphoreType.DMA((7 * n,)), pltpu.SemaphoreType.DMA((7 * n,)),
                   pltpu.SemaphoreType.DMA((n,)),
                   *[pltpu.HBM(a.shape, a.dtype) for a in srcs], *[pltpu.HBM(a.shape, a.dtype) for a in lands],
                   jax.ShapeDtypeStruct((8, LANES), F32)),
        in_specs=[HBM] * (2 * n) + [ANY] * len(after),
        out_specs=(SEM, SEM, SEM, *([HBM] * (2 * n)), pl.BlockSpec(memory_space=pltpu.VMEM)),
        input_output_aliases={i: 3 + i for i in range(2 * n)},
        compiler_params=pltpu.CompilerParams(has_side_effects=EFFECT),
    )(*srcs, *lands, *after)
    return outs[:3], outs[3:3 + n], outs[3 + n:3 + 2 * n], outs[3 + 2 * n]


def _push_wait(sems, srcs, lands, by_peer, after, name):
    n = len(srcs)

    def body(*refs):
        local, remote = _push_copies(refs[:n], refs[n:2 * n], refs[2 * n], refs[2 * n + 1], refs[2 * n + 2], by_peer)
        for cp in local:
            cp.wait()
        for cp in remote:
            cp.wait_send()
            cp.wait_recv()

    outs = pl.pallas_call(
        body, name=name,
        out_shape=[pltpu.HBM(a.shape, a.dtype) for a in list(srcs) + list(lands)],
        in_specs=[HBM] * (2 * n) + [SEM] * 3 + [ANY] * len(after),
        out_specs=[HBM] * (2 * n),
        input_output_aliases={i: i for i in range(2 * n)},
        compiler_params=pltpu.CompilerParams(has_side_effects=EFFECT),
    )(*srcs, *lands, *sems, *after)
    return outs[n:]


def _adam_math(g, w, m, v):
    m = ADAM_B1 * m + (1.0 - ADAM_B1) * g
    v = ADAM_B2 * v + (1.0 - ADAM_B2) * (g * g)
    m_hat = m / (1.0 - ADAM_B1 ** ADAM_STEP)
    v_hat = v / (1.0 - ADAM_B2 ** ADAM_STEP)
    delta = -ADAM_LR * (m_hat / (jnp.sqrt(v_hat) + ADAM_EPS) + ADAM_WD * w)
    return delta, m, v


def _adam_sharded(recvs, first, w, m, v, prev, name):
    nl, r, c = w.shape
    n = len(recvs)
    by_cols = r % 128 != 0
    tr, tc = (r, _tile(c, LANES)) if by_cols else (_rtile(r, 128), c)
    nr = c // tc if by_cols else r // tr
    prev = list(prev) if prev is not None else []

    def tile(t):
        return (0, t) if by_cols else (t, 0)

    def body(*refs):
        g_refs = refs[:n]
        w_ref, m_ref, v_ref = refs[n:n + 3]
        go_ref, d_ref, mo_ref, vo_ref = refs[n + 3 + len(prev):]
        layer = pl.program_id(0)
        for l in range(n):
            @pl.when(layer == l)
            def _(l=l):
                g = g_refs[l][0].astype(F32)
                for i in range(1, N_DEV):
                    g = g + g_refs[l][i].astype(F32)
                delta, mn, vn = _adam_math(g, w_ref[...], m_ref[...], v_ref[...])
                go_ref[...] = g
                d_ref[...] = delta
                mo_ref[...] = mn
                vo_ref[...] = vn

    def recv_spec(l):
        def index(layer, i):
            return (0,) + tile(jnp.where(layer == l, i, jnp.where(layer < l, 0, nr - 1)))
        return pl.BlockSpec((N_DEV, tr, tc), index)

    blk = pl.BlockSpec((None, tr, tc), lambda layer, i: (first + layer,) + tile(i))
    out = jax.ShapeDtypeStruct(w.shape, F32)
    return pl.pallas_call(
        body, name=name, grid=(n, nr),
        in_specs=[recv_spec(l) for l in range(n)] + [blk, blk, blk] + [ANY] * len(prev),
        out_specs=[blk] * 4, out_shape=[out] * 4,
        input_output_aliases={n + 3 + k: k for k in range(len(prev))},
        compiler_params=_cp("arbitrary", "arbitrary"),
    )(*recvs, w, m, v, *prev)


def _adam_mod(c_all_t, dmod, w, m, v):
    nl, d, c = w.shape
    tr = _rtile(d, 128)

    def body(ct_ref, dm_ref, w_ref, m_ref, v_ref, go_ref, d_ref, mo_ref, vo_ref):
        ct = ct_ref[...].astype(BF16).astype(F32)
        dm = dm_ref[...].astype(BF16).astype(F32)
        g = ct[:, 0:1] * dm[0:1, :]
        for b in range(1, N_DEV):
            g = g + ct[:, b:b + 1] * dm[b:b + 1, :]
        delta, mn, vn = _adam_math(g, w_ref[...], m_ref[...], v_ref[...])
        go_ref[...] = g
        d_ref[...] = delta
        mo_ref[...] = mn
        vo_ref[...] = vn

    blk = pl.BlockSpec((None, tr, c), lambda layer, i: (layer, i, 0))
    out = jax.ShapeDtypeStruct(w.shape, F32)
    return pl.pallas_call(
        body, name="adam_mod", grid=(nl, d // tr),
        in_specs=[pl.BlockSpec((tr, N_DEV), lambda layer, i: (i, 0)),
                  pl.BlockSpec((None, N_DEV, c), lambda layer, i: (layer, 0, 0)), blk, blk, blk],
        out_specs=[blk] * 4, out_shape=[out] * 4,
        compiler_params=_cp("parallel", "parallel"),
    )(c_all_t, dmod, w, m, v)


def _adam_small(g, w, m, v, name):
    def body(g_ref, w_ref, m_ref, v_ref, d_ref, mo_ref, vo_ref):
        delta, mn, vn = _adam_math(g_ref[...], w_ref[...], m_ref[...], v_ref[...])
        d_ref[...] = delta
        mo_ref[...] = mn
        vo_ref[...] = vn

    out = jax.ShapeDtypeStruct(w.shape, F32)
    return pl.pallas_call(body, name=name, out_shape=[out] * 3)(g, w, m, v)


def _sum_devices(parts):
    def body(p_ref, o_ref):
        acc = p_ref[0]
        for i in range(1, N_DEV):
            acc = acc + p_ref[i]
        o_ref[...] = acc

    return pl.pallas_call(body, name="sum_devices",
                          out_shape=jax.ShapeDtypeStruct(parts.shape[1:], F32))(parts)


def _rope_tables(positions):
    pos = positions.astype(F32)[:, None]

    def cs(dim):
        inv = 1.0 / (ROPE_BASE ** (jnp.arange(0, dim, 2, dtype=F32) / dim))
        ang = pos * inv
        return jnp.cos(ang), jnp.sin(ang)

    cr, sr = cs(RET_DK)
    cm, sm = cs(MLA_ROPE)
    z = jnp.zeros_like(cm)
    pad = jnp.zeros((pos.shape[0], LANES - MLA_ROPE), F32)
    cosr = jnp.concatenate([cr, cr], axis=1)
    sinr = jnp.concatenate([-sr, sr], axis=1)
    ta = jnp.concatenate([cm, cm, pad], axis=1)
    tb = jnp.concatenate([-sm, z, pad], axis=1)
    tc = jnp.concatenate([z, sm, pad], axis=1)
    return cosr, sinr, ta, tb, tc


def _layer_fwd(x, mod, g_norm, g_cq, g_ckv, wts, tabs, lg, lay, deps):
    d = x.shape[1]
    cosr, sinr, ta, tb, tc = tabs
    shift, scale, gate = mod[:, :d], mod[:, d:2 * d], mod[:, 2 * d:]
    h = _norm_mod_fwd(x, g_norm, scale, shift, deps)
    proj = _matmul(h, wts["in"], name="mm_proj", tn_cap=1920)
    o_ret, u_ret, states = _ret_fwd(proj, lg, cosr, sinr, lay)
    y_ret = _matmul(u_ret, wts["ret"], name="mm_y")
    cqn, ckvn = _mla_prep(proj, g_cq, g_ckv, lay)
    qp = _matmul(cqn, wts["uq"], name="mm_up")
    kvp = _matmul(ckvn, wts["ukv"], name="mm_up")
    q_cat, k_cat, v, k_t, v_t = _qk_prep(qp, kvp, proj, ta, tb, tc, lay)
    o_mla, u_mla, lse = _attn_fwd(q_cat, k_cat, v_t, proj, lay)
    y_mla = _matmul(u_mla, wts["mla"], name="mm_y")
    merged = _merge_fwd(y_ret, y_mla, proj, lay)
    out, x_next = _out_proj_resid(merged, wts["out"], x, gate)
    saved = dict(x=x, h=h, proj=proj, o_ret=o_ret, u_ret=u_ret, states=states, y_ret=y_ret, cqn=cqn,
                 ckvn=ckvn, q_cat=q_cat, k_cat=k_cat, v=v, k_t=k_t, o_mla=o_mla, u_mla=u_mla, lse=lse,
                 y_mla=y_mla, merged=merged, out=out)
    return x_next, saved


def _to_owner_blocks_cols(g, n_local):
    k = g.shape[0]
    return g.reshape(k, N_DEV, n_local).transpose(1, 0, 2)


def _from_owner_blocks_cols(g):
    return g.transpose(1, 0, 2).reshape(g.shape[1], -1)


def _layer_bwd(dxn, sv, mod, g_norm, g_cq, g_ckv, wts, tabs, lg, lay, deps, shard_cols, push):
    d = dxn.shape[1]
    n_in, n_uq, n_ukv = shard_cols
    cosr, sinr, ta, tb, tc = tabs
    scale, gate = mod[:, d:2 * d], mod[:, 2 * d:]
    gdt = BF16
    dout, dgate = _resid_bwd(dxn, sv["out"], gate, deps)
    dmerged = _matmul(dout, wts["out"], tb=True, name="mm_dy")
    dw_out = _matmul(sv["merged"], dout, ta=True, out_dtype=gdt, name="mm_dw")
    dy_ret, dy_mla, dproj = _merge_bwd(dmerged, sv["y_ret"], sv["y_mla"], sv["proj"], lay)
    du_ret = _matmul(dy_ret, wts["ret"], tb=True, name="mm_dy")
    dw_ret = _matmul(sv["u_ret"], dy_ret, ta=True, out_dtype=gdt, name="mm_dw")
    dproj = _ret_bwd(sv["proj"], lg, cosr, sinr, sv["o_ret"], du_ret, sv["states"], dproj, lay)
    du_mla = _matmul(dy_mla, wts["mla"], tb=True, name="mm_dy")
    dw_mla = _matmul(sv["u_mla"], dy_mla, ta=True, out_dtype=gdt, name="mm_dw")
    do_mla, delta, dproj = _mla_gate_bwd(du_mla, sv["o_mla"], sv["proj"], dproj, lay)
    delta = delta[:, :MLA_HEADS].T.reshape(sv["lse"].shape)
    dqp, dk_cat, dv = _attn_bwd(sv["q_cat"], sv["k_cat"], sv["v"], sv["k_t"], do_mla, sv["lse"], delta,
                                ta, tb, tc)
    dkvp, dproj = _kv_bwd_prep(dk_cat, dv, ta, tb, tc, dproj, lay)
    dcqn = _matmul(dqp, wts["uq"], tb=True, name="mm_dlat")
    dckvn = _matmul(dkvp, wts["ukv"], tb=True, name="mm_dlat")
    dw_uq = _matmul(sv["cqn"], dqp, ta=True, out_dtype=gdt, name="mm_dwup")
    dw_ukv = _matmul(sv["ckvn"], dkvp, ta=True, out_dtype=gdt, name="mm_dwup")
    sent = push("a", [_to_owner_blocks_cols(_uq_to_logical(dw_uq), n_uq),
                      _to_owner_blocks_cols(_ukv_to_logical(dw_ukv), n_ukv),
                      dw_ret.reshape(N_DEV, -1, d), dw_mla.reshape(N_DEV, -1, d), dw_out.reshape(N_DEV, -1, d)])
    dproj, dg_cq, dg_ckv = _mla_prep_bwd(sv["proj"], dcqn, dckvn, g_cq, g_ckv, dproj, lay)
    dw_in = _matmul(sv["h"], dproj, ta=True, out_dtype=gdt, name="mm_dwin", tn_cap=1920, deps=sent)
    sent = push("b", [_scatter_dw_in(dw_in, lay, n_in)])
    dh = _matmul(dproj, wts["in"], tb=True, name="mm_dh", deps=sent)
    dx, dshift, dscale, dg_norm = _norm_mod_bwd(sv["x"], g_norm, scale, dh, dxn)
    dmod = jnp.concatenate([dshift, dscale, dgate], axis=1)
    small = dict(dmod=dmod, g_norm=dg_norm, g_cq=dg_cq, g_ckv=dg_ckv)
    return dx, small


def kernel(x, c, positions, w_mod, b_mod, g_norm, w_in, g_cq, g_ckv, w_uq, w_ukv, w_ret_proj, w_mla_proj, w_out, g_final, loss_target, m_w_mod, m_b_mod, m_g_norm, m_w_in, m_g_cq, m_g_ckv, m_w_uq, m_w_ukv, m_w_ret_proj, m_w_mla_proj, m_w_out, m_g_final, v_w_mod, v_b_mod, v_g_norm, v_w_in, v_g_cq, v_g_ckv, v_w_uq, v_w_ukv, v_w_ret_proj, v_w_mla_proj, v_w_out, v_g_final):
    nl, d, _ = w_mod.shape
    s = x.shape[1]
    rank = g_cq.shape[1]
    lay = Layout(d, rank, g_ckv.shape[1])
    me = _slot(*_place())
    x0 = x.reshape(s, d)
    target = loss_target.reshape(s, d)
    tabs = _rope_tables(positions.reshape(s))
    lg = jnp.log(1.0 - 2.0 ** (-5.0 - jnp.arange(RET_HEADS, dtype=F32)))

    c_act = c * _sigmoid(c)
    (c_all,) = _all_gather([c_act.reshape(d // LANES, LANES)], None, "gather_c", vmem=True)
    c_all = c_all.reshape(N_DEV, d)
    n_mod = w_mod.shape[2]
    mod_part = jnp.stack([_matmul(c_all, w_mod[l], name="mm_mod", tm_cap=8) for l in range(nl)])
    (mod_all,) = _all_gather([mod_part.reshape(-1, LANES)], None, "gather_mod", vmem=True)
    mod_all = mod_all.reshape(N_DEV, nl, N_DEV, n_mod)
    mod = lax.dynamic_index_in_dim(mod_all, me, axis=2, keepdims=False)
    mod = mod.transpose(1, 0, 2).reshape(nl, N_DEV * n_mod) + b_mod

    shards = [[w[l].astype(BF16) for w in (w_in, w_uq, w_ukv, w_ret_proj, w_mla_proj, w_out)] for l in range(nl)]
    xl, saved, wts_all = x0, [], []
    gathered = _all_gather(shards[0], None, "gather_w")
    for l in range(nl):
        g_in, g_uq, g_ukv, g_ret, g_mla, g_out = gathered
        deps = []
        if l + 1 < nl:
            sems, srcs, lands, token = _push_start(shards[l + 1], False, [g_out, mod], "gather_start_%d" % (l + 1))
            deps = [token]
        wts = {
            "in": _assemble_w_in(g_in, lay),
            "uq": _uq_to_physical(_from_owner_blocks_cols(g_uq)),
            "ukv": _ukv_to_physical(_from_owner_blocks_cols(g_ukv)),
            "ret": g_ret.reshape(-1, d), "mla": g_mla.reshape(-1, d), "out": g_out.reshape(-1, d),
        }
        wts_all.append(wts)
        xl, sv = _layer_fwd(xl, mod[l:l + 1], g_norm[l:l + 1], g_cq[l:l + 1], g_ckv[l:l + 1], wts, tabs, lg, lay,
                            deps)
        saved.append(sv)
        if l + 1 < nl:
            gathered = _push_wait(sems, srcs, lands, False, [xl], "gather_wait_%d" % (l + 1))
    loss_lanes, dx, dg_final = _final_loss(xl, g_final.reshape(1, d), target)

    small = [None] * nl
    flying = {l: [] for l in range(nl)}
    recv = {}
    shard_cols = (w_in.shape[2], w_uq.shape[2], w_ukv.shape[2])

    def pusher(l):
        def push(group, arrays):
            sems, srcs, lands, token = _push_start(arrays, True, [], "exchange_start_%d%s" % (l, group))
            flying[l].append((group, sems, srcs, lands))
            return [token]
        return push

    def land(l, after):
        got = {}
        for group, sems, srcs, lands in flying[l]:
            got[group] = _push_wait(sems, srcs, lands, True, after, "exchange_wait_%d%s" % (l, group))
        recv[l] = list(got["b"]) + list(got["a"])

    for l in reversed(range(nl)):
        dx, small[l] = _layer_bwd(dx, saved[l], mod[l:l + 1], g_norm[l:l + 1], g_cq[l:l + 1], g_ckv[l:l + 1],
                                  wts_all[l], tabs, lg, lay, [], shard_cols, pusher(l))
        if l + 1 < nl:
            land(l + 1, [dx])
    grad_x = dx.reshape(x.shape)

    out = {}
    w_in_t, m_w_in_t, v_w_in_t = (jnp.swapaxes(a, 1, 2) for a in (w_in, m_w_in, v_w_in))
    sharded = (("w_in", w_in_t, m_w_in_t, v_w_in_t), ("w_uq", w_uq, m_w_uq, v_w_uq), ("w_ukv", w_ukv, m_w_ukv, v_w_ukv),
               ("w_ret_proj", w_ret_proj, m_w_ret_proj, v_w_ret_proj),
               ("w_mla_proj", w_mla_proj, m_w_mla_proj, v_w_mla_proj), ("w_out", w_out, m_w_out, v_w_out))
    if nl > 1:
        for i, (key, w, m, v) in enumerate(sharded):
            out[key] = _adam_sharded([recv[l][i] for l in range(1, nl)], 1, w, m, v, None, "adam_" + key)
    done = [out[key][0] for key, _, _, _ in sharded if key in out]

    pack = jnp.concatenate(
        [jnp.concatenate([sm[k] for sm in small], axis=0).reshape(-1)
         for k in ("dmod", "g_norm", "g_cq", "g_ckv")] + [dg_final.reshape(-1), loss_lanes.reshape(-1)])
    (pack_all,) = _all_gather([pack.reshape(-1, LANES)], None, "gather_small", vmem=True, deps=done)
    tot = _sum_devices(pack_all).reshape(-1)
    sizes = [nl * 3 * d, nl * d, nl * rank, nl * rank, d]
    offs = np.cumsum([0] + sizes)
    grad_b_mod = tot[offs[0]:offs[1]].reshape(nl, 3 * d)
    grad_g_norm = tot[offs[1]:offs[2]].reshape(nl, d)
    grad_g_cq = tot[offs[2]:offs[3]].reshape(nl, rank)
    grad_g_ckv = tot[offs[3]:offs[4]].reshape(nl, rank)
    grad_g_final = tot[offs[4]:offs[5]]
    loss = tot[offs[5]]
    dmod_all = pack_all.reshape(N_DEV, -1)[:, :sizes[0]].reshape(N_DEV, nl, 3 * d)
    dmod_mine = lax.dynamic_slice_in_dim(dmod_all, me * n_mod, n_mod, axis=2).transpose(1, 0, 2)

    out["w_mod"] = _adam_mod(c_all.T, dmod_mine, w_mod, m_w_mod, v_w_mod)
    land(0, [out["w_mod"][0]])
    for i, (key, w, m, v) in enumerate(sharded):
        out[key] = _adam_sharded([recv[0][i]], 0, w, m, v, out.get(key), "adam0_" + key)
    out["w_in"] = tuple(jnp.swapaxes(a, 1, 2) for a in out["w_in"])
    for key, g, w, m, v in (("b_mod", grad_b_mod, b_mod, m_b_mod, v_b_mod),
                            ("g_norm", grad_g_norm, g_norm, m_g_norm, v_g_norm),
                            ("g_cq", grad_g_cq, g_cq, m_g_cq, v_g_cq),
                            ("g_ckv", grad_g_ckv, g_ckv, m_g_ckv, v_g_ckv),
                            ("g_final", grad_g_final.reshape(1, d), g_final.reshape(1, d),
                             m_g_final.reshape(1, d), v_g_final.reshape(1, d))):
        out[key] = (g,) + tuple(_adam_small(g, w, m, v, "adam_" + key))
    out["g_final"] = tuple(a.reshape(d) for a in out["g_final"])

    names = ("w_mod", "b_mod", "g_norm", "w_in", "g_cq", "g_ckv", "w_uq", "w_ukv", "w_ret_proj",
             "w_mla_proj", "w_out", "g_final")
    return (loss, grad_x, *[out[k][0] for k in names], *[out[k][1] for k in names],
            *[out[k][2] for k in names], *[out[k][3] for k in names])
```
